```python
import math
import jax
import jax.numpy as jnp
from jax import lax
import numpy as np

D_MODEL = 1024
BATCH = 16
SEQ = 256
DEPTH = 4
DEC_BATCH = 2
DEC_SEQ = 4096
PAST_LEN = 512

GRID_W = 64
POS_BASE = 10000.0
EPS = 1e-6
A_HEADS = 4
A_DK = 128
A_DV = 128
A_DIM = A_HEADS * A_DV
SHORT_CONV = 7
CHUNK = 64
B_CH = D_MODEL // 2
B_CONV = 31
MIX_WIDTH = A_DIM + B_CH
P_AB = 4 * A_DIM + 4 * A_HEADS + 2 * B_CH
C_GROUPS = 8
PEER_HEADS = 8
PEER_KEYS = 128
PEER_N = PEER_KEYS * PEER_KEYS
PEER_DK = 128
PEER_TOPK = 16
PEER_BLOCK = 128

kernel_name = 'hybrid_deltanet_conformer_fnet_peer_diffusion_step'

F32 = jnp.float32


def rmsnorm(x, g):
    xf = x.astype(F32)
    y = xf * lax.rsqrt(jnp.mean(xf * xf, axis=-1, keepdims=True) + EPS) * g.astype(F32)
    return y.astype(x.dtype)


def layernorm(x, g, b):
    xf = x.astype(F32)
    mu = jnp.mean(xf, axis=-1, keepdims=True)
    var = jnp.mean(jnp.square(xf - mu), axis=-1, keepdims=True)
    return ((xf - mu) * lax.rsqrt(var + EPS) * g.astype(F32) + b.astype(F32)).astype(x.dtype)


def l2norm(x):
    xf = x.astype(F32)
    return xf * lax.rsqrt(jnp.sum(xf * xf, axis=-1, keepdims=True) + EPS)


def depthwise_conv(x, w):
    k, ch = w.shape
    return lax.conv_general_dilated(x, w.astype(x.dtype)[:, None, :], window_strides=(1,),
                                    padding=[(k // 2, k // 2)],
                                    dimension_numbers=('NWC', 'WIO', 'NWC'),
                                    feature_group_count=ch)


def grid_pos_emb(n_tokens):
    rows = n_tokens // GRID_W
    r = jnp.repeat(jnp.arange(rows, dtype=F32), GRID_W)
    col = jnp.tile(jnp.arange(GRID_W, dtype=F32), rows)
    nf = D_MODEL // 4
    freqs = jnp.exp(-math.log(POS_BASE) * jnp.arange(nf, dtype=F32) / nf)
    ar = r[:, None] * freqs
    ac = col[:, None] * freqs
    return jnp.concatenate([jnp.sin(ar), jnp.cos(ar), jnp.sin(ac), jnp.cos(ac)], axis=-1)


def gated_delta_chunked(q, k, v, log_g, beta, s0):
    b, s, h, dk = q.shape
    dv = v.shape[-1]
    n = s // CHUNK

    def to_blocks(t):
        t = t.astype(F32).reshape(b, n, CHUNK, h, *t.shape[3:])
        return jnp.moveaxis(t, 3, 1)

    qb, kb, vb = to_blocks(q), to_blocks(k), to_blocks(v)
    lg, bt = to_blocks(log_g), to_blocks(beta)
    gam = jnp.cumsum(lg, axis=-1)
    idx = jnp.arange(CHUNK)
    incl = idx[:, None] >= idx[None, :]
    strict = idx[:, None] > idx[None, :]
    diff = gam[..., :, None] - gam[..., None, :]
    decay = jnp.where(incl, jnp.exp(jnp.where(incl, diff, 0.0)), 0.0)
    kk = jnp.einsum('bhntd,bhnid->bhnti', kb, kb)
    a_mat = jnp.where(strict, bt[..., :, None] * decay * kk, 0.0) + jnp.eye(CHUNK, dtype=F32)
    rhs = jnp.concatenate([kb * (bt * jnp.exp(gam))[..., None], vb * bt[..., None]], axis=-1)
    sol = lax.linalg.triangular_solve(a_mat, rhs, left_side=True, lower=True, unit_diagonal=True)
    w_blk, u_blk = sol[..., :dk], sol[..., dk:]
    p_blk = decay * jnp.einsum('bhntd,bhnid->bhnti', qb, kb)
    q_dec = qb * jnp.exp(gam)[..., None]
    k_dec = kb * jnp.exp(gam[..., -1:] - gam)[..., None]
    g_last = jnp.exp(gam[..., -1])

    def step(state, xs):
        w_c, u_c, p_c, q_c, k_c, g_c = xs
        u = u_c - jnp.einsum('bhcd,bhde->bhce', w_c, state)
        o = jnp.einsum('bhcd,bhde->bhce', q_c, state) + jnp.einsum('bhti,bhie->bhte', p_c, u)
        state = g_c[..., None, None] * state + jnp.einsum('bhcd,bhce->bhde', k_c, u)
        return state, o

    xs = tuple(jnp.moveaxis(t, 2, 0) for t in (w_blk, u_blk, p_blk, q_dec, k_dec, g_last))
    s_fin, o = lax.scan(step, s0.astype(F32), xs)
    o = jnp.transpose(o, (1, 0, 3, 2, 4)).reshape(b, s, h, dv)
    return o, s_fin


def delta_mixer(qkv, z, alpha, beta_raw, conv_w, a_log, dt_bias, norm_g, s0):
    b, s, _ = qkv.shape
    qkv = jax.nn.silu(depthwise_conv(qkv, conv_w))
    q, k, v = jnp.split(qkv, 3, axis=-1)
    q = l2norm(q.reshape(b, s, A_HEADS, A_DK)) * (A_DK ** -0.5)
    k = l2norm(k.reshape(b, s, A_HEADS, A_DK))
    v = v.reshape(b, s, A_HEADS, A_DV)
    beta = jax.nn.sigmoid(beta_raw.astype(F32)).reshape(b, s, 2, A_HEADS)
    alpha = alpha.astype(F32).reshape(b, s, 2, A_HEADS)
    log_g = -jnp.exp(a_log.astype(F32)) * jax.nn.softplus(alpha + dt_bias.astype(F32))
    o_f, s_f = gated_delta_chunked(q, k, v, log_g[:, :, 0], beta[:, :, 0], s0[:, 0])
    fl = lambda t: jnp.flip(t, axis=1)
    o_b, s_b = gated_delta_chunked(fl(q), fl(k), fl(v), fl(log_g[:, :, 1]), fl(beta[:, :, 1]), s0[:, 1])
    o = o_f + fl(o_b)
    o = (o * lax.rsqrt(jnp.mean(o * o, axis=-1, keepdims=True) + EPS) * norm_g.astype(F32)
         * jax.nn.silu(z.astype(F32).reshape(b, s, A_HEADS, A_DV)))
    return o.reshape(b, s, A_DIM).astype(qkv.dtype), jnp.stack([s_f, s_b], axis=1)


def conformer_conv(u, dw_w, dw_b, ln_g, ln_b):
    a, gt = jnp.split(u, 2, axis=-1)
    hh = a * jax.nn.sigmoid(gt)
    hh = depthwise_conv(hh, dw_w) + dw_b.astype(hh.dtype)
    return jax.nn.silu(layernorm(hh, ln_g, ln_b))


def fourier_mix(h):
    b, s, d = h.shape
    hg = h.astype(F32).reshape(b, s, C_GROUPS, d // C_GROUPS)
    f = jnp.fft.fft2(hg, axes=(1, 3), norm='ortho').real
    return f.reshape(b, s, d).astype(h.dtype)


def peer(h, wq, k1, k2, u_tab, v_tab):
    b, s, d = h.shape
    t = b * s
    x = h.reshape(t, d)
    q = (x @ wq).reshape(t, PEER_HEADS, 2, PEER_DK).astype(F32)
    s1 = jnp.einsum('thd,hnd->thn', q[:, :, 0], k1.astype(F32))
    s2 = jnp.einsum('thd,hnd->thn', q[:, :, 1], k2.astype(F32))
    v1, i1 = lax.top_k(s1, PEER_TOPK)
    v2, i2 = lax.top_k(s2, PEER_TOPK)
    cand = (v1[..., :, None] + v2[..., None, :]).reshape(t, PEER_HEADS, PEER_TOPK * PEER_TOPK)
    cidx = (i1[..., :, None] * PEER_KEYS + i2[..., None, :]).reshape(t, PEER_HEADS, PEER_TOPK * PEER_TOPK)
    sv, si = lax.top_k(cand, PEER_TOPK)
    eidx = jnp.take_along_axis(cidx, si, axis=-1)
    gates = jax.nn.softmax(sv, axis=-1).astype(h.dtype)
    nb = t // PEER_BLOCK

    def block(args):
        xb, eb, gb = args
        act = jax.nn.gelu(jnp.einsum('phkd,pd->phk', u_tab[eb], xb))
        return jnp.einsum('phk,phkd->pd', gb * act, v_tab[eb])

    out = lax.map(block, (x.reshape(nb, PEER_BLOCK, d),
                          eidx.reshape(nb, PEER_BLOCK, PEER_HEADS, PEER_TOPK),
                          gates.reshape(nb, PEER_BLOCK, PEER_HEADS, PEER_TOPK)))
    return out.reshape(b, s, d)


def trunk(x, cond, s0, prm):
    states = []
    for l in range(DEPTH):
        mod = (jax.nn.silu(cond) @ prm['ada_w'][l] + prm['ada_b'][l])[:, None, :]
        sh1, sc1, g1, sh2, sc2, g2 = jnp.split(mod, 6, axis=-1)
        h = rmsnorm(x, prm['norm1_g'][l]) * (1 + sc1) + sh1
        e = l // 2
        if l % 2 == 0:
            p = h @ prm['w_in_ab'][e]
            o4 = 4 * A_DIM
            o_a, st = delta_mixer(p[..., :3 * A_DIM], p[..., 3 * A_DIM:o4],
                                  p[..., o4:o4 + 2 * A_HEADS], p[..., o4 + 2 * A_HEADS:o4 + 4 * A_HEADS],
                                  prm['conv_qkv_w'][e], prm['a_log'][e], prm['dt_bias'][e],
                                  prm['delta_norm_g'][e], s0[:, e])
            o_b = conformer_conv(p[..., o4 + 4 * A_HEADS:], prm['conf_dw_w'][e], prm['conf_dw_b'][e],
                                 prm['conf_ln_g'][e], prm['conf_ln_b'][e])
            mix = jnp.concatenate([o_a, o_b], axis=-1) @ prm['w_out_ab'][e]
            states.append(st)
        else:
            mix = fourier_mix(h) @ prm['w_out_c'][e]
        x = x + g1 * mix
        h = rmsnorm(x, prm['norm2_g'][l]) * (1 + sc2) + sh2
        x = x + g2 * peer(h, prm['peer_wq'][l], prm['peer_k1'][l], prm['peer_k2'][l],
                          prm['peer_u'][l], prm['peer_v'][l])
    return rmsnorm(x, prm['final_norm_g']), jnp.stack(states, axis=1)


def setup_inputs(seed: int = 0) -> dict:
    key = jax.random.key(seed)
    ks = jax.random.split(key, 32)
    ne = (DEPTH + 1) // 2
    no = DEPTH // 2
    nrm = lambda i, shape, scale: jax.random.normal(ks[i], shape, F32) * scale
    dt = jnp.exp(jax.random.uniform(ks[12], (ne, 2, A_HEADS), F32, math.log(1e-3), math.log(0.1)))
    return {
        'x_prompt': nrm(0, (BATCH, SEQ, D_MODEL), 1.0),
        'x_sample': nrm(1, (DEC_BATCH, DEC_SEQ, D_MODEL), 1.0),
        'state_delta': nrm(2, (DEC_BATCH, ne, 2, A_HEADS, A_DK, A_DV), 0.1),
        'c': nrm(3, (DEC_BATCH, D_MODEL), 1.0),
        'c_ctx': nrm(4, (D_MODEL,), 1.0),
        'ada_w': nrm(5, (DEPTH, D_MODEL, 6 * D_MODEL), 0.5 * D_MODEL ** -0.5),
        'ada_b': nrm(6, (DEPTH, 6 * D_MODEL), 0.02),
        'norm1_g': 1.0 + nrm(7, (DEPTH, D_MODEL), 0.02),
        'norm2_g': 1.0 + nrm(8, (DEPTH, D_MODEL), 0.02),
        'w_in_ab': nrm(9, (ne, D_MODEL, P_AB), D_MODEL ** -0.5),
        'conv_qkv_w': nrm(10, (ne, SHORT_CONV, 3 * A_DIM), SHORT_CONV ** -0.5),
        'a_log': jnp.log(jax.random.uniform(ks[11], (ne, 2, A_HEADS), F32, 1.0, 16.0)),
        'dt_bias': dt + jnp.log(-jnp.expm1(-dt)),
        'delta_norm_g': 1.0 + nrm(13, (ne, A_DV), 0.02),
        'conf_dw_w': nrm(14, (ne, B_CONV, B_CH), B_CONV ** -0.5),
        'conf_dw_b': nrm(15, (ne, B_CH), 0.02),
        'conf_ln_g': 1.0 + nrm(16, (ne, B_CH), 0.02),
        'conf_ln_b': nrm(17, (ne, B_CH), 0.02),
        'w_out_ab': nrm(18, (ne, MIX_WIDTH, D_MODEL), MIX_WIDTH ** -0.5),
        'w_out_c': nrm(19, (no, D_MODEL, D_MODEL), D_MODEL ** -0.5),
        'peer_wq': nrm(20, (DEPTH, D_MODEL, PEER_HEADS * 2 * PEER_DK), D_MODEL ** -0.5),
        'peer_k1': nrm(21, (DEPTH, PEER_HEADS, PEER_KEYS, PEER_DK), PEER_DK ** -0.5),
        'peer_k2': nrm(22, (DEPTH, PEER_HEADS, PEER_KEYS, PEER_DK), PEER_DK ** -0.5),
        'peer_u': nrm(23, (DEPTH, PEER_N, D_MODEL), D_MODEL ** -0.5),
        'peer_v': nrm(24, (DEPTH, PEER_N, D_MODEL), 0.5),
        'final_norm_g': 1.0 + nrm(25, (D_MODEL,), 0.02),
    }


def reference(x_prompt, x_sample, state_delta, c, c_ctx, ada_w, ada_b, norm1_g, norm2_g,
              w_in_ab, conv_qkv_w, a_log, dt_bias, delta_norm_g, conf_dw_w, conf_dw_b,
              conf_ln_g, conf_ln_b, w_out_ab, w_out_c, peer_wq, peer_k1, peer_k2,
              peer_u, peer_v, final_norm_g):
    prm = {'ada_w': ada_w, 'ada_b': ada_b, 'norm1_g': norm1_g, 'norm2_g': norm2_g,
           'w_in_ab': w_in_ab, 'conv_qkv_w': conv_qkv_w, 'a_log': a_log, 'dt_bias': dt_bias,
           'delta_norm_g': delta_norm_g, 'conf_dw_w': conf_dw_w, 'conf_dw_b': conf_dw_b,
           'conf_ln_g': conf_ln_g, 'conf_ln_b': conf_ln_b, 'w_out_ab': w_out_ab,
           'w_out_c': w_out_c, 'peer_wq': peer_wq, 'peer_k1': peer_k1, 'peer_k2': peer_k2,
           'peer_u': peer_u, 'peer_v': peer_v, 'final_norm_g': final_norm_g}
    ne = (DEPTH + 1) // 2
    s0_ctx = jnp.zeros((x_prompt.shape[0], ne, 2, A_HEADS, A_DK, A_DV), x_prompt.dtype)
    y_prompt, ctx_states = trunk(x_prompt, c_ctx[None, :], s0_ctx, prm)
    new_state_delta = ctx_states.astype(x_prompt.dtype)
    xs = x_sample + grid_pos_emb(x_sample.shape[1]).astype(x_sample.dtype)[None]
    y_sample, _ = trunk(xs, c, state_delta, prm)
    return (y_prompt, y_sample, new_state_delta)
```

```python
import functools
import math

import jax
import jax.numpy as jnp
from jax import lax
from jax.experimental import pallas as pl
from jax.experimental.pallas import tpu as pltpu

D_MODEL = 1024
DEPTH = 4
GRID_W = 64
POS_BASE = 10000.0
EPS = 1e-6
A_HEADS = 4
A_DK = 128
A_DV = 128
A_DIM = A_HEADS * A_DV
CHUNK = 64
B_CH = D_MODEL // 2
C_GROUPS = 8
PEER_HEADS = 8
PEER_KEYS = 128
PEER_N = PEER_KEYS * PEER_KEYS
PEER_DK = 128
PEER_TOPK = 16

F32 = jnp.float32
BF16 = jnp.bfloat16

PEER_TOKEN_TILE = 512
PEER_KEYS_PER_STEP = 8
PEER_EXPERT_TILE = PEER_KEYS_PER_STEP * PEER_KEYS
PEER_J_BLOCK = 32
LANES = 128
VMEM_LIMIT_BYTES = 56 * 1024 * 1024


def _gelu_tanh(x):
    return 0.5 * x * (1.0 + jnp.tanh(math.sqrt(2.0 / math.pi) * (x + 0.044715 * (x * x * x))))


def _peer_dense_kernel(h_ref, u_ref, vt_ref, a_ref, n_ref, b_ref, r_ref, o_ref, hid_ref, w_ref):
    e = pl.program_id(1)

    @pl.when(e == 0)
    def _():
        o_ref[...] = jnp.zeros_like(o_ref)

    hid_ref[...] = lax.dot_general(u_ref[...], h_ref[...], (((1,), (1,)), ((), ())),
                                   preferred_element_type=F32)
    i0 = pl.multiple_of(e * PEER_KEYS_PER_STEP, PEER_KEYS_PER_STEP)
    n_jb = PEER_KEYS // PEER_J_BLOCK
    n_it = (PEER_TOKEN_TILE // LANES) * n_jb

    def body(it, carry):
        c = it // n_jb
        jb = it % n_jb
        ls = pl.ds(pl.multiple_of(c * LANES, LANES), LANES)
        j0 = pl.multiple_of(jb * PEER_J_BLOCK, PEER_J_BLOCK)
        js = pl.ds(j0, PEER_J_BLOCK)
        accs = [jnp.zeros((PEER_J_BLOCK, LANES), F32) for _ in range(PEER_KEYS_PER_STEP)]
        for h in range(PEER_HEADS):
            b = b_ref[h, js, ls]
            r = r_ref[h, js, ls]
            a_rows = a_ref[h, pl.ds(i0, PEER_KEYS_PER_STEP), ls]
            n_rows = n_ref[h, pl.ds(i0, PEER_KEYS_PER_STEP), ls]
            for ii in range(PEER_KEYS_PER_STEP):
                a = a_rows[ii:ii + 1, :]
                n = n_rows[ii:ii + 1, :]
                accs[ii] = accs[ii] + jnp.where(r < n, b, 0.0) * a
        for ii in range(PEER_KEYS_PER_STEP):
            rows = pl.ds(pl.multiple_of(ii * PEER_KEYS + j0, PEER_J_BLOCK), PEER_J_BLOCK)
            w_ref[rows, ls] = (accs[ii] * _gelu_tanh(hid_ref[rows, ls])).astype(BF16)
        return carry

    lax.fori_loop(0, n_it, body, 0)
    o_ref[...] += jnp.dot(vt_ref[...], w_ref[...], preferred_element_type=F32)


def _peer_dense(h_bf, u_bf, vt_bf, a_t, n_t, b_t, r_t):
    t = h_bf.shape[0]
    assert t % PEER_TOKEN_TILE == 0
    grid = (t // PEER_TOKEN_TILE, PEER_N // PEER_EXPERT_TILE)
    gate_spec = pl.BlockSpec((PEER_HEADS, PEER_KEYS, PEER_TOKEN_TILE), lambda ti, ei: (0, 0, ti))
    return pl.pallas_call(
        _peer_dense_kernel,
        grid=grid,
        in_specs=[
            pl.BlockSpec((PEER_TOKEN_TILE, D_MODEL), lambda ti, ei: (ti, 0)),
            pl.BlockSpec((PEER_EXPERT_TILE, D_MODEL), lambda ti, ei: (ei, 0)),
            pl.BlockSpec((D_MODEL, PEER_EXPERT_TILE), lambda ti, ei: (0, ei)),
            gate_spec, gate_spec, gate_spec, gate_spec,
        ],
        out_specs=pl.BlockSpec((D_MODEL, PEER_TOKEN_TILE), lambda ti, ei: (0, ti)),
        out_shape=jax.ShapeDtypeStruct((D_MODEL, t), F32),
        scratch_shapes=[pltpu.VMEM((PEER_EXPERT_TILE, PEER_TOKEN_TILE), F32),
                        pltpu.VMEM((PEER_EXPERT_TILE, PEER_TOKEN_TILE), BF16)],
        compiler_params=pltpu.CompilerParams(
            dimension_semantics=("parallel", "arbitrary"),
            vmem_limit_bytes=VMEM_LIMIT_BYTES),
        name="peer_dense",
    )(h_bf, u_bf, vt_bf, a_t, n_t, b_t, r_t)


def rmsnorm(x, g):
    return x * lax.rsqrt(jnp.mean(x * x, axis=-1, keepdims=True) + EPS) * g


def layernorm(x, g, b):
    mu = jnp.mean(x, axis=-1, keepdims=True)
    var = jnp.mean(jnp.square(x - mu), axis=-1, keepdims=True)
    return (x - mu) * lax.rsqrt(var + EPS) * g + b


def l2norm(x):
    return x * lax.rsqrt(jnp.sum(x * x, axis=-1, keepdims=True) + EPS)


def depthwise_conv(x, w):
    k, ch = w.shape
    return lax.conv_general_dilated(x, w[:, None, :], window_strides=(1,),
                                    padding=[(k // 2, k // 2)],
                                    dimension_numbers=('NWC', 'WIO', 'NWC'),
                                    feature_group_count=ch)


def grid_pos_emb(n_tokens):
    rows = n_tokens // GRID_W
    r = jnp.repeat(jnp.arange(rows, dtype=F32), GRID_W)
    col = jnp.tile(jnp.arange(GRID_W, dtype=F32), rows)
    nf = D_MODEL // 4
    freqs = jnp.exp(-math.log(POS_BASE) * jnp.arange(nf, dtype=F32) / nf)
    ar = r[:, None] * freqs
    ac = col[:, None] * freqs
    return jnp.concatenate([jnp.sin(ar), jnp.cos(ar), jnp.sin(ac), jnp.cos(ac)], axis=-1)


def gated_delta_chunked(q, k, v, log_g, beta, s0):
    b, s, h, dk = q.shape
    dv = v.shape[-1]
    n = s // CHUNK

    def to_blocks(t):
        t = t.astype(F32).reshape(b, n, CHUNK, h, *t.shape[3:])
        return jnp.moveaxis(t, 3, 1)

    qb, kb, vb = to_blocks(q), to_blocks(k), to_blocks(v)
    lg, bt = to_blocks(log_g), to_blocks(beta)
    gam = jnp.cumsum(lg, axis=-1)
    idx = jnp.arange(CHUNK)
    incl = idx[:, None] >= idx[None, :]
    strict = idx[:, None] > idx[None, :]
    diff = gam[..., :, None] - gam[..., None, :]
    decay = jnp.where(incl, jnp.exp(jnp.where(incl, diff, 0.0)), 0.0)
    kk = jnp.einsum('bhntd,bhnid->bhnti', kb, kb)
    a_mat = jnp.where(strict, bt[..., :, None] * decay * kk, 0.0) + jnp.eye(CHUNK, dtype=F32)
    rhs = jnp.concatenate([kb * (bt * jnp.exp(gam))[..., None], vb * bt[..., None]], axis=-1)
    sol = lax.linalg.triangular_solve(a_mat, rhs, left_side=True, lower=True, unit_diagonal=True)
    w_blk, u_blk = sol[..., :dk], sol[..., dk:]
    p_blk = decay * jnp.einsum('bhntd,bhnid->bhnti', qb, kb)
    q_dec = qb * jnp.exp(gam)[..., None]
    k_dec = kb * jnp.exp(gam[..., -1:] - gam)[..., None]
    g_last = jnp.exp(gam[..., -1])

    def step(state, xs):
        w_c, u_c, p_c, q_c, k_c, g_c = xs
        u = u_c - jnp.einsum('bhcd,bhde->bhce', w_c, state)
        o = jnp.einsum('bhcd,bhde->bhce', q_c, state) + jnp.einsum('bhti,bhie->bhte', p_c, u)
        state = g_c[..., None, None] * state + jnp.einsum('bhcd,bhce->bhde', k_c, u)
        return state, o

    xs = tuple(jnp.moveaxis(t, 2, 0) for t in (w_blk, u_blk, p_blk, q_dec, k_dec, g_last))
    s_fin, o = lax.scan(step, s0.astype(F32), xs)
    o = jnp.transpose(o, (1, 0, 3, 2, 4)).reshape(b, s, h, dv)
    return o, s_fin


def delta_mixer(qkv, z, alpha, beta_raw, conv_w, a_log, dt_bias, norm_g, s0):
    b, s, _ = qkv.shape
    qkv = jax.nn.silu(depthwise_conv(qkv, conv_w))
    q, k, v = jnp.split(qkv, 3, axis=-1)
    q = l2norm(q.reshape(b, s, A_HEADS, A_DK)) * (A_DK ** -0.5)
    k = l2norm(k.reshape(b, s, A_HEADS, A_DK))
    v = v.reshape(b, s, A_HEADS, A_DV)
    beta = jax.nn.sigmoid(beta_raw).reshape(b, s, 2, A_HEADS)
    alpha = alpha.reshape(b, s, 2, A_HEADS)
    log_g = -jnp.exp(a_log) * jax.nn.softplus(alpha + dt_bias)
    o_f, s_f = gated_delta_chunked(q, k, v, log_g[:, :, 0], beta[:, :, 0], s0[:, 0])
    fl = lambda t: jnp.flip(t, axis=1)
    o_b, s_b = gated_delta_chunked(fl(q), fl(k), fl(v), fl(log_g[:, :, 1]), fl(beta[:, :, 1]), s0[:, 1])
    o = o_f + fl(o_b)
    o = (o * lax.rsqrt(jnp.mean(o * o, axis=-1, keepdims=True) + EPS) * norm_g
         * jax.nn.silu(z.reshape(b, s, A_HEADS, A_DV)))
    return o.reshape(b, s, A_DIM), jnp.stack([s_f, s_b], axis=1)


def conformer_conv(u, dw_w, dw_b, ln_g, ln_b):
    a, gt = jnp.split(u, 2, axis=-1)
    hh = a * jax.nn.sigmoid(gt)
    hh = depthwise_conv(hh, dw_w) + dw_b
    return jax.nn.silu(layernorm(hh, ln_g, ln_b))


def fourier_mix(h):
    b, s, d = h.shape
    hg = h.reshape(b, s, C_GROUPS, d // C_GROUPS)
    f = jnp.fft.fft2(hg, axes=(1, 3), norm='ortho').real
    return f.reshape(b, s, d)


def peer_gating(x, wq, k1, k2):
    t = x.shape[0]
    q = (x @ wq).reshape(t, PEER_HEADS, 2, PEER_DK)
    s1 = jnp.einsum('thd,hnd->thn', q[:, :, 0], k1)
    s2 = jnp.einsum('thd,hnd->thn', q[:, :, 1], k2)
    v1, i1 = lax.top_k(s1, PEER_TOPK)
    v2, i2 = lax.top_k(s2, PEER_TOPK)
    cand = (v1[..., :, None] + v2[..., None, :]).reshape(t, PEER_HEADS, PEER_TOPK * PEER_TOPK)
    sv, si = lax.top_k(cand, PEER_TOPK)
    row = si // PEER_TOPK
    n_r = jnp.sum(row[..., None, :] == jnp.arange(PEER_TOPK)[:, None], axis=-1).astype(F32)
    z = jnp.sum(jnp.exp(sv - sv[..., :1]), axis=-1, keepdims=True)
    a = jnp.exp(s1 - v1[..., :1]) / z
    b = jnp.exp(s2 - v2[..., :1])
    keys = jnp.arange(PEER_KEYS)
    oh1 = (i1[..., None] == keys).astype(F32)
    oh2 = (i2[..., None] == keys).astype(F32)
    nn = jnp.einsum('thr,thri->thi', n_r, oh1)
    in2 = jnp.sum(oh2, axis=2)
    r2 = jnp.einsum('r,thri->thi', jnp.arange(PEER_TOPK, dtype=F32), oh2) + PEER_TOPK * (1.0 - in2)
    b = b * in2
    tr = lambda m: jnp.transpose(m, (1, 2, 0))
    return tr(a), tr(nn), tr(b), tr(r2)


def peer(h, wq, k1, k2, u_bf, vt_bf):
    b, s, d = h.shape
    x = h.reshape(b * s, d)
    a_t, n_t, b_t, r_t = peer_gating(x, wq, k1, k2)
    out_t = _peer_dense(x.astype(BF16), u_bf, vt_bf, a_t, n_t, b_t, r_t)
    return out_t.T.reshape(b, s, d)


def trunk(x, cond, s0, prm):
    states = []
    for l in range(DEPTH):
        mod = (jax.nn.silu(cond) @ prm['ada_w'][l] + prm['ada_b'][l])[:, None, :]
        sh1, sc1, g1, sh2, sc2, g2 = jnp.split(mod, 6, axis=-1)
        h = rmsnorm(x, prm['norm1_g'][l]) * (1 + sc1) + sh1
        e = l // 2
        if l % 2 == 0:
            p = h @ prm['w_in_ab'][e]
            o4 = 4 * A_DIM
            o_a, st = delta_mixer(p[..., :3 * A_DIM], p[..., 3 * A_DIM:o4],
                                  p[..., o4:o4 + 2 * A_HEADS], p[..., o4 + 2 * A_HEADS:o4 + 4 * A_HEADS],
                                  prm['conv_qkv_w'][e], prm['a_log'][e], prm['dt_bias'][e],
                                  prm['delta_norm_g'][e], s0[:, e])
            o_b = conformer_conv(p[..., o4 + 4 * A_HEADS:], prm['conf_dw_w'][e], prm['conf_dw_b'][e],
                                 prm['conf_ln_g'][e], prm['conf_ln_b'][e])
            mix = jnp.concatenate([o_a, o_b], axis=-1) @ prm['w_out_ab'][e]
            states.append(st)
        else:
            mix = fourier_mix(h) @ prm['w_out_c'][e]
        x = x + g1 * mix
        h = rmsnorm(x, prm['norm2_g'][l]) * (1 + sc2) + sh2
        x = x + g2 * peer(h, prm['peer_wq'][l], prm['peer_k1'][l], prm['peer_k2'][l],
                          prm['peer_u_bf'][l], prm['peer_vt_bf'][l])
    return rmsnorm(x, prm['final_norm_g']), jnp.stack(states, axis=1)


def kernel(x_prompt, x_sample, state_delta, c, c_ctx, ada_w, ada_b, norm1_g, norm2_g, w_in_ab, conv_qkv_w,
           a_log, dt_bias, delta_norm_g, conf_dw_w, conf_dw_b, conf_ln_g, conf_ln_b, w_out_ab, w_out_c,
           peer_wq, peer_k1, peer_k2, peer_u, peer_v, final_norm_g):
    prm = {'ada_w': ada_w, 'ada_b': ada_b, 'norm1_g': norm1_g, 'norm2_g': norm2_g,
           'w_in_ab': w_in_ab, 'conv_qkv_w': conv_qkv_w, 'a_log': a_log, 'dt_bias': dt_bias,
           'delta_norm_g': delta_norm_g, 'conf_dw_w': conf_dw_w, 'conf_dw_b': conf_dw_b,
           'conf_ln_g': conf_ln_g, 'conf_ln_b': conf_ln_b, 'w_out_ab': w_out_ab,
           'w_out_c': w_out_c, 'peer_wq': peer_wq, 'peer_k1': peer_k1, 'peer_k2': peer_k2,
           'peer_u_bf': peer_u.astype(BF16),
           'peer_vt_bf': jnp.transpose(peer_v.astype(BF16), (0, 2, 1)),
           'final_norm_g': final_norm_g}
    ne = (DEPTH + 1) // 2
    s0_ctx = jnp.zeros((x_prompt.shape[0], ne, 2, A_HEADS, A_DK, A_DV), x_prompt.dtype)
    y_prompt, ctx_states = trunk(x_prompt, c_ctx[None, :], s0_ctx, prm)
    xs = x_sample + grid_pos_emb(x_sample.shape[1])[None]
    y_sample, _ = trunk(xs, c, state_delta, prm)
    return (y_prompt, y_sample, ctx_states)
```

```python
import math

import jax
import jax.numpy as jnp
import numpy as np
from jax import lax
from jax.experimental import pallas as pl
from jax.experimental.pallas import tpu as pltpu

D_MODEL = 1024
DEPTH = 4
GRID_W = 64
POS_BASE = 10000.0
EPS = 1e-6
A_HEADS = 4
A_DK = 128
A_DV = 128
A_DIM = A_HEADS * A_DV
CHUNK = 64
B_CH = D_MODEL // 2
P_AB = 4 * A_DIM + 4 * A_HEADS + 2 * B_CH
C_GROUPS = 8
PEER_HEADS = 8
PEER_KEYS = 128
PEER_N = PEER_KEYS * PEER_KEYS
PEER_DK = 128
PEER_TOPK = 16

F32 = jnp.float32
BF16 = jnp.bfloat16
NEG_INF = float("-inf")

LANES = 128
VMEM_LIMIT_BYTES = 56 * 1024 * 1024
MM_ROW_TILE = 512
ROUTE_TOKEN_TILE = 256
PEER_TOKEN_TILE = 512
PEER_KEYS_PER_STEP = 8
PEER_EXPERT_TILE = PEER_KEYS_PER_STEP * PEER_KEYS
PEER_J_BLOCK = 32
PREP_CHUNKS = 2
SCAN_SEQS = 2
P_AB_PAD = 3200

NT_DIMS = (((1,), (1,)), ((), ()))
TN_DIMS = (((0,), (0,)), ((), ()))


def _bdot(a, b):
    return jnp.dot(a, b, preferred_element_type=F32)


def _split_bf16(a):
    hi = a.astype(BF16)
    lo = (a - hi.astype(F32)).astype(BF16)
    return hi, lo


def _dot3(a, b):
    ah, al = _split_bf16(a)
    bh, bl = _split_bf16(b)
    return _bdot(ah, bh) + (_bdot(ah, bl) + _bdot(al, bh))


def _dot_exact_lhs(a01, b):
    a = a01.astype(BF16)
    bh = b.astype(BF16)
    r1 = b - bh.astype(F32)
    bm = r1.astype(BF16)
    bl = (r1 - bm.astype(F32)).astype(BF16)
    return _bdot(a, bh) + (_bdot(a, bm) + _bdot(a, bl))


def _gelu_tanh(x):
    return 0.5 * x * (1.0 + jnp.tanh(math.sqrt(2.0 / math.pi) * (x + 0.044715 * (x * x * x))))


def _seg_spec(rows_per_seg, tile):
    per = rows_per_seg // tile
    return pl.BlockSpec((1, 1, D_MODEL), lambda i, *_: (i // per, 0, 0))


def _norm_modulate(x, g, sc, sh):
    hn = x * lax.rsqrt(jnp.mean(x * x, axis=-1, keepdims=True) + EPS) * g
    return hn * (1.0 + sc) + sh


def _norm_mm_kernel(x_ref, g_ref, sc_ref, sh_ref, w_ref, o_ref):
    h = _norm_modulate(x_ref[...], g_ref[...], sc_ref[0], sh_ref[0]).astype(BF16)
    o_ref[...] = _bdot(h, w_ref[...]).astype(o_ref.dtype)


def norm_mm(x, norm_g, sc, sh, w_bf, rows_per_seg, out_dtype):
    m = x.shape[0]
    n = w_bf.shape[1]
    tm = MM_ROW_TILE
    seg = _seg_spec(rows_per_seg, tm)
    return pl.pallas_call(
        _norm_mm_kernel, grid=(m // tm,),
        in_specs=[pl.BlockSpec((tm, D_MODEL), lambda i: (i, 0)), pl.BlockSpec((1, D_MODEL), lambda i: (0, 0)),
                  seg, seg, pl.BlockSpec((D_MODEL, n), lambda i: (0, 0))],
        out_specs=pl.BlockSpec((tm, n), lambda i: (i, 0)),
        out_shape=jax.ShapeDtypeStruct((m, n), out_dtype),
        compiler_params=pltpu.CompilerParams(dimension_semantics=("parallel",),
                                             vmem_limit_bytes=VMEM_LIMIT_BYTES),
        name="norm_mm",
    )(x, norm_g.reshape(1, D_MODEL), sc, sh, w_bf)


def _mm_res_kernel(a_ref, w_ref, x_ref, gate_ref, o_ref):
    o_ref[...] = x_ref[...] + gate_ref[0] * _bdot(a_ref[...].astype(BF16), w_ref[...])


def mm_res(a, w_bf, x, gate, rows_per_seg):
    m, k = a.shape
    tm = MM_ROW_TILE
    return pl.pallas_call(
        _mm_res_kernel, grid=(m // tm,),
        in_specs=[pl.BlockSpec((tm, k), lambda i: (i, 0)), pl.BlockSpec((k, D_MODEL), lambda i: (0, 0)),
                  pl.BlockSpec((tm, D_MODEL), lambda i: (i, 0)), _seg_spec(rows_per_seg, tm)],
        out_specs=pl.BlockSpec((tm, D_MODEL), lambda i: (i, 0)),
        out_shape=jax.ShapeDtypeStruct((m, D_MODEL), F32),
        compiler_params=pltpu.CompilerParams(dimension_semantics=("parallel",),
                                             vmem_limit_bytes=VMEM_LIMIT_BYTES),
        name="mm_res",
    )(a, w_bf, x, gate)


def _mm3_kernel(a_ref, b_ref, o_ref):
    o_ref[...] = _dot3(a_ref[...], b_ref[...]).astype(o_ref.dtype)


def mm3(a, b, out_dtype):
    return pl.pallas_call(_mm3_kernel, out_shape=jax.ShapeDtypeStruct((a.shape[0], b.shape[1]), out_dtype),
                          compiler_params=pltpu.CompilerParams(vmem_limit_bytes=VMEM_LIMIT_BYTES),
                          name="mm3")(a, b)


def _seqmix_kernel(f_ref, z_ref, x_ref, gate_ref, o_ref, acc_ref):
    k = pl.program_id(2)

    @pl.when(k == 0)
    def _():
        acc_ref[...] = jnp.zeros_like(acc_ref)

    acc_ref[...] += _bdot(f_ref[...], z_ref[...])

    @pl.when(k == pl.num_programs(2) - 1)
    def _():
        o_ref[...] = x_ref[...] + gate_ref[0] * acc_ref[...]


def seq_mix_res(fmat, z, x, gate, seq_len, tm, tk):
    t = x.shape[0]
    nseq = t // seq_len
    seqs_per_seg = nseq // gate.shape[0]
    mt = seq_len // tm
    kt_half = seq_len // tk
    return pl.pallas_call(
        _seqmix_kernel, grid=(nseq, mt, 2 * kt_half),
        in_specs=[pl.BlockSpec((tm, tk), lambda s, i, k: (i, k)),
                  pl.BlockSpec((tk, D_MODEL), lambda s, i, k: (s * kt_half + k % kt_half, k // kt_half)),
                  pl.BlockSpec((tm, D_MODEL), lambda s, i, k: (s * mt + i, 0)),
                  pl.BlockSpec((1, 1, D_MODEL), lambda s, i, k: (s // seqs_per_seg, 0, 0))],
        out_specs=pl.BlockSpec((tm, D_MODEL), lambda s, i, k: (s * mt + i, 0)),
        out_shape=jax.ShapeDtypeStruct((t, D_MODEL), F32),
        scratch_shapes=[pltpu.VMEM((tm, D_MODEL), F32)],
        compiler_params=pltpu.CompilerParams(dimension_semantics=("parallel", "parallel", "arbitrary"),
                                             vmem_limit_bytes=VMEM_LIMIT_BYTES),
        name="seq_mix_res",
    )(fmat, z, x, gate)


def _dft_tables(n, cols):
    r = jnp.arange(n, dtype=jnp.int32)[:, None]
    ang = ((r * cols[None, :]) % n).astype(F32) * (2.0 * math.pi / n)
    return jnp.cos(ang), jnp.sin(ang)


def dft_seq_matrix(s):
    w = 1 << (int(math.log2(s)) // 2)
    ch, sh_ = _dft_tables(s, jnp.arange(s // w, dtype=jnp.int32) * w)
    cl, sl = _dft_tables(s, jnp.arange(w, dtype=jnp.int32))
    sc = 1.0 / math.sqrt(s)
    c = (ch[:, :, None] * cl[:, None, :] - sh_[:, :, None] * sl[:, None, :]).reshape(s, s) * sc
    sn = (sh_[:, :, None] * cl[:, None, :] + ch[:, :, None] * sl[:, None, :]).reshape(s, s) * sc
    return jnp.concatenate([c, -sn], axis=1).astype(BF16)


def dft_group_matrices(n, groups):
    c, s = _dft_tables(n, jnp.arange(n, dtype=jnp.int32))
    sc = 1.0 / math.sqrt(n)
    eye = jnp.eye(groups, dtype=F32)
    return jnp.kron(eye, c * sc), jnp.kron(eye, s * sc)


def _delta_prep_kernel(q_ref, k_ref, v_ref, lg_ref, bt_ref, w_ref, u_ref, qd_ref, kd_ref, p_ref, g_ref):
    r = lax.broadcasted_iota(jnp.int32, (CHUNK, CHUNK), 0)
    c = lax.broadcasted_iota(jnp.int32, (CHUNK, CHUNK), 1)
    eye = (r == c).astype(F32)
    ones = jnp.ones((CHUNK, CHUNK), F32)
    incl = (r >= c, r <= c)
    strict = (r > c, r < c)
    tri = (incl[0].astype(F32), incl[1].astype(F32))
    tri_t = (tri[1], tri[0])
    last = (CHUNK - 1, 0)
    chains = [(cg, d, h) for cg in range(PREP_CHUNKS) for d in range(2) for h in range(A_HEADS)]
    rows = lambda cg: slice(cg * CHUNK, (cg + 1) * CHUNK)
    cols = lambda h: slice(h * A_DK, (h + 1) * A_DK)
    kk = {}
    qk = {}
    for cg in range(PREP_CHUNKS):
        for h in range(A_HEADS):
            kb = k_ref[0, rows(cg), cols(h)].astype(BF16)
            kk[cg, h] = lax.dot_general(kb, kb, NT_DIMS, preferred_element_type=F32)
            qk[cg, h] = lax.dot_general(q_ref[0, rows(cg), cols(h)].astype(BF16), kb, NT_DIMS,
                                        preferred_element_type=F32)
    lgw = {}
    btw = {}
    for (cg, d, h) in chains:
        col = d * A_HEADS + h
        lgw[cg, d, h] = jnp.broadcast_to(lg_ref[0, rows(cg), col:col + 1], (CHUNK, A_DK))
        btw[cg, d, h] = jnp.broadcast_to(bt_ref[0, rows(cg), col:col + 1], (CHUNK, A_DK))
    gam = {ch: _dot_exact_lhs(tri[ch[1]], lgw[ch]) for ch in chains}
    gam_row = {ch: _dot_exact_lhs(ones, lgw[ch][:, :CHUNK] * tri_t[ch[1]]) for ch in chains}
    decay = {}
    x = {}
    pinv = {}
    for ch in chains:
        cg, d, h = ch
        diff = gam[ch][:, :CHUNK] - gam_row[ch]
        decay[ch] = jnp.where(incl[d], jnp.exp(jnp.where(incl[d], diff, 0.0)), 0.0)
        x[ch] = jnp.where(strict[d], -(btw[ch][:, :CHUNK] * decay[ch] * kk[cg, h]), 0.0)
        pinv[ch] = eye + x[ch]
    for _ in range(int(math.log2(CHUNK)) - 1):
        x = {ch: _dot3(x[ch], x[ch]) for ch in chains}
        pinv = {ch: pinv[ch] + _dot3(pinv[ch], x[ch]) for ch in chains}
    for ch in chains:
        cg, d, h = ch
        kh = k_ref[0, rows(cg), cols(h)]
        vh = v_ref[0, rows(cg), cols(h)]
        qh = q_ref[0, rows(cg), cols(h)]
        egam = jnp.exp(gam[ch])
        rhs = jnp.concatenate([kh * (btw[ch] * egam), vh * btw[ch]], axis=1)
        sol = _dot3(pinv[ch], rhs)
        w_ref[0, d, rows(cg), cols(h)] = sol[:, :A_DK].astype(BF16)
        u_ref[0, d, rows(cg), cols(h)] = sol[:, A_DK:]
        qd_ref[0, d, rows(cg), cols(h)] = (qh * egam).astype(BF16)
        glast = jnp.broadcast_to(gam[ch][last[d]:last[d] + 1, :], (CHUNK, A_DK))
        kd_ref[0, d, rows(cg), cols(h)] = (kh * jnp.exp(glast - gam[ch])).astype(BF16)
        g_ref[0, d, rows(cg), cols(h)] = jnp.exp(glast)
        p_ref[0, d, rows(cg), h * CHUNK:(h + 1) * CHUNK] = (decay[ch] * qk[cg, h]).astype(BF16)


def delta_prep(q, k, v, lg, bt):
    b, s, _ = q.shape
    rt = PREP_CHUNKS * CHUNK
    blk = lambda w: pl.BlockSpec((1, rt, w), lambda bi, ni: (bi, ni, 0))
    oblk = lambda w: pl.BlockSpec((1, 2, rt, w), lambda bi, ni: (bi, 0, ni, 0))
    sh = lambda w, dt: jax.ShapeDtypeStruct((b, 2, s, w), dt)
    return pl.pallas_call(
        _delta_prep_kernel, grid=(b, s // rt),
        in_specs=[blk(A_DIM), blk(A_DIM), blk(A_DIM), blk(2 * A_HEADS), blk(2 * A_HEADS)],
        out_specs=[oblk(A_DIM), oblk(A_DIM), oblk(A_DIM), oblk(A_DIM), oblk(A_HEADS * CHUNK), oblk(A_DIM)],
        out_shape=[sh(A_DIM, BF16), sh(A_DIM, F32), sh(A_DIM, BF16), sh(A_DIM, BF16),
                   sh(A_HEADS * CHUNK, BF16), sh(A_DIM, F32)],
        compiler_params=pltpu.CompilerParams(dimension_semantics=("parallel", "parallel")),
        name="delta_prep",
    )(q, k, v, lg, bt)


def _delta_scan_kernel(*refs):
    ins = refs[:12]
    s0_ref = refs[12]
    of_ref, ob_ref, sout_ref, state = refs[13:]
    n = pl.program_id(1)

    @pl.when(n == 0)
    def _():
        state[...] = s0_ref[...]

    outs = (of_ref, ob_ref)
    chains = [(g, d, h) for g in range(SCAN_SEQS) for d in range(2) for h in range(A_HEADS)]
    cs = lambda h: slice(h * A_DK, (h + 1) * A_DK)
    ref = lambda d, i: ins[d * 6 + i]
    s_old = {ch: state[ch] for ch in chains}
    wqs = {}
    for (g, d, h) in chains:
        wq = jnp.concatenate([ref(d, 0)[g, 0, :, cs(h)], ref(d, 2)[g, 0, :, cs(h)]], axis=0)
        wqs[g, d, h] = _bdot(wq, s_old[g, d, h].astype(BF16))
    unb = {}
    for (g, d, h) in chains:
        unb[g, d, h] = (ref(d, 1)[g, 0, :, cs(h)] - wqs[g, d, h][:CHUNK]).astype(BF16)
    for (g, d, h) in chains:
        o = wqs[g, d, h][CHUNK:] + _bdot(ref(d, 4)[g, 0, :, h * CHUNK:(h + 1) * CHUNK], unb[g, d, h])
        outs[d][g, :, cs(h)] = o
    for (g, d, h) in chains:
        upd = lax.dot_general(ref(d, 3)[g, 0, :, cs(h)], unb[g, d, h], TN_DIMS, preferred_element_type=F32)
        gs = jnp.broadcast_to(ref(d, 5)[g, 0, 0:1, cs(h)], (A_DK, A_DV))
        state[g, d, h] = gs * s_old[g, d, h] + upd

    @pl.when(n == pl.num_programs(1) - 1)
    def _():
        sout_ref[...] = state[...]


def delta_scan(w, u, qd, kd, p, gl, s0):
    b, _, s, _ = u.shape
    n = s // CHUNK

    def spec(wd, d):
        if d == 0:
            return pl.BlockSpec((SCAN_SEQS, 1, CHUNK, wd), lambda bi, ni: (bi, 0, ni, 0))
        return pl.BlockSpec((SCAN_SEQS, 1, CHUNK, wd), lambda bi, ni: (bi, 1, n - 1 - ni, 0))

    arrs = (w, u, qd, kd, p, gl)
    in_specs = [spec(a.shape[-1], d) for d in range(2) for a in arrs]
    st = pl.BlockSpec((SCAN_SEQS, 2, A_HEADS, A_DK, A_DV), lambda bi, ni: (bi, 0, 0, 0, 0))
    of = pl.BlockSpec((SCAN_SEQS, CHUNK, A_DIM), lambda bi, ni: (bi, ni, 0))
    ob = pl.BlockSpec((SCAN_SEQS, CHUNK, A_DIM), lambda bi, ni: (bi, n - 1 - ni, 0))
    return pl.pallas_call(
        _delta_scan_kernel, grid=(b // SCAN_SEQS, n),
        in_specs=in_specs + [st],
        out_specs=[of, ob, st],
        out_shape=[jax.ShapeDtypeStruct((b, s, A_DIM), F32), jax.ShapeDtypeStruct((b, s, A_DIM), F32),
                   jax.ShapeDtypeStruct((b, 2, A_HEADS, A_DK, A_DV), F32)],
        scratch_shapes=[pltpu.VMEM((SCAN_SEQS, 2, A_HEADS, A_DK, A_DV), F32)],
        compiler_params=pltpu.CompilerParams(dimension_semantics=("parallel", "arbitrary")),
        name="delta_scan",
    )(*(arrs + arrs), s0)


def _cand_tables():
    pairs = [(r, c) for r in range(PEER_TOPK) for c in range(PEER_TOPK) if (r + 1) * (c + 1) <= PEER_TOPK]
    npad = 64
    e1 = np.zeros((npad, PEER_TOPK), np.float32)
    e2 = np.zeros((npad, PEER_TOPK), np.float32)
    m = np.zeros((PEER_TOPK, npad), np.float32)
    for k, (r, c) in enumerate(pairs):
        e1[k, r] = 1
        e2[k, c] = 1
        m[r, k] = 1
    return len(pairs), e1, e2, m


N_CAND, _CAND_E1, _CAND_E2, _CAND_ROW = _cand_tables()


def _extract_topk(s, n_iter):
    work = s
    rank = jnp.full(s.shape, float(n_iter), F32)
    vals = []
    for r in range(n_iter):
        m = jnp.max(work, axis=0, keepdims=True)
        hit = work == m
        rank = jnp.where(hit, float(r), rank)
        work = jnp.where(hit, NEG_INF, work)
        vals.append(m)
    return vals, rank


def _route_kernel(x_ref, g_ref, sc_ref, sh_ref, wq_ref, k1_ref, k2_ref, e1_ref, e2_ref, mrow_ref,
                  h_ref, a_ref, n_ref, b_ref, r_ref, q_scr):
    hb = _norm_modulate(x_ref[...], g_ref[...], sc_ref[0], sh_ref[0]).astype(BF16)
    h_ref[...] = hb
    q_scr[...] = _bdot(hb, wq_ref[...])

    def head(h, carry):
        c1 = pl.ds(pl.multiple_of(h * 2 * PEER_DK, PEER_DK), PEER_DK)
        c2 = pl.ds(pl.multiple_of(h * 2 * PEER_DK + PEER_DK, PEER_DK), PEER_DK)
        hp = lax.Precision.HIGHEST
        s1 = lax.dot_general(k1_ref[h], q_scr[:, c1], NT_DIMS, precision=hp, preferred_element_type=F32)
        s2 = lax.dot_general(k2_ref[h], q_scr[:, c2], NT_DIMS, precision=hp, preferred_element_type=F32)
        v1, rank1 = _extract_topk(s1, PEER_TOPK)
        v2, rank2 = _extract_topk(s2, PEER_TOPK)
        v1m = jnp.concatenate(v1, axis=0)
        v2m = jnp.concatenate(v2, axis=0)
        cand = (jnp.dot(e1_ref[...], v1m, precision=hp, preferred_element_type=F32)
                + jnp.dot(e2_ref[...], v2m, precision=hp, preferred_element_type=F32))
        row = lax.broadcasted_iota(jnp.int32, cand.shape, 0)
        cand = jnp.where(row < N_CAND, cand, NEG_INF)
        _, crank = _extract_topk(cand, PEER_TOPK)
        sel = crank < float(PEER_TOPK)
        cmax = v1[0] + v2[0]
        z = jnp.sum(jnp.where(sel, jnp.exp(cand - cmax), 0.0), axis=0, keepdims=True)
        n_r = _bdot(mrow_ref[...], jnp.where(sel, 1.0, 0.0).astype(BF16))
        nn = jnp.zeros_like(s1)
        for r in range(PEER_TOPK):
            nn = jnp.where(rank1 == float(r), n_r[r:r + 1, :], nn)
        a_ref[h] = jnp.exp(s1 - v1[0]) / z
        n_ref[h] = nn
        b_ref[h] = jnp.where(rank2 < float(PEER_TOPK), jnp.exp(s2 - v2[0]), 0.0)
        r_ref[h] = rank2
        return carry

    lax.fori_loop(0, PEER_HEADS, head, 0)


def peer_route(x, norm_g, sc, sh, wq_bf, k1, k2, rows_per_seg):
    t = x.shape[0]
    tt = ROUTE_TOKEN_TILE
    gate = pl.BlockSpec((PEER_HEADS, PEER_KEYS, tt), lambda i: (0, 0, i))
    full = lambda shp: pl.BlockSpec(shp, lambda i: (0,) * len(shp))
    seg = _seg_spec(rows_per_seg, tt)
    return pl.pallas_call(
        _route_kernel, grid=(t // tt,),
        in_specs=[pl.BlockSpec((tt, D_MODEL), lambda i: (i, 0)), full((1, D_MODEL)), seg, seg,
                  full((D_MODEL, 2 * PEER_HEADS * PEER_DK)), full((PEER_HEADS, PEER_KEYS, PEER_DK)),
                  full((PEER_HEADS, PEER_KEYS, PEER_DK)), full((64, PEER_TOPK)), full((64, PEER_TOPK)),
                  full((PEER_TOPK, 64))],
        out_specs=[pl.BlockSpec((tt, D_MODEL), lambda i: (i, 0)), gate, gate, gate, gate],
        out_shape=[jax.ShapeDtypeStruct((t, D_MODEL), BF16)]
        + [jax.ShapeDtypeStruct((PEER_HEADS, PEER_KEYS, t), F32)] * 4,
        scratch_shapes=[pltpu.VMEM((tt, 2 * PEER_HEADS * PEER_DK), F32)],
        compiler_params=pltpu.CompilerParams(dimension_semantics=("parallel",),
                                             vmem_limit_bytes=VMEM_LIMIT_BYTES),
        name="peer_route",
    )(x, norm_g.reshape(1, D_MODEL), sc, sh, wq_bf, k1, k2,
      jnp.asarray(_CAND_E1), jnp.asarray(_CAND_E2), jnp.asarray(_CAND_ROW, BF16))


def _peer_dense_kernel(h_ref, u_ref, vt_ref, a_ref, n_ref, b_ref, r_ref, x_ref, g2_ref, o_ref,
                       acc_ref, hid_ref, w_ref):
    e = pl.program_id(1)

    @pl.when(e == 0)
    def _():
        acc_ref[...] = jnp.zeros_like(acc_ref)

    half = PEER_TOKEN_TILE // 2
    halves = [slice(0, half), slice(half, PEER_TOKEN_TILE)]
    for hs in halves:
        hid_ref[:, hs] = lax.dot_general(u_ref[...], h_ref[hs, :], NT_DIMS, preferred_element_type=F32)
    i0 = pl.multiple_of(e * PEER_KEYS_PER_STEP, PEER_KEYS_PER_STEP)
    for hi, hs in enumerate(halves):
        for c in range(half // LANES):
            ls = slice(hi * half + c * LANES, hi * half + (c + 1) * LANES)
            for jb in range(PEER_KEYS // PEER_J_BLOCK):
                js = slice(jb * PEER_J_BLOCK, (jb + 1) * PEER_J_BLOCK)
                accs = [jnp.zeros((PEER_J_BLOCK, LANES), F32) for _ in range(PEER_KEYS_PER_STEP)]
                for h in range(PEER_HEADS):
                    b = b_ref[h, js, ls]
                    r = r_ref[h, js, ls]
                    a_rows = a_ref[h, pl.ds(i0, PEER_KEYS_PER_STEP), ls]
                    n_rows = n_ref[h, pl.ds(i0, PEER_KEYS_PER_STEP), ls]
                    for ii in range(PEER_KEYS_PER_STEP):
                        accs[ii] = accs[ii] + jnp.where(r < n_rows[ii:ii + 1, :], b, 0.0) * a_rows[ii:ii + 1, :]
                for ii in range(PEER_KEYS_PER_STEP):
                    rows = slice(ii * PEER_KEYS + jb * PEER_J_BLOCK, ii * PEER_KEYS + (jb + 1) * PEER_J_BLOCK)
                    w_ref[rows, ls] = (accs[ii] * _gelu_tanh(hid_ref[rows, ls])).astype(BF16)
        acc_ref[:, hs] += _bdot(vt_ref[...], w_ref[:, hs])

    @pl.when(e == pl.num_programs(1) - 1)
    def _():
        o_ref[...] = x_ref[...] + g2_ref[0] * acc_ref[...].T


def peer_dense(h_bf, u_bf, vt_bf, a_t, n_t, b_t, r_t, x, g2, rows_per_seg):
    t = h_bf.shape[0]
    tt = PEER_TOKEN_TILE
    grid = (t // tt, PEER_N // PEER_EXPERT_TILE)
    gate_spec = pl.BlockSpec((PEER_HEADS, PEER_KEYS, tt), lambda ti, ei: (0, 0, ti))
    return pl.pallas_call(
        _peer_dense_kernel, grid=grid,
        in_specs=[pl.BlockSpec((tt, D_MODEL), lambda ti, ei: (ti, 0)),
                  pl.BlockSpec((PEER_EXPERT_TILE, D_MODEL), lambda ti, ei: (ei, 0)),
                  pl.BlockSpec((D_MODEL, PEER_EXPERT_TILE), lambda ti, ei: (0, ei)),
                  gate_spec, gate_spec, gate_spec, gate_spec,
                  pl.BlockSpec((tt, D_MODEL), lambda ti, ei: (ti, 0)),
                  _seg_spec(rows_per_seg, tt)],
        out_specs=pl.BlockSpec((tt, D_MODEL), lambda ti, ei: (ti, 0)),
        out_shape=jax.ShapeDtypeStruct((t, D_MODEL), F32),
        scratch_shapes=[pltpu.VMEM((D_MODEL, tt), F32),
                        pltpu.VMEM((PEER_EXPERT_TILE, tt), F32),
                        pltpu.VMEM((PEER_EXPERT_TILE, tt), BF16)],
        compiler_params=pltpu.CompilerParams(dimension_semantics=("parallel", "arbitrary"),
                                             vmem_limit_bytes=VMEM_LIMIT_BYTES),
        name="peer_dense",
    )(h_bf, u_bf, vt_bf, a_t, n_t, b_t, r_t, x, g2)


def depthwise_conv(x, w):
    k, ch = w.shape
    return lax.conv_general_dilated(x, w[:, None, :], window_strides=(1,),
                                    padding=[(k // 2, k // 2)],
                                    dimension_numbers=('NWC', 'WIO', 'NWC'),
                                    feature_group_count=ch)


def grid_pos_emb(n_tokens):
    rows = n_tokens // GRID_W
    r = jnp.repeat(jnp.arange(rows, dtype=F32), GRID_W)
    col = jnp.tile(jnp.arange(GRID_W, dtype=F32), rows)
    nf = D_MODEL // 4
    freqs = jnp.exp(-math.log(POS_BASE) * jnp.arange(nf, dtype=F32) / nf)
    ar = r[:, None] * freqs
    ac = col[:, None] * freqs
    return jnp.concatenate([jnp.sin(ar), jnp.cos(ar), jnp.sin(ac), jnp.cos(ac)], axis=-1)


def l2norm(x):
    return x * lax.rsqrt(jnp.sum(x * x, axis=-1, keepdims=True) + EPS)


def delta_conformer_mix(p, nseq, seq_len, s0, e, prm):
    t = p.shape[0]
    o4 = 4 * A_DIM
    qkv = p[:, :3 * A_DIM].reshape(nseq, seq_len, 3 * A_DIM)
    z = p[:, 3 * A_DIM:o4]
    alpha = p[:, o4:o4 + 2 * A_HEADS]
    beta_raw = p[:, o4 + 2 * A_HEADS:o4 + 4 * A_HEADS]
    glu = p[:, o4 + 4 * A_HEADS:P_AB].reshape(nseq, seq_len, 2 * B_CH)

    qkv = jax.nn.silu(depthwise_conv(qkv, prm['conv_qkv_w'][e]))
    q, k, v = jnp.split(qkv, 3, axis=-1)
    hd = lambda m: m.reshape(nseq, seq_len, A_HEADS, A_DK)
    q = (l2norm(hd(q)) * (A_DK ** -0.5)).reshape(nseq, seq_len, A_DIM)
    k = l2norm(hd(k)).reshape(nseq, seq_len, A_DIM)
    beta = jax.nn.sigmoid(beta_raw).reshape(nseq, seq_len, 2 * A_HEADS)
    log_g = (-jnp.exp(prm['a_log'][e]).reshape(1, 2 * A_HEADS)
             * jax.nn.softplus(alpha + prm['dt_bias'][e].reshape(1, 2 * A_HEADS)))
    log_g = log_g.reshape(nseq, seq_len, 2 * A_HEADS)
    o_f, o_b, st = delta_scan(*delta_prep(q, k, v, log_g, beta), s0)
    o = (o_f + o_b).reshape(t, A_HEADS, A_DV)
    o = (o * lax.rsqrt(jnp.mean(o * o, axis=-1, keepdims=True) + EPS) * prm['delta_norm_g'][e]
         * jax.nn.silu(z.reshape(t, A_HEADS, A_DV)))

    a, gt = jnp.split(glu, 2, axis=-1)
    hh = depthwise_conv(a * jax.nn.sigmoid(gt), prm['conf_dw_w'][e]) + prm['conf_dw_b'][e]
    mu = jnp.mean(hh, axis=-1, keepdims=True)
    var = jnp.mean(jnp.square(hh - mu), axis=-1, keepdims=True)
    hh = jax.nn.silu((hh - mu) * lax.rsqrt(var + EPS) * prm['conf_ln_g'][e] + prm['conf_ln_b'][e])
    return jnp.concatenate([o.reshape(t, A_DIM), hh.reshape(t, B_CH)], axis=-1), st


def trunk(x, nseq, seq_len, cond, s0, prm):
    t = x.shape[0]
    rows_per_seg = t // cond.shape[0]
    states = []
    for l in range(DEPTH):
        mod = jax.nn.silu(cond) @ prm['ada_w'][l] + prm['ada_b'][l]
        sh1, sc1, g1, sh2, sc2, g2 = [m[:, None, :] for m in jnp.split(mod, 6, axis=-1)]
        e = l // 2
        if l % 2 == 0:
            p = norm_mm(x, prm['norm1_g'][l], sc1, sh1, prm['w_in_bf'][e], rows_per_seg, F32)
            feats, st = delta_conformer_mix(p, nseq, seq_len, s0[:, e], e, prm)
            x = mm_res(feats, prm['w_out_bf'][e], x, g1, rows_per_seg)
            states.append(st)
        else:
            z = norm_mm(x, prm['norm1_g'][l], sc1, sh1, prm['w_fnet_bf'][e], rows_per_seg, BF16)
            tm = min(seq_len, 512)
            x = seq_mix_res(prm['dft_seq'][seq_len], z, x, g1, seq_len, tm, min(seq_len, 1024))
        hb, a_t, n_t, b_t, r_t = peer_route(x, prm['norm2_g'][l], sc2, sh2, prm['peer_wq_bf'][l],
                                            prm['peer_k1'][l], prm['peer_k2'][l], rows_per_seg)
        x = peer_dense(hb, prm['peer_u_bf'][l], prm['peer_vt_bf'][l], a_t, n_t, b_t, r_t, x, g2, rows_per_seg)
    xf = x * lax.rsqrt(jnp.mean(x * x, axis=-1, keepdims=True) + EPS) * prm['final_norm_g']
    return xf, jnp.stack(states, axis=1)


def kernel(x_prompt, x_sample, state_delta, c, c_ctx, ada_w, ada_b, norm1_g, norm2_g, w_in_ab, conv_qkv_w,
           a_log, dt_bias, delta_norm_g, conf_dw_w, conf_dw_b, conf_ln_g, conf_ln_b, w_out_ab, w_out_c,
           peer_wq, peer_k1, peer_k2, peer_u, peer_v, final_norm_g):
    bp, sp, _ = x_prompt.shape
    bs, ss, _ = x_sample.shape
    bdc, bds = dft_group_matrices(D_MODEL // C_GROUPS, C_GROUPS)
    w_fnet = [jnp.concatenate([mm3(bdc, w_out_c[e], BF16), mm3(bds, w_out_c[e], BF16)], axis=1)
              for e in range(DEPTH // 2)]
    prm = {'ada_w': ada_w, 'ada_b': ada_b, 'norm1_g': norm1_g, 'norm2_g': norm2_g,
           'w_in_bf': jnp.pad(w_in_ab.astype(BF16), ((0, 0), (0, 0), (0, P_AB_PAD - P_AB))),
           'conv_qkv_w': conv_qkv_w, 'a_log': a_log, 'dt_bias': dt_bias,
           'delta_norm_g': delta_norm_g, 'conf_dw_w': conf_dw_w, 'conf_dw_b': conf_dw_b,
           'conf_ln_g': conf_ln_g, 'conf_ln_b': conf_ln_b, 'w_out_bf': w_out_ab.astype(BF16),
           'w_fnet_bf': w_fnet, 'dft_seq': {s: dft_seq_matrix(s) for s in {sp, ss}},
           'peer_wq_bf': peer_wq.astype(BF16), 'peer_k1': peer_k1, 'peer_k2': peer_k2,
           'peer_u_bf': peer_u.astype(BF16),
           'peer_vt_bf': jnp.transpose(peer_v.astype(BF16), (0, 2, 1)),
           'final_norm_g': final_norm_g}
    ne = (DEPTH + 1) // 2
    s0_ctx = jnp.zeros((bp, ne, 2, A_HEADS, A_DK, A_DV), F32)
    y_prompt, ctx_states = trunk(x_prompt.reshape(bp * sp, D_MODEL), bp, sp, c_ctx[None, :], s0_ctx, prm)
    xs = (x_sample + grid_pos_emb(ss)[None]).reshape(bs * ss, D_MODEL)
    y_sample, _ = trunk(xs, bs, ss, c, state_delta, prm)
    return (y_prompt.reshape(bp, sp, D_MODEL), y_sample.reshape(bs, ss, D_MODEL), ctx_states)
```

```python
import math

import jax
import jax.numpy as jnp
import numpy as np
from jax import lax
from jax.experimental import pallas as pl
from jax.experimental.pallas import tpu as pltpu

D_MODEL = 1024
DEPTH = 4
GRID_W = 64
POS_BASE = 10000.0
EPS = 1e-6
A_HEADS = 4
A_DK = 128
A_DV = 128
A_DIM = A_HEADS * A_DV
CHUNK = 64
B_CH = D_MODEL // 2
P_AB = 4 * A_DIM + 4 * A_HEADS + 2 * B_CH
C_GROUPS = 8
PEER_HEADS = 8
PEER_KEYS = 128
PEER_N = PEER_KEYS * PEER_KEYS
PEER_DK = 128
PEER_TOPK = 16

F32 = jnp.float32
BF16 = jnp.bfloat16
NEG_INF = float("-inf")

LANES = 128
VMEM_LIMIT_BYTES = 56 * 1024 * 1024
MM_ROW_TILE = 512
ROUTE_TOKEN_TILE = 256
PEER_TOKEN_TILE = 512
PEER_KEYS_PER_STEP = 8
PEER_EXPERT_TILE = PEER_KEYS_PER_STEP * PEER_KEYS
PEER_J_BLOCK = 32
PREP_CHUNKS = 2
SCAN_SEQS = 2
P_AB_PAD = 3200
COL_Z = 3 * A_DIM
COL_GLU = 4 * A_DIM
COL_AB = 4 * A_DIM + 2 * B_CH
CONV_ROW_TILE = 256
SHORT_CONV = 7
B_CONV = 31
CONF_HALO = 16
QKV_HALO = 8
PEER_PIECES = 4
PIECE_KEYS = PEER_KEYS_PER_STEP // PEER_PIECES
PIECE_ROWS = PIECE_KEYS * PEER_KEYS
BF16_ROWS = 16

NT_DIMS = (((1,), (1,)), ((), ()))
TN_DIMS = (((0,), (0,)), ((), ()))


def _bdot(a, b):
    return jnp.dot(a, b, preferred_element_type=F32)


def _split_bf16(a):
    hi = a.astype(BF16)
    lo = (a - hi.astype(F32)).astype(BF16)
    return hi, lo


def _dot3(a, b):
    ah, al = _split_bf16(a)
    bh, bl = _split_bf16(b)
    return _bdot(ah, bh) + (_bdot(ah, bl) + _bdot(al, bh))


def _dot_exact_lhs(a01, b):
    a = a01.astype(BF16)
    bh = b.astype(BF16)
    r1 = b - bh.astype(F32)
    bm = r1.astype(BF16)
    bl = (r1 - bm.astype(F32)).astype(BF16)
    return _bdot(a, bh) + (_bdot(a, bm) + _bdot(a, bl))


def _gelu_tanh(x):
    return 0.5 * x * (1.0 + jnp.tanh(math.sqrt(2.0 / math.pi) * (x + 0.044715 * (x * x * x))))


def _seg_spec(rows_per_seg, tile):
    per = rows_per_seg // tile
    return pl.BlockSpec((1, 1, D_MODEL), lambda i, *_: (i // per, 0, 0))


def _norm_modulate(x, g, sc, sh):
    hn = x * lax.rsqrt(jnp.mean(x * x, axis=-1, keepdims=True) + EPS) * g
    return hn * (1.0 + sc) + sh


def _norm_mm_kernel(x_ref, g_ref, sc_ref, sh_ref, w_ref, o_ref):
    h = _norm_modulate(x_ref[...], g_ref[...], sc_ref[0], sh_ref[0]).astype(BF16)
    o_ref[...] = _bdot(h, w_ref[...]).astype(o_ref.dtype)


def norm_mm(x, norm_g, sc, sh, w_bf, rows_per_seg, out_dtype):
    m = x.shape[0]
    n = w_bf.shape[1]
    tm = MM_ROW_TILE
    seg = _seg_spec(rows_per_seg, tm)
    return pl.pallas_call(
        _norm_mm_kernel, grid=(m // tm,),
        in_specs=[pl.BlockSpec((tm, D_MODEL), lambda i: (i, 0)), pl.BlockSpec((1, D_MODEL), lambda i: (0, 0)),
                  seg, seg, pl.BlockSpec((D_MODEL, n), lambda i: (0, 0))],
        out_specs=pl.BlockSpec((tm, n), lambda i: (i, 0)),
        out_shape=jax.ShapeDtypeStruct((m, n), out_dtype),
        compiler_params=pltpu.CompilerParams(dimension_semantics=("parallel",),
                                             vmem_limit_bytes=VMEM_LIMIT_BYTES),
        name="norm_mm",
    )(x, norm_g.reshape(1, D_MODEL), sc, sh, w_bf)


def _split_kernel(a_ref, hi_ref, lo_ref):
    hi, lo = _split_bf16(a_ref[0])
    hi_ref[0] = hi
    lo_ref[0] = lo


def split_bf16(a):
    nl, nr, nc = a.shape
    blk = pl.BlockSpec((1, MM_ROW_TILE, nc), lambda l, i: (l, i, 0))
    return pl.pallas_call(
        _split_kernel, grid=(nl, nr // MM_ROW_TILE), in_specs=[blk], out_specs=[blk, blk],
        out_shape=[jax.ShapeDtypeStruct(a.shape, BF16)] * 2,
        compiler_params=pltpu.CompilerParams(dimension_semantics=("parallel", "parallel")),
        name="split_bf16",
    )(a)


def _mm3_kernel(a_ref, b_ref, o_ref):
    o_ref[...] = _dot3(a_ref[...], b_ref[...]).astype(o_ref.dtype)


def mm3(a, b, out_dtype):
    return pl.pallas_call(_mm3_kernel, out_shape=jax.ShapeDtypeStruct((a.shape[0], b.shape[1]), out_dtype),
                          compiler_params=pltpu.CompilerParams(vmem_limit_bytes=VMEM_LIMIT_BYTES),
                          name="mm3")(a, b)


def _seqmix_kernel(f_ref, z_ref, x_ref, gate_ref, o_ref, acc_ref):
    k = pl.program_id(2)

    @pl.when(k == 0)
    def _():
        acc_ref[...] = jnp.zeros_like(acc_ref)

    acc_ref[...] += _bdot(f_ref[...], z_ref[...])

    @pl.when(k == pl.num_programs(2) - 1)
    def _():
        o_ref[...] = x_ref[...] + gate_ref[0] * acc_ref[...]


def seq_mix_res(fmat, z, x, gate, seq_len, tm, tk):
    t = x.shape[0]
    nseq = t // seq_len
    seqs_per_seg = nseq // gate.shape[0]
    mt = seq_len // tm
    kt_half = seq_len // tk
    return pl.pallas_call(
        _seqmix_kernel, grid=(nseq, mt, 2 * kt_half),
        in_specs=[pl.BlockSpec((tm, tk), lambda s, i, k: (i, k)),
                  pl.BlockSpec((tk, D_MODEL), lambda s, i, k: (s * kt_half + k % kt_half, k // kt_half)),
                  pl.BlockSpec((tm, D_MODEL), lambda s, i, k: (s * mt + i, 0)),
                  pl.BlockSpec((1, 1, D_MODEL), lambda s, i, k: (s // seqs_per_seg, 0, 0))],
        out_specs=pl.BlockSpec((tm, D_MODEL), lambda s, i, k: (s * mt + i, 0)),
        out_shape=jax.ShapeDtypeStruct((t, D_MODEL), F32),
        scratch_shapes=[pltpu.VMEM((tm, D_MODEL), F32)],
        compiler_params=pltpu.CompilerParams(dimension_semantics=("parallel", "parallel", "arbitrary"),
                                             vmem_limit_bytes=VMEM_LIMIT_BYTES),
        name="seq_mix_res",
    )(fmat, z, x, gate)


def _dft_tables(n, cols):
    r = jnp.arange(n, dtype=jnp.int32)[:, None]
    ang = ((r * cols[None, :]) % n).astype(F32) * (2.0 * math.pi / n)
    return jnp.cos(ang), jnp.sin(ang)


def dft_seq_matrix(s):
    w = 1 << (int(math.log2(s)) // 2)
    ch, sh_ = _dft_tables(s, jnp.arange(s // w, dtype=jnp.int32) * w)
    cl, sl = _dft_tables(s, jnp.arange(w, dtype=jnp.int32))
    sc = 1.0 / math.sqrt(s)
    c = (ch[:, :, None] * cl[:, None, :] - sh_[:, :, None] * sl[:, None, :]).reshape(s, s) * sc
    sn = (sh_[:, :, None] * cl[:, None, :] + ch[:, :, None] * sl[:, None, :]).reshape(s, s) * sc
    return jnp.concatenate([c, -sn], axis=1).astype(BF16)


def dft_group_matrices(n, groups):
    c, s = _dft_tables(n, jnp.arange(n, dtype=jnp.int32))
    sc = 1.0 / math.sqrt(n)
    eye = jnp.eye(groups, dtype=F32)
    return jnp.kron(eye, c * sc), jnp.kron(eye, s * sc)


def _delta_prep_kernel(q_ref, k_ref, v_ref, lg_ref, bt_ref, w_ref, u_ref, qd_ref, kd_ref, p_ref, g_ref):
    r = lax.broadcasted_iota(jnp.int32, (CHUNK, CHUNK), 0)
    c = lax.broadcasted_iota(jnp.int32, (CHUNK, CHUNK), 1)
    eye = (r == c).astype(F32)
    ones = jnp.ones((CHUNK, CHUNK), F32)
    incl = (r >= c, r <= c)
    strict = (r > c, r < c)
    tri = (incl[0].astype(F32), incl[1].astype(F32))
    tri_t = (tri[1], tri[0])
    last = (CHUNK - 1, 0)
    chains = [(cg, d, h) for cg in range(PREP_CHUNKS) for d in range(2) for h in range(A_HEADS)]
    rows = lambda cg: slice(cg * CHUNK, (cg + 1) * CHUNK)
    cols = lambda h: slice(h * A_DK, (h + 1) * A_DK)
    kk = {}
    qk = {}
    for cg in range(PREP_CHUNKS):
        for h in range(A_HEADS):
            kb = k_ref[0, rows(cg), cols(h)].astype(BF16)
            kk[cg, h] = lax.dot_general(kb, kb, NT_DIMS, preferred_element_type=F32)
            qk[cg, h] = lax.dot_general(q_ref[0, rows(cg), cols(h)].astype(BF16), kb, NT_DIMS,
                                        preferred_element_type=F32)
    lgw = {}
    btw = {}
    for (cg, d, h) in chains:
        col = d * A_HEADS + h
        lgw[cg, d, h] = jnp.broadcast_to(lg_ref[0, rows(cg), col:col + 1], (CHUNK, A_DK))
        btw[cg, d, h] = jnp.broadcast_to(bt_ref[0, rows(cg), col:col + 1], (CHUNK, A_DK))
    gam = {ch: _dot_exact_lhs(tri[ch[1]], lgw[ch]) for ch in chains}
    gam_row = {ch: _dot_exact_lhs(ones, lgw[ch][:, :CHUNK] * tri_t[ch[1]]) for ch in chains}
    decay = {}
    lmat = {}
    for ch in chains:
        cg, d, h = ch
        diff = gam[ch][:, :CHUNK] - gam_row[ch]
        decay[ch] = jnp.where(incl[d], jnp.exp(jnp.where(incl[d], diff, 0.0)), 0.0)
        lmat[ch] = jnp.where(strict[d], btw[ch][:, :CHUNK] * decay[ch] * kk[cg, h], 0.0)
    pinv = {ch: eye - jnp.where((r // 2 == c // 2) & (r != c), lmat[ch], 0.0) for ch in chains}
    s = 2
    while s < CHUNK:
        join = (r // (2 * s) == c // (2 * s)) & (r // s != c // s)
        tc = {ch: _dot3(pinv[ch], jnp.where(join, lmat[ch], 0.0)) for ch in chains}
        pinv = {ch: pinv[ch] - _dot3(tc[ch], pinv[ch]) for ch in chains}
        s *= 2
    for ch in chains:
        cg, d, h = ch
        kh = k_ref[0, rows(cg), cols(h)]
        vh = v_ref[0, rows(cg), cols(h)]
        qh = q_ref[0, rows(cg), cols(h)]
        egam = jnp.exp(gam[ch])
        rhs = jnp.concatenate([kh * (btw[ch] * egam), vh * btw[ch]], axis=1)
        sol = _dot3(pinv[ch], rhs)
        w_ref[0, d, rows(cg), cols(h)] = sol[:, :A_DK].astype(BF16)
        u_ref[0, d, rows(cg), cols(h)] = sol[:, A_DK:]
        qd_ref[0, d, rows(cg), cols(h)] = (qh * egam).astype(BF16)
        glast = jnp.broadcast_to(gam[ch][last[d]:last[d] + 1, :], (CHUNK, A_DK))
        kd_ref[0, d, rows(cg), cols(h)] = (kh * jnp.exp(glast - gam[ch])).astype(BF16)
        g_ref[0, d, rows(cg), cols(h)] = jnp.exp(glast)
        p_ref[0, d, rows(cg), h * CHUNK:(h + 1) * CHUNK] = (decay[ch] * qk[cg, h]).astype(BF16)


def delta_prep(q, k, v, lg, bt):
    b, s, _ = q.shape
    rt = PREP_CHUNKS * CHUNK
    blk = lambda w: pl.BlockSpec((1, rt, w), lambda bi, ni: (bi, ni, 0))
    oblk = lambda w: pl.BlockSpec((1, 2, rt, w), lambda bi, ni: (bi, 0, ni, 0))
    sh = lambda w, dt: jax.ShapeDtypeStruct((b, 2, s, w), dt)
    return pl.pallas_call(
        _delta_prep_kernel, grid=(b, s // rt),
        in_specs=[blk(A_DIM), blk(A_DIM), blk(A_DIM), blk(2 * A_HEADS), blk(2 * A_HEADS)],
        out_specs=[oblk(A_DIM), oblk(A_DIM), oblk(A_DIM), oblk(A_DIM), oblk(A_HEADS * CHUNK), oblk(A_DIM)],
        out_shape=[sh(A_DIM, BF16), sh(A_DIM, F32), sh(A_DIM, BF16), sh(A_DIM, BF16),
                   sh(A_HEADS * CHUNK, BF16), sh(A_DIM, F32)],
        compiler_params=pltpu.CompilerParams(dimension_semantics=("parallel", "parallel")),
        name="delta_prep",
    )(q, k, v, lg, bt)


def _delta_scan_kernel(*refs):
    ins = refs[:12]
    s0_ref = refs[12]
    of_ref, ob_ref, sout_ref, state = refs[13:]
    n = pl.program_id(1)

    @pl.when(n == 0)
    def _():
        state[...] = s0_ref[...]

    outs = (of_ref, ob_ref)
    chains = [(g, d, h) for g in range(SCAN_SEQS) for d in range(2) for h in range(A_HEADS)]
    cs = lambda h: slice(h * A_DK, (h + 1) * A_DK)
    ref = lambda d, i: ins[d * 6 + i]
    s_old = {ch: state[ch] for ch in chains}
    wqs = {}
    for (g, d, h) in chains:
        wq = jnp.concatenate([ref(d, 0)[g, 0, :, cs(h)], ref(d, 2)[g, 0, :, cs(h)]], axis=0)
        wqs[g, d, h] = _bdot(wq, s_old[g, d, h].astype(BF16))
    unb = {}
    for (g, d, h) in chains:
        unb[g, d, h] = (ref(d, 1)[g, 0, :, cs(h)] - wqs[g, d, h][:CHUNK]).astype(BF16)
    for (g, d, h) in chains:
        o = wqs[g, d, h][CHUNK:] + _bdot(ref(d, 4)[g, 0, :, h * CHUNK:(h + 1) * CHUNK], unb[g, d, h])
        outs[d][g, :, cs(h)] = o
    for (g, d, h) in chains:
        upd = lax.dot_general(ref(d, 3)[g, 0, :, cs(h)], unb[g, d, h], TN_DIMS, preferred_element_type=F32)
        gs = jnp.broadcast_to(ref(d, 5)[g, 0, 0:1, cs(h)], (A_DK, A_DV))
        state[g, d, h] = gs * s_old[g, d, h] + upd

    @pl.when(n == pl.num_programs(1) - 1)
    def _():
        sout_ref[...] = state[...]


def delta_scan(w, u, qd, kd, p, gl, s0):
    b, _, s, _ = u.shape
    n = s // CHUNK

    def spec(wd, d):
        if d == 0:
            return pl.BlockSpec((SCAN_SEQS, 1, CHUNK, wd), lambda bi, ni: (bi, 0, ni, 0))
        return pl.BlockSpec((SCAN_SEQS, 1, CHUNK, wd), lambda bi, ni: (bi, 1, n - 1 - ni, 0))

    arrs = (w, u, qd, kd, p, gl)
    in_specs = [spec(a.shape[-1], d) for d in range(2) for a in arrs]
    st = pl.BlockSpec((SCAN_SEQS, 2, A_HEADS, A_DK, A_DV), lambda bi, ni: (bi, 0, 0, 0, 0))
    of = pl.BlockSpec((SCAN_SEQS, CHUNK, A_DIM), lambda bi, ni: (bi, ni, 0))
    ob = pl.BlockSpec((SCAN_SEQS, CHUNK, A_DIM), lambda bi, ni: (bi, n - 1 - ni, 0))
    return pl.pallas_call(
        _delta_scan_kernel, grid=(b // SCAN_SEQS, n),
        in_specs=in_specs + [st],
        out_specs=[of, ob, st],
        out_shape=[jax.ShapeDtypeStruct((b, s, A_DIM), F32), jax.ShapeDtypeStruct((b, s, A_DIM), F32),
                   jax.ShapeDtypeStruct((b, 2, A_HEADS, A_DK, A_DV), F32)],
        scratch_shapes=[pltpu.VMEM((SCAN_SEQS, 2, A_HEADS, A_DK, A_DV), F32)],
        compiler_params=pltpu.CompilerParams(dimension_semantics=("parallel", "arbitrary")),
        name="delta_scan",
    )(*(arrs + arrs), s0)


def _cand_tables():
    pairs = [(r, c) for r in range(PEER_TOPK) for c in range(PEER_TOPK) if (r + 1) * (c + 1) <= PEER_TOPK]
    npad = 64
    e1 = np.zeros((npad, PEER_TOPK), np.float32)
    e2 = np.zeros((npad, PEER_TOPK), np.float32)
    m = np.zeros((PEER_TOPK, npad), np.float32)
    for k, (r, c) in enumerate(pairs):
        e1[k, r] = 1
        e2[k, c] = 1
        m[r, k] = 1
    return len(pairs), e1, e2, m


N_CAND, _CAND_E1, _CAND_E2, _CAND_ROW = _cand_tables()


def _extract_topk(s, n_iter):
    k, t = s.shape
    work = s.reshape(k // 8, 8, t)
    rank = jnp.full(work.shape, float(n_iter), F32)
    vals = []
    for r in range(n_iter):
        m = jnp.max(jnp.max(work, axis=0), axis=0, keepdims=True)
        hit = work == jnp.broadcast_to(m, (8, t))[None]
        rank = jnp.where(hit, float(r), rank)
        work = jnp.where(hit, NEG_INF, work)
        vals.append(m)
    return vals, rank.reshape(k, t)


def _route_kernel(x_ref, g_ref, sc_ref, sh_ref, wqh_ref, wql_ref, k1_ref, k2_ref, e1_ref, e2_ref, mrow_ref,
                  ht_ref, a_ref, n_ref, b_ref, r_ref, q_scr):
    hmod = _norm_modulate(x_ref[...], g_ref[...], sc_ref[0], sh_ref[0])
    ht_ref[...] = hmod.T.astype(BF16)
    hh, hl = _split_bf16(hmod)
    q_scr[...] = _bdot(hh, wqh_ref[...]) + (_bdot(hh, wql_ref[...]) + _bdot(hl, wqh_ref[...]))
    tt = x_ref.shape[0]

    def head(h, carry):
        c1 = pl.ds(pl.multiple_of(h * 2 * PEER_DK, PEER_DK), PEER_DK)
        c2 = pl.ds(pl.multiple_of(h * 2 * PEER_DK + PEER_DK, PEER_DK), PEER_DK)
        hp = lax.Precision.HIGHEST
        s1 = lax.dot_general(k1_ref[h], q_scr[:, c1], NT_DIMS, precision=hp, preferred_element_type=F32)
        s2 = lax.dot_general(k2_ref[h], q_scr[:, c2], NT_DIMS, precision=hp, preferred_element_type=F32)
        v1, rank1 = _extract_topk(s1, PEER_TOPK)
        v2, rank2 = _extract_topk(s2, PEER_TOPK)
        v1m = jnp.concatenate(v1, axis=0)
        v2m = jnp.concatenate(v2, axis=0)
        cand = (jnp.dot(e1_ref[...], v1m, precision=hp, preferred_element_type=F32)
                + jnp.dot(e2_ref[...], v2m, precision=hp, preferred_element_type=F32))
        row = lax.broadcasted_iota(jnp.int32, cand.shape, 0)
        cand = jnp.where(row < N_CAND, cand, NEG_INF)
        _, crank = _extract_topk(cand, PEER_TOPK)
        sel = crank < float(PEER_TOPK)
        cmax = v1[0] + v2[0]
        z = jnp.sum(jnp.where(sel, jnp.exp(cand - cmax), 0.0), axis=0, keepdims=True)
        n_r = _bdot(mrow_ref[...], jnp.where(sel, 1.0, 0.0).astype(BF16))
        rank1_3 = rank1.reshape(PEER_KEYS // 8, 8, tt)
        nn = jnp.zeros_like(rank1_3)
        for r in range(PEER_TOPK):
            nn = jnp.where(rank1_3 == float(r), jnp.broadcast_to(n_r[r:r + 1, :], (8, tt))[None], nn)
        a_ref[h] = jnp.exp(s1 - v1[0]) / z
        n_ref[h] = nn.reshape(PEER_KEYS, tt)
        b_ref[h] = jnp.where(rank2 < float(PEER_TOPK), jnp.exp(s2 - v2[0]), 0.0)
        r_ref[h] = rank2
        return carry

    lax.fori_loop(0, PEER_HEADS, head, 0, unroll=2)


def peer_route(x, norm_g, sc, sh, wq_hi, wq_lo, k1, k2, rows_per_seg):
    t = x.shape[0]
    tt = ROUTE_TOKEN_TILE
    gate = pl.BlockSpec((PEER_HEADS, PEER_KEYS, tt), lambda i: (0, 0, i))
    full = lambda shp: pl.BlockSpec(shp, lambda i: (0,) * len(shp))
    seg = _seg_spec(rows_per_seg, tt)
    return pl.pallas_call(
        _route_kernel, grid=(t // tt,),
        in_specs=[pl.BlockSpec((tt, D_MODEL), lambda i: (i, 0)), full((1, D_MODEL)), seg, seg,
                  full((D_MODEL, 2 * PEER_HEADS * PEER_DK)), full((D_MODEL, 2 * PEER_HEADS * PEER_DK)),
                  full((PEER_HEADS, PEER_KEYS, PEER_DK)),
                  full((PEER_HEADS, PEER_KEYS, PEER_DK)), full((64, PEER_TOPK)), full((64, PEER_TOPK)),
                  full((PEER_TOPK, 64))],
        out_specs=[pl.BlockSpec((D_MODEL, tt), lambda i: (0, i)), gate, gate, gate, gate],
        out_shape=[jax.ShapeDtypeStruct((D_MODEL, t), BF16)]
        + [jax.ShapeDtypeStruct((PEER_HEADS, PEER_KEYS, t), F32)] * 4,
        scratch_shapes=[pltpu.VMEM((tt, 2 * PEER_HEADS * PEER_DK), F32)],
        compiler_params=pltpu.CompilerParams(dimension_semantics=("parallel",),
                                             vmem_limit_bytes=VMEM_LIMIT_BYTES),
        name="peer_route",
    )(x, norm_g.reshape(1, D_MODEL), sc, sh, wq_hi, wq_lo, k1, k2,
      jnp.asarray(_CAND_E1), jnp.asarray(_CAND_E2), jnp.asarray(_CAND_ROW, BF16))


def _peer_dense_kernel(ht_ref, u_ref, vt_ref, a_ref, n_ref, b_ref, r_ref, x_ref, g2_ref, o_ref,
                       acc_ref, hid_ref, w_ref, bp_ref, rp_ref):
    e = pl.program_id(1)

    @pl.when(e == 0)
    def _():
        acc_ref[...] = jnp.zeros_like(acc_ref)
        bp_ref[...] = b_ref[...].astype(BF16)
        rp_ref[...] = r_ref[...].astype(BF16)

    i0 = pl.multiple_of(e * PEER_KEYS_PER_STEP, PEER_KEYS_PER_STEP)

    def hidden(pp):
        rows = slice(pp * PIECE_ROWS, (pp + 1) * PIECE_ROWS)
        hid_ref[rows, :] = _bdot(u_ref[rows, :], ht_ref[...])

    def gates(pp):
        for c in range(PEER_TOKEN_TILE // LANES):
            ls = slice(c * LANES, (c + 1) * LANES)
            for ii in range(PIECE_KEYS):
                k = pp * PIECE_KEYS + ii
                acc = jnp.zeros((PEER_KEYS, LANES), BF16)
                for h in range(PEER_HEADS):
                    a_rows = a_ref[h, pl.ds(i0, PEER_KEYS_PER_STEP), ls]
                    n_rows = n_ref[h, pl.ds(i0, PEER_KEYS_PER_STEP), ls]
                    a16 = jnp.broadcast_to(a_rows[k:k + 1, :], (BF16_ROWS, LANES)).astype(BF16)
                    n16 = jnp.broadcast_to(n_rows[k:k + 1, :], (BF16_ROWS, LANES)).astype(BF16)
                    a128 = pltpu.repeat(a16, PEER_KEYS // BF16_ROWS, axis=0)
                    n128 = pltpu.repeat(n16, PEER_KEYS // BF16_ROWS, axis=0)
                    b = bp_ref[h, :, ls]
                    acc = acc + jnp.where(rp_ref[h, :, ls] < n128, b, jnp.zeros_like(b)) * a128
                rows = slice(k * PEER_KEYS, (k + 1) * PEER_KEYS)
                w_ref[rows, ls] = _gelu_tanh(hid_ref[rows, ls]).astype(BF16) * acc

    def project(pp):
        rows = slice(pp * PIECE_ROWS, (pp + 1) * PIECE_ROWS)
        acc_ref[...] += _bdot(vt_ref[:, rows], w_ref[rows, :])

    hidden(0)
    for pp in range(PEER_PIECES):
        if pp + 1 < PEER_PIECES:
            hidden(pp + 1)
        gates(pp)
        project(pp)

    @pl.when(e == pl.num_programs(1) - 1)
    def _():
        o_ref[...] = x_ref[...] + g2_ref[0] * acc_ref[...].T


def peer_dense(ht_bf, u_bf, vt_bf, a_t, n_t, b_t, r_t, x, g2, rows_per_seg):
    t = x.shape[0]
    tt = PEER_TOKEN_TILE
    grid = (t // tt, PEER_N // PEER_EXPERT_TILE)
    gate_spec = pl.BlockSpec((PEER_HEADS, PEER_KEYS, tt), lambda ti, ei: (0, 0, ti))
    return pl.pallas_call(
        _peer_dense_kernel, grid=grid,
        in_specs=[pl.BlockSpec((D_MODEL, tt), lambda ti, ei: (0, ti)),
                  pl.BlockSpec((PEER_EXPERT_TILE, D_MODEL), lambda ti, ei: (ei, 0)),
                  pl.BlockSpec((D_MODEL, PEER_EXPERT_TILE), lambda ti, ei: (0, ei)),
                  gate_spec, gate_spec, gate_spec, gate_spec,
                  pl.BlockSpec((tt, D_MODEL), lambda ti, ei: (ti, 0)),
                  _seg_spec(rows_per_seg, tt)],
        out_specs=pl.BlockSpec((tt, D_MODEL), lambda ti, ei: (ti, 0)),
        out_shape=jax.ShapeDtypeStruct((t, D_MODEL), F32),
        scratch_shapes=[pltpu.VMEM((D_MODEL, tt), F32),
                        pltpu.VMEM((PEER_EXPERT_TILE, tt), F32),
                        pltpu.VMEM((PEER_EXPERT_TILE, tt), BF16),
                        pltpu.VMEM((PEER_HEADS, PEER_KEYS, tt), BF16),
                        pltpu.VMEM((PEER_HEADS, PEER_KEYS, tt), BF16)],
        compiler_params=pltpu.CompilerParams(dimension_semantics=("parallel", "arbitrary"),
                                             vmem_limit_bytes=VMEM_LIMIT_BYTES),
        name="peer_dense",
    )(ht_bf, u_bf, vt_bf, a_t, n_t, b_t, r_t, x, g2)


def _fill_halo_scratch(scr, prev, cur, nxt, halo, ts):
    i = pl.program_id(1)
    scr[0:halo, :] = jnp.where(i > 0, prev, 0.0)
    scr[halo:halo + ts, :] = cur
    scr[halo + ts:halo + ts + halo, :] = jnp.where(i < pl.num_programs(1) - 1, nxt, 0.0)


def _depthwise_taps(scr, w_ref, taps, halo, r0, rows):
    off = halo - taps // 2
    acc = scr[off + r0:off + r0 + rows, :] * w_ref[0:1, :]
    for k in range(1, taps):
        acc = acc + scr[off + r0 + k:off + r0 + k + rows, :] * w_ref[k:k + 1, :]
    return acc


def _halo_specs(width, col_block, halo, ts, seq_len, total_rows):
    tiles = seq_len // ts
    per_tile = ts // halo
    last = total_rows // halo - 1
    prev = pl.BlockSpec((halo, width), lambda s, i: (jnp.maximum((s * tiles + i) * per_tile - 1, 0), col_block))
    cur = pl.BlockSpec((ts, width), lambda s, i: (s * tiles + i, col_block))
    nxt = pl.BlockSpec((halo, width), lambda s, i: (jnp.minimum((s * tiles + i + 1) * per_tile, last), col_block))
    return prev, cur, nxt


CONF_ROW_BLOCK = 32
QKV_ROW_BLOCK = 16


def _conformer_kernel(prev_ref, cur_ref, next_ref, w_ref, b_ref, lg_ref, lb_ref, o_ref, scr):
    ts = cur_ref.shape[0]
    glu = lambda blk: blk[:, :B_CH] * jax.nn.sigmoid(blk[:, B_CH:])
    _fill_halo_scratch(scr, glu(prev_ref[...]), glu(cur_ref[...]), glu(next_ref[...]), CONF_HALO, ts)
    for rb in range(ts // CONF_ROW_BLOCK):
        r0 = rb * CONF_ROW_BLOCK
        hh = _depthwise_taps(scr, w_ref, B_CONV, CONF_HALO, r0, CONF_ROW_BLOCK) + b_ref[...]
        mu = jnp.mean(hh, axis=-1, keepdims=True)
        var = jnp.mean(jnp.square(hh - mu), axis=-1, keepdims=True)
        y = (hh - mu) * lax.rsqrt(var + EPS) * lg_ref[...] + lb_ref[...]
        o_ref[r0:r0 + CONF_ROW_BLOCK, :] = (y * jax.nn.sigmoid(y)).astype(o_ref.dtype)


def conformer_branch(p, dw_w, dw_b, ln_g, ln_b, nseq, seq_len):
    t = p.shape[0]
    ts = CONV_ROW_TILE
    prev, cur, nxt = _halo_specs(2 * B_CH, COL_GLU // (2 * B_CH), CONF_HALO, ts, seq_len, t)
    row = lambda: pl.BlockSpec((1, B_CH), lambda s, i: (0, 0))
    w_pad = jnp.pad(dw_w, ((0, 32 - B_CONV), (0, 0)))
    return pl.pallas_call(
        _conformer_kernel, grid=(nseq, seq_len // ts),
        in_specs=[prev, cur, nxt, pl.BlockSpec((32, B_CH), lambda s, i: (0, 0)), row(), row(), row()],
        out_specs=pl.BlockSpec((ts, B_CH), lambda s, i: (s * (seq_len // ts) + i, 0)),
        out_shape=jax.ShapeDtypeStruct((t, B_CH), BF16),
        scratch_shapes=[pltpu.VMEM((ts + 2 * CONF_HALO, B_CH), F32)],
        compiler_params=pltpu.CompilerParams(dimension_semantics=("parallel", "parallel")),
        name="conformer_branch",
    )(p, p, p, w_pad, dw_b.reshape(1, B_CH), ln_g.reshape(1, B_CH), ln_b.reshape(1, B_CH))


def _qkv_conv_kernel(prev_ref, cur_ref, next_ref, w_ref, q_ref, k_ref, v_ref, scr):
    ts = cur_ref.shape[0]
    _fill_halo_scratch(scr, prev_ref[...], cur_ref[...], next_ref[...], QKV_HALO, ts)
    for rb in range(ts // QKV_ROW_BLOCK):
        r0 = rb * QKV_ROW_BLOCK
        rows = slice(r0, r0 + QKV_ROW_BLOCK)
        y = _depthwise_taps(scr, w_ref, SHORT_CONV, QKV_HALO, r0, QKV_ROW_BLOCK)
        y = y * jax.nn.sigmoid(y)
        for h in range(A_HEADS):
            cs = slice(h * A_DK, (h + 1) * A_DK)
            qh = y[:, h * A_DK:(h + 1) * A_DK]
            kh = y[:, A_DIM + h * A_DK:A_DIM + (h + 1) * A_DK]
            q_ref[rows, cs] = qh * (lax.rsqrt(jnp.sum(qh * qh, axis=-1, keepdims=True) + EPS) * (A_DK ** -0.5))
            k_ref[rows, cs] = kh * lax.rsqrt(jnp.sum(kh * kh, axis=-1, keepdims=True) + EPS)
        v_ref[rows, :] = y[:, 2 * A_DIM:]


def qkv_conv(p, conv_w, nseq, seq_len):
    t = p.shape[0]
    ts = CONV_ROW_TILE
    prev, cur, nxt = _halo_specs(3 * A_DIM, 0, QKV_HALO, ts, seq_len, t)
    out = pl.BlockSpec((ts, A_DIM), lambda s, i: (s * (seq_len // ts) + i, 0))
    w_pad = jnp.pad(conv_w, ((0, 8 - SHORT_CONV), (0, 0)))
    return pl.pallas_call(
        _qkv_conv_kernel, grid=(nseq, seq_len // ts),
        in_specs=[prev, cur, nxt, pl.BlockSpec((8, 3 * A_DIM), lambda s, i: (0, 0))],
        out_specs=[out, out, out],
        out_shape=[jax.ShapeDtypeStruct((t, A_DIM), F32)] * 3,
        scratch_shapes=[pltpu.VMEM((ts + 2 * QKV_HALO, 3 * A_DIM), F32)],
        compiler_params=pltpu.CompilerParams(dimension_semantics=("parallel", "parallel")),
        name="qkv_conv",
    )(p, p, p, w_pad)


def _mix_out_kernel(of_ref, ob_ref, z_ref, conf_ref, ng_ref, w_ref, x_ref, gate_ref, o_ref):
    o = of_ref[...] + ob_ref[...]
    z = z_ref[...]
    parts = []
    for h in range(A_HEADS):
        cs = slice(h * A_DV, (h + 1) * A_DV)
        oh = o[:, cs]
        zh = z[:, cs]
        scale = lax.rsqrt(jnp.mean(oh * oh, axis=-1, keepdims=True) + EPS)
        parts.append((oh * scale * ng_ref[...] * (zh * jax.nn.sigmoid(zh))).astype(BF16))
    oa = jnp.concatenate(parts, axis=1)
    mix = _bdot(oa, w_ref[:A_DIM, :]) + _bdot(conf_ref[...], w_ref[A_DIM:, :])
    o_ref[...] = x_ref[...] + gate_ref[0] * mix


def mix_out(o_f, o_b, p, conf, norm_g, w_bf, x, gate, rows_per_seg):
    t = x.shape[0]
    tm = MM_ROW_TILE
    half = lambda cb: pl.BlockSpec((tm, A_DIM), lambda i: (i, cb))
    return pl.pallas_call(
        _mix_out_kernel, grid=(t // tm,),
        in_specs=[half(0), half(0), half(COL_Z // A_DIM), half(0), pl.BlockSpec((1, A_DV), lambda i: (0, 0)),
                  pl.BlockSpec((D_MODEL, D_MODEL), lambda i: (0, 0)), pl.BlockSpec((tm, D_MODEL), lambda i: (i, 0)),
                  _seg_spec(rows_per_seg, tm)],
        out_specs=pl.BlockSpec((tm, D_MODEL), lambda i: (i, 0)),
        out_shape=jax.ShapeDtypeStruct((t, D_MODEL), F32),
        compiler_params=pltpu.CompilerParams(dimension_semantics=("parallel",),
                                             vmem_limit_bytes=VMEM_LIMIT_BYTES),
        name="mix_out",
    )(o_f, o_b, p, conf, norm_g.reshape(1, A_DV), w_bf, x, gate)


def grid_pos_emb(n_tokens):
    rows = n_tokens // GRID_W
    r = jnp.repeat(jnp.arange(rows, dtype=F32), GRID_W)
    col = jnp.tile(jnp.arange(GRID_W, dtype=F32), rows)
    nf = D_MODEL // 4
    freqs = jnp.exp(-math.log(POS_BASE) * jnp.arange(nf, dtype=F32) / nf)
    ar = r[:, None] * freqs
    ac = col[:, None] * freqs
    return jnp.concatenate([jnp.sin(ar), jnp.cos(ar), jnp.sin(ac), jnp.cos(ac)], axis=-1)


def delta_conformer_layer(x, p, g1, nseq, seq_len, s0, e, prm, rows_per_seg):
    t = p.shape[0]
    seq = lambda m: m.reshape(nseq, seq_len, m.shape[-1])
    q, k, v = qkv_conv(p, prm['conv_qkv_w'][e], nseq, seq_len)
    alpha = p[:, COL_AB:COL_AB + 2 * A_HEADS]
    beta = jax.nn.sigmoid(p[:, COL_AB + 2 * A_HEADS:COL_AB + 4 * A_HEADS])
    log_g = (-jnp.exp(prm['a_log'][e]).reshape(1, 2 * A_HEADS)
             * jax.nn.softplus(alpha + prm['dt_bias'][e].reshape(1, 2 * A_HEADS)))
    o_f, o_b, st = delta_scan(*delta_prep(seq(q), seq(k), seq(v), seq(log_g), seq(beta)), s0)
    conf = conformer_branch(p, prm['conf_dw_w'][e], prm['conf_dw_b'][e], prm['conf_ln_g'][e],
                            prm['conf_ln_b'][e], nseq, seq_len)
    x = mix_out(o_f.reshape(t, A_DIM), o_b.reshape(t, A_DIM), p, conf, prm['delta_norm_g'][e],
                prm['w_out_bf'][e], x, g1, rows_per_seg)
    return x, st


def trunk(x, nseq, seq_len, cond, s0, prm):
    t = x.shape[0]
    rows_per_seg = t // cond.shape[0]
    states = []
    for l in range(DEPTH):
        mod = jax.nn.silu(cond) @ prm['ada_w'][l] + prm['ada_b'][l]
        sh1, sc1, g1, sh2, sc2, g2 = [m[:, None, :] for m in jnp.split(mod, 6, axis=-1)]
        e = l // 2
        if l % 2 == 0:
            p = norm_mm(x, prm['norm1_g'][l], sc1, sh1, prm['w_in_bf'][e], rows_per_seg, F32)
            x, st = delta_conformer_layer(x, p, g1, nseq, seq_len, s0[:, e], e, prm, rows_per_seg)
            states.append(st)
        else:
            z = norm_mm(x, prm['norm1_g'][l], sc1, sh1, prm['w_fnet_bf'][e], rows_per_seg, BF16)
            tm = min(seq_len, 512)
            x = seq_mix_res(prm['dft_seq'][seq_len], z, x, g1, seq_len, tm, min(seq_len, 1024))
        ht, a_t, n_t, b_t, r_t = peer_route(x, prm['norm2_g'][l], sc2, sh2, prm['peer_wq_hi'][l],
                                            prm['peer_wq_lo'][l], prm['peer_k1'][l], prm['peer_k2'][l],
                                            rows_per_seg)
        x = peer_dense(ht, prm['peer_u_bf'][l], prm['peer_vt_bf'][l], a_t, n_t, b_t, r_t, x, g2, rows_per_seg)
    xf = x * lax.rsqrt(jnp.mean(x * x, axis=-1, keepdims=True) + EPS) * prm['final_norm_g']
    return xf, jnp.stack(states, axis=1)


def kernel(x_prompt, x_sample, state_delta, c, c_ctx, ada_w, ada_b, norm1_g, norm2_g, w_in_ab, conv_qkv_w,
           a_log, dt_bias, delta_norm_g, conf_dw_w, conf_dw_b, conf_ln_g, conf_ln_b, w_out_ab, w_out_c,
           peer_wq, peer_k1, peer_k2, peer_u, peer_v, final_norm_g):
    bp, sp, _ = x_prompt.shape
    bs, ss, _ = x_sample.shape
    bdc, bds = dft_group_matrices(D_MODEL // C_GROUPS, C_GROUPS)
    w_fnet = [jnp.concatenate([mm3(bdc, w_out_c[e], BF16), mm3(bds, w_out_c[e], BF16)], axis=1)
              for e in range(DEPTH // 2)]
    o4 = 4 * A_DIM
    w_in = jnp.concatenate([w_in_ab[:, :, :o4], w_in_ab[:, :, o4 + 4 * A_HEADS:], w_in_ab[:, :, o4:o4 + 4 * A_HEADS]],
                           axis=-1).astype(BF16)
    wq_hi, wq_lo = split_bf16(peer_wq)
    prm = {'ada_w': ada_w, 'ada_b': ada_b, 'norm1_g': norm1_g, 'norm2_g': norm2_g,
           'w_in_bf': jnp.pad(w_in, ((0, 0), (0, 0), (0, P_AB_PAD - P_AB))),
           'conv_qkv_w': conv_qkv_w, 'a_log': a_log, 'dt_bias': dt_bias,
           'delta_norm_g': delta_norm_g, 'conf_dw_w': conf_dw_w, 'conf_dw_b': conf_dw_b,
           'conf_ln_g': conf_ln_g, 'conf_ln_b': conf_ln_b, 'w_out_bf': w_out_ab.astype(BF16),
           'w_fnet_bf': w_fnet, 'dft_seq': {s: dft_seq_matrix(s) for s in {sp, ss}},
           'peer_wq_hi': wq_hi, 'peer_wq_lo': wq_lo, 'peer_k1': peer_k1, 'peer_k2': peer_k2,
           'peer_u_bf': peer_u.astype(BF16),
           'peer_vt_bf': jnp.transpose(peer_v.astype(BF16), (0, 2, 1)),
           'final_norm_g': final_norm_g}
    ne = (DEPTH + 1) // 2
    s0_ctx = jnp.zeros((bp, ne, 2, A_HEADS, A_DK, A_DV), F32)
    y_prompt, ctx_states = trunk(x_prompt.reshape(bp * sp, D_MODEL), bp, sp, c_ctx[None, :], s0_ctx, prm)
    xs = (x_sample + grid_pos_emb(ss)[None]).reshape(bs * ss, D_MODEL)
    y_sample, _ = trunk(xs, bs, ss, c, state_delta, prm)
    return (y_prompt.reshape(bp, sp, D_MODEL), y_sample.reshape(bs, ss, D_MODEL), ctx_states)
```

```python
import math

import jax
import jax.numpy as jnp
import numpy as np
from jax import lax
from jax.experimental import pallas as pl
from jax.experimental.pallas import tpu as pltpu

D_MODEL = 1024
DEPTH = 4
GRID_W = 64
POS_BASE = 10000.0
EPS = 1e-6
A_HEADS = 4
A_DK = 128
A_DV = 128
A_DIM = A_HEADS * A_DV
CHUNK = 64
B_CH = D_MODEL // 2
P_AB = 4 * A_DIM + 4 * A_HEADS + 2 * B_CH
C_GROUPS = 8
PEER_HEADS = 8
PEER_KEYS = 128
PEER_N = PEER_KEYS * PEER_KEYS
PEER_DK = 128
PEER_TOPK = 16

F32 = jnp.float32
BF16 = jnp.bfloat16
NEG_INF = float("-inf")

LANES = 128
VMEM_LIMIT_BYTES = 56 * 1024 * 1024
MM_ROW_TILE = 512
ROUTE_TOKEN_TILE = 256
PEER_TOKEN_TILE = 512
PEER_KEYS_PER_STEP = 8
PEER_EXPERT_TILE = PEER_KEYS_PER_STEP * PEER_KEYS
PEER_J_BLOCK = 32
PREP_CHUNKS = 2
SCAN_SEQS = 2
P_AB_PAD = 3200
COL_Z = 3 * A_DIM
COL_GLU = 4 * A_DIM
COL_AB = 4 * A_DIM + 2 * B_CH
CONV_ROW_TILE = 256
SHORT_CONV = 7
B_CONV = 31
CONF_HALO = 16
QKV_HALO = 8
PEER_PIECES = 4
PIECE_KEYS = PEER_KEYS_PER_STEP // PEER_PIECES
PIECE_ROWS = PIECE_KEYS * PEER_KEYS
BF16_ROWS = 16

NT_DIMS = (((1,), (1,)), ((), ()))
TN_DIMS = (((0,), (0,)), ((), ()))


def _bdot(a, b):
    return jnp.dot(a, b, preferred_element_type=F32)


def _split_bf16(a):
    hi = a.astype(BF16)
    lo = (a - hi.astype(F32)).astype(BF16)
    return hi, lo


def _dot3(a, b):
    ah, al = _split_bf16(a)
    bh, bl = _split_bf16(b)
    return _bdot(ah, bh) + (_bdot(ah, bl) + _bdot(al, bh))


def _dot_exact_lhs(a01, b):
    a = a01.astype(BF16)
    bh = b.astype(BF16)
    r1 = b - bh.astype(F32)
    bm = r1.astype(BF16)
    bl = (r1 - bm.astype(F32)).astype(BF16)
    return _bdot(a, bh) + (_bdot(a, bm) + _bdot(a, bl))


def _gelu_tanh(x):
    return 0.5 * x * (1.0 + jnp.tanh(math.sqrt(2.0 / math.pi) * (x + 0.044715 * (x * x * x))))


def _seg_spec(rows_per_seg, tile):
    per = rows_per_seg // tile
    return pl.BlockSpec((1, 1, D_MODEL), lambda i, *_: (i // per, 0, 0))


def _norm_modulate(x, g, sc, sh):
    hn = x * lax.rsqrt(jnp.mean(x * x, axis=-1, keepdims=True) + EPS) * g
    return hn * (1.0 + sc) + sh


def _norm_mm_kernel(x_ref, g_ref, sc_ref, sh_ref, w_ref, o_ref):
    h = _norm_modulate(x_ref[...], g_ref[...], sc_ref[0], sh_ref[0]).astype(BF16)
    o_ref[...] = _bdot(h, w_ref[...]).astype(o_ref.dtype)


def norm_mm(x, norm_g, sc, sh, w_bf, rows_per_seg, out_dtype):
    m = x.shape[0]
    n = w_bf.shape[1]
    tm = MM_ROW_TILE
    seg = _seg_spec(rows_per_seg, tm)
    return pl.pallas_call(
        _norm_mm_kernel, grid=(m // tm,),
        in_specs=[pl.BlockSpec((tm, D_MODEL), lambda i: (i, 0)), pl.BlockSpec((1, D_MODEL), lambda i: (0, 0)),
                  seg, seg, pl.BlockSpec((D_MODEL, n), lambda i: (0, 0))],
        out_specs=pl.BlockSpec((tm, n), lambda i: (i, 0)),
        out_shape=jax.ShapeDtypeStruct((m, n), out_dtype),
        compiler_params=pltpu.CompilerParams(dimension_semantics=("parallel",),
                                             vmem_limit_bytes=VMEM_LIMIT_BYTES),
        name="norm_mm",
    )(x, norm_g.reshape(1, D_MODEL), sc, sh, w_bf)


def _split_kernel(a_ref, hi_ref, lo_ref):
    hi, lo = _split_bf16(a_ref[0])
    hi_ref[0] = hi
    lo_ref[0] = lo


def split_bf16(a):
    nl, nr, nc = a.shape
    blk = pl.BlockSpec((1, MM_ROW_TILE, nc), lambda l, i: (l, i, 0))
    return pl.pallas_call(
        _split_kernel, grid=(nl, nr // MM_ROW_TILE), in_specs=[blk], out_specs=[blk, blk],
        out_shape=[jax.ShapeDtypeStruct(a.shape, BF16)] * 2,
        compiler_params=pltpu.CompilerParams(dimension_semantics=("parallel", "parallel")),
        name="split_bf16",
    )(a)


def _mm3_kernel(a_ref, b_ref, o_ref):
    o_ref[...] = _dot3(a_ref[...], b_ref[...]).astype(o_ref.dtype)


def mm3(a, b, out_dtype):
    return pl.pallas_call(_mm3_kernel, out_shape=jax.ShapeDtypeStruct((a.shape[0], b.shape[1]), out_dtype),
                          compiler_params=pltpu.CompilerParams(vmem_limit_bytes=VMEM_LIMIT_BYTES),
                          name="mm3")(a, b)


def _seqmix_kernel(f_ref, z_ref, x_ref, gate_ref, o_ref, acc_ref):
    k = pl.program_id(2)

    @pl.when(k == 0)
    def _():
        acc_ref[...] = jnp.zeros_like(acc_ref)

    acc_ref[...] += _bdot(f_ref[...], z_ref[...])

    @pl.when(k == pl.num_programs(2) - 1)
    def _():
        o_ref[...] = x_ref[...] + gate_ref[0] * acc_ref[...]


def seq_mix_res(fmat, z, x, gate, seq_len, tm, tk):
    t = x.shape[0]
    nseq = t // seq_len
    seqs_per_seg = nseq // gate.shape[0]
    mt = seq_len // tm
    kt_half = seq_len // tk
    return pl.pallas_call(
        _seqmix_kernel, grid=(nseq, mt, 2 * kt_half),
        in_specs=[pl.BlockSpec((tm, tk), lambda s, i, k: (i, k)),
                  pl.BlockSpec((tk, D_MODEL), lambda s, i, k: (s * kt_half + k % kt_half, k // kt_half)),
                  pl.BlockSpec((tm, D_MODEL), lambda s, i, k: (s * mt + i, 0)),
                  pl.BlockSpec((1, 1, D_MODEL), lambda s, i, k: (s // seqs_per_seg, 0, 0))],
        out_specs=pl.BlockSpec((tm, D_MODEL), lambda s, i, k: (s * mt + i, 0)),
        out_shape=jax.ShapeDtypeStruct((t, D_MODEL), F32),
        scratch_shapes=[pltpu.VMEM((tm, D_MODEL), F32)],
        compiler_params=pltpu.CompilerParams(dimension_semantics=("parallel", "parallel", "arbitrary"),
                                             vmem_limit_bytes=VMEM_LIMIT_BYTES),
        name="seq_mix_res",
    )(fmat, z, x, gate)


def _dft_tables(n, cols):
    r = jnp.arange(n, dtype=jnp.int32)[:, None]
    ang = ((r * cols[None, :]) % n).astype(F32) * (2.0 * math.pi / n)
    return jnp.cos(ang), jnp.sin(ang)


def dft_seq_matrix(s):
    w = 1 << (int(math.log2(s)) // 2)
    ch, sh_ = _dft_tables(s, jnp.arange(s // w, dtype=jnp.int32) * w)
    cl, sl = _dft_tables(s, jnp.arange(w, dtype=jnp.int32))
    sc = 1.0 / math.sqrt(s)
    c = (ch[:, :, None] * cl[:, None, :] - sh_[:, :, None] * sl[:, None, :]).reshape(s, s) * sc
    sn = (sh_[:, :, None] * cl[:, None, :] + ch[:, :, None] * sl[:, None, :]).reshape(s, s) * sc
    return jnp.concatenate([c, -sn], axis=1).astype(BF16)


def dft_group_matrices(n, groups):
    c, s = _dft_tables(n, jnp.arange(n, dtype=jnp.int32))
    sc = 1.0 / math.sqrt(n)
    eye = jnp.eye(groups, dtype=F32)
    return jnp.kron(eye, c * sc), jnp.kron(eye, s * sc)


def _delta_prep_kernel(q_ref, k_ref, v_ref, lg_ref, bt_ref, w_ref, u_ref, qd_ref, kd_ref, p_ref, g_ref):
    r = lax.broadcasted_iota(jnp.int32, (CHUNK, CHUNK), 0)
    c = lax.broadcasted_iota(jnp.int32, (CHUNK, CHUNK), 1)
    eye = (r == c).astype(F32)
    ones = jnp.ones((CHUNK, CHUNK), F32)
    incl = (r >= c, r <= c)
    strict = (r > c, r < c)
    tri = (incl[0].astype(F32), incl[1].astype(F32))
    tri_t = (tri[1], tri[0])
    last = (CHUNK - 1, 0)
    chains = [(cg, d, h) for cg in range(PREP_CHUNKS) for d in range(2) for h in range(A_HEADS)]
    rows = lambda cg: slice(cg * CHUNK, (cg + 1) * CHUNK)
    cols = lambda h: slice(h * A_DK, (h + 1) * A_DK)
    kk = {}
    qk = {}
    for cg in range(PREP_CHUNKS):
        for h in range(A_HEADS):
            kb = k_ref[0, rows(cg), cols(h)].astype(BF16)
            kk[cg, h] = lax.dot_general(kb, kb, NT_DIMS, preferred_element_type=F32)
            qk[cg, h] = lax.dot_general(q_ref[0, rows(cg), cols(h)].astype(BF16), kb, NT_DIMS,
                                        preferred_element_type=F32)
    lgw = {}
    btw = {}
    for (cg, d, h) in chains:
        col = d * A_HEADS + h
        lgw[cg, d, h] = jnp.broadcast_to(lg_ref[0, rows(cg), col:col + 1], (CHUNK, A_DK))
        btw[cg, d, h] = jnp.broadcast_to(bt_ref[0, rows(cg), col:col + 1], (CHUNK, A_DK))
    gam = {ch: _dot_exact_lhs(tri[ch[1]], lgw[ch]) for ch in chains}
    gam_row = {ch: _dot_exact_lhs(ones, lgw[ch][:, :CHUNK] * tri_t[ch[1]]) for ch in chains}
    decay = {}
    lmat = {}
    for ch in chains:
        cg, d, h = ch
        diff = gam[ch][:, :CHUNK] - gam_row[ch]
        decay[ch] = jnp.where(incl[d], jnp.exp(jnp.where(incl[d], diff, 0.0)), 0.0)
        lmat[ch] = jnp.where(strict[d], btw[ch][:, :CHUNK] * decay[ch] * kk[cg, h], 0.0)
    pinv = {ch: eye - jnp.where((r // 2 == c // 2) & (r != c), lmat[ch], 0.0) for ch in chains}
    s = 2
    while s < CHUNK:
        join = (r // (2 * s) == c // (2 * s)) & (r // s != c // s)
        tc = {ch: _dot3(pinv[ch], jnp.where(join, lmat[ch], 0.0)) for ch in chains}
        pinv = {ch: pinv[ch] - _dot3(tc[ch], pinv[ch]) for ch in chains}
        s *= 2
    for ch in chains:
        cg, d, h = ch
        kh = k_ref[0, rows(cg), cols(h)]
        vh = v_ref[0, rows(cg), cols(h)]
        qh = q_ref[0, rows(cg), cols(h)]
        egam = jnp.exp(gam[ch])
        rhs = jnp.concatenate([kh * (btw[ch] * egam), vh * btw[ch]], axis=1)
        sol = _dot3(pinv[ch], rhs)
        w_ref[0, d, rows(cg), cols(h)] = sol[:, :A_DK].astype(BF16)
        u_ref[0, d, rows(cg), cols(h)] = sol[:, A_DK:]
        qd_ref[0, d, rows(cg), cols(h)] = (qh * egam).astype(BF16)
        glast = jnp.broadcast_to(gam[ch][last[d]:last[d] + 1, :], (CHUNK, A_DK))
        kd_ref[0, d, rows(cg), cols(h)] = (kh * jnp.exp(glast - gam[ch])).astype(BF16)
        g_ref[0, d, rows(cg), cols(h)] = jnp.exp(glast)
        p_ref[0, d, rows(cg), h * CHUNK:(h + 1) * CHUNK] = (decay[ch] * qk[cg, h]).astype(BF16)


def delta_prep(q, k, v, lg, bt):
    b, s, _ = q.shape
    rt = PREP_CHUNKS * CHUNK
    blk = lambda w: pl.BlockSpec((1, rt, w), lambda bi, ni: (bi, ni, 0))
    oblk = lambda w: pl.BlockSpec((1, 2, rt, w), lambda bi, ni: (bi, 0, ni, 0))
    sh = lambda w, dt: jax.ShapeDtypeStruct((b, 2, s, w), dt)
    return pl.pallas_call(
        _delta_prep_kernel, grid=(b, s // rt),
        in_specs=[blk(A_DIM), blk(A_DIM), blk(A_DIM), blk(2 * A_HEADS), blk(2 * A_HEADS)],
        out_specs=[oblk(A_DIM), oblk(A_DIM), oblk(A_DIM), oblk(A_DIM), oblk(A_HEADS * CHUNK), oblk(A_DIM)],
        out_shape=[sh(A_DIM, BF16), sh(A_DIM, F32), sh(A_DIM, BF16), sh(A_DIM, BF16),
                   sh(A_HEADS * CHUNK, BF16), sh(A_DIM, F32)],
        compiler_params=pltpu.CompilerParams(dimension_semantics=("parallel", "parallel")),
        name="delta_prep",
    )(q, k, v, lg, bt)


def _delta_scan_kernel(*refs):
    ins = refs[:12]
    s0_ref = refs[12]
    of_ref, ob_ref, sout_ref, state = refs[13:]
    n = pl.program_id(1)

    @pl.when(n == 0)
    def _():
        state[...] = s0_ref[...]

    outs = (of_ref, ob_ref)
    chains = [(g, d, h) for g in range(SCAN_SEQS) for d in range(2) for h in range(A_HEADS)]
    cs = lambda h: slice(h * A_DK, (h + 1) * A_DK)
    ref = lambda d, i: ins[d * 6 + i]
    s_old = {ch: state[ch] for ch in chains}
    wqs = {}
    for (g, d, h) in chains:
        wq = jnp.concatenate([ref(d, 0)[g, 0, :, cs(h)], ref(d, 2)[g, 0, :, cs(h)]], axis=0)
        wqs[g, d, h] = _bdot(wq, s_old[g, d, h].astype(BF16))
    unb = {}
    for (g, d, h) in chains:
        unb[g, d, h] = (ref(d, 1)[g, 0, :, cs(h)] - wqs[g, d, h][:CHUNK]).astype(BF16)
    for (g, d, h) in chains:
        o = wqs[g, d, h][CHUNK:] + _bdot(ref(d, 4)[g, 0, :, h * CHUNK:(h + 1) * CHUNK], unb[g, d, h])
        outs[d][g, :, cs(h)] = o
    for (g, d, h) in chains:
        upd = lax.dot_general(ref(d, 3)[g, 0, :, cs(h)], unb[g, d, h], TN_DIMS, preferred_element_type=F32)
        gs = jnp.broadcast_to(ref(d, 5)[g, 0, 0:1, cs(h)], (A_DK, A_DV))
        state[g, d, h] = gs * s_old[g, d, h] + upd

    @pl.when(n == pl.num_programs(1) - 1)
    def _():
        sout_ref[...] = state[...]


def delta_scan(w, u, qd, kd, p, gl, s0):
    b, _, s, _ = u.shape
    n = s // CHUNK

    def spec(wd, d):
        if d == 0:
            return pl.BlockSpec((SCAN_SEQS, 1, CHUNK, wd), lambda bi, ni: (bi, 0, ni, 0))
        return pl.BlockSpec((SCAN_SEQS, 1, CHUNK, wd), lambda bi, ni: (bi, 1, n - 1 - ni, 0))

    arrs = (w, u, qd, kd, p, gl)
    in_specs = [spec(a.shape[-1], d) for d in range(2) for a in arrs]
    st = pl.BlockSpec((SCAN_SEQS, 2, A_HEADS, A_DK, A_DV), lambda bi, ni: (bi, 0, 0, 0, 0))
    of = pl.BlockSpec((SCAN_SEQS, CHUNK, A_DIM), lambda bi, ni: (bi, ni, 0))
    ob = pl.BlockSpec((SCAN_SEQS, CHUNK, A_DIM), lambda bi, ni: (bi, n - 1 - ni, 0))
    return pl.pallas_call(
        _delta_scan_kernel, grid=(b // SCAN_SEQS, n),
        in_specs=in_specs + [st],
        out_specs=[of, ob, st],
        out_shape=[jax.ShapeDtypeStruct((b, s, A_DIM), F32), jax.ShapeDtypeStruct((b, s, A_DIM), F32),
                   jax.ShapeDtypeStruct((b, 2, A_HEADS, A_DK, A_DV), F32)],
        scratch_shapes=[pltpu.VMEM((SCAN_SEQS, 2, A_HEADS, A_DK, A_DV), F32)],
        compiler_params=pltpu.CompilerParams(dimension_semantics=("parallel", "arbitrary")),
        name="delta_scan",
    )(*(arrs + arrs), s0)


def _cand_tables():
    pairs = [(r, c) for r in range(PEER_TOPK) for c in range(PEER_TOPK) if (r + 1) * (c + 1) <= PEER_TOPK]
    npad = 64
    e1 = np.zeros((npad, PEER_TOPK), np.float32)
    e2 = np.zeros((npad, PEER_TOPK), np.float32)
    m = np.zeros((PEER_TOPK, npad), np.float32)
    for k, (r, c) in enumerate(pairs):
        e1[k, r] = 1
        e2[k, c] = 1
        m[r, k] = 1
    return len(pairs), e1, e2, m


N_CAND, _CAND_E1, _CAND_E2, _CAND_ROW = _cand_tables()


def _extract_topk(s, n_iter):
    k, t = s.shape
    work = s.reshape(k // 8, 8, t)
    rank = jnp.full(work.shape, float(n_iter), F32)
    vals = []
    for r in range(n_iter):
        m = jnp.max(jnp.max(work, axis=0), axis=0, keepdims=True)
        hit = work == jnp.broadcast_to(m, (8, t))[None]
        rank = jnp.where(hit, float(r), rank)
        work = jnp.where(hit, NEG_INF, work)
        vals.append(m)
    return vals, rank.reshape(k, t)


def _route_kernel(x_ref, g_ref, sc_ref, sh_ref, wqh_ref, wql_ref, k1_ref, k2_ref, e1_ref, e2_ref, mrow_ref,
                  ht_ref, a_ref, n_ref, b_ref, r_ref, q_scr):
    hmod = _norm_modulate(x_ref[...], g_ref[...], sc_ref[0], sh_ref[0])
    ht_ref[...] = hmod.T.astype(BF16)
    hh, hl = _split_bf16(hmod)
    q_scr[...] = _bdot(hh, wqh_ref[...]) + (_bdot(hh, wql_ref[...]) + _bdot(hl, wqh_ref[...]))
    tt = x_ref.shape[0]

    def head(h, carry):
        c1 = pl.ds(pl.multiple_of(h * 2 * PEER_DK, PEER_DK), PEER_DK)
        c2 = pl.ds(pl.multiple_of(h * 2 * PEER_DK + PEER_DK, PEER_DK), PEER_DK)
        hp = lax.Precision.HIGHEST
        s1 = lax.dot_general(k1_ref[h], q_scr[:, c1], NT_DIMS, precision=hp, preferred_element_type=F32)
        s2 = lax.dot_general(k2_ref[h], q_scr[:, c2], NT_DIMS, precision=hp, preferred_element_type=F32)
        v1, rank1 = _extract_topk(s1, PEER_TOPK)
        v2, rank2 = _extract_topk(s2, PEER_TOPK)
        v1m = jnp.concatenate(v1, axis=0)
        v2m = jnp.concatenate(v2, axis=0)
        cand = (jnp.dot(e1_ref[...], v1m, precision=hp, preferred_element_type=F32)
                + jnp.dot(e2_ref[...], v2m, precision=hp, preferred_element_type=F32))
        row = lax.broadcasted_iota(jnp.int32, cand.shape, 0)
        cand = jnp.where(row < N_CAND, cand, NEG_INF)
        _, crank = _extract_topk(cand, PEER_TOPK)
        sel = crank < float(PEER_TOPK)
        cmax = v1[0] + v2[0]
        z = jnp.sum(jnp.where(sel, jnp.exp(cand - cmax), 0.0), axis=0, keepdims=True)
        n_r = _bdot(mrow_ref[...], jnp.where(sel, 1.0, 0.0).astype(BF16))
        rank1_3 = rank1.reshape(PEER_KEYS // 8, 8, tt)
        nn = jnp.zeros_like(rank1_3)
        for r in range(PEER_TOPK):
            nn = jnp.where(rank1_3 == float(r), jnp.broadcast_to(n_r[r:r + 1, :], (8, tt))[None], nn)
        a_ref[h] = jnp.exp(s1 - v1[0]) / z
        n_ref[h] = nn.reshape(PEER_KEYS, tt)
        b_ref[h] = jnp.where(rank2 < float(PEER_TOPK), jnp.exp(s2 - v2[0]), 0.0)
        r_ref[h] = rank2
        return carry

    lax.fori_loop(0, PEER_HEADS, head, 0, unroll=2)


def peer_route(x, norm_g, sc, sh, wq_hi, wq_lo, k1, k2, rows_per_seg):
    t = x.shape[0]
    tt = ROUTE_TOKEN_TILE
    gate = pl.BlockSpec((PEER_HEADS, PEER_KEYS, tt), lambda i: (0, 0, i))
    full = lambda shp: pl.BlockSpec(shp, lambda i: (0,) * len(shp))
    seg = _seg_spec(rows_per_seg, tt)
    return pl.pallas_call(
        _route_kernel, grid=(t // tt,),
        in_specs=[pl.BlockSpec((tt, D_MODEL), lambda i: (i, 0)), full((1, D_MODEL)), seg, seg,
                  full((D_MODEL, 2 * PEER_HEADS * PEER_DK)), full((D_MODEL, 2 * PEER_HEADS * PEER_DK)),
                  full((PEER_HEADS, PEER_KEYS, PEER_DK)),
                  full((PEER_HEADS, PEER_KEYS, PEER_DK)), full((64, PEER_TOPK)), full((64, PEER_TOPK)),
                  full((PEER_TOPK, 64))],
        out_specs=[pl.BlockSpec((D_MODEL, tt), lambda i: (0, i)), gate, gate, gate, gate],
        out_shape=[jax.ShapeDtypeStruct((D_MODEL, t), BF16)]
        + [jax.ShapeDtypeStruct((PEER_HEADS, PEER_KEYS, t), F32)] * 4,
        scratch_shapes=[pltpu.VMEM((tt, 2 * PEER_HEADS * PEER_DK), F32)],
        compiler_params=pltpu.CompilerParams(dimension_semantics=("parallel",),
                                             vmem_limit_bytes=VMEM_LIMIT_BYTES),
        name="peer_route",
    )(x, norm_g.reshape(1, D_MODEL), sc, sh, wq_hi, wq_lo, k1, k2,
      jnp.asarray(_CAND_E1), jnp.asarray(_CAND_E2), jnp.asarray(_CAND_ROW, BF16))


def _peer_dense_kernel(ht_ref, u_ref, vt_ref, a_ref, n_ref, b_ref, r_ref, x_ref, g2_ref, o_ref,
                       acc_ref, hid_ref, w_ref, bp_ref, rp_ref):
    e = pl.program_id(1)

    @pl.when(e == 0)
    def _():
        acc_ref[...] = jnp.zeros_like(acc_ref)
        bp_ref[...] = b_ref[...].astype(BF16)
        rp_ref[...] = r_ref[...].astype(BF16)

    i0 = pl.multiple_of(e * PEER_KEYS_PER_STEP, PEER_KEYS_PER_STEP)

    def hidden(pp):
        rows = slice(pp * PIECE_ROWS, (pp + 1) * PIECE_ROWS)
        hid_ref[rows, :] = _bdot(u_ref[rows, :], ht_ref[...])

    def gates(pp):
        for c in range(PEER_TOKEN_TILE // LANES):
            ls = slice(c * LANES, (c + 1) * LANES)
            for ii in range(PIECE_KEYS):
                k = pp * PIECE_KEYS + ii
                acc = jnp.zeros((PEER_KEYS, LANES), BF16)
                for h in range(PEER_HEADS):
                    a_rows = a_ref[h, pl.ds(i0, PEER_KEYS_PER_STEP), ls]
                    n_rows = n_ref[h, pl.ds(i0, PEER_KEYS_PER_STEP), ls]
                    a16 = jnp.broadcast_to(a_rows[k:k + 1, :], (BF16_ROWS, LANES)).astype(BF16)
                    n16 = jnp.broadcast_to(n_rows[k:k + 1, :], (BF16_ROWS, LANES)).astype(BF16)
                    a128 = pltpu.repeat(a16, PEER_KEYS // BF16_ROWS, axis=0)
                    n128 = pltpu.repeat(n16, PEER_KEYS // BF16_ROWS, axis=0)
                    b = bp_ref[h, :, ls]
                    acc = acc + jnp.where(rp_ref[h, :, ls] < n128, b, jnp.zeros_like(b)) * a128
                rows = slice(k * PEER_KEYS, (k + 1) * PEER_KEYS)
                w_ref[rows, ls] = _gelu_tanh(hid_ref[rows, ls]).astype(BF16) * acc

    def project(pp):
        rows = slice(pp * PIECE_ROWS, (pp + 1) * PIECE_ROWS)
        acc_ref[...] += _bdot(vt_ref[0, :, rows], w_ref[rows, :])

    hidden(0)
    for pp in range(PEER_PIECES):
        if pp + 1 < PEER_PIECES:
            hidden(pp + 1)
        gates(pp)
        project(pp)

    @pl.when(e == pl.num_programs(1) - 1)
    def _():
        o_ref[...] = x_ref[...] + g2_ref[0] * acc_ref[...].T


def peer_dense(ht_bf, u_bf, vt_bf, a_t, n_t, b_t, r_t, x, g2, rows_per_seg):
    t = x.shape[0]
    tt = PEER_TOKEN_TILE
    grid = (t // tt, PEER_N // PEER_EXPERT_TILE)
    gate_spec = pl.BlockSpec((PEER_HEADS, PEER_KEYS, tt), lambda ti, ei: (0, 0, ti))
    return pl.pallas_call(
        _peer_dense_kernel, grid=grid,
        in_specs=[pl.BlockSpec((D_MODEL, tt), lambda ti, ei: (0, ti)),
                  pl.BlockSpec((PEER_EXPERT_TILE, D_MODEL), lambda ti, ei: (ei, 0)),
                  pl.BlockSpec((1, D_MODEL, PEER_EXPERT_TILE), lambda ti, ei: (ei, 0, 0)),
                  gate_spec, gate_spec, gate_spec, gate_spec,
                  pl.BlockSpec((tt, D_MODEL), lambda ti, ei: (ti, 0)),
                  _seg_spec(rows_per_seg, tt)],
        out_specs=pl.BlockSpec((tt, D_MODEL), lambda ti, ei: (ti, 0)),
        out_shape=jax.ShapeDtypeStruct((t, D_MODEL), F32),
        scratch_shapes=[pltpu.VMEM((D_MODEL, tt), F32),
                        pltpu.VMEM((PEER_EXPERT_TILE, tt), F32),
                        pltpu.VMEM((PEER_EXPERT_TILE, tt), BF16),
                        pltpu.VMEM((PEER_HEADS, PEER_KEYS, tt), BF16),
                        pltpu.VMEM((PEER_HEADS, PEER_KEYS, tt), BF16)],
        compiler_params=pltpu.CompilerParams(dimension_semantics=("parallel", "arbitrary"),
                                             vmem_limit_bytes=VMEM_LIMIT_BYTES),
        name="peer_dense",
    )(ht_bf, u_bf, vt_bf, a_t, n_t, b_t, r_t, x, g2)


def _fill_halo_scratch(scr, prev, cur, nxt, halo, ts):
    i = pl.program_id(1)
    scr[0:halo, :] = jnp.where(i > 0, prev, 0.0)
    scr[halo:halo + ts, :] = cur
    scr[halo + ts:halo + ts + halo, :] = jnp.where(i < pl.num_programs(1) - 1, nxt, 0.0)


def _depthwise_taps(scr, w_ref, taps, halo, r0, rows):
    off = halo - taps // 2
    acc = scr[off + r0:off + r0 + rows, :] * w_ref[0:1, :]
    for k in range(1, taps):
        acc = acc + scr[off + r0 + k:off + r0 + k + rows, :] * w_ref[k:k + 1, :]
    return acc


def _halo_specs(width, col_block, halo, ts, seq_len, total_rows):
    tiles = seq_len // ts
    per_tile = ts // halo
    last = total_rows // halo - 1
    prev = pl.BlockSpec((halo, width), lambda s, i: (jnp.maximum((s * tiles + i) * per_tile - 1, 0), col_block))
    cur = pl.BlockSpec((ts, width), lambda s, i: (s * tiles + i, col_block))
    nxt = pl.BlockSpec((halo, width), lambda s, i: (jnp.minimum((s * tiles + i + 1) * per_tile, last), col_block))
    return prev, cur, nxt


CONF_ROW_BLOCK = 32
QKV_ROW_BLOCK = 16


def _conformer_kernel(prev_ref, cur_ref, next_ref, w_ref, b_ref, lg_ref, lb_ref, o_ref, scr):
    ts = cur_ref.shape[0]
    glu = lambda blk: blk[:, :B_CH] * jax.nn.sigmoid(blk[:, B_CH:])
    _fill_halo_scratch(scr, glu(prev_ref[...]), glu(cur_ref[...]), glu(next_ref[...]), CONF_HALO, ts)
    for rb in range(ts // CONF_ROW_BLOCK):
        r0 = rb * CONF_ROW_BLOCK
        hh = _depthwise_taps(scr, w_ref, B_CONV, CONF_HALO, r0, CONF_ROW_BLOCK) + b_ref[...]
        mu = jnp.mean(hh, axis=-1, keepdims=True)
        var = jnp.mean(jnp.square(hh - mu), axis=-1, keepdims=True)
        y = (hh - mu) * lax.rsqrt(var + EPS) * lg_ref[...] + lb_ref[...]
        o_ref[r0:r0 + CONF_ROW_BLOCK, :] = (y * jax.nn.sigmoid(y)).astype(o_ref.dtype)


def conformer_branch(p, dw_w, dw_b, ln_g, ln_b, nseq, seq_len):
    t = p.shape[0]
    ts = CONV_ROW_TILE
    prev, cur, nxt = _halo_specs(2 * B_CH, COL_GLU // (2 * B_CH), CONF_HALO, ts, seq_len, t)
    row = lambda: pl.BlockSpec((1, B_CH), lambda s, i: (0, 0))
    w_pad = jnp.pad(dw_w, ((0, 32 - B_CONV), (0, 0)))
    return pl.pallas_call(
        _conformer_kernel, grid=(nseq, seq_len // ts),
        in_specs=[prev, cur, nxt, pl.BlockSpec((32, B_CH), lambda s, i: (0, 0)), row(), row(), row()],
        out_specs=pl.BlockSpec((ts, B_CH), lambda s, i: (s * (seq_len // ts) + i, 0)),
        out_shape=jax.ShapeDtypeStruct((t, B_CH), BF16),
        scratch_shapes=[pltpu.VMEM((ts + 2 * CONF_HALO, B_CH), F32)],
        compiler_params=pltpu.CompilerParams(dimension_semantics=("parallel", "parallel")),
        name="conformer_branch",
    )(p, p, p, w_pad, dw_b.reshape(1, B_CH), ln_g.reshape(1, B_CH), ln_b.reshape(1, B_CH))


def _qkv_conv_kernel(prev_ref, cur_ref, next_ref, w_ref, q_ref, k_ref, v_ref, scr):
    ts = cur_ref.shape[0]
    _fill_halo_scratch(scr, prev_ref[...], cur_ref[...], next_ref[...], QKV_HALO, ts)
    for rb in range(ts // QKV_ROW_BLOCK):
        r0 = rb * QKV_ROW_BLOCK
        rows = slice(r0, r0 + QKV_ROW_BLOCK)
        y = _depthwise_taps(scr, w_ref, SHORT_CONV, QKV_HALO, r0, QKV_ROW_BLOCK)
        y = y * jax.nn.sigmoid(y)
        for h in range(A_HEADS):
            cs = slice(h * A_DK, (h + 1) * A_DK)
            qh = y[:, h * A_DK:(h + 1) * A_DK]
            kh = y[:, A_DIM + h * A_DK:A_DIM + (h + 1) * A_DK]
            q_ref[rows, cs] = qh * (lax.rsqrt(jnp.sum(qh * qh, axis=-1, keepdims=True) + EPS) * (A_DK ** -0.5))
            k_ref[rows, cs] = kh * lax.rsqrt(jnp.sum(kh * kh, axis=-1, keepdims=True) + EPS)
        v_ref[rows, :] = y[:, 2 * A_DIM:]


def qkv_conv(p, conv_w, nseq, seq_len):
    t = p.shape[0]
    ts = CONV_ROW_TILE
    prev, cur, nxt = _halo_specs(3 * A_DIM, 0, QKV_HALO, ts, seq_len, t)
    out = pl.BlockSpec((ts, A_DIM), lambda s, i: (s * (seq_len // ts) + i, 0))
    w_pad = jnp.pad(conv_w, ((0, 8 - SHORT_CONV), (0, 0)))
    return pl.pallas_call(
        _qkv_conv_kernel, grid=(nseq, seq_len // ts),
        in_specs=[prev, cur, nxt, pl.BlockSpec((8, 3 * A_DIM), lambda s, i: (0, 0))],
        out_specs=[out, out, out],
        out_shape=[jax.ShapeDtypeStruct((t, A_DIM), F32)] * 3,
        scratch_shapes=[pltpu.VMEM((ts + 2 * QKV_HALO, 3 * A_DIM), F32)],
        compiler_params=pltpu.CompilerParams(dimension_semantics=("parallel", "parallel")),
        name="qkv_conv",
    )(p, p, p, w_pad)


def _mix_out_kernel(of_ref, ob_ref, z_ref, conf_ref, ng_ref, w_ref, x_ref, gate_ref, o_ref):
    o = of_ref[...] + ob_ref[...]
    z = z_ref[...]
    parts = []
    for h in range(A_HEADS):
        cs = slice(h * A_DV, (h + 1) * A_DV)
        oh = o[:, cs]
        zh = z[:, cs]
        scale = lax.rsqrt(jnp.mean(oh * oh, axis=-1, keepdims=True) + EPS)
        parts.append((oh * scale * ng_ref[...] * (zh * jax.nn.sigmoid(zh))).astype(BF16))
    oa = jnp.concatenate(parts, axis=1)
    mix = _bdot(oa, w_ref[:A_DIM, :]) + _bdot(conf_ref[...], w_ref[A_DIM:, :])
    o_ref[...] = x_ref[...] + gate_ref[0] * mix


def mix_out(o_f, o_b, p, conf, norm_g, w_bf, x, gate, rows_per_seg):
    t = x.shape[0]
    tm = MM_ROW_TILE
    half = lambda cb: pl.BlockSpec((tm, A_DIM), lambda i: (i, cb))
    return pl.pallas_call(
        _mix_out_kernel, grid=(t // tm,),
        in_specs=[half(0), half(0), half(COL_Z // A_DIM), half(0), pl.BlockSpec((1, A_DV), lambda i: (0, 0)),
                  pl.BlockSpec((D_MODEL, D_MODEL), lambda i: (0, 0)), pl.BlockSpec((tm, D_MODEL), lambda i: (i, 0)),
                  _seg_spec(rows_per_seg, tm)],
        out_specs=pl.BlockSpec((tm, D_MODEL), lambda i: (i, 0)),
        out_shape=jax.ShapeDtypeStruct((t, D_MODEL), F32),
        compiler_params=pltpu.CompilerParams(dimension_semantics=("parallel",),
                                             vmem_limit_bytes=VMEM_LIMIT_BYTES),
        name="mix_out",
    )(o_f, o_b, p, conf, norm_g.reshape(1, A_DV), w_bf, x, gate)


def grid_pos_emb(n_tokens):
    rows = n_tokens // GRID_W
    r = jnp.repeat(jnp.arange(rows, dtype=F32), GRID_W)
    col = jnp.tile(jnp.arange(GRID_W, dtype=F32), rows)
    nf = D_MODEL // 4
    freqs = jnp.exp(-math.log(POS_BASE) * jnp.arange(nf, dtype=F32) / nf)
    ar = r[:, None] * freqs
    ac = col[:, None] * freqs
    return jnp.concatenate([jnp.sin(ar), jnp.cos(ar), jnp.sin(ac), jnp.cos(ac)], axis=-1)


def delta_conformer_layer(x, p, g1, nseq, seq_len, s0, e, prm, rows_per_seg):
    t = p.shape[0]
    seq = lambda m: m.reshape(nseq, seq_len, m.shape[-1])
    q, k, v = qkv_conv(p, prm['conv_qkv_w'][e], nseq, seq_len)
    alpha = p[:, COL_AB:COL_AB + 2 * A_HEADS]
    beta = jax.nn.sigmoid(p[:, COL_AB + 2 * A_HEADS:COL_AB + 4 * A_HEADS])
    log_g = (-jnp.exp(prm['a_log'][e]).reshape(1, 2 * A_HEADS)
             * jax.nn.softplus(alpha + prm['dt_bias'][e].reshape(1, 2 * A_HEADS)))
    o_f, o_b, st = delta_scan(*delta_prep(seq(q), seq(k), seq(v), seq(log_g), seq(beta)), s0)
    conf = conformer_branch(p, prm['conf_dw_w'][e], prm['conf_dw_b'][e], prm['conf_ln_g'][e],
                            prm['conf_ln_b'][e], nseq, seq_len)
    x = mix_out(o_f.reshape(t, A_DIM), o_b.reshape(t, A_DIM), p, conf, prm['delta_norm_g'][e],
                prm['w_out_bf'][e], x, g1, rows_per_seg)
    return x, st


def trunk(x, nseq, seq_len, cond, s0, prm):
    t = x.shape[0]
    rows_per_seg = t // cond.shape[0]
    states = []
    for l in range(DEPTH):
        mod = jax.nn.silu(cond) @ prm['ada_w'][l] + prm['ada_b'][l]
        sh1, sc1, g1, sh2, sc2, g2 = [m[:, None, :] for m in jnp.split(mod, 6, axis=-1)]
        e = l // 2
        if l % 2 == 0:
            p = norm_mm(x, prm['norm1_g'][l], sc1, sh1, prm['w_in_bf'][e], rows_per_seg, F32)
            x, st = delta_conformer_layer(x, p, g1, nseq, seq_len, s0[:, e], e, prm, rows_per_seg)
            states.append(st)
        else:
            z = norm_mm(x, prm['norm1_g'][l], sc1, sh1, prm['w_fnet_bf'][e], rows_per_seg, BF16)
            tm = min(seq_len, 512)
            x = seq_mix_res(prm['dft_seq'][seq_len], z, x, g1, seq_len, tm, min(seq_len, 1024))
        ht, a_t, n_t, b_t, r_t = peer_route(x, prm['norm2_g'][l], sc2, sh2, prm['peer_wq_hi'][l],
                                            prm['peer_wq_lo'][l], prm['peer_k1'][l], prm['peer_k2'][l],
                                            rows_per_seg)
        x = peer_dense(ht, prm['peer_u_bf'][l], prm['peer_vt_bf'][l], a_t, n_t, b_t, r_t, x, g2, rows_per_seg)
    xf = x * lax.rsqrt(jnp.mean(x * x, axis=-1, keepdims=True) + EPS) * prm['final_norm_g']
    return xf, jnp.stack(states, axis=1)


def kernel(x_prompt, x_sample, state_delta, c, c_ctx, ada_w, ada_b, norm1_g, norm2_g, w_in_ab, conv_qkv_w,
           a_log, dt_bias, delta_norm_g, conf_dw_w, conf_dw_b, conf_ln_g, conf_ln_b, w_out_ab, w_out_c,
           peer_wq, peer_k1, peer_k2, peer_u, peer_v, final_norm_g):
    bp, sp, _ = x_prompt.shape
    bs, ss, _ = x_sample.shape
    bdc, bds = dft_group_matrices(D_MODEL // C_GROUPS, C_GROUPS)
    w_fnet = [jnp.concatenate([mm3(bdc, w_out_c[e], BF16), mm3(bds, w_out_c[e], BF16)], axis=1)
              for e in range(DEPTH // 2)]
    o4 = 4 * A_DIM
    w_in = jnp.concatenate([w_in_ab[:, :, :o4], w_in_ab[:, :, o4 + 4 * A_HEADS:], w_in_ab[:, :, o4:o4 + 4 * A_HEADS]],
                           axis=-1).astype(BF16)
    wq_hi, wq_lo = split_bf16(peer_wq)
    prm = {'ada_w': ada_w, 'ada_b': ada_b, 'norm1_g': norm1_g, 'norm2_g': norm2_g,
           'w_in_bf': jnp.pad(w_in, ((0, 0), (0, 0), (0, P_AB_PAD - P_AB))),
           'conv_qkv_w': conv_qkv_w, 'a_log': a_log, 'dt_bias': dt_bias,
           'delta_norm_g': delta_norm_g, 'conf_dw_w': conf_dw_w, 'conf_dw_b': conf_dw_b,
           'conf_ln_g': conf_ln_g, 'conf_ln_b': conf_ln_b, 'w_out_bf': w_out_ab.astype(BF16),
           'w_fnet_bf': w_fnet, 'dft_seq': {s: dft_seq_matrix(s) for s in {sp, ss}},
           'peer_wq_hi': wq_hi, 'peer_wq_lo': wq_lo, 'peer_k1': peer_k1, 'peer_k2': peer_k2,
           'peer_u_bf': peer_u.astype(BF16),
           'peer_vt_bf': jnp.transpose(peer_v.astype(BF16).reshape(DEPTH, PEER_N // PEER_EXPERT_TILE,
                                                                   PEER_EXPERT_TILE, D_MODEL), (0, 1, 3, 2)),
           'final_norm_g': final_norm_g}
    ne = (DEPTH + 1) // 2
    s0_ctx = jnp.zeros((bp, ne, 2, A_HEADS, A_DK, A_DV), F32)
    y_prompt, ctx_states = trunk(x_prompt.reshape(bp * sp, D_MODEL), bp, sp, c_ctx[None, :], s0_ctx, prm)
    xs = (x_sample + grid_pos_emb(ss)[None]).reshape(bs * ss, D_MODEL)
    y_sample, _ = trunk(xs, bs, ss, c, state_delta, prm)
    return (y_prompt.reshape(bp, sp, D_MODEL), y_sample.reshape(bs, ss, D_MODEL), ctx_states)
```

```python
import math

import jax
import jax.numpy as jnp
import numpy as np
from jax import lax
from jax.experimental import pallas as pl
from jax.experimental.pallas import tpu as pltpu

D_MODEL = 1024
DEPTH = 4
GRID_W = 64
POS_BASE = 10000.0
EPS = 1e-6
A_HEADS = 4
A_DK = 128
A_DV = 128
A_DIM = A_HEADS * A_DV
CHUNK = 64
B_CH = D_MODEL // 2
P_AB = 4 * A_DIM + 4 * A_HEADS + 2 * B_CH
C_GROUPS = 8
PEER_HEADS = 8
PEER_KEYS = 128
PEER_N = PEER_KEYS * PEER_KEYS
PEER_DK = 128
PEER_TOPK = 16

F32 = jnp.float32
BF16 = jnp.bfloat16
NEG_INF = float("-inf")

LANES = 128
VMEM_LIMIT_BYTES = 56 * 1024 * 1024
MM_ROW_TILE = 512
ROUTE_TOKEN_TILE = 256
PEER_TOKEN_TILE = 512
PEER_KEYS_PER_STEP = 8
PEER_EXPERT_TILE = PEER_KEYS_PER_STEP * PEER_KEYS
PEER_J_BLOCK = 32
PREP_CHUNKS = 2
SCAN_SEQS = 2
P_AB_PAD = 3200
COL_Z = 3 * A_DIM
COL_GLU = 4 * A_DIM
COL_AB = 4 * A_DIM + 2 * B_CH
CONV_ROW_TILE = 256
SHORT_CONV = 7
B_CONV = 31
CONF_HALO = 16
QKV_HALO = 8
PEER_PIECES = 4
PIECE_KEYS = PEER_KEYS_PER_STEP // PEER_PIECES
PIECE_ROWS = PIECE_KEYS * PEER_KEYS
BF16_ROWS = 16

NT_DIMS = (((1,), (1,)), ((), ()))
TN_DIMS = (((0,), (0,)), ((), ()))


def _bdot(a, b):
    return jnp.dot(a, b, preferred_element_type=F32)


def _split_bf16(a):
    hi = a.astype(BF16)
    lo = (a - hi.astype(F32)).astype(BF16)
    return hi, lo


def _dot3(a, b):
    ah, al = _split_bf16(a)
    bh, bl = _split_bf16(b)
    return _bdot(ah, bh) + (_bdot(ah, bl) + _bdot(al, bh))


def _dot_exact_lhs(a01, b):
    a = a01.astype(BF16)
    bh = b.astype(BF16)
    r1 = b - bh.astype(F32)
    bm = r1.astype(BF16)
    bl = (r1 - bm.astype(F32)).astype(BF16)
    return _bdot(a, bh) + (_bdot(a, bm) + _bdot(a, bl))


def _gelu_tanh(x):
    return 0.5 * x * (1.0 + jnp.tanh(math.sqrt(2.0 / math.pi) * (x + 0.044715 * (x * x * x))))


def _seg_spec(rows_per_seg, tile):
    per = rows_per_seg // tile
    return pl.BlockSpec((1, 1, D_MODEL), lambda i, *_: (i // per, 0, 0))


def _norm_modulate(x, g, sc, sh):
    hn = x * lax.rsqrt(jnp.mean(x * x, axis=-1, keepdims=True) + EPS) * g
    return hn * (1.0 + sc) + sh


def _norm_mm_kernel(x_ref, g_ref, sc_ref, sh_ref, w_ref, o_ref):
    h = _norm_modulate(x_ref[...], g_ref[...], sc_ref[0], sh_ref[0]).astype(BF16)
    o_ref[...] = _bdot(h, w_ref[...]).astype(o_ref.dtype)


def norm_mm(x, norm_g, sc, sh, w_bf, rows_per_seg, out_dtype):
    m = x.shape[0]
    n = w_bf.shape[1]
    tm = MM_ROW_TILE
    seg = _seg_spec(rows_per_seg, tm)
    return pl.pallas_call(
        _norm_mm_kernel, grid=(m // tm,),
        in_specs=[pl.BlockSpec((tm, D_MODEL), lambda i: (i, 0)), pl.BlockSpec((1, D_MODEL), lambda i: (0, 0)),
                  seg, seg, pl.BlockSpec((D_MODEL, n), lambda i: (0, 0))],
        out_specs=pl.BlockSpec((tm, n), lambda i: (i, 0)),
        out_shape=jax.ShapeDtypeStruct((m, n), out_dtype),
        compiler_params=pltpu.CompilerParams(dimension_semantics=("parallel",),
                                             vmem_limit_bytes=VMEM_LIMIT_BYTES),
        name="norm_mm",
    )(x, norm_g.reshape(1, D_MODEL), sc, sh, w_bf)


def _split_kernel(a_ref, hi_ref, lo_ref):
    hi, lo = _split_bf16(a_ref[0])
    hi_ref[0] = hi
    lo_ref[0] = lo


def split_bf16(a):
    nl, nr, nc = a.shape
    blk = pl.BlockSpec((1, MM_ROW_TILE, nc), lambda l, i: (l, i, 0))
    return pl.pallas_call(
        _split_kernel, grid=(nl, nr // MM_ROW_TILE), in_specs=[blk], out_specs=[blk, blk],
        out_shape=[jax.ShapeDtypeStruct(a.shape, BF16)] * 2,
        compiler_params=pltpu.CompilerParams(dimension_semantics=("parallel", "parallel")),
        name="split_bf16",
    )(a)


def _mm3_kernel(a_ref, b_ref, o_ref):
    o_ref[...] = _dot3(a_ref[...], b_ref[...]).astype(o_ref.dtype)


def mm3(a, b, out_dtype):
    return pl.pallas_call(_mm3_kernel, out_shape=jax.ShapeDtypeStruct((a.shape[0], b.shape[1]), out_dtype),
                          compiler_params=pltpu.CompilerParams(vmem_limit_bytes=VMEM_LIMIT_BYTES),
                          name="mm3")(a, b)


def _seqmix_kernel(f_ref, z_ref, x_ref, gate_ref, o_ref, acc_ref):
    k = pl.program_id(2)

    @pl.when(k == 0)
    def _():
        acc_ref[...] = jnp.zeros_like(acc_ref)

    acc_ref[...] += _bdot(f_ref[...], z_ref[...])

    @pl.when(k == pl.num_programs(2) - 1)
    def _():
        o_ref[...] = x_ref[...] + gate_ref[0] * acc_ref[...]


def seq_mix_res(fmat, z, x, gate, seq_len, tm, tk):
    t = x.shape[0]
    nseq = t // seq_len
    seqs_per_seg = nseq // gate.shape[0]
    mt = seq_len // tm
    kt_half = seq_len // tk
    return pl.pallas_call(
        _seqmix_kernel, grid=(nseq, mt, 2 * kt_half),
        in_specs=[pl.BlockSpec((tm, tk), lambda s, i, k: (i, k)),
                  pl.BlockSpec((tk, D_MODEL), lambda s, i, k: (s * kt_half + k % kt_half, k // kt_half)),
                  pl.BlockSpec((tm, D_MODEL), lambda s, i, k: (s * mt + i, 0)),
                  pl.BlockSpec((1, 1, D_MODEL), lambda s, i, k: (s // seqs_per_seg, 0, 0))],
        out_specs=pl.BlockSpec((tm, D_MODEL), lambda s, i, k: (s * mt + i, 0)),
        out_shape=jax.ShapeDtypeStruct((t, D_MODEL), F32),
        scratch_shapes=[pltpu.VMEM((tm, D_MODEL), F32)],
        compiler_params=pltpu.CompilerParams(dimension_semantics=("parallel", "parallel", "arbitrary"),
                                             vmem_limit_bytes=VMEM_LIMIT_BYTES),
        name="seq_mix_res",
    )(fmat, z, x, gate)


def _dft_tables(n, cols):
    r = jnp.arange(n, dtype=jnp.int32)[:, None]
    ang = ((r * cols[None, :]) % n).astype(F32) * (2.0 * math.pi / n)
    return jnp.cos(ang), jnp.sin(ang)


def dft_seq_matrix(s):
    w = 1 << (int(math.log2(s)) // 2)
    ch, sh_ = _dft_tables(s, jnp.arange(s // w, dtype=jnp.int32) * w)
    cl, sl = _dft_tables(s, jnp.arange(w, dtype=jnp.int32))
    sc = 1.0 / math.sqrt(s)
    c = (ch[:, :, None] * cl[:, None, :] - sh_[:, :, None] * sl[:, None, :]).reshape(s, s) * sc
    sn = (sh_[:, :, None] * cl[:, None, :] + ch[:, :, None] * sl[:, None, :]).reshape(s, s) * sc
    return jnp.concatenate([c, -sn], axis=1).astype(BF16)


def dft_group_matrices(n, groups):
    c, s = _dft_tables(n, jnp.arange(n, dtype=jnp.int32))
    sc = 1.0 / math.sqrt(n)
    eye = jnp.eye(groups, dtype=F32)
    return jnp.kron(eye, c * sc), jnp.kron(eye, s * sc)


def _delta_prep_kernel(q_ref, k_ref, v_ref, lg_ref, bt_ref, w_ref, u_ref, qd_ref, kd_ref, p_ref, g_ref):
    r = lax.broadcasted_iota(jnp.int32, (CHUNK, CHUNK), 0)
    c = lax.broadcasted_iota(jnp.int32, (CHUNK, CHUNK), 1)
    eye = (r == c).astype(F32)
    ones = jnp.ones((CHUNK, CHUNK), F32)
    incl = (r >= c, r <= c)
    strict = (r > c, r < c)
    tri = (incl[0].astype(F32), incl[1].astype(F32))
    tri_t = (tri[1], tri[0])
    last = (CHUNK - 1, 0)
    chains = [(cg, d, h) for cg in range(PREP_CHUNKS) for d in range(2) for h in range(A_HEADS)]
    rows = lambda cg: slice(cg * CHUNK, (cg + 1) * CHUNK)
    cols = lambda h: slice(h * A_DK, (h + 1) * A_DK)
    kk = {}
    qk = {}
    for cg in range(PREP_CHUNKS):
        for h in range(A_HEADS):
            kb = k_ref[0, rows(cg), cols(h)].astype(BF16)
            kk[cg, h] = lax.dot_general(kb, kb, NT_DIMS, preferred_element_type=F32)
            qk[cg, h] = lax.dot_general(q_ref[0, rows(cg), cols(h)].astype(BF16), kb, NT_DIMS,
                                        preferred_element_type=F32)
    lgw = {}
    btw = {}
    for (cg, d, h) in chains:
        col = d * A_HEADS + h
        lgw[cg, d, h] = jnp.broadcast_to(lg_ref[0, rows(cg), col:col + 1], (CHUNK, A_DK))
        btw[cg, d, h] = jnp.broadcast_to(bt_ref[0, rows(cg), col:col + 1], (CHUNK, A_DK))
    gam = {ch: _dot_exact_lhs(tri[ch[1]], lgw[ch]) for ch in chains}
    gam_row = {ch: _dot_exact_lhs(ones, lgw[ch][:, :CHUNK] * tri_t[ch[1]]) for ch in chains}
    decay = {}
    lmat = {}
    for ch in chains:
        cg, d, h = ch
        diff = gam[ch][:, :CHUNK] - gam_row[ch]
        decay[ch] = jnp.where(incl[d], jnp.exp(jnp.where(incl[d], diff, 0.0)), 0.0)
        lmat[ch] = jnp.where(strict[d], btw[ch][:, :CHUNK] * decay[ch] * kk[cg, h], 0.0)
    pinv = {ch: eye - jnp.where((r // 2 == c // 2) & (r != c), lmat[ch], 0.0) for ch in chains}
    s = 2
    while s < CHUNK:
        join = (r // (2 * s) == c // (2 * s)) & (r // s != c // s)
        tc = {ch: _dot3(pinv[ch], jnp.where(join, lmat[ch], 0.0)) for ch in chains}
        pinv = {ch: pinv[ch] - _dot3(tc[ch], pinv[ch]) for ch in chains}
        s *= 2
    for ch in chains:
        cg, d, h = ch
        kh = k_ref[0, rows(cg), cols(h)]
        vh = v_ref[0, rows(cg), cols(h)]
        qh = q_ref[0, rows(cg), cols(h)]
        egam = jnp.exp(gam[ch])
        rhs = jnp.concatenate([kh * (btw[ch] * egam), vh * btw[ch]], axis=1)
        sol = _dot3(pinv[ch], rhs)
        w_ref[0, d, rows(cg), cols(h)] = sol[:, :A_DK].astype(BF16)
        u_ref[0, d, rows(cg), cols(h)] = sol[:, A_DK:]
        qd_ref[0, d, rows(cg), cols(h)] = (qh * egam).astype(BF16)
        glast = jnp.broadcast_to(gam[ch][last[d]:last[d] + 1, :], (CHUNK, A_DK))
        kd_ref[0, d, rows(cg), cols(h)] = (kh * jnp.exp(glast - gam[ch])).astype(BF16)
        g_ref[0, d, rows(cg), cols(h)] = jnp.exp(glast)
        p_ref[0, d, rows(cg), h * CHUNK:(h + 1) * CHUNK] = (decay[ch] * qk[cg, h]).astype(BF16)


def delta_prep(q, k, v, lg, bt):
    b, s, _ = q.shape
    rt = PREP_CHUNKS * CHUNK
    blk = lambda w: pl.BlockSpec((1, rt, w), lambda bi, ni: (bi, ni, 0))
    oblk = lambda w: pl.BlockSpec((1, 2, rt, w), lambda bi, ni: (bi, 0, ni, 0))
    sh = lambda w, dt: jax.ShapeDtypeStruct((b, 2, s, w), dt)
    return pl.pallas_call(
        _delta_prep_kernel, grid=(b, s // rt),
        in_specs=[blk(A_DIM), blk(A_DIM), blk(A_DIM), blk(2 * A_HEADS), blk(2 * A_HEADS)],
        out_specs=[oblk(A_DIM), oblk(A_DIM), oblk(A_DIM), oblk(A_DIM), oblk(A_HEADS * CHUNK), oblk(A_DIM)],
        out_shape=[sh(A_DIM, BF16), sh(A_DIM, F32), sh(A_DIM, BF16), sh(A_DIM, BF16),
                   sh(A_HEADS * CHUNK, BF16), sh(A_DIM, F32)],
        compiler_params=pltpu.CompilerParams(dimension_semantics=("parallel", "parallel")),
        name="delta_prep",
    )(q, k, v, lg, bt)


def _delta_scan_kernel(*refs):
    ins = refs[:12]
    s0_ref = refs[12]
    of_ref, ob_ref, sout_ref, state = refs[13:]
    n = pl.program_id(1)

    @pl.when(n == 0)
    def _():
        state[...] = s0_ref[...]

    outs = (of_ref, ob_ref)
    chains = [(g, d, h) for g in range(SCAN_SEQS) for d in range(2) for h in range(A_HEADS)]
    cs = lambda h: slice(h * A_DK, (h + 1) * A_DK)
    ref = lambda d, i: ins[d * 6 + i]
    s_old = {ch: state[ch] for ch in chains}
    wqs = {}
    for (g, d, h) in chains:
        wq = jnp.concatenate([ref(d, 0)[g, 0, :, cs(h)], ref(d, 2)[g, 0, :, cs(h)]], axis=0)
        wqs[g, d, h] = _bdot(wq, s_old[g, d, h].astype(BF16))
    unb = {}
    for (g, d, h) in chains:
        unb[g, d, h] = (ref(d, 1)[g, 0, :, cs(h)] - wqs[g, d, h][:CHUNK]).astype(BF16)
    for (g, d, h) in chains:
        o = wqs[g, d, h][CHUNK:] + _bdot(ref(d, 4)[g, 0, :, h * CHUNK:(h + 1) * CHUNK], unb[g, d, h])
        outs[d][g, :, cs(h)] = o
    for (g, d, h) in chains:
        upd = lax.dot_general(ref(d, 3)[g, 0, :, cs(h)], unb[g, d, h], TN_DIMS, preferred_element_type=F32)
        gs = jnp.broadcast_to(ref(d, 5)[g, 0, 0:1, cs(h)], (A_DK, A_DV))
        state[g, d, h] = gs * s_old[g, d, h] + upd

    @pl.when(n == pl.num_programs(1) - 1)
    def _():
        sout_ref[...] = state[...]


def delta_scan(w, u, qd, kd, p, gl, s0):
    b, _, s, _ = u.shape
    n = s // CHUNK

    def spec(wd, d):
        if d == 0:
            return pl.BlockSpec((SCAN_SEQS, 1, CHUNK, wd), lambda bi, ni: (bi, 0, ni, 0))
        return pl.BlockSpec((SCAN_SEQS, 1, CHUNK, wd), lambda bi, ni: (bi, 1, n - 1 - ni, 0))

    arrs = (w, u, qd, kd, p, gl)
    in_specs = [spec(a.shape[-1], d) for d in range(2) for a in arrs]
    st = pl.BlockSpec((SCAN_SEQS, 2, A_HEADS, A_DK, A_DV), lambda bi, ni: (bi, 0, 0, 0, 0))
    of = pl.BlockSpec((SCAN_SEQS, CHUNK, A_DIM), lambda bi, ni: (bi, ni, 0))
    ob = pl.BlockSpec((SCAN_SEQS, CHUNK, A_DIM), lambda bi, ni: (bi, n - 1 - ni, 0))
    return pl.pallas_call(
        _delta_scan_kernel, grid=(b // SCAN_SEQS, n),
        in_specs=in_specs + [st],
        out_specs=[of, ob, st],
        out_shape=[jax.ShapeDtypeStruct((b, s, A_DIM), F32), jax.ShapeDtypeStruct((b, s, A_DIM), F32),
                   jax.ShapeDtypeStruct((b, 2, A_HEADS, A_DK, A_DV), F32)],
        scratch_shapes=[pltpu.VMEM((SCAN_SEQS, 2, A_HEADS, A_DK, A_DV), F32)],
        compiler_params=pltpu.CompilerParams(dimension_semantics=("parallel", "arbitrary")),
        name="delta_scan",
    )(*(arrs + arrs), s0)


def _cand_tables():
    pairs = [(r, c) for r in range(PEER_TOPK) for c in range(PEER_TOPK) if (r + 1) * (c + 1) <= PEER_TOPK]
    npad = 64
    e1 = np.zeros((npad, PEER_TOPK), np.float32)
    e2 = np.zeros((npad, PEER_TOPK), np.float32)
    m = np.zeros((PEER_TOPK, npad), np.float32)
    for k, (r, c) in enumerate(pairs):
        e1[k, r] = 1
        e2[k, c] = 1
        m[r, k] = 1
    return len(pairs), e1, e2, m


N_CAND, _CAND_E1, _CAND_E2, _CAND_ROW = _cand_tables()


def _extract_topk(s, n_iter):
    k, t = s.shape
    work = s.reshape(k // 8, 8, t)
    rank = jnp.full(work.shape, float(n_iter), F32)
    vals = []
    for r in range(n_iter):
        m = jnp.max(jnp.max(work, axis=0), axis=0, keepdims=True)
        hit = work == jnp.broadcast_to(m, (8, t))[None]
        rank = jnp.where(hit, float(r), rank)
        work = jnp.where(hit, NEG_INF, work)
        vals.append(m)
    return vals, rank.reshape(k, t)


def _route_kernel(x_ref, g_ref, sc_ref, sh_ref, wqh_ref, wql_ref, k1_ref, k2_ref, e1_ref, e2_ref, mrow_ref,
                  ht_ref, a_ref, n_ref, b_ref, r_ref, q_scr):
    hmod = _norm_modulate(x_ref[...], g_ref[...], sc_ref[0], sh_ref[0])
    ht_ref[...] = hmod.T.astype(BF16)
    hh, hl = _split_bf16(hmod)
    q_scr[...] = _bdot(hh, wqh_ref[...]) + (_bdot(hh, wql_ref[...]) + _bdot(hl, wqh_ref[...]))
    tt = x_ref.shape[0]

    def head(h, carry):
        c1 = pl.ds(pl.multiple_of(h * 2 * PEER_DK, PEER_DK), PEER_DK)
        c2 = pl.ds(pl.multiple_of(h * 2 * PEER_DK + PEER_DK, PEER_DK), PEER_DK)
        hp = lax.Precision.HIGHEST
        s1 = lax.dot_general(k1_ref[h], q_scr[:, c1], NT_DIMS, precision=hp, preferred_element_type=F32)
        s2 = lax.dot_general(k2_ref[h], q_scr[:, c2], NT_DIMS, precision=hp, preferred_element_type=F32)
        v1, rank1 = _extract_topk(s1, PEER_TOPK)
        v2, rank2 = _extract_topk(s2, PEER_TOPK)
        v1m = jnp.concatenate(v1, axis=0)
        v2m = jnp.concatenate(v2, axis=0)
        cand = (jnp.dot(e1_ref[...], v1m, precision=hp, preferred_element_type=F32)
                + jnp.dot(e2_ref[...], v2m, precision=hp, preferred_element_type=F32))
        row = lax.broadcasted_iota(jnp.int32, cand.shape, 0)
        cand = jnp.where(row < N_CAND, cand, NEG_INF)
        _, crank = _extract_topk(cand, PEER_TOPK)
        sel = crank < float(PEER_TOPK)
        cmax = v1[0] + v2[0]
        z = jnp.sum(jnp.where(sel, jnp.exp(cand - cmax), 0.0), axis=0, keepdims=True)
        n_r = _bdot(mrow_ref[...], jnp.where(sel, 1.0, 0.0).astype(BF16))
        rank1_3 = rank1.reshape(PEER_KEYS // 8, 8, tt)
        nn = jnp.zeros_like(rank1_3)
        for r in range(PEER_TOPK):
            nn = jnp.where(rank1_3 == float(r), jnp.broadcast_to(n_r[r:r + 1, :], (8, tt))[None], nn)
        a_ref[h] = jnp.exp(s1 - v1[0]) / z
        n_ref[h] = nn.reshape(PEER_KEYS, tt)
        b_ref[h] = jnp.where(rank2 < float(PEER_TOPK), jnp.exp(s2 - v2[0]), 0.0)
        r_ref[h] = rank2
        return carry

    lax.fori_loop(0, PEER_HEADS, head, 0, unroll=2)


def peer_route(x, norm_g, sc, sh, wq_hi, wq_lo, k1, k2, rows_per_seg):
    t = x.shape[0]
    tt = ROUTE_TOKEN_TILE
    gate = pl.BlockSpec((PEER_HEADS, PEER_KEYS, tt), lambda i: (0, 0, i))
    full = lambda shp: pl.BlockSpec(shp, lambda i: (0,) * len(shp))
    seg = _seg_spec(rows_per_seg, tt)
    return pl.pallas_call(
        _route_kernel, grid=(t // tt,),
        in_specs=[pl.BlockSpec((tt, D_MODEL), lambda i: (i, 0)), full((1, D_MODEL)), seg, seg,
                  full((D_MODEL, 2 * PEER_HEADS * PEER_DK)), full((D_MODEL, 2 * PEER_HEADS * PEER_DK)),
                  full((PEER_HEADS, PEER_KEYS, PEER_DK)),
                  full((PEER_HEADS, PEER_KEYS, PEER_DK)), full((64, PEER_TOPK)), full((64, PEER_TOPK)),
                  full((PEER_TOPK, 64))],
        out_specs=[pl.BlockSpec((D_MODEL, tt), lambda i: (0, i)), gate, gate, gate, gate],
        out_shape=[jax.ShapeDtypeStruct((D_MODEL, t), BF16)]
        + [jax.ShapeDtypeStruct((PEER_HEADS, PEER_KEYS, t), F32)] * 4,
        scratch_shapes=[pltpu.VMEM((tt, 2 * PEER_HEADS * PEER_DK), F32)],
        compiler_params=pltpu.CompilerParams(dimension_semantics=("parallel",),
                                             vmem_limit_bytes=VMEM_LIMIT_BYTES),
        name="peer_route",
    )(x, norm_g.reshape(1, D_MODEL), sc, sh, wq_hi, wq_lo, k1, k2,
      jnp.asarray(_CAND_E1), jnp.asarray(_CAND_E2), jnp.asarray(_CAND_ROW, BF16))


def _peer_dense_kernel(ht_ref, u_ref, vt_ref, a_ref, n_ref, b_ref, r_ref, x_ref, g2_ref, o_ref,
                       acc_ref, hid_ref, w_ref, bp_ref, rp_ref):
    e = pl.program_id(1)

    @pl.when(e == 0)
    def _():
        acc_ref[...] = jnp.zeros_like(acc_ref)
        for c in range(PEER_TOKEN_TILE // LANES):
            ls = slice(c * LANES, (c + 1) * LANES)
            bp_ref[:, c] = b_ref[:, :, ls].astype(BF16)
            rp_ref[:, c] = r_ref[:, :, ls].astype(BF16)

    i0 = pl.multiple_of(e * PEER_KEYS_PER_STEP, PEER_KEYS_PER_STEP)

    def hidden(pp):
        rows = slice(pp * PIECE_ROWS, (pp + 1) * PIECE_ROWS)
        hid_ref[rows, :] = _bdot(u_ref[rows, :], ht_ref[...])

    def gates(pp):
        for c in range(PEER_TOKEN_TILE // LANES):
            ls = slice(c * LANES, (c + 1) * LANES)
            for ii in range(PIECE_KEYS):
                k = pp * PIECE_KEYS + ii
                acc = jnp.zeros((PEER_KEYS, LANES), BF16)
                for h in range(PEER_HEADS):
                    a_rows = a_ref[h, pl.ds(i0, PEER_KEYS_PER_STEP), ls]
                    n_rows = n_ref[h, pl.ds(i0, PEER_KEYS_PER_STEP), ls]
                    a16 = jnp.broadcast_to(a_rows[k:k + 1, :], (BF16_ROWS, LANES)).astype(BF16)
                    n16 = jnp.broadcast_to(n_rows[k:k + 1, :], (BF16_ROWS, LANES)).astype(BF16)
                    a128 = pltpu.repeat(a16, PEER_KEYS // BF16_ROWS, axis=0)
                    n128 = pltpu.repeat(n16, PEER_KEYS // BF16_ROWS, axis=0)
                    b = bp_ref[h, c]
                    acc = acc + jnp.where(rp_ref[h, c] < n128, b, jnp.zeros_like(b)) * a128
                rows = slice(k * PEER_KEYS, (k + 1) * PEER_KEYS)
                w_ref[rows, ls] = _gelu_tanh(hid_ref[rows, ls]).astype(BF16) * acc

    def project(pp):
        rows = slice(pp * PIECE_ROWS, (pp + 1) * PIECE_ROWS)
        acc_ref[...] += _bdot(vt_ref[0, :, rows], w_ref[rows, :])

    hidden(0)
    for pp in range(PEER_PIECES):
        if pp + 1 < PEER_PIECES:
            hidden(pp + 1)
        gates(pp)
        project(pp)

    @pl.when(e == pl.num_programs(1) - 1)
    def _():
        o_ref[...] = x_ref[...] + g2_ref[0] * acc_ref[...].T


def peer_dense(ht_bf, u_bf, vt_bf, a_t, n_t, b_t, r_t, x, g2, rows_per_seg):
    t = x.shape[0]
    tt = PEER_TOKEN_TILE
    grid = (t // tt, PEER_N // PEER_EXPERT_TILE)
    gate_spec = pl.BlockSpec((PEER_HEADS, PEER_KEYS, tt), lambda ti, ei: (0, 0, ti))
    return pl.pallas_call(
        _peer_dense_kernel, grid=grid,
        in_specs=[pl.BlockSpec((D_MODEL, tt), lambda ti, ei: (0, ti)),
                  pl.BlockSpec((PEER_EXPERT_TILE, D_MODEL), lambda ti, ei: (ei, 0)),
                  pl.BlockSpec((1, D_MODEL, PEER_EXPERT_TILE), lambda ti, ei: (ei, 0, 0)),
                  gate_spec, gate_spec, gate_spec, gate_spec,
                  pl.BlockSpec((tt, D_MODEL), lambda ti, ei: (ti, 0)),
                  _seg_spec(rows_per_seg, tt)],
        out_specs=pl.BlockSpec((tt, D_MODEL), lambda ti, ei: (ti, 0)),
        out_shape=jax.ShapeDtypeStruct((t, D_MODEL), F32),
        scratch_shapes=[pltpu.VMEM((D_MODEL, tt), F32),
                        pltpu.VMEM((PEER_EXPERT_TILE, tt), F32),
                        pltpu.VMEM((PEER_EXPERT_TILE, tt), BF16),
                        pltpu.VMEM((PEER_HEADS, tt // LANES, PEER_KEYS, LANES), BF16),
                        pltpu.VMEM((PEER_HEADS, tt // LANES, PEER_KEYS, LANES), BF16)],
        compiler_params=pltpu.CompilerParams(dimension_semantics=("parallel", "arbitrary"),
                                             vmem_limit_bytes=VMEM_LIMIT_BYTES),
        name="peer_dense",
    )(ht_bf, u_bf, vt_bf, a_t, n_t, b_t, r_t, x, g2)


def _fill_halo_scratch(scr, prev, cur, nxt, halo, ts):
    i = pl.program_id(1)
    scr[0:halo, :] = jnp.where(i > 0, prev, 0.0)
    scr[halo:halo + ts, :] = cur
    scr[halo + ts:halo + ts + halo, :] = jnp.where(i < pl.num_programs(1) - 1, nxt, 0.0)


def _depthwise_taps(scr, w_ref, taps, halo, r0, rows):
    off = halo - taps // 2
    acc = scr[off + r0:off + r0 + rows, :] * w_ref[0:1, :]
    for k in range(1, taps):
        acc = acc + scr[off + r0 + k:off + r0 + k + rows, :] * w_ref[k:k + 1, :]
    return acc


def _halo_specs(width, col_block, halo, ts, seq_len, total_rows):
    tiles = seq_len // ts
    per_tile = ts // halo
    last = total_rows // halo - 1
    prev = pl.BlockSpec((halo, width), lambda s, i: (jnp.maximum((s * tiles + i) * per_tile - 1, 0), col_block))
    cur = pl.BlockSpec((ts, width), lambda s, i: (s * tiles + i, col_block))
    nxt = pl.BlockSpec((halo, width), lambda s, i: (jnp.minimum((s * tiles + i + 1) * per_tile, last), col_block))
    return prev, cur, nxt


CONF_ROW_BLOCK = 32
QKV_ROW_BLOCK = 16


def _conformer_kernel(prev_ref, cur_ref, next_ref, w_ref, b_ref, lg_ref, lb_ref, o_ref, scr):
    ts = cur_ref.shape[0]
    glu = lambda blk: blk[:, :B_CH] * jax.nn.sigmoid(blk[:, B_CH:])
    _fill_halo_scratch(scr, glu(prev_ref[...]), glu(cur_ref[...]), glu(next_ref[...]), CONF_HALO, ts)
    for rb in range(ts // CONF_ROW_BLOCK):
        r0 = rb * CONF_ROW_BLOCK
        hh = _depthwise_taps(scr, w_ref, B_CONV, CONF_HALO, r0, CONF_ROW_BLOCK) + b_ref[...]
        mu = jnp.mean(hh, axis=-1, keepdims=True)
        var = jnp.mean(jnp.square(hh - mu), axis=-1, keepdims=True)
        y = (hh - mu) * lax.rsqrt(var + EPS) * lg_ref[...] + lb_ref[...]
        o_ref[r0:r0 + CONF_ROW_BLOCK, :] = (y * jax.nn.sigmoid(y)).astype(o_ref.dtype)


def conformer_branch(p, dw_w, dw_b, ln_g, ln_b, nseq, seq_len):
    t = p.shape[0]
    ts = CONV_ROW_TILE
    prev, cur, nxt = _halo_specs(2 * B_CH, COL_GLU // (2 * B_CH), CONF_HALO, ts, seq_len, t)
    row = lambda: pl.BlockSpec((1, B_CH), lambda s, i: (0, 0))
    w_pad = jnp.pad(dw_w, ((0, 32 - B_CONV), (0, 0)))
    return pl.pallas_call(
        _conformer_kernel, grid=(nseq, seq_len // ts),
        in_specs=[prev, cur, nxt, pl.BlockSpec((32, B_CH), lambda s, i: (0, 0)), row(), row(), row()],
        out_specs=pl.BlockSpec((ts, B_CH), lambda s, i: (s * (seq_len // ts) + i, 0)),
        out_shape=jax.ShapeDtypeStruct((t, B_CH), BF16),
        scratch_shapes=[pltpu.VMEM((ts + 2 * CONF_HALO, B_CH), F32)],
        compiler_params=pltpu.CompilerParams(dimension_semantics=("parallel", "parallel")),
        name="conformer_branch",
    )(p, p, p, w_pad, dw_b.reshape(1, B_CH), ln_g.reshape(1, B_CH), ln_b.reshape(1, B_CH))


def _qkv_conv_kernel(prev_ref, cur_ref, next_ref, w_ref, q_ref, k_ref, v_ref, scr):
    ts = cur_ref.shape[0]
    _fill_halo_scratch(scr, prev_ref[...], cur_ref[...], next_ref[...], QKV_HALO, ts)
    for rb in range(ts // QKV_ROW_BLOCK):
        r0 = rb * QKV_ROW_BLOCK
        rows = slice(r0, r0 + QKV_ROW_BLOCK)
        y = _depthwise_taps(scr, w_ref, SHORT_CONV, QKV_HALO, r0, QKV_ROW_BLOCK)
        y = y * jax.nn.sigmoid(y)
        for h in range(A_HEADS):
            cs = slice(h * A_DK, (h + 1) * A_DK)
            qh = y[:, h * A_DK:(h + 1) * A_DK]
            kh = y[:, A_DIM + h * A_DK:A_DIM + (h + 1) * A_DK]
            q_ref[rows, cs] = qh * (lax.rsqrt(jnp.sum(qh * qh, axis=-1, keepdims=True) + EPS) * (A_DK ** -0.5))
            k_ref[rows, cs] = kh * lax.rsqrt(jnp.sum(kh * kh, axis=-1, keepdims=True) + EPS)
        v_ref[rows, :] = y[:, 2 * A_DIM:]


def qkv_conv(p, conv_w, nseq, seq_len):
    t = p.shape[0]
    ts = CONV_ROW_TILE
    prev, cur, nxt = _halo_specs(3 * A_DIM, 0, QKV_HALO, ts, seq_len, t)
    out = pl.BlockSpec((ts, A_DIM), lambda s, i: (s * (seq_len // ts) + i, 0))
    w_pad = jnp.pad(conv_w, ((0, 8 - SHORT_CONV), (0, 0)))
    return pl.pallas_call(
        _qkv_conv_kernel, grid=(nseq, seq_len // ts),
        in_specs=[prev, cur, nxt, pl.BlockSpec((8, 3 * A_DIM), lambda s, i: (0, 0))],
        out_specs=[out, out, out],
        out_shape=[jax.ShapeDtypeStruct((t, A_DIM), F32)] * 3,
        scratch_shapes=[pltpu.VMEM((ts + 2 * QKV_HALO, 3 * A_DIM), F32)],
        compiler_params=pltpu.CompilerParams(dimension_semantics=("parallel", "parallel")),
        name="qkv_conv",
    )(p, p, p, w_pad)


def _mix_out_kernel(of_ref, ob_ref, z_ref, conf_ref, ng_ref, w_ref, x_ref, gate_ref, o_ref):
    o = of_ref[...] + ob_ref[...]
    z = z_ref[...]
    parts = []
    for h in range(A_HEADS):
        cs = slice(h * A_DV, (h + 1) * A_DV)
        oh = o[:, cs]
        zh = z[:, cs]
        scale = lax.rsqrt(jnp.mean(oh * oh, axis=-1, keepdims=True) + EPS)
        parts.append((oh * scale * ng_ref[...] * (zh * jax.nn.sigmoid(zh))).astype(BF16))
    oa = jnp.concatenate(parts, axis=1)
    mix = _bdot(oa, w_ref[:A_DIM, :]) + _bdot(conf_ref[...], w_ref[A_DIM:, :])
    o_ref[...] = x_ref[...] + gate_ref[0] * mix


def mix_out(o_f, o_b, p, conf, norm_g, w_bf, x, gate, rows_per_seg):
    t = x.shape[0]
    tm = MM_ROW_TILE
    half = lambda cb: pl.BlockSpec((tm, A_DIM), lambda i: (i, cb))
    return pl.pallas_call(
        _mix_out_kernel, grid=(t // tm,),
        in_specs=[half(0), half(0), half(COL_Z // A_DIM), half(0), pl.BlockSpec((1, A_DV), lambda i: (0, 0)),
                  pl.BlockSpec((D_MODEL, D_MODEL), lambda i: (0, 0)), pl.BlockSpec((tm, D_MODEL), lambda i: (i, 0)),
                  _seg_spec(rows_per_seg, tm)],
        out_specs=pl.BlockSpec((tm, D_MODEL), lambda i: (i, 0)),
        out_shape=jax.ShapeDtypeStruct((t, D_MODEL), F32),
        compiler_params=pltpu.CompilerParams(dimension_semantics=("parallel",),
                                             vmem_limit_bytes=VMEM_LIMIT_BYTES),
        name="mix_out",
    )(o_f, o_b, p, conf, norm_g.reshape(1, A_DV), w_bf, x, gate)


def grid_pos_emb(n_tokens):
    rows = n_tokens // GRID_W
    r = jnp.repeat(jnp.arange(rows, dtype=F32), GRID_W)
    col = jnp.tile(jnp.arange(GRID_W, dtype=F32), rows)
    nf = D_MODEL // 4
    freqs = jnp.exp(-math.log(POS_BASE) * jnp.arange(nf, dtype=F32) / nf)
    ar = r[:, None] * freqs
    ac = col[:, None] * freqs
    return jnp.concatenate([jnp.sin(ar), jnp.cos(ar), jnp.sin(ac), jnp.cos(ac)], axis=-1)


def delta_conformer_layer(x, p, g1, nseq, seq_len, s0, e, prm, rows_per_seg):
    t = p.shape[0]
    seq = lambda m: m.reshape(nseq, seq_len, m.shape[-1])
    q, k, v = qkv_conv(p, prm['conv_qkv_w'][e], nseq, seq_len)
    alpha = p[:, COL_AB:COL_AB + 2 * A_HEADS]
    beta = jax.nn.sigmoid(p[:, COL_AB + 2 * A_HEADS:COL_AB + 4 * A_HEADS])
    log_g = (-jnp.exp(prm['a_log'][e]).reshape(1, 2 * A_HEADS)
             * jax.nn.softplus(alpha + prm['dt_bias'][e].reshape(1, 2 * A_HEADS)))
    o_f, o_b, st = delta_scan(*delta_prep(seq(q), seq(k), seq(v), seq(log_g), seq(beta)), s0)
    conf = conformer_branch(p, prm['conf_dw_w'][e], prm['conf_dw_b'][e], prm['conf_ln_g'][e],
                            prm['conf_ln_b'][e], nseq, seq_len)
    x = mix_out(o_f.reshape(t, A_DIM), o_b.reshape(t, A_DIM), p, conf, prm['delta_norm_g'][e],
                prm['w_out_bf'][e], x, g1, rows_per_seg)
    return x, st


def trunk(x, nseq, seq_len, cond, s0, prm):
    t = x.shape[0]
    rows_per_seg = t // cond.shape[0]
    states = []
    for l in range(DEPTH):
        mod = jax.nn.silu(cond) @ prm['ada_w'][l] + prm['ada_b'][l]
        sh1, sc1, g1, sh2, sc2, g2 = [m[:, None, :] for m in jnp.split(mod, 6, axis=-1)]
        e = l // 2
        if l % 2 == 0:
            p = norm_mm(x, prm['norm1_g'][l], sc1, sh1, prm['w_in_bf'][e], rows_per_seg, F32)
            x, st = delta_conformer_layer(x, p, g1, nseq, seq_len, s0[:, e], e, prm, rows_per_seg)
            states.append(st)
        else:
            z = norm_mm(x, prm['norm1_g'][l], sc1, sh1, prm['w_fnet_bf'][e], rows_per_seg, BF16)
            tm = min(seq_len, 512)
            x = seq_mix_res(prm['dft_seq'][seq_len], z, x, g1, seq_len, tm, min(seq_len, 1024))
        ht, a_t, n_t, b_t, r_t = peer_route(x, prm['norm2_g'][l], sc2, sh2, prm['peer_wq_hi'][l],
                                            prm['peer_wq_lo'][l], prm['peer_k1'][l], prm['peer_k2'][l],
                                            rows_per_seg)
        x = peer_dense(ht, prm['peer_u_bf'][l], prm['peer_vt_bf'][l], a_t, n_t, b_t, r_t, x, g2, rows_per_seg)
    xf = x * lax.rsqrt(jnp.mean(x * x, axis=-1, keepdims=True) + EPS) * prm['final_norm_g']
    return xf, jnp.stack(states, axis=1)


def kernel(x_prompt, x_sample, state_delta, c, c_ctx, ada_w, ada_b, norm1_g, norm2_g, w_in_ab, conv_qkv_w,
           a_log, dt_bias, delta_norm_g, conf_dw_w, conf_dw_b, conf_ln_g, conf_ln_b, w_out_ab, w_out_c,
           peer_wq, peer_k1, peer_k2, peer_u, peer_v, final_norm_g):
    bp, sp, _ = x_prompt.shape
    bs, ss, _ = x_sample.shape
    bdc, bds = dft_group_matrices(D_MODEL // C_GROUPS, C_GROUPS)
    w_fnet = [jnp.concatenate([mm3(bdc, w_out_c[e], BF16), mm3(bds, w_out_c[e], BF16)], axis=1)
              for e in range(DEPTH // 2)]
    o4 = 4 * A_DIM
    w_in = jnp.concatenate([w_in_ab[:, :, :o4], w_in_ab[:, :, o4 + 4 * A_HEADS:], w_in_ab[:, :, o4:o4 + 4 * A_HEADS]],
                           axis=-1).astype(BF16)
    wq_hi, wq_lo = split_bf16(peer_wq)
    prm = {'ada_w': ada_w, 'ada_b': ada_b, 'norm1_g': norm1_g, 'norm2_g': norm2_g,
           'w_in_bf': jnp.pad(w_in, ((0, 0), (0, 0), (0, P_AB_PAD - P_AB))),
           'conv_qkv_w': conv_qkv_w, 'a_log': a_log, 'dt_bias': dt_bias,
           'delta_norm_g': delta_norm_g, 'conf_dw_w': conf_dw_w, 'conf_dw_b': conf_dw_b,
           'conf_ln_g': conf_ln_g, 'conf_ln_b': conf_ln_b, 'w_out_bf': w_out_ab.astype(BF16),
           'w_fnet_bf': w_fnet, 'dft_seq': {s: dft_seq_matrix(s) for s in {sp, ss}},
           'peer_wq_hi': wq_hi, 'peer_wq_lo': wq_lo, 'peer_k1': peer_k1, 'peer_k2': peer_k2,
           'peer_u_bf': peer_u.astype(BF16),
           'peer_vt_bf': jnp.transpose(peer_v.astype(BF16).reshape(DEPTH, PEER_N // PEER_EXPERT_TILE,
                                                                   PEER_EXPERT_TILE, D_MODEL), (0, 1, 3, 2)),
           'final_norm_g': final_norm_g}
    ne = (DEPTH + 1) // 2
    s0_ctx = jnp.zeros((bp, ne, 2, A_HEADS, A_DK, A_DV), F32)
    y_prompt, ctx_states = trunk(x_prompt.reshape(bp * sp, D_MODEL), bp, sp, c_ctx[None, :], s0_ctx, prm)
    xs = (x_sample + grid_pos_emb(ss)[None]).reshape(bs * ss, D_MODEL)
    y_sample, _ = trunk(xs, bs, ss, c, state_delta, prm)
    return (y_prompt.reshape(bp, sp, D_MODEL), y_sample.reshape(bs, ss, D_MODEL), ctx_states)
```

```python
import math

import jax
import jax.numpy as jnp
import numpy as np
from jax import lax
from jax.experimental import pallas as pl
from jax.experimental.pallas import tpu as pltpu

D_MODEL = 1024
DEPTH = 4
GRID_W = 64
POS_BASE = 10000.0
EPS = 1e-6
A_HEADS = 4
A_DK = 128
A_DV = 128
A_DIM = A_HEADS * A_DV
CHUNK = 64
B_CH = D_MODEL // 2
P_AB = 4 * A_DIM + 4 * A_HEADS + 2 * B_CH
C_GROUPS = 8
PEER_HEADS = 8
PEER_KEYS = 128
PEER_N = PEER_KEYS * PEER_KEYS
PEER_DK = 128
PEER_TOPK = 16

F32 = jnp.float32
BF16 = jnp.bfloat16
NEG_INF = float("-inf")

LANES = 128
VMEM_LIMIT_BYTES = 56 * 1024 * 1024
MM_ROW_TILE = 512
ROUTE_TOKEN_TILE = 256
PEER_TOKEN_TILE = 1024
PEER_KEYS_PER_STEP = 8
PEER_EXPERT_TILE = PEER_KEYS_PER_STEP * PEER_KEYS
PEER_J_BLOCK = 32
PREP_CHUNKS = 2
SCAN_SEQS = 2
P_AB_PAD = 3200
COL_Z = 3 * A_DIM
COL_GLU = 4 * A_DIM
COL_AB = 4 * A_DIM + 2 * B_CH
CONV_ROW_TILE = 256
SHORT_CONV = 7
B_CONV = 31
CONF_HALO = 16
QKV_HALO = 8
PEER_PIECES = 4
PIECE_KEYS = PEER_KEYS_PER_STEP // PEER_PIECES
PIECE_ROWS = PIECE_KEYS * PEER_KEYS
BF16_ROWS = 16

NT_DIMS = (((1,), (1,)), ((), ()))
TN_DIMS = (((0,), (0,)), ((), ()))


def _bdot(a, b):
    return jnp.dot(a, b, preferred_element_type=F32)


def _split_bf16(a):
    hi = a.astype(BF16)
    lo = (a - hi.astype(F32)).astype(BF16)
    return hi, lo


def _dot3(a, b):
    ah, al = _split_bf16(a)
    bh, bl = _split_bf16(b)
    return _bdot(ah, bh) + (_bdot(ah, bl) + _bdot(al, bh))


def _dot_exact_lhs(a01, b):
    a = a01.astype(BF16)
    bh = b.astype(BF16)
    r1 = b - bh.astype(F32)
    bm = r1.astype(BF16)
    bl = (r1 - bm.astype(F32)).astype(BF16)
    return _bdot(a, bh) + (_bdot(a, bm) + _bdot(a, bl))


def _gelu_tanh(x):
    return 0.5 * x * (1.0 + jnp.tanh(math.sqrt(2.0 / math.pi) * (x + 0.044715 * (x * x * x))))


def _seg_spec(rows_per_seg, tile):
    per = rows_per_seg // tile
    return pl.BlockSpec((1, 1, D_MODEL), lambda i, *_: (i // per, 0, 0))


def _norm_modulate(x, g, sc, sh):
    hn = x * lax.rsqrt(jnp.mean(x * x, axis=-1, keepdims=True) + EPS) * g
    return hn * (1.0 + sc) + sh


def _norm_mm_kernel(x_ref, g_ref, sc_ref, sh_ref, w_ref, o_ref):
    h = _norm_modulate(x_ref[...], g_ref[...], sc_ref[0], sh_ref[0]).astype(BF16)
    o_ref[...] = _bdot(h, w_ref[...]).astype(o_ref.dtype)


def norm_mm(x, norm_g, sc, sh, w_bf, rows_per_seg, out_dtype):
    m = x.shape[0]
    n = w_bf.shape[1]
    tm = MM_ROW_TILE
    seg = _seg_spec(rows_per_seg, tm)
    return pl.pallas_call(
        _norm_mm_kernel, grid=(m // tm,),
        in_specs=[pl.BlockSpec((tm, D_MODEL), lambda i: (i, 0)), pl.BlockSpec((1, D_MODEL), lambda i: (0, 0)),
                  seg, seg, pl.BlockSpec((D_MODEL, n), lambda i: (0, 0))],
        out_specs=pl.BlockSpec((tm, n), lambda i: (i, 0)),
        out_shape=jax.ShapeDtypeStruct((m, n), out_dtype),
        compiler_params=pltpu.CompilerParams(dimension_semantics=("parallel",),
                                             vmem_limit_bytes=VMEM_LIMIT_BYTES),
        name="norm_mm",
    )(x, norm_g.reshape(1, D_MODEL), sc, sh, w_bf)


def _split_kernel(a_ref, hi_ref, lo_ref):
    hi, lo = _split_bf16(a_ref[0])
    hi_ref[0] = hi
    lo_ref[0] = lo


def split_bf16(a):
    nl, nr, nc = a.shape
    blk = pl.BlockSpec((1, MM_ROW_TILE, nc), lambda l, i: (l, i, 0))
    return pl.pallas_call(
        _split_kernel, grid=(nl, nr // MM_ROW_TILE), in_specs=[blk], out_specs=[blk, blk],
        out_shape=[jax.ShapeDtypeStruct(a.shape, BF16)] * 2,
        compiler_params=pltpu.CompilerParams(dimension_semantics=("parallel", "parallel")),
        name="split_bf16",
    )(a)


def _mm3_kernel(a_ref, b_ref, o_ref):
    o_ref[...] = _dot3(a_ref[...], b_ref[...]).astype(o_ref.dtype)


def mm3(a, b, out_dtype):
    return pl.pallas_call(_mm3_kernel, out_shape=jax.ShapeDtypeStruct((a.shape[0], b.shape[1]), out_dtype),
                          compiler_params=pltpu.CompilerParams(vmem_limit_bytes=VMEM_LIMIT_BYTES),
                          name="mm3")(a, b)


def _seqmix_kernel(f_ref, z_ref, x_ref, gate_ref, o_ref, acc_ref):
    k = pl.program_id(2)

    @pl.when(k == 0)
    def _():
        acc_ref[...] = jnp.zeros_like(acc_ref)

    acc_ref[...] += _bdot(f_ref[...], z_ref[...])

    @pl.when(k == pl.num_programs(2) - 1)
    def _():
        o_ref[...] = x_ref[...] + gate_ref[0] * acc_ref[...]


def seq_mix_res(fmat, z, x, gate, seq_len, tm, tk):
    t = x.shape[0]
    nseq = t // seq_len
    seqs_per_seg = nseq // gate.shape[0]
    mt = seq_len // tm
    kt_half = seq_len // tk
    return pl.pallas_call(
        _seqmix_kernel, grid=(nseq, mt, 2 * kt_half),
        in_specs=[pl.BlockSpec((tm, tk), lambda s, i, k: (i, k)),
                  pl.BlockSpec((tk, D_MODEL), lambda s, i, k: (s * kt_half + k % kt_half, k // kt_half)),
                  pl.BlockSpec((tm, D_MODEL), lambda s, i, k: (s * mt + i, 0)),
                  pl.BlockSpec((1, 1, D_MODEL), lambda s, i, k: (s // seqs_per_seg, 0, 0))],
        out_specs=pl.BlockSpec((tm, D_MODEL), lambda s, i, k: (s * mt + i, 0)),
        out_shape=jax.ShapeDtypeStruct((t, D_MODEL), F32),
        scratch_shapes=[pltpu.VMEM((tm, D_MODEL), F32)],
        compiler_params=pltpu.CompilerParams(dimension_semantics=("parallel", "parallel", "arbitrary"),
                                             vmem_limit_bytes=VMEM_LIMIT_BYTES),
        name="seq_mix_res",
    )(fmat, z, x, gate)


def _dft_tables(n, cols):
    r = jnp.arange(n, dtype=jnp.int32)[:, None]
    ang = ((r * cols[None, :]) % n).astype(F32) * (2.0 * math.pi / n)
    return jnp.cos(ang), jnp.sin(ang)


def dft_seq_matrix(s):
    w = 1 << (int(math.log2(s)) // 2)
    ch, sh_ = _dft_tables(s, jnp.arange(s // w, dtype=jnp.int32) * w)
    cl, sl = _dft_tables(s, jnp.arange(w, dtype=jnp.int32))
    sc = 1.0 / math.sqrt(s)
    c = (ch[:, :, None] * cl[:, None, :] - sh_[:, :, None] * sl[:, None, :]).reshape(s, s) * sc
    sn = (sh_[:, :, None] * cl[:, None, :] + ch[:, :, None] * sl[:, None, :]).reshape(s, s) * sc
    return jnp.concatenate([c, -sn], axis=1).astype(BF16)


def dft_group_matrices(n, groups):
    c, s = _dft_tables(n, jnp.arange(n, dtype=jnp.int32))
    sc = 1.0 / math.sqrt(n)
    eye = jnp.eye(groups, dtype=F32)
    return jnp.kron(eye, c * sc), jnp.kron(eye, s * sc)


def _delta_prep_kernel(q_ref, k_ref, v_ref, lg_ref, bt_ref, w_ref, u_ref, qd_ref, kd_ref, p_ref, g_ref):
    r = lax.broadcasted_iota(jnp.int32, (CHUNK, CHUNK), 0)
    c = lax.broadcasted_iota(jnp.int32, (CHUNK, CHUNK), 1)
    eye = (r == c).astype(F32)
    ones = jnp.ones((CHUNK, CHUNK), F32)
    incl = (r >= c, r <= c)
    strict = (r > c, r < c)
    tri = (incl[0].astype(F32), incl[1].astype(F32))
    tri_t = (tri[1], tri[0])
    last = (CHUNK - 1, 0)
    chains = [(cg, d, h) for cg in range(PREP_CHUNKS) for d in range(2) for h in range(A_HEADS)]
    rows = lambda cg: slice(cg * CHUNK, (cg + 1) * CHUNK)
    cols = lambda h: slice(h * A_DK, (h + 1) * A_DK)
    kk = {}
    qk = {}
    for cg in range(PREP_CHUNKS):
        for h in range(A_HEADS):
            kb = k_ref[0, rows(cg), cols(h)].astype(BF16)
            kk[cg, h] = lax.dot_general(kb, kb, NT_DIMS, preferred_element_type=F32)
            qk[cg, h] = lax.dot_general(q_ref[0, rows(cg), cols(h)].astype(BF16), kb, NT_DIMS,
                                        preferred_element_type=F32)
    lgw = {}
    btw = {}
    for (cg, d, h) in chains:
        col = d * A_HEADS + h
        lgw[cg, d, h] = jnp.broadcast_to(lg_ref[0, rows(cg), col:col + 1], (CHUNK, A_DK))
        btw[cg, d, h] = jnp.broadcast_to(bt_ref[0, rows(cg), col:col + 1], (CHUNK, A_DK))
    gam = {ch: _dot_exact_lhs(tri[ch[1]], lgw[ch]) for ch in chains}
    gam_row = {ch: _dot_exact_lhs(ones, lgw[ch][:, :CHUNK] * tri_t[ch[1]]) for ch in chains}
    decay = {}
    lmat = {}
    for ch in chains:
        cg, d, h = ch
        diff = gam[ch][:, :CHUNK] - gam_row[ch]
        decay[ch] = jnp.where(incl[d], jnp.exp(jnp.where(incl[d], diff, 0.0)), 0.0)
        lmat[ch] = jnp.where(strict[d], btw[ch][:, :CHUNK] * decay[ch] * kk[cg, h], 0.0)
    pinv = {ch: eye - jnp.where((r // 2 == c // 2) & (r != c), lmat[ch], 0.0) for ch in chains}
    s = 2
    while s < CHUNK:
        join = (r // (2 * s) == c // (2 * s)) & (r // s != c // s)
        tc = {ch: _dot3(pinv[ch], jnp.where(join, lmat[ch], 0.0)) for ch in chains}
        pinv = {ch: pinv[ch] - _dot3(tc[ch], pinv[ch]) for ch in chains}
        s *= 2
    for ch in chains:
        cg, d, h = ch
        kh = k_ref[0, rows(cg), cols(h)]
        vh = v_ref[0, rows(cg), cols(h)]
        qh = q_ref[0, rows(cg), cols(h)]
        egam = jnp.exp(gam[ch])
        rhs = jnp.concatenate([kh * (btw[ch] * egam), vh * btw[ch]], axis=1)
        sol = _dot3(pinv[ch], rhs)
        w_ref[0, d, rows(cg), cols(h)] = sol[:, :A_DK].astype(BF16)
        u_ref[0, d, rows(cg), cols(h)] = sol[:, A_DK:]
        qd_ref[0, d, rows(cg), cols(h)] = (qh * egam).astype(BF16)
        glast = jnp.broadcast_to(gam[ch][last[d]:last[d] + 1, :], (CHUNK, A_DK))
        kd_ref[0, d, rows(cg), cols(h)] = (kh * jnp.exp(glast - gam[ch])).astype(BF16)
        g_ref[0, d, rows(cg), cols(h)] = jnp.exp(glast)
        p_ref[0, d, rows(cg), h * CHUNK:(h + 1) * CHUNK] = (decay[ch] * qk[cg, h]).astype(BF16)


def delta_prep(q, k, v, lg, bt):
    b, s, _ = q.shape
    rt = PREP_CHUNKS * CHUNK
    blk = lambda w: pl.BlockSpec((1, rt, w), lambda bi, ni: (bi, ni, 0))
    oblk = lambda w: pl.BlockSpec((1, 2, rt, w), lambda bi, ni: (bi, 0, ni, 0))
    sh = lambda w, dt: jax.ShapeDtypeStruct((b, 2, s, w), dt)
    return pl.pallas_call(
        _delta_prep_kernel, grid=(b, s // rt),
        in_specs=[blk(A_DIM), blk(A_DIM), blk(A_DIM), blk(2 * A_HEADS), blk(2 * A_HEADS)],
        out_specs=[oblk(A_DIM), oblk(A_DIM), oblk(A_DIM), oblk(A_DIM), oblk(A_HEADS * CHUNK), oblk(A_DIM)],
        out_shape=[sh(A_DIM, BF16), sh(A_DIM, F32), sh(A_DIM, BF16), sh(A_DIM, BF16),
                   sh(A_HEADS * CHUNK, BF16), sh(A_DIM, F32)],
        compiler_params=pltpu.CompilerParams(dimension_semantics=("parallel", "parallel")),
        name="delta_prep",
    )(q, k, v, lg, bt)


def _delta_scan_kernel(*refs):
    ins = refs[:12]
    s0_ref = refs[12]
    of_ref, ob_ref, sout_ref, state = refs[13:]
    n = pl.program_id(1)

    @pl.when(n == 0)
    def _():
        state[...] = s0_ref[...]

    outs = (of_ref, ob_ref)
    chains = [(g, d, h) for g in range(SCAN_SEQS) for d in range(2) for h in range(A_HEADS)]
    cs = lambda h: slice(h * A_DK, (h + 1) * A_DK)
    ref = lambda d, i: ins[d * 6 + i]
    s_old = {ch: state[ch] for ch in chains}
    wqs = {}
    for (g, d, h) in chains:
        wq = jnp.concatenate([ref(d, 0)[g, 0, :, cs(h)], ref(d, 2)[g, 0, :, cs(h)]], axis=0)
        wqs[g, d, h] = _bdot(wq, s_old[g, d, h].astype(BF16))
    unb = {}
    for (g, d, h) in chains:
        unb[g, d, h] = (ref(d, 1)[g, 0, :, cs(h)] - wqs[g, d, h][:CHUNK]).astype(BF16)
    for (g, d, h) in chains:
        o = wqs[g, d, h][CHUNK:] + _bdot(ref(d, 4)[g, 0, :, h * CHUNK:(h + 1) * CHUNK], unb[g, d, h])
        outs[d][g, :, cs(h)] = o
    for (g, d, h) in chains:
        upd = lax.dot_general(ref(d, 3)[g, 0, :, cs(h)], unb[g, d, h], TN_DIMS, preferred_element_type=F32)
        gs = jnp.broadcast_to(ref(d, 5)[g, 0, 0:1, cs(h)], (A_DK, A_DV))
        state[g, d, h] = gs * s_old[g, d, h] + upd

    @pl.when(n == pl.num_programs(1) - 1)
    def _():
        sout_ref[...] = state[...]


def delta_scan(w, u, qd, kd, p, gl, s0):
    b, _, s, _ = u.shape
    n = s // CHUNK

    def spec(wd, d):
        if d == 0:
            return pl.BlockSpec((SCAN_SEQS, 1, CHUNK, wd), lambda bi, ni: (bi, 0, ni, 0))
        return pl.BlockSpec((SCAN_SEQS, 1, CHUNK, wd), lambda bi, ni: (bi, 1, n - 1 - ni, 0))

    arrs = (w, u, qd, kd, p, gl)
    in_specs = [spec(a.shape[-1], d) for d in range(2) for a in arrs]
    st = pl.BlockSpec((SCAN_SEQS, 2, A_HEADS, A_DK, A_DV), lambda bi, ni: (bi, 0, 0, 0, 0))
    of = pl.BlockSpec((SCAN_SEQS, CHUNK, A_DIM), lambda bi, ni: (bi, ni, 0))
    ob = pl.BlockSpec((SCAN_SEQS, CHUNK, A_DIM), lambda bi, ni: (bi, n - 1 - ni, 0))
    return pl.pallas_call(
        _delta_scan_kernel, grid=(b // SCAN_SEQS, n),
        in_specs=in_specs + [st],
        out_specs=[of, ob, st],
        out_shape=[jax.ShapeDtypeStruct((b, s, A_DIM), F32), jax.ShapeDtypeStruct((b, s, A_DIM), F32),
                   jax.ShapeDtypeStruct((b, 2, A_HEADS, A_DK, A_DV), F32)],
        scratch_shapes=[pltpu.VMEM((SCAN_SEQS, 2, A_HEADS, A_DK, A_DV), F32)],
        compiler_params=pltpu.CompilerParams(dimension_semantics=("parallel", "arbitrary")),
        name="delta_scan",
    )(*(arrs + arrs), s0)


def _cand_tables():
    pairs = [(r, c) for r in range(PEER_TOPK) for c in range(PEER_TOPK) if (r + 1) * (c + 1) <= PEER_TOPK]
    npad = 64
    e1 = np.zeros((npad, PEER_TOPK), np.float32)
    e2 = np.zeros((npad, PEER_TOPK), np.float32)
    m = np.zeros((PEER_TOPK, npad), np.float32)
    for k, (r, c) in enumerate(pairs):
        e1[k, r] = 1
        e2[k, c] = 1
        m[r, k] = 1
    return len(pairs), e1, e2, m


N_CAND, _CAND_E1, _CAND_E2, _CAND_ROW = _cand_tables()


def _extract_topk(s, n_iter):
    k, t = s.shape
    work = s.reshape(k // 8, 8, t)
    rank = jnp.full(work.shape, float(n_iter), F32)
    vals = []
    for r in range(n_iter):
        m = jnp.max(jnp.max(work, axis=0), axis=0, keepdims=True)
        hit = work == jnp.broadcast_to(m, (8, t))[None]
        rank = jnp.where(hit, float(r), rank)
        work = jnp.where(hit, NEG_INF, work)
        vals.append(m)
    return vals, rank.reshape(k, t)


def _route_kernel(x_ref, g_ref, sc_ref, sh_ref, wqh_ref, wql_ref, k1_ref, k2_ref, e1_ref, e2_ref, mrow_ref,
                  ht_ref, a_ref, n_ref, b_ref, r_ref, q_scr):
    hmod = _norm_modulate(x_ref[...], g_ref[...], sc_ref[0], sh_ref[0])
    ht_ref[...] = hmod.T.astype(BF16)
    hh, hl = _split_bf16(hmod)
    q_scr[...] = _bdot(hh, wqh_ref[...]) + (_bdot(hh, wql_ref[...]) + _bdot(hl, wqh_ref[...]))
    tt = x_ref.shape[0]

    def head(h, carry):
        c1 = pl.ds(pl.multiple_of(h * 2 * PEER_DK, PEER_DK), PEER_DK)
        c2 = pl.ds(pl.multiple_of(h * 2 * PEER_DK + PEER_DK, PEER_DK), PEER_DK)
        hp = lax.Precision.HIGHEST
        s1 = lax.dot_general(k1_ref[h], q_scr[:, c1], NT_DIMS, precision=hp, preferred_element_type=F32)
        s2 = lax.dot_general(k2_ref[h], q_scr[:, c2], NT_DIMS, precision=hp, preferred_element_type=F32)
        v1, rank1 = _extract_topk(s1, PEER_TOPK)
        v2, rank2 = _extract_topk(s2, PEER_TOPK)
        v1m = jnp.concatenate(v1, axis=0)
        v2m = jnp.concatenate(v2, axis=0)
        cand = (jnp.dot(e1_ref[...], v1m, precision=hp, preferred_element_type=F32)
                + jnp.dot(e2_ref[...], v2m, precision=hp, preferred_element_type=F32))
        row = lax.broadcasted_iota(jnp.int32, cand.shape, 0)
        cand = jnp.where(row < N_CAND, cand, NEG_INF)
        _, crank = _extract_topk(cand, PEER_TOPK)
        sel = crank < float(PEER_TOPK)
        cmax = v1[0] + v2[0]
        z = jnp.sum(jnp.where(sel, jnp.exp(cand - cmax), 0.0), axis=0, keepdims=True)
        n_r = _bdot(mrow_ref[...], jnp.where(sel, 1.0, 0.0).astype(BF16))
        rank1_3 = rank1.reshape(PEER_KEYS // 8, 8, tt)
        nn = jnp.zeros_like(rank1_3)
        for r in range(PEER_TOPK):
            nn = jnp.where(rank1_3 == float(r), jnp.broadcast_to(n_r[r:r + 1, :], (8, tt))[None], nn)
        a_ref[h] = jnp.exp(s1 - v1[0]) / z
        n_ref[h] = nn.reshape(PEER_KEYS, tt)
        b_ref[h] = jnp.where(rank2 < float(PEER_TOPK), jnp.exp(s2 - v2[0]), 0.0).astype(BF16)
        r_ref[h] = rank2.astype(BF16)
        return carry

    lax.fori_loop(0, PEER_HEADS, head, 0, unroll=2)


def peer_route(x, norm_g, sc, sh, wq_hi, wq_lo, k1, k2, rows_per_seg):
    t = x.shape[0]
    tt = ROUTE_TOKEN_TILE
    gate = pl.BlockSpec((PEER_HEADS, PEER_KEYS, tt), lambda i: (0, 0, i))
    full = lambda shp: pl.BlockSpec(shp, lambda i: (0,) * len(shp))
    seg = _seg_spec(rows_per_seg, tt)
    return pl.pallas_call(
        _route_kernel, grid=(t // tt,),
        in_specs=[pl.BlockSpec((tt, D_MODEL), lambda i: (i, 0)), full((1, D_MODEL)), seg, seg,
                  full((D_MODEL, 2 * PEER_HEADS * PEER_DK)), full((D_MODEL, 2 * PEER_HEADS * PEER_DK)),
                  full((PEER_HEADS, PEER_KEYS, PEER_DK)),
                  full((PEER_HEADS, PEER_KEYS, PEER_DK)), full((64, PEER_TOPK)), full((64, PEER_TOPK)),
                  full((PEER_TOPK, 64))],
        out_specs=[pl.BlockSpec((D_MODEL, tt), lambda i: (0, i)), gate, gate, gate, gate],
        out_shape=[jax.ShapeDtypeStruct((D_MODEL, t), BF16)]
        + [jax.ShapeDtypeStruct((PEER_HEADS, PEER_KEYS, t), dt) for dt in (F32, F32, BF16, BF16)],
        scratch_shapes=[pltpu.VMEM((tt, 2 * PEER_HEADS * PEER_DK), F32)],
        compiler_params=pltpu.CompilerParams(dimension_semantics=("parallel",),
                                             vmem_limit_bytes=VMEM_LIMIT_BYTES),
        name="peer_route",
    )(x, norm_g.reshape(1, D_MODEL), sc, sh, wq_hi, wq_lo, k1, k2,
      jnp.asarray(_CAND_E1), jnp.asarray(_CAND_E2), jnp.asarray(_CAND_ROW, BF16))


def _peer_dense_kernel(ht_ref, u_ref, vt_ref, a_ref, n_ref, b_ref, r_ref, x_ref, g2_ref, o_ref,
                       acc_ref, hid_ref, w_ref, bp_ref, rp_ref):
    e = pl.program_id(1)

    @pl.when(e == 0)
    def _():
        acc_ref[...] = jnp.zeros_like(acc_ref)
        for c in range(PEER_TOKEN_TILE // LANES):
            ls = slice(c * LANES, (c + 1) * LANES)
            bp_ref[:, c] = b_ref[:, :, ls].astype(BF16)
            rp_ref[:, c] = r_ref[:, :, ls].astype(BF16)

    def hidden(pp):
        rows = slice(pp * PIECE_ROWS, (pp + 1) * PIECE_ROWS)
        hid_ref[pp % 2] = _bdot(u_ref[rows, :], ht_ref[...])

    def gates(pp):
        for c in range(PEER_TOKEN_TILE // LANES):
            ls = slice(c * LANES, (c + 1) * LANES)
            for ii in range(PIECE_KEYS):
                k = pp * PIECE_KEYS + ii
                acc = jnp.zeros((PEER_KEYS, LANES), BF16)
                for h in range(PEER_HEADS):
                    a16 = jnp.broadcast_to(a_ref[h, k:k + 1, ls], (BF16_ROWS, LANES)).astype(BF16)
                    n16 = jnp.broadcast_to(n_ref[h, k:k + 1, ls], (BF16_ROWS, LANES)).astype(BF16)
                    a128 = pltpu.repeat(a16, PEER_KEYS // BF16_ROWS, axis=0)
                    n128 = pltpu.repeat(n16, PEER_KEYS // BF16_ROWS, axis=0)
                    b = bp_ref[h, c]
                    acc = acc + jnp.where(rp_ref[h, c] < n128, b, jnp.zeros_like(b)) * a128
                rows = slice(ii * PEER_KEYS, (ii + 1) * PEER_KEYS)
                w_ref[pp % 2, rows, ls] = _gelu_tanh(hid_ref[pp % 2, rows, ls]).astype(BF16) * acc

    def project(pp):
        rows = slice(pp * PIECE_ROWS, (pp + 1) * PIECE_ROWS)
        acc_ref[...] += _bdot(vt_ref[0, :, rows], w_ref[pp % 2])

    hidden(0)
    for pp in range(PEER_PIECES):
        if pp + 1 < PEER_PIECES:
            hidden(pp + 1)
        gates(pp)
        project(pp)

    @pl.when(e == pl.num_programs(1) - 1)
    def _():
        o_ref[...] = x_ref[...] + g2_ref[0] * acc_ref[...].T


def peer_dense(ht_bf, u_bf, vt_bf, a_t, n_t, b_t, r_t, x, g2, rows_per_seg):
    t = x.shape[0]
    tt = PEER_TOKEN_TILE
    grid = (t // tt, PEER_N // PEER_EXPERT_TILE)
    gate_spec = pl.BlockSpec((PEER_HEADS, PEER_KEYS, tt), lambda ti, ei: (0, 0, ti))
    step_keys = pl.BlockSpec((PEER_HEADS, PEER_KEYS_PER_STEP, tt), lambda ti, ei: (0, ei, ti))
    return pl.pallas_call(
        _peer_dense_kernel, grid=grid,
        in_specs=[pl.BlockSpec((D_MODEL, tt), lambda ti, ei: (0, ti)),
                  pl.BlockSpec((PEER_EXPERT_TILE, D_MODEL), lambda ti, ei: (ei, 0)),
                  pl.BlockSpec((1, D_MODEL, PEER_EXPERT_TILE), lambda ti, ei: (ei, 0, 0)),
                  step_keys, step_keys, gate_spec, gate_spec,
                  pl.BlockSpec((tt, D_MODEL), lambda ti, ei: (ti, 0)),
                  _seg_spec(rows_per_seg, tt)],
        out_specs=pl.BlockSpec((tt, D_MODEL), lambda ti, ei: (ti, 0)),
        out_shape=jax.ShapeDtypeStruct((t, D_MODEL), F32),
        scratch_shapes=[pltpu.VMEM((D_MODEL, tt), F32),
                        pltpu.VMEM((2, PIECE_ROWS, tt), F32),
                        pltpu.VMEM((2, PIECE_ROWS, tt), BF16),
                        pltpu.VMEM((PEER_HEADS, tt // LANES, PEER_KEYS, LANES), BF16),
                        pltpu.VMEM((PEER_HEADS, tt // LANES, PEER_KEYS, LANES), BF16)],
        compiler_params=pltpu.CompilerParams(dimension_semantics=("parallel", "arbitrary"),
                                             vmem_limit_bytes=VMEM_LIMIT_BYTES),
        name="peer_dense",
    )(ht_bf, u_bf, vt_bf, a_t, n_t, b_t, r_t, x, g2)


def _fill_halo_scratch(scr, prev, cur, nxt, halo, ts):
    i = pl.program_id(1)
    scr[0:halo, :] = jnp.where(i > 0, prev, 0.0)
    scr[halo:halo + ts, :] = cur
    scr[halo + ts:halo + ts + halo, :] = jnp.where(i < pl.num_programs(1) - 1, nxt, 0.0)


def _depthwise_taps(scr, w_ref, taps, halo, r0, rows):
    off = halo - taps // 2
    acc = scr[off + r0:off + r0 + rows, :] * w_ref[0:1, :]
    for k in range(1, taps):
        acc = acc + scr[off + r0 + k:off + r0 + k + rows, :] * w_ref[k:k + 1, :]
    return acc


def _halo_specs(width, col_block, halo, ts, seq_len, total_rows):
    tiles = seq_len // ts
    per_tile = ts // halo
    last = total_rows // halo - 1
    prev = pl.BlockSpec((halo, width), lambda s, i: (jnp.maximum((s * tiles + i) * per_tile - 1, 0), col_block))
    cur = pl.BlockSpec((ts, width), lambda s, i: (s * tiles + i, col_block))
    nxt = pl.BlockSpec((halo, width), lambda s, i: (jnp.minimum((s * tiles + i + 1) * per_tile, last), col_block))
    return prev, cur, nxt


CONF_ROW_BLOCK = 32
QKV_ROW_BLOCK = 16


def _conformer_kernel(prev_ref, cur_ref, next_ref, w_ref, b_ref, lg_ref, lb_ref, o_ref, scr):
    ts = cur_ref.shape[0]
    glu = lambda blk: blk[:, :B_CH] * jax.nn.sigmoid(blk[:, B_CH:])
    _fill_halo_scratch(scr, glu(prev_ref[...]), glu(cur_ref[...]), glu(next_ref[...]), CONF_HALO, ts)
    for rb in range(ts // CONF_ROW_BLOCK):
        r0 = rb * CONF_ROW_BLOCK
        hh = _depthwise_taps(scr, w_ref, B_CONV, CONF_HALO, r0, CONF_ROW_BLOCK) + b_ref[...]
        mu = jnp.mean(hh, axis=-1, keepdims=True)
        var = jnp.mean(jnp.square(hh - mu), axis=-1, keepdims=True)
        y = (hh - mu) * lax.rsqrt(var + EPS) * lg_ref[...] + lb_ref[...]
        o_ref[r0:r0 + CONF_ROW_BLOCK, :] = (y * jax.nn.sigmoid(y)).astype(o_ref.dtype)


def conformer_branch(p, dw_w, dw_b, ln_g, ln_b, nseq, seq_len):
    t = p.shape[0]
    ts = CONV_ROW_TILE
    prev, cur, nxt = _halo_specs(2 * B_CH, COL_GLU // (2 * B_CH), CONF_HALO, ts, seq_len, t)
    row = lambda: pl.BlockSpec((1, B_CH), lambda s, i: (0, 0))
    w_pad = jnp.pad(dw_w, ((0, 32 - B_CONV), (0, 0)))
    return pl.pallas_call(
        _conformer_kernel, grid=(nseq, seq_len // ts),
        in_specs=[prev, cur, nxt, pl.BlockSpec((32, B_CH), lambda s, i: (0, 0)), row(), row(), row()],
        out_specs=pl.BlockSpec((ts, B_CH), lambda s, i: (s * (seq_len // ts) + i, 0)),
        out_shape=jax.ShapeDtypeStruct((t, B_CH), BF16),
        scratch_shapes=[pltpu.VMEM((ts + 2 * CONF_HALO, B_CH), F32)],
        compiler_params=pltpu.CompilerParams(dimension_semantics=("parallel", "parallel")),
        name="conformer_branch",
    )(p, p, p, w_pad, dw_b.reshape(1, B_CH), ln_g.reshape(1, B_CH), ln_b.reshape(1, B_CH))


def _qkv_conv_kernel(prev_ref, cur_ref, next_ref, w_ref, q_ref, k_ref, v_ref, scr):
    ts = cur_ref.shape[0]
    _fill_halo_scratch(scr, prev_ref[...], cur_ref[...], next_ref[...], QKV_HALO, ts)
    for rb in range(ts // QKV_ROW_BLOCK):
        r0 = rb * QKV_ROW_BLOCK
        rows = slice(r0, r0 + QKV_ROW_BLOCK)
        y = _depthwise_taps(scr, w_ref, SHORT_CONV, QKV_HALO, r0, QKV_ROW_BLOCK)
        y = y * jax.nn.sigmoid(y)
        for h in range(A_HEADS):
            cs = slice(h * A_DK, (h + 1) * A_DK)
            qh = y[:, h * A_DK:(h + 1) * A_DK]
            kh = y[:, A_DIM + h * A_DK:A_DIM + (h + 1) * A_DK]
            q_ref[rows, cs] = qh * (lax.rsqrt(jnp.sum(qh * qh, axis=-1, keepdims=True) + EPS) * (A_DK ** -0.5))
            k_ref[rows, cs] = kh * lax.rsqrt(jnp.sum(kh * kh, axis=-1, keepdims=True) + EPS)
        v_ref[rows, :] = y[:, 2 * A_DIM:]


def qkv_conv(p, conv_w, nseq, seq_len):
    t = p.shape[0]
    ts = CONV_ROW_TILE
    prev, cur, nxt = _halo_specs(3 * A_DIM, 0, QKV_HALO, ts, seq_len, t)
    out = pl.BlockSpec((ts, A_DIM), lambda s, i: (s * (seq_len // ts) + i, 0))
    w_pad = jnp.pad(conv_w, ((0, 8 - SHORT_CONV), (0, 0)))
    return pl.pallas_call(
        _qkv_conv_kernel, grid=(nseq, seq_len // ts),
        in_specs=[prev, cur, nxt, pl.BlockSpec((8, 3 * A_DIM), lambda s, i: (0, 0))],
        out_specs=[out, out, out],
        out_shape=[jax.ShapeDtypeStruct((t, A_DIM), F32)] * 3,
        scratch_shapes=[pltpu.VMEM((ts + 2 * QKV_HALO, 3 * A_DIM), F32)],
        compiler_params=pltpu.CompilerParams(dimension_semantics=("parallel", "parallel")),
        name="qkv_conv",
    )(p, p, p, w_pad)


def _mix_out_kernel(of_ref, ob_ref, z_ref, conf_ref, ng_ref, w_ref, x_ref, gate_ref, o_ref):
    o = of_ref[...] + ob_ref[...]
    z = z_ref[...]
    parts = []
    for h in range(A_HEADS):
        cs = slice(h * A_DV, (h + 1) * A_DV)
        oh = o[:, cs]
        zh = z[:, cs]
        scale = lax.rsqrt(jnp.mean(oh * oh, axis=-1, keepdims=True) + EPS)
        parts.append((oh * scale * ng_ref[...] * (zh * jax.nn.sigmoid(zh))).astype(BF16))
    oa = jnp.concatenate(parts, axis=1)
    mix = _bdot(oa, w_ref[:A_DIM, :]) + _bdot(conf_ref[...], w_ref[A_DIM:, :])
    o_ref[...] = x_ref[...] + gate_ref[0] * mix


def mix_out(o_f, o_b, p, conf, norm_g, w_bf, x, gate, rows_per_seg):
    t = x.shape[0]
    tm = MM_ROW_TILE
    half = lambda cb: pl.BlockSpec((tm, A_DIM), lambda i: (i, cb))
    return pl.pallas_call(
        _mix_out_kernel, grid=(t // tm,),
        in_specs=[half(0), half(0), half(COL_Z // A_DIM), half(0), pl.BlockSpec((1, A_DV), lambda i: (0, 0)),
                  pl.BlockSpec((D_MODEL, D_MODEL), lambda i: (0, 0)), pl.BlockSpec((tm, D_MODEL), lambda i: (i, 0)),
                  _seg_spec(rows_per_seg, tm)],
        out_specs=pl.BlockSpec((tm, D_MODEL), lambda i: (i, 0)),
        out_shape=jax.ShapeDtypeStruct((t, D_MODEL), F32),
        compiler_params=pltpu.CompilerParams(dimension_semantics=("parallel",),
                                             vmem_limit_bytes=VMEM_LIMIT_BYTES),
        name="mix_out",
    )(o_f, o_b, p, conf, norm_g.reshape(1, A_DV), w_bf, x, gate)


def grid_pos_emb(n_tokens):
    rows = n_tokens // GRID_W
    r = jnp.repeat(jnp.arange(rows, dtype=F32), GRID_W)
    col = jnp.tile(jnp.arange(GRID_W, dtype=F32), rows)
    nf = D_MODEL // 4
    freqs = jnp.exp(-math.log(POS_BASE) * jnp.arange(nf, dtype=F32) / nf)
    ar = r[:, None] * freqs
    ac = col[:, None] * freqs
    return jnp.concatenate([jnp.sin(ar), jnp.cos(ar), jnp.sin(ac), jnp.cos(ac)], axis=-1)


def delta_conformer_layer(x, p, g1, nseq, seq_len, s0, e, prm, rows_per_seg):
    t = p.shape[0]
    seq = lambda m: m.reshape(nseq, seq_len, m.shape[-1])
    q, k, v = qkv_conv(p, prm['conv_qkv_w'][e], nseq, seq_len)
    alpha = p[:, COL_AB:COL_AB + 2 * A_HEADS]
    beta = jax.nn.sigmoid(p[:, COL_AB + 2 * A_HEADS:COL_AB + 4 * A_HEADS])
    log_g = (-jnp.exp(prm['a_log'][e]).reshape(1, 2 * A_HEADS)
             * jax.nn.softplus(alpha + prm['dt_bias'][e].reshape(1, 2 * A_HEADS)))
    o_f, o_b, st = delta_scan(*delta_prep(seq(q), seq(k), seq(v), seq(log_g), seq(beta)), s0)
    conf = conformer_branch(p, prm['conf_dw_w'][e], prm['conf_dw_b'][e], prm['conf_ln_g'][e],
                            prm['conf_ln_b'][e], nseq, seq_len)
    x = mix_out(o_f.reshape(t, A_DIM), o_b.reshape(t, A_DIM), p, conf, prm['delta_norm_g'][e],
                prm['w_out_bf'][e], x, g1, rows_per_seg)
    return x, st


def trunk(x, nseq, seq_len, cond, s0, prm):
    t = x.shape[0]
    rows_per_seg = t // cond.shape[0]
    states = []
    for l in range(DEPTH):
        mod = jax.nn.silu(cond) @ prm['ada_w'][l] + prm['ada_b'][l]
        sh1, sc1, g1, sh2, sc2, g2 = [m[:, None, :] for m in jnp.split(mod, 6, axis=-1)]
        e = l // 2
        if l % 2 == 0:
            p = norm_mm(x, prm['norm1_g'][l], sc1, sh1, prm['w_in_bf'][e], rows_per_seg, F32)
            x, st = delta_conformer_layer(x, p, g1, nseq, seq_len, s0[:, e], e, prm, rows_per_seg)
            states.append(st)
        else:
            z = norm_mm(x, prm['norm1_g'][l], sc1, sh1, prm['w_fnet_bf'][e], rows_per_seg, BF16)
            tm = min(seq_len, 512)
            x = seq_mix_res(prm['dft_seq'][seq_len], z, x, g1, seq_len, tm, min(seq_len, 1024))
        ht, a_t, n_t, b_t, r_t = peer_route(x, prm['norm2_g'][l], sc2, sh2, prm['peer_wq_hi'][l],
                                            prm['peer_wq_lo'][l], prm['peer_k1'][l], prm['peer_k2'][l],
                                            rows_per_seg)
        x = peer_dense(ht, prm['peer_u_bf'][l], prm['peer_vt_bf'][l], a_t, n_t, b_t, r_t, x, g2, rows_per_seg)
    xf = x * lax.rsqrt(jnp.mean(x * x, axis=-1, keepdims=True) + EPS) * prm['final_norm_g']
    return xf, jnp.stack(states, axis=1)


def kernel(x_prompt, x_sample, state_delta, c, c_ctx, ada_w, ada_b, norm1_g, norm2_g, w_in_ab, conv_qkv_w,
           a_log, dt_bias, delta_norm_g, conf_dw_w, conf_dw_b, conf_ln_g, conf_ln_b, w_out_ab, w_out_c,
           peer_wq, peer_k1, peer_k2, peer_u, peer_v, final_norm_g):
    bp, sp, _ = x_prompt.shape
    bs, ss, _ = x_sample.shape
    bdc, bds = dft_group_matrices(D_MODEL // C_GROUPS, C_GROUPS)
    w_fnet = [jnp.concatenate([mm3(bdc, w_out_c[e], BF16), mm3(bds, w_out_c[e], BF16)], axis=1)
              for e in range(DEPTH // 2)]
    o4 = 4 * A_DIM
    w_in = jnp.concatenate([w_in_ab[:, :, :o4], w_in_ab[:, :, o4 + 4 * A_HEADS:], w_in_ab[:, :, o4:o4 + 4 * A_HEADS]],
                           axis=-1).astype(BF16)
    wq_hi, wq_lo = split_bf16(peer_wq)
    prm = {'ada_w': ada_w, 'ada_b': ada_b, 'norm1_g': norm1_g, 'norm2_g': norm2_g,
           'w_in_bf': jnp.pad(w_in, ((0, 0), (0, 0), (0, P_AB_PAD - P_AB))),
           'conv_qkv_w': conv_qkv_w, 'a_log': a_log, 'dt_bias': dt_bias,
           'delta_norm_g': delta_norm_g, 'conf_dw_w': conf_dw_w, 'conf_dw_b': conf_dw_b,
           'conf_ln_g': conf_ln_g, 'conf_ln_b': conf_ln_b, 'w_out_bf': w_out_ab.astype(BF16),
           'w_fnet_bf': w_fnet, 'dft_seq': {s: dft_seq_matrix(s) for s in {sp, ss}},
           'peer_wq_hi': wq_hi, 'peer_wq_lo': wq_lo, 'peer_k1': peer_k1, 'peer_k2': peer_k2,
           'peer_u_bf': peer_u.astype(BF16),
           'peer_vt_bf': jnp.transpose(peer_v.astype(BF16).reshape(DEPTH, PEER_N // PEER_EXPERT_TILE,
                                                                   PEER_EXPERT_TILE, D_MODEL), (0, 1, 3, 2)),
           'final_norm_g': final_norm_g}
    ne = (DEPTH + 1) // 2
    s0_ctx = jnp.zeros((bp, ne, 2, A_HEADS, A_DK, A_DV), F32)
    y_prompt, ctx_states = trunk(x_prompt.reshape(bp * sp, D_MODEL), bp, sp, c_ctx[None, :], s0_ctx, prm)
    xs = (x_sample + grid_pos_emb(ss)[None]).reshape(bs * ss, D_MODEL)
    y_sample, _ = trunk(xs, bs, ss, c, state_delta, prm)
    return (y_prompt.reshape(bp, sp, D_MODEL), y_sample.reshape(bs, ss, D_MODEL), ctx_states)
```

```python
import math

import jax
import jax.numpy as jnp
import numpy as np
from jax import lax
from jax.experimental import pallas as pl
from jax.experimental.pallas import tpu as pltpu

D_MODEL = 1024
DEPTH = 4
GRID_W = 64
POS_BASE = 10000.0
EPS = 1e-6
A_HEADS = 4
A_DK = 128
A_DV = 128
A_DIM = A_HEADS * A_DV
CHUNK = 64
B_CH = D_MODEL // 2
P_AB = 4 * A_DIM + 4 * A_HEADS + 2 * B_CH
C_GROUPS = 8
PEER_HEADS = 8
PEER_KEYS = 128
PEER_N = PEER_KEYS * PEER_KEYS
PEER_DK = 128
PEER_TOPK = 16

F32 = jnp.float32
BF16 = jnp.bfloat16
NEG_INF = float("-inf")

LANES = 128
VMEM_LIMIT_BYTES = 56 * 1024 * 1024
MM_ROW_TILE = 512
ROUTE_TOKEN_TILE = 256
PEER_TOKEN_TILE = 512
PEER_KEYS_PER_STEP = 8
PEER_EXPERT_TILE = PEER_KEYS_PER_STEP * PEER_KEYS
PEER_J_BLOCK = 32
PREP_CHUNKS = 2
SCAN_SEQS = 2
P_AB_PAD = 3200
COL_Z = 3 * A_DIM
COL_GLU = 4 * A_DIM
COL_AB = 4 * A_DIM + 2 * B_CH
CONV_ROW_TILE = 256
SHORT_CONV = 7
B_CONV = 31
CONF_HALO = 16
QKV_HALO = 8
PEER_PIECES = 2
PIECE_KEYS = PEER_KEYS_PER_STEP // PEER_PIECES
PIECE_ROWS = PIECE_KEYS * PEER_KEYS
BF16_ROWS = 16

NT_DIMS = (((1,), (1,)), ((), ()))
TN_DIMS = (((0,), (0,)), ((), ()))


def _bdot(a, b):
    return jnp.dot(a, b, preferred_element_type=F32)


def _split_bf16(a):
    hi = a.astype(BF16)
    lo = (a - hi.astype(F32)).astype(BF16)
    return hi, lo


def _dot3(a, b):
    ah, al = _split_bf16(a)
    bh, bl = _split_bf16(b)
    return _bdot(ah, bh) + (_bdot(ah, bl) + _bdot(al, bh))


def _dot_exact_lhs(a01, b):
    a = a01.astype(BF16)
    bh = b.astype(BF16)
    r1 = b - bh.astype(F32)
    bm = r1.astype(BF16)
    bl = (r1 - bm.astype(F32)).astype(BF16)
    return _bdot(a, bh) + (_bdot(a, bm) + _bdot(a, bl))


def _gelu_tanh(x):
    return 0.5 * x * (1.0 + jnp.tanh(math.sqrt(2.0 / math.pi) * (x + 0.044715 * (x * x * x))))


def _seg_spec(rows_per_seg, tile):
    per = rows_per_seg // tile
    return pl.BlockSpec((1, 1, D_MODEL), lambda i, *_: (i // per, 0, 0))


def _norm_modulate(x, g, sc, sh):
    hn = x * lax.rsqrt(jnp.mean(x * x, axis=-1, keepdims=True) + EPS) * g
    return hn * (1.0 + sc) + sh


def _norm_mm_kernel(x_ref, g_ref, sc_ref, sh_ref, w_ref, o_ref):
    h = _norm_modulate(x_ref[...], g_ref[...], sc_ref[0], sh_ref[0]).astype(BF16)
    o_ref[...] = _bdot(h, w_ref[...]).astype(o_ref.dtype)


def norm_mm(x, norm_g, sc, sh, w_bf, rows_per_seg, out_dtype):
    m = x.shape[0]
    n = w_bf.shape[1]
    tm = MM_ROW_TILE
    seg = _seg_spec(rows_per_seg, tm)
    return pl.pallas_call(
        _norm_mm_kernel, grid=(m // tm,),
        in_specs=[pl.BlockSpec((tm, D_MODEL), lambda i: (i, 0)), pl.BlockSpec((1, D_MODEL), lambda i: (0, 0)),
                  seg, seg, pl.BlockSpec((D_MODEL, n), lambda i: (0, 0))],
        out_specs=pl.BlockSpec((tm, n), lambda i: (i, 0)),
        out_shape=jax.ShapeDtypeStruct((m, n), out_dtype),
        compiler_params=pltpu.CompilerParams(dimension_semantics=("parallel",),
                                             vmem_limit_bytes=VMEM_LIMIT_BYTES),
        name="norm_mm",
    )(x, norm_g.reshape(1, D_MODEL), sc, sh, w_bf)


def _split_kernel(a_ref, hi_ref, lo_ref):
    hi, lo = _split_bf16(a_ref[0])
    hi_ref[0] = hi
    lo_ref[0] = lo


def split_bf16(a):
    nl, nr, nc = a.shape
    blk = pl.BlockSpec((1, MM_ROW_TILE, nc), lambda l, i: (l, i, 0))
    return pl.pallas_call(
        _split_kernel, grid=(nl, nr // MM_ROW_TILE), in_specs=[blk], out_specs=[blk, blk],
        out_shape=[jax.ShapeDtypeStruct(a.shape, BF16)] * 2,
        compiler_params=pltpu.CompilerParams(dimension_semantics=("parallel", "parallel")),
        name="split_bf16",
    )(a)


def _mm3_kernel(a_ref, b_ref, o_ref):
    o_ref[...] = _dot3(a_ref[...], b_ref[...]).astype(o_ref.dtype)


def mm3(a, b, out_dtype):
    return pl.pallas_call(_mm3_kernel, out_shape=jax.ShapeDtypeStruct((a.shape[0], b.shape[1]), out_dtype),
                          compiler_params=pltpu.CompilerParams(vmem_limit_bytes=VMEM_LIMIT_BYTES),
                          name="mm3")(a, b)


def _seqmix_kernel(f_ref, z_ref, x_ref, gate_ref, o_ref, acc_ref):
    k = pl.program_id(2)

    @pl.when(k == 0)
    def _():
        acc_ref[...] = jnp.zeros_like(acc_ref)

    acc_ref[...] += _bdot(f_ref[...], z_ref[...])

    @pl.when(k == pl.num_programs(2) - 1)
    def _():
        o_ref[...] = x_ref[...] + gate_ref[0] * acc_ref[...]


def seq_mix_res(fmat, z, x, gate, seq_len, tm, tk):
    t = x.shape[0]
    nseq = t // seq_len
    seqs_per_seg = nseq // gate.shape[0]
    mt = seq_len // tm
    kt_half = seq_len // tk
    return pl.pallas_call(
        _seqmix_kernel, grid=(nseq, mt, 2 * kt_half),
        in_specs=[pl.BlockSpec((tm, tk), lambda s, i, k: (i, k)),
                  pl.BlockSpec((tk, D_MODEL), lambda s, i, k: (s * kt_half + k % kt_half, k // kt_half)),
                  pl.BlockSpec((tm, D_MODEL), lambda s, i, k: (s * mt + i, 0)),
                  pl.BlockSpec((1, 1, D_MODEL), lambda s, i, k: (s // seqs_per_seg, 0, 0))],
        out_specs=pl.BlockSpec((tm, D_MODEL), lambda s, i, k: (s * mt + i, 0)),
        out_shape=jax.ShapeDtypeStruct((t, D_MODEL), F32),
        scratch_shapes=[pltpu.VMEM((tm, D_MODEL), F32)],
        compiler_params=pltpu.CompilerParams(dimension_semantics=("parallel", "parallel", "arbitrary"),
                                             vmem_limit_bytes=VMEM_LIMIT_BYTES),
        name="seq_mix_res",
    )(fmat, z, x, gate)


def _dft_tables(n, cols):
    r = jnp.arange(n, dtype=jnp.int32)[:, None]
    ang = ((r * cols[None, :]) % n).astype(F32) * (2.0 * math.pi / n)
    return jnp.cos(ang), jnp.sin(ang)


def dft_seq_matrix(s):
    w = 1 << (int(math.log2(s)) // 2)
    ch, sh_ = _dft_tables(s, jnp.arange(s // w, dtype=jnp.int32) * w)
    cl, sl = _dft_tables(s, jnp.arange(w, dtype=jnp.int32))
    sc = 1.0 / math.sqrt(s)
    c = (ch[:, :, None] * cl[:, None, :] - sh_[:, :, None] * sl[:, None, :]).reshape(s, s) * sc
    sn = (sh_[:, :, None] * cl[:, None, :] + ch[:, :, None] * sl[:, None, :]).reshape(s, s) * sc
    return jnp.concatenate([c, -sn], axis=1).astype(BF16)


def dft_group_matrices(n, groups):
    c, s = _dft_tables(n, jnp.arange(n, dtype=jnp.int32))
    sc = 1.0 / math.sqrt(n)
    eye = jnp.eye(groups, dtype=F32)
    return jnp.kron(eye, c * sc), jnp.kron(eye, s * sc)


def _delta_prep_kernel(q_ref, k_ref, v_ref, lg_ref, bt_ref, w_ref, u_ref, qd_ref, kd_ref, p_ref, g_ref):
    r = lax.broadcasted_iota(jnp.int32, (CHUNK, CHUNK), 0)
    c = lax.broadcasted_iota(jnp.int32, (CHUNK, CHUNK), 1)
    eye = (r == c).astype(F32)
    ones = jnp.ones((CHUNK, CHUNK), F32)
    incl = (r >= c, r <= c)
    strict = (r > c, r < c)
    tri = (incl[0].astype(F32), incl[1].astype(F32))
    tri_t = (tri[1], tri[0])
    last = (CHUNK - 1, 0)
    chains = [(cg, d, h) for cg in range(PREP_CHUNKS) for d in range(2) for h in range(A_HEADS)]
    rows = lambda cg: slice(cg * CHUNK, (cg + 1) * CHUNK)
    cols = lambda h: slice(h * A_DK, (h + 1) * A_DK)
    kk = {}
    qk = {}
    for cg in range(PREP_CHUNKS):
        for h in range(A_HEADS):
            kb = k_ref[0, rows(cg), cols(h)].astype(BF16)
            kk[cg, h] = lax.dot_general(kb, kb, NT_DIMS, preferred_element_type=F32)
            qk[cg, h] = lax.dot_general(q_ref[0, rows(cg), cols(h)].astype(BF16), kb, NT_DIMS,
                                        preferred_element_type=F32)
    lgw = {}
    btw = {}
    for (cg, d, h) in chains:
        col = d * A_HEADS + h
        lgw[cg, d, h] = jnp.broadcast_to(lg_ref[0, rows(cg), col:col + 1], (CHUNK, A_DK))
        btw[cg, d, h] = jnp.broadcast_to(bt_ref[0, rows(cg), col:col + 1], (CHUNK, A_DK))
    gam = {ch: _dot_exact_lhs(tri[ch[1]], lgw[ch]) for ch in chains}
    gam_row = {ch: _dot_exact_lhs(ones, lgw[ch][:, :CHUNK] * tri_t[ch[1]]) for ch in chains}
    decay = {}
    lmat = {}
    for ch in chains:
        cg, d, h = ch
        diff = gam[ch][:, :CHUNK] - gam_row[ch]
        decay[ch] = jnp.where(incl[d], jnp.exp(jnp.where(incl[d], diff, 0.0)), 0.0)
        lmat[ch] = jnp.where(strict[d], btw[ch][:, :CHUNK] * decay[ch] * kk[cg, h], 0.0)
    pinv = {ch: eye - jnp.where((r // 2 == c // 2) & (r != c), lmat[ch], 0.0) for ch in chains}
    s = 2
    while s < CHUNK:
        join = (r // (2 * s) == c // (2 * s)) & (r // s != c // s)
        tc = {ch: _dot3(pinv[ch], jnp.where(join, lmat[ch], 0.0)) for ch in chains}
        pinv = {ch: pinv[ch] - _dot3(tc[ch], pinv[ch]) for ch in chains}
        s *= 2
    for ch in chains:
        cg, d, h = ch
        kh = k_ref[0, rows(cg), cols(h)]
        vh = v_ref[0, rows(cg), cols(h)]
        qh = q_ref[0, rows(cg), cols(h)]
        egam = jnp.exp(gam[ch])
        rhs = jnp.concatenate([kh * (btw[ch] * egam), vh * btw[ch]], axis=1)
        sol = _dot3(pinv[ch], rhs)
        w_ref[0, d, rows(cg), cols(h)] = sol[:, :A_DK].astype(BF16)
        u_ref[0, d, rows(cg), cols(h)] = sol[:, A_DK:]
        qd_ref[0, d, rows(cg), cols(h)] = (qh * egam).astype(BF16)
        glast = jnp.broadcast_to(gam[ch][last[d]:last[d] + 1, :], (CHUNK, A_DK))
        kd_ref[0, d, rows(cg), cols(h)] = (kh * jnp.exp(glast - gam[ch])).astype(BF16)
        g_ref[0, d, rows(cg), cols(h)] = jnp.exp(glast)
        p_ref[0, d, rows(cg), h * CHUNK:(h + 1) * CHUNK] = (decay[ch] * qk[cg, h]).astype(BF16)


def delta_prep(q, k, v, lg, bt):
    b, s, _ = q.shape
    rt = PREP_CHUNKS * CHUNK
    blk = lambda w: pl.BlockSpec((1, rt, w), lambda bi, ni: (bi, ni, 0))
    oblk = lambda w: pl.BlockSpec((1, 2, rt, w), lambda bi, ni: (bi, 0, ni, 0))
    sh = lambda w, dt: jax.ShapeDtypeStruct((b, 2, s, w), dt)
    return pl.pallas_call(
        _delta_prep_kernel, grid=(b, s // rt),
        in_specs=[blk(A_DIM), blk(A_DIM), blk(A_DIM), blk(2 * A_HEADS), blk(2 * A_HEADS)],
        out_specs=[oblk(A_DIM), oblk(A_DIM), oblk(A_DIM), oblk(A_DIM), oblk(A_HEADS * CHUNK), oblk(A_DIM)],
        out_shape=[sh(A_DIM, BF16), sh(A_DIM, F32), sh(A_DIM, BF16), sh(A_DIM, BF16),
                   sh(A_HEADS * CHUNK, BF16), sh(A_DIM, F32)],
        compiler_params=pltpu.CompilerParams(dimension_semantics=("parallel", "parallel")),
        name="delta_prep",
    )(q, k, v, lg, bt)


def _delta_scan_kernel(*refs):
    ins = refs[:12]
    s0_ref = refs[12]
    of_ref, ob_ref, sout_ref, state = refs[13:]
    n = pl.program_id(1)

    @pl.when(n == 0)
    def _():
        state[...] = s0_ref[...]

    outs = (of_ref, ob_ref)
    chains = [(g, d, h) for g in range(SCAN_SEQS) for d in range(2) for h in range(A_HEADS)]
    cs = lambda h: slice(h * A_DK, (h + 1) * A_DK)
    ref = lambda d, i: ins[d * 6 + i]
    s_old = {ch: state[ch] for ch in chains}
    wqs = {}
    for (g, d, h) in chains:
        wq = jnp.concatenate([ref(d, 0)[g, 0, :, cs(h)], ref(d, 2)[g, 0, :, cs(h)]], axis=0)
        wqs[g, d, h] = _bdot(wq, s_old[g, d, h].astype(BF16))
    unb = {}
    for (g, d, h) in chains:
        unb[g, d, h] = (ref(d, 1)[g, 0, :, cs(h)] - wqs[g, d, h][:CHUNK]).astype(BF16)
    for (g, d, h) in chains:
        o = wqs[g, d, h][CHUNK:] + _bdot(ref(d, 4)[g, 0, :, h * CHUNK:(h + 1) * CHUNK], unb[g, d, h])
        outs[d][g, :, cs(h)] = o
    for (g, d, h) in chains:
        upd = lax.dot_general(ref(d, 3)[g, 0, :, cs(h)], unb[g, d, h], TN_DIMS, preferred_element_type=F32)
        gs = jnp.broadcast_to(ref(d, 5)[g, 0, 0:1, cs(h)], (A_DK, A_DV))
        state[g, d, h] = gs * s_old[g, d, h] + upd

    @pl.when(n == pl.num_programs(1) - 1)
    def _():
        sout_ref[...] = state[...]


def delta_scan(w, u, qd, kd, p, gl, s0):
    b, _, s, _ = u.shape
    n = s // CHUNK

    def spec(wd, d):
        if d == 0:
            return pl.BlockSpec((SCAN_SEQS, 1, CHUNK, wd), lambda bi, ni: (bi, 0, ni, 0))
        return pl.BlockSpec((SCAN_SEQS, 1, CHUNK, wd), lambda bi, ni: (bi, 1, n - 1 - ni, 0))

    arrs = (w, u, qd, kd, p, gl)
    in_specs = [spec(a.shape[-1], d) for d in range(2) for a in arrs]
    st = pl.BlockSpec((SCAN_SEQS, 2, A_HEADS, A_DK, A_DV), lambda bi, ni: (bi, 0, 0, 0, 0))
    of = pl.BlockSpec((SCAN_SEQS, CHUNK, A_DIM), lambda bi, ni: (bi, ni, 0))
    ob = pl.BlockSpec((SCAN_SEQS, CHUNK, A_DIM), lambda bi, ni: (bi, n - 1 - ni, 0))
    return pl.pallas_call(
        _delta_scan_kernel, grid=(b // SCAN_SEQS, n),
        in_specs=in_specs + [st],
        out_specs=[of, ob, st],
        out_shape=[jax.ShapeDtypeStruct((b, s, A_DIM), F32), jax.ShapeDtypeStruct((b, s, A_DIM), F32),
                   jax.ShapeDtypeStruct((b, 2, A_HEADS, A_DK, A_DV), F32)],
        scratch_shapes=[pltpu.VMEM((SCAN_SEQS, 2, A_HEADS, A_DK, A_DV), F32)],
        compiler_params=pltpu.CompilerParams(dimension_semantics=("parallel", "arbitrary")),
        name="delta_scan",
    )(*(arrs + arrs), s0)


def _cand_tables():
    pairs = [(r, c) for r in range(PEER_TOPK) for c in range(PEER_TOPK) if (r + 1) * (c + 1) <= PEER_TOPK]
    npad = 64
    e1 = np.zeros((npad, PEER_TOPK), np.float32)
    e2 = np.zeros((npad, PEER_TOPK), np.float32)
    m = np.zeros((PEER_TOPK, npad), np.float32)
    for k, (r, c) in enumerate(pairs):
        e1[k, r] = 1
        e2[k, c] = 1
        m[r, k] = 1
    return len(pairs), e1, e2, m


N_CAND, _CAND_E1, _CAND_E2, _CAND_ROW = _cand_tables()


def _extract_topk(s, n_iter):
    k, t = s.shape
    work = s.reshape(k // 8, 8, t)
    rank = jnp.full(work.shape, float(n_iter), F32)
    vals = []
    for r in range(n_iter):
        m = jnp.max(jnp.max(work, axis=0), axis=0, keepdims=True)
        hit = work == jnp.broadcast_to(m, (8, t))[None]
        rank = jnp.where(hit, float(r), rank)
        work = jnp.where(hit, NEG_INF, work)
        vals.append(m)
    return vals, rank.reshape(k, t)


def _route_kernel(x_ref, g_ref, sc_ref, sh_ref, wqh_ref, wql_ref, k1_ref, k2_ref, e1_ref, e2_ref, mrow_ref,
                  ht_ref, a_ref, n_ref, b_ref, r_ref, q_scr):
    hmod = _norm_modulate(x_ref[...], g_ref[...], sc_ref[0], sh_ref[0])
    ht_ref[...] = hmod.T.astype(BF16)
    hh, hl = _split_bf16(hmod)
    q_scr[...] = _bdot(hh, wqh_ref[...]) + (_bdot(hh, wql_ref[...]) + _bdot(hl, wqh_ref[...]))
    tt = x_ref.shape[0]

    def head(h, carry):
        c1 = pl.ds(pl.multiple_of(h * 2 * PEER_DK, PEER_DK), PEER_DK)
        c2 = pl.ds(pl.multiple_of(h * 2 * PEER_DK + PEER_DK, PEER_DK), PEER_DK)
        hp = lax.Precision.HIGHEST
        s1 = lax.dot_general(k1_ref[h], q_scr[:, c1], NT_DIMS, precision=hp, preferred_element_type=F32)
        s2 = lax.dot_general(k2_ref[h], q_scr[:, c2], NT_DIMS, precision=hp, preferred_element_type=F32)
        v1, rank1 = _extract_topk(s1, PEER_TOPK)
        v2, rank2 = _extract_topk(s2, PEER_TOPK)
        v1m = jnp.concatenate(v1, axis=0)
        v2m = jnp.concatenate(v2, axis=0)
        cand = (jnp.dot(e1_ref[...], v1m, precision=hp, preferred_element_type=F32)
                + jnp.dot(e2_ref[...], v2m, precision=hp, preferred_element_type=F32))
        row = lax.broadcasted_iota(jnp.int32, cand.shape, 0)
        cand = jnp.where(row < N_CAND, cand, NEG_INF)
        _, crank = _extract_topk(cand, PEER_TOPK)
        sel = crank < float(PEER_TOPK)
        cmax = v1[0] + v2[0]
        z = jnp.sum(jnp.where(sel, jnp.exp(cand - cmax), 0.0), axis=0, keepdims=True)
        n_r = _bdot(mrow_ref[...], jnp.where(sel, 1.0, 0.0).astype(BF16))
        rank1_3 = rank1.reshape(PEER_KEYS // 8, 8, tt)
        nn = jnp.zeros_like(rank1_3)
        for r in range(PEER_TOPK):
            nn = jnp.where(rank1_3 == float(r), jnp.broadcast_to(n_r[r:r + 1, :], (8, tt))[None], nn)
        a_ref[h] = jnp.exp(s1 - v1[0]) / z
        n_ref[h] = nn.reshape(PEER_KEYS, tt)
        b_ref[h] = jnp.where(rank2 < float(PEER_TOPK), jnp.exp(s2 - v2[0]), 0.0).astype(BF16)
        r_ref[h] = rank2.astype(BF16)
        return carry

    lax.fori_loop(0, PEER_HEADS, head, 0, unroll=2)


def peer_route(x, norm_g, sc, sh, wq_hi, wq_lo, k1, k2, rows_per_seg):
    t = x.shape[0]
    tt = ROUTE_TOKEN_TILE
    gate = pl.BlockSpec((PEER_HEADS, PEER_KEYS, tt), lambda i: (0, 0, i))
    full = lambda shp: pl.BlockSpec(shp, lambda i: (0,) * len(shp))
    seg = _seg_spec(rows_per_seg, tt)
    return pl.pallas_call(
        _route_kernel, grid=(t // tt,),
        in_specs=[pl.BlockSpec((tt, D_MODEL), lambda i: (i, 0)), full((1, D_MODEL)), seg, seg,
                  full((D_MODEL, 2 * PEER_HEADS * PEER_DK)), full((D_MODEL, 2 * PEER_HEADS * PEER_DK)),
                  full((PEER_HEADS, PEER_KEYS, PEER_DK)),
                  full((PEER_HEADS, PEER_KEYS, PEER_DK)), full((64, PEER_TOPK)), full((64, PEER_TOPK)),
                  full((PEER_TOPK, 64))],
        out_specs=[pl.BlockSpec((D_MODEL, tt), lambda i: (0, i)), gate, gate, gate, gate],
        out_shape=[jax.ShapeDtypeStruct((D_MODEL, t), BF16)]
        + [jax.ShapeDtypeStruct((PEER_HEADS, PEER_KEYS, t), dt) for dt in (F32, F32, BF16, BF16)],
        scratch_shapes=[pltpu.VMEM((tt, 2 * PEER_HEADS * PEER_DK), F32)],
        compiler_params=pltpu.CompilerParams(dimension_semantics=("parallel",),
                                             vmem_limit_bytes=VMEM_LIMIT_BYTES),
        name="peer_route",
    )(x, norm_g.reshape(1, D_MODEL), sc, sh, wq_hi, wq_lo, k1, k2,
      jnp.asarray(_CAND_E1), jnp.asarray(_CAND_E2), jnp.asarray(_CAND_ROW, BF16))


def _peer_dense_kernel(ht_ref, u_ref, vt_ref, a_ref, n_ref, b_ref, r_ref, x_ref, g2_ref, o_ref,
                       acc_ref, hid_ref, w_ref, bp_ref, rp_ref):
    e = pl.program_id(1)

    @pl.when(e == 0)
    def _():
        acc_ref[...] = jnp.zeros_like(acc_ref)
        for c in range(PEER_TOKEN_TILE // LANES):
            ls = slice(c * LANES, (c + 1) * LANES)
            bp_ref[:, c] = b_ref[:, :, ls].astype(BF16)
            rp_ref[:, c] = r_ref[:, :, ls].astype(BF16)

    def hidden(pp):
        rows = slice(pp * PIECE_ROWS, (pp + 1) * PIECE_ROWS)
        hid_ref[pp % 2] = _bdot(u_ref[rows, :], ht_ref[...])

    def gates(pp):
        for c in range(PEER_TOKEN_TILE // LANES):
            ls = slice(c * LANES, (c + 1) * LANES)
            for ii in range(PIECE_KEYS):
                k = pp * PIECE_KEYS + ii
                acc = jnp.zeros((PEER_KEYS, LANES), BF16)
                for h in range(PEER_HEADS):
                    a16 = jnp.broadcast_to(a_ref[h, k:k + 1, ls], (BF16_ROWS, LANES)).astype(BF16)
                    n16 = jnp.broadcast_to(n_ref[h, k:k + 1, ls], (BF16_ROWS, LANES)).astype(BF16)
                    a128 = pltpu.repeat(a16, PEER_KEYS // BF16_ROWS, axis=0)
                    n128 = pltpu.repeat(n16, PEER_KEYS // BF16_ROWS, axis=0)
                    b = bp_ref[h, c]
                    acc = acc + jnp.where(rp_ref[h, c] < n128, b, jnp.zeros_like(b)) * a128
                rows = slice(ii * PEER_KEYS, (ii + 1) * PEER_KEYS)
                w_ref[pp % 2, rows, ls] = _gelu_tanh(hid_ref[pp % 2, rows, ls]).astype(BF16) * acc

    def project(pp):
        rows = slice(pp * PIECE_ROWS, (pp + 1) * PIECE_ROWS)
        acc_ref[...] += _bdot(vt_ref[0, :, rows], w_ref[pp % 2])

    hidden(0)
    for pp in range(PEER_PIECES):
        if pp + 1 < PEER_PIECES:
            hidden(pp + 1)
        gates(pp)
        project(pp)

    @pl.when(e == pl.num_programs(1) - 1)
    def _():
        o_ref[...] = x_ref[...] + g2_ref[0] * acc_ref[...].T


def peer_dense(ht_bf, u_bf, vt_bf, a_t, n_t, b_t, r_t, x, g2, rows_per_seg):
    t = x.shape[0]
    tt = PEER_TOKEN_TILE
    grid = (t // tt, PEER_N // PEER_EXPERT_TILE)
    gate_spec = pl.BlockSpec((PEER_HEADS, PEER_KEYS, tt), lambda ti, ei: (0, 0, ti))
    step_keys = pl.BlockSpec((PEER_HEADS, PEER_KEYS_PER_STEP, tt), lambda ti, ei: (0, ei, ti))
    return pl.pallas_call(
        _peer_dense_kernel, grid=grid,
        in_specs=[pl.BlockSpec((D_MODEL, tt), lambda ti, ei: (0, ti)),
                  pl.BlockSpec((PEER_EXPERT_TILE, D_MODEL), lambda ti, ei: (ei, 0)),
                  pl.BlockSpec((1, D_MODEL, PEER_EXPERT_TILE), lambda ti, ei: (ei, 0, 0)),
                  step_keys, step_keys, gate_spec, gate_spec,
                  pl.BlockSpec((tt, D_MODEL), lambda ti, ei: (ti, 0)),
                  _seg_spec(rows_per_seg, tt)],
        out_specs=pl.BlockSpec((tt, D_MODEL), lambda ti, ei: (ti, 0)),
        out_shape=jax.ShapeDtypeStruct((t, D_MODEL), F32),
        scratch_shapes=[pltpu.VMEM((D_MODEL, tt), F32),
                        pltpu.VMEM((2, PIECE_ROWS, tt), F32),
                        pltpu.VMEM((2, PIECE_ROWS, tt), BF16),
                        pltpu.VMEM((PEER_HEADS, tt // LANES, PEER_KEYS, LANES), BF16),
                        pltpu.VMEM((PEER_HEADS, tt // LANES, PEER_KEYS, LANES), BF16)],
        compiler_params=pltpu.CompilerParams(dimension_semantics=("parallel", "arbitrary"),
                                             vmem_limit_bytes=VMEM_LIMIT_BYTES),
        name="peer_dense",
    )(ht_bf, u_bf, vt_bf, a_t, n_t, b_t, r_t, x, g2)


def _fill_halo_scratch(scr, prev, cur, nxt, halo, ts):
    i = pl.program_id(1)
    scr[0:halo, :] = jnp.where(i > 0, prev, 0.0)
    scr[halo:halo + ts, :] = cur
    scr[halo + ts:halo + ts + halo, :] = jnp.where(i < pl.num_programs(1) - 1, nxt, 0.0)


def _depthwise_taps(scr, w_ref, taps, halo, r0, rows):
    off = halo - taps // 2
    acc = scr[off + r0:off + r0 + rows, :] * w_ref[0:1, :]
    for k in range(1, taps):
        acc = acc + scr[off + r0 + k:off + r0 + k + rows, :] * w_ref[k:k + 1, :]
    return acc


def _halo_specs(width, col_block, halo, ts, seq_len, total_rows):
    tiles = seq_len // ts
    per_tile = ts // halo
    last = total_rows // halo - 1
    prev = pl.BlockSpec((halo, width), lambda s, i: (jnp.maximum((s * tiles + i) * per_tile - 1, 0), col_block))
    cur = pl.BlockSpec((ts, width), lambda s, i: (s * tiles + i, col_block))
    nxt = pl.BlockSpec((halo, width), lambda s, i: (jnp.minimum((s * tiles + i + 1) * per_tile, last), col_block))
    return prev, cur, nxt


CONF_ROW_BLOCK = 32
QKV_ROW_BLOCK = 16


def _conformer_kernel(prev_ref, cur_ref, next_ref, w_ref, b_ref, lg_ref, lb_ref, o_ref, scr):
    ts = cur_ref.shape[0]
    glu = lambda blk: blk[:, :B_CH] * jax.nn.sigmoid(blk[:, B_CH:])
    _fill_halo_scratch(scr, glu(prev_ref[...]), glu(cur_ref[...]), glu(next_ref[...]), CONF_HALO, ts)
    for rb in range(ts // CONF_ROW_BLOCK):
        r0 = rb * CONF_ROW_BLOCK
        hh = _depthwise_taps(scr, w_ref, B_CONV, CONF_HALO, r0, CONF_ROW_BLOCK) + b_ref[...]
        mu = jnp.mean(hh, axis=-1, keepdims=True)
        var = jnp.mean(jnp.square(hh - mu), axis=-1, keepdims=True)
        y = (hh - mu) * lax.rsqrt(var + EPS) * lg_ref[...] + lb_ref[...]
        o_ref[r0:r0 + CONF_ROW_BLOCK, :] = (y * jax.nn.sigmoid(y)).astype(o_ref.dtype)


def conformer_branch(p, dw_w, dw_b, ln_g, ln_b, nseq, seq_len):
    t = p.shape[0]
    ts = CONV_ROW_TILE
    prev, cur, nxt = _halo_specs(2 * B_CH, COL_GLU // (2 * B_CH), CONF_HALO, ts, seq_len, t)
    row = lambda: pl.BlockSpec((1, B_CH), lambda s, i: (0, 0))
    w_pad = jnp.pad(dw_w, ((0, 32 - B_CONV), (0, 0)))
    return pl.pallas_call(
        _conformer_kernel, grid=(nseq, seq_len // ts),
        in_specs=[prev, cur, nxt, pl.BlockSpec((32, B_CH), lambda s, i: (0, 0)), row(), row(), row()],
        out_specs=pl.BlockSpec((ts, B_CH), lambda s, i: (s * (seq_len // ts) + i, 0)),
        out_shape=jax.ShapeDtypeStruct((t, B_CH), BF16),
        scratch_shapes=[pltpu.VMEM((ts + 2 * CONF_HALO, B_CH), F32)],
        compiler_params=pltpu.CompilerParams(dimension_semantics=("parallel", "parallel")),
        name="conformer_branch",
    )(p, p, p, w_pad, dw_b.reshape(1, B_CH), ln_g.reshape(1, B_CH), ln_b.reshape(1, B_CH))


def _qkv_conv_kernel(prev_ref, cur_ref, next_ref, w_ref, q_ref, k_ref, v_ref, scr):
    ts = cur_ref.shape[0]
    _fill_halo_scratch(scr, prev_ref[...], cur_ref[...], next_ref[...], QKV_HALO, ts)
    for rb in range(ts // QKV_ROW_BLOCK):
        r0 = rb * QKV_ROW_BLOCK
        rows = slice(r0, r0 + QKV_ROW_BLOCK)
        y = _depthwise_taps(scr, w_ref, SHORT_CONV, QKV_HALO, r0, QKV_ROW_BLOCK)
        y = y * jax.nn.sigmoid(y)
        for h in range(A_HEADS):
            cs = slice(h * A_DK, (h + 1) * A_DK)
            qh = y[:, h * A_DK:(h + 1) * A_DK]
            kh = y[:, A_DIM + h * A_DK:A_DIM + (h + 1) * A_DK]
            q_ref[rows, cs] = qh * (lax.rsqrt(jnp.sum(qh * qh, axis=-1, keepdims=True) + EPS) * (A_DK ** -0.5))
            k_ref[rows, cs] = kh * lax.rsqrt(jnp.sum(kh * kh, axis=-1, keepdims=True) + EPS)
        v_ref[rows, :] = y[:, 2 * A_DIM:]


def qkv_conv(p, conv_w, nseq, seq_len):
    t = p.shape[0]
    ts = CONV_ROW_TILE
    prev, cur, nxt = _halo_specs(3 * A_DIM, 0, QKV_HALO, ts, seq_len, t)
    out = pl.BlockSpec((ts, A_DIM), lambda s, i: (s * (seq_len // ts) + i, 0))
    w_pad = jnp.pad(conv_w, ((0, 8 - SHORT_CONV), (0, 0)))
    return pl.pallas_call(
        _qkv_conv_kernel, grid=(nseq, seq_len // ts),
        in_specs=[prev, cur, nxt, pl.BlockSpec((8, 3 * A_DIM), lambda s, i: (0, 0))],
        out_specs=[out, out, out],
        out_shape=[jax.ShapeDtypeStruct((t, A_DIM), F32)] * 3,
        scratch_shapes=[pltpu.VMEM((ts + 2 * QKV_HALO, 3 * A_DIM), F32)],
        compiler_params=pltpu.CompilerParams(dimension_semantics=("parallel", "parallel")),
        name="qkv_conv",
    )(p, p, p, w_pad)


def _mix_out_kernel(of_ref, ob_ref, z_ref, conf_ref, ng_ref, w_ref, x_ref, gate_ref, o_ref):
    o = of_ref[...] + ob_ref[...]
    z = z_ref[...]
    parts = []
    for h in range(A_HEADS):
        cs = slice(h * A_DV, (h + 1) * A_DV)
        oh = o[:, cs]
        zh = z[:, cs]
        scale = lax.rsqrt(jnp.mean(oh * oh, axis=-1, keepdims=True) + EPS)
        parts.append((oh * scale * ng_ref[...] * (zh * jax.nn.sigmoid(zh))).astype(BF16))
    oa = jnp.concatenate(parts, axis=1)
    mix = _bdot(oa, w_ref[:A_DIM, :]) + _bdot(conf_ref[...], w_ref[A_DIM:, :])
    o_ref[...] = x_ref[...] + gate_ref[0] * mix


def mix_out(o_f, o_b, p, conf, norm_g, w_bf, x, gate, rows_per_seg):
    t = x.shape[0]
    tm = MM_ROW_TILE
    half = lambda cb: pl.BlockSpec((tm, A_DIM), lambda i: (i, cb))
    return pl.pallas_call(
        _mix_out_kernel, grid=(t // tm,),
        in_specs=[half(0), half(0), half(COL_Z // A_DIM), half(0), pl.BlockSpec((1, A_DV), lambda i: (0, 0)),
                  pl.BlockSpec((D_MODEL, D_MODEL), lambda i: (0, 0)), pl.BlockSpec((tm, D_MODEL), lambda i: (i, 0)),
                  _seg_spec(rows_per_seg, tm)],
        out_specs=pl.BlockSpec((tm, D_MODEL), lambda i: (i, 0)),
        out_shape=jax.ShapeDtypeStruct((t, D_MODEL), F32),
        compiler_params=pltpu.CompilerParams(dimension_semantics=("parallel",),
                                             vmem_limit_bytes=VMEM_LIMIT_BYTES),
        name="mix_out",
    )(o_f, o_b, p, conf, norm_g.reshape(1, A_DV), w_bf, x, gate)


def grid_pos_emb(n_tokens):
    rows = n_tokens // GRID_W
    r = jnp.repeat(jnp.arange(rows, dtype=F32), GRID_W)
    col = jnp.tile(jnp.arange(GRID_W, dtype=F32), rows)
    nf = D_MODEL // 4
    freqs = jnp.exp(-math.log(POS_BASE) * jnp.arange(nf, dtype=F32) / nf)
    ar = r[:, None] * freqs
    ac = col[:, None] * freqs
    return jnp.concatenate([jnp.sin(ar), jnp.cos(ar), jnp.sin(ac), jnp.cos(ac)], axis=-1)


def delta_conformer_layer(x, p, g1, nseq, seq_len, s0, e, prm, rows_per_seg):
    t = p.shape[0]
    seq = lambda m: m.reshape(nseq, seq_len, m.shape[-1])
    q, k, v = qkv_conv(p, prm['conv_qkv_w'][e], nseq, seq_len)
    alpha = p[:, COL_AB:COL_AB + 2 * A_HEADS]
    beta = jax.nn.sigmoid(p[:, COL_AB + 2 * A_HEADS:COL_AB + 4 * A_HEADS])
    log_g = (-jnp.exp(prm['a_log'][e]).reshape(1, 2 * A_HEADS)
             * jax.nn.softplus(alpha + prm['dt_bias'][e].reshape(1, 2 * A_HEADS)))
    o_f, o_b, st = delta_scan(*delta_prep(seq(q), seq(k), seq(v), seq(log_g), seq(beta)), s0)
    conf = conformer_branch(p, prm['conf_dw_w'][e], prm['conf_dw_b'][e], prm['conf_ln_g'][e],
                            prm['conf_ln_b'][e], nseq, seq_len)
    x = mix_out(o_f.reshape(t, A_DIM), o_b.reshape(t, A_DIM), p, conf, prm['delta_norm_g'][e],
                prm['w_out_bf'][e], x, g1, rows_per_seg)
    return x, st


def trunk(x, nseq, seq_len, cond, s0, prm):
    t = x.shape[0]
    rows_per_seg = t // cond.shape[0]
    states = []
    for l in range(DEPTH):
        mod = jax.nn.silu(cond) @ prm['ada_w'][l] + prm['ada_b'][l]
        sh1, sc1, g1, sh2, sc2, g2 = [m[:, None, :] for m in jnp.split(mod, 6, axis=-1)]
        e = l // 2
        if l % 2 == 0:
            p = norm_mm(x, prm['norm1_g'][l], sc1, sh1, prm['w_in_bf'][e], rows_per_seg, F32)
            x, st = delta_conformer_layer(x, p, g1, nseq, seq_len, s0[:, e], e, prm, rows_per_seg)
            states.append(st)
        else:
            z = norm_mm(x, prm['norm1_g'][l], sc1, sh1, prm['w_fnet_bf'][e], rows_per_seg, BF16)
            tm = min(seq_len, 512)
            x = seq_mix_res(prm['dft_seq'][seq_len], z, x, g1, seq_len, tm, min(seq_len, 1024))
        ht, a_t, n_t, b_t, r_t = peer_route(x, prm['norm2_g'][l], sc2, sh2, prm['peer_wq_hi'][l],
                                            prm['peer_wq_lo'][l], prm['peer_k1'][l], prm['peer_k2'][l],
                                            rows_per_seg)
        x = peer_dense(ht, prm['peer_u_bf'][l], prm['peer_vt_bf'][l], a_t, n_t, b_t, r_t, x, g2, rows_per_seg)
    xf = x * lax.rsqrt(jnp.mean(x * x, axis=-1, keepdims=True) + EPS) * prm['final_norm_g']
    return xf, jnp.stack(states, axis=1)


def kernel(x_prompt, x_sample, state_delta, c, c_ctx, ada_w, ada_b, norm1_g, norm2_g, w_in_ab, conv_qkv_w,
           a_log, dt_bias, delta_norm_g, conf_dw_w, conf_dw_b, conf_ln_g, conf_ln_b, w_out_ab, w_out_c,
           peer_wq, peer_k1, peer_k2, peer_u, peer_v, final_norm_g):
    bp, sp, _ = x_prompt.shape
    bs, ss, _ = x_sample.shape
    bdc, bds = dft_group_matrices(D_MODEL // C_GROUPS, C_GROUPS)
    w_fnet = [jnp.concatenate([mm3(bdc, w_out_c[e], BF16), mm3(bds, w_out_c[e], BF16)], axis=1)
              for e in range(DEPTH // 2)]
    o4 = 4 * A_DIM
    w_in = jnp.concatenate([w_in_ab[:, :, :o4], w_in_ab[:, :, o4 + 4 * A_HEADS:], w_in_ab[:, :, o4:o4 + 4 * A_HEADS]],
                           axis=-1).astype(BF16)
    wq_hi, wq_lo = split_bf16(peer_wq)
    prm = {'ada_w': ada_w, 'ada_b': ada_b, 'norm1_g': norm1_g, 'norm2_g': norm2_g,
           'w_in_bf': jnp.pad(w_in, ((0, 0), (0, 0), (0, P_AB_PAD - P_AB))),
           'conv_qkv_w': conv_qkv_w, 'a_log': a_log, 'dt_bias': dt_bias,
           'delta_norm_g': delta_norm_g, 'conf_dw_w': conf_dw_w, 'conf_dw_b': conf_dw_b,
           'conf_ln_g': conf_ln_g, 'conf_ln_b': conf_ln_b, 'w_out_bf': w_out_ab.astype(BF16),
           'w_fnet_bf': w_fnet, 'dft_seq': {s: dft_seq_matrix(s) for s in {sp, ss}},
           'peer_wq_hi': wq_hi, 'peer_wq_lo': wq_lo, 'peer_k1': peer_k1, 'peer_k2': peer_k2,
           'peer_u_bf': peer_u.astype(BF16),
           'peer_vt_bf': jnp.transpose(peer_v.astype(BF16).reshape(DEPTH, PEER_N // PEER_EXPERT_TILE,
                                                                   PEER_EXPERT_TILE, D_MODEL), (0, 1, 3, 2)),
           'final_norm_g': final_norm_g}
    ne = (DEPTH + 1) // 2
    s0_ctx = jnp.zeros((bp, ne, 2, A_HEADS, A_DK, A_DV), F32)
    y_prompt, ctx_states = trunk(x_prompt.reshape(bp * sp, D_MODEL), bp, sp, c_ctx[None, :], s0_ctx, prm)
    xs = (x_sample + grid_pos_emb(ss)[None]).reshape(bs * ss, D_MODEL)
    y_sample, _ = trunk(xs, bs, ss, c, state_delta, prm)
    return (y_prompt.reshape(bp, sp, D_MODEL), y_sample.reshape(bs, ss, D_MODEL), ctx_states)
```

```python
import math

import jax
import jax.numpy as jnp
import numpy as np
from jax import lax
from jax.experimental import pallas as pl
from jax.experimental.pallas import tpu as pltpu

D_MODEL = 1024
DEPTH = 4
GRID_W = 64
POS_BASE = 10000.0
EPS = 1e-6
A_HEADS = 4
A_DK = 128
A_DV = 128
A_DIM = A_HEADS * A_DV
CHUNK = 64
B_CH = D_MODEL // 2
P_AB = 4 * A_DIM + 4 * A_HEADS + 2 * B_CH
C_GROUPS = 8
PEER_HEADS = 8
PEER_KEYS = 128
PEER_N = PEER_KEYS * PEER_KEYS
PEER_DK = 128
PEER_TOPK = 16

F32 = jnp.float32
BF16 = jnp.bfloat16
NEG_INF = float("-inf")

LANES = 128
VMEM_LIMIT_BYTES = 56 * 1024 * 1024
MM_ROW_TILE = 512
ROUTE_TOKEN_TILE = 256
PEER_TOKEN_TILE = 512
PEER_KEYS_PER_STEP = 8
PEER_EXPERT_TILE = PEER_KEYS_PER_STEP * PEER_KEYS
PEER_J_BLOCK = 32
PREP_CHUNKS = 2
SCAN_SEQS = 2
P_AB_PAD = 3200
COL_Z = 3 * A_DIM
COL_GLU = 4 * A_DIM
COL_AB = 4 * A_DIM + 2 * B_CH
CONV_ROW_TILE = 256
SHORT_CONV = 7
B_CONV = 31
CONF_HALO = 16
QKV_HALO = 8
PEER_PIECES = 2
PIECE_KEYS = PEER_KEYS_PER_STEP // PEER_PIECES
PIECE_ROWS = PIECE_KEYS * PEER_KEYS
BF16_ROWS = 16

NT_DIMS = (((1,), (1,)), ((), ()))
TN_DIMS = (((0,), (0,)), ((), ()))


def _bdot(a, b):
    return jnp.dot(a, b, preferred_element_type=F32)


def _split_bf16(a):
    hi = a.astype(BF16)
    lo = (a - hi.astype(F32)).astype(BF16)
    return hi, lo


def _dot3(a, b):
    ah, al = _split_bf16(a)
    bh, bl = _split_bf16(b)
    return _bdot(ah, bh) + (_bdot(ah, bl) + _bdot(al, bh))


def _dot_exact_lhs(a01, b):
    a = a01.astype(BF16)
    bh = b.astype(BF16)
    r1 = b - bh.astype(F32)
    bm = r1.astype(BF16)
    bl = (r1 - bm.astype(F32)).astype(BF16)
    return _bdot(a, bh) + (_bdot(a, bm) + _bdot(a, bl))


def _gelu_tanh(x):
    return 0.5 * x * (1.0 + jnp.tanh(math.sqrt(2.0 / math.pi) * (x + 0.044715 * (x * x * x))))


def _seg_spec(rows_per_seg, tile):
    per = rows_per_seg // tile
    return pl.BlockSpec((1, 1, D_MODEL), lambda i, *_: (i // per, 0, 0))


def _norm_modulate(x, g, sc, sh):
    hn = x * lax.rsqrt(jnp.mean(x * x, axis=-1, keepdims=True) + EPS) * g
    return hn * (1.0 + sc) + sh


def _norm_mm_kernel(x_ref, g_ref, sc_ref, sh_ref, w_ref, o_ref):
    h = _norm_modulate(x_ref[...], g_ref[...], sc_ref[0], sh_ref[0]).astype(BF16)
    o_ref[...] = _bdot(h, w_ref[...]).astype(o_ref.dtype)


def norm_mm(x, norm_g, sc, sh, w_bf, rows_per_seg, out_dtype):
    m = x.shape[0]
    n = w_bf.shape[1]
    tm = MM_ROW_TILE
    seg = _seg_spec(rows_per_seg, tm)
    return pl.pallas_call(
        _norm_mm_kernel, grid=(m // tm,),
        in_specs=[pl.BlockSpec((tm, D_MODEL), lambda i: (i, 0)), pl.BlockSpec((1, D_MODEL), lambda i: (0, 0)),
                  seg, seg, pl.BlockSpec((D_MODEL, n), lambda i: (0, 0))],
        out_specs=pl.BlockSpec((tm, n), lambda i: (i, 0)),
        out_shape=jax.ShapeDtypeStruct((m, n), out_dtype),
        compiler_params=pltpu.CompilerParams(dimension_semantics=("parallel",),
                                             vmem_limit_bytes=VMEM_LIMIT_BYTES),
        name="norm_mm",
    )(x, norm_g.reshape(1, D_MODEL), sc, sh, w_bf)


def _split_kernel(a_ref, hi_ref, lo_ref):
    hi, lo = _split_bf16(a_ref[0])
    hi_ref[0] = hi
    lo_ref[0] = lo


def split_bf16(a):
    nl, nr, nc = a.shape
    blk = pl.BlockSpec((1, MM_ROW_TILE, nc), lambda l, i: (l, i, 0))
    return pl.pallas_call(
        _split_kernel, grid=(nl, nr // MM_ROW_TILE), in_specs=[blk], out_specs=[blk, blk],
        out_shape=[jax.ShapeDtypeStruct(a.shape, BF16)] * 2,
        compiler_params=pltpu.CompilerParams(dimension_semantics=("parallel", "parallel")),
        name="split_bf16",
    )(a)


def _mm3_kernel(a_ref, b_ref, o_ref):
    o_ref[...] = _dot3(a_ref[...], b_ref[...]).astype(o_ref.dtype)


def mm3(a, b, out_dtype):
    return pl.pallas_call(_mm3_kernel, out_shape=jax.ShapeDtypeStruct((a.shape[0], b.shape[1]), out_dtype),
                          compiler_params=pltpu.CompilerParams(vmem_limit_bytes=VMEM_LIMIT_BYTES),
                          name="mm3")(a, b)


def _seqmix_kernel(f_ref, z_ref, x_ref, gate_ref, o_ref, acc_ref):
    k = pl.program_id(2)

    @pl.when(k == 0)
    def _():
        acc_ref[...] = jnp.zeros_like(acc_ref)

    acc_ref[...] += _bdot(f_ref[...], z_ref[...])

    @pl.when(k == pl.num_programs(2) - 1)
    def _():
        o_ref[...] = x_ref[...] + gate_ref[0] * acc_ref[...]


def seq_mix_res(fmat, z, x, gate, seq_len, tm, tk):
    t = x.shape[0]
    nseq = t // seq_len
    seqs_per_seg = nseq // gate.shape[0]
    mt = seq_len // tm
    kt_half = seq_len // tk
    return pl.pallas_call(
        _seqmix_kernel, grid=(nseq, mt, 2 * kt_half),
        in_specs=[pl.BlockSpec((tm, tk), lambda s, i, k: (i, k)),
                  pl.BlockSpec((tk, D_MODEL), lambda s, i, k: (s * kt_half + k % kt_half, k // kt_half)),
                  pl.BlockSpec((tm, D_MODEL), lambda s, i, k: (s * mt + i, 0)),
                  pl.BlockSpec((1, 1, D_MODEL), lambda s, i, k: (s // seqs_per_seg, 0, 0))],
        out_specs=pl.BlockSpec((tm, D_MODEL), lambda s, i, k: (s * mt + i, 0)),
        out_shape=jax.ShapeDtypeStruct((t, D_MODEL), F32),
        scratch_shapes=[pltpu.VMEM((tm, D_MODEL), F32)],
        compiler_params=pltpu.CompilerParams(dimension_semantics=("parallel", "parallel", "arbitrary"),
                                             vmem_limit_bytes=VMEM_LIMIT_BYTES),
        name="seq_mix_res",
    )(fmat, z, x, gate)


def _dft_tables(n, cols):
    r = jnp.arange(n, dtype=jnp.int32)[:, None]
    ang = ((r * cols[None, :]) % n).astype(F32) * (2.0 * math.pi / n)
    return jnp.cos(ang), jnp.sin(ang)


def dft_seq_matrix(s):
    w = 1 << (int(math.log2(s)) // 2)
    ch, sh_ = _dft_tables(s, jnp.arange(s // w, dtype=jnp.int32) * w)
    cl, sl = _dft_tables(s, jnp.arange(w, dtype=jnp.int32))
    sc = 1.0 / math.sqrt(s)
    c = (ch[:, :, None] * cl[:, None, :] - sh_[:, :, None] * sl[:, None, :]).reshape(s, s) * sc
    sn = (sh_[:, :, None] * cl[:, None, :] + ch[:, :, None] * sl[:, None, :]).reshape(s, s) * sc
    return jnp.concatenate([c, -sn], axis=1).astype(BF16)


def dft_group_matrices(n, groups):
    c, s = _dft_tables(n, jnp.arange(n, dtype=jnp.int32))
    sc = 1.0 / math.sqrt(n)
    eye = jnp.eye(groups, dtype=F32)
    return jnp.kron(eye, c * sc), jnp.kron(eye, s * sc)


def _delta_prep_kernel(q_ref, k_ref, v_ref, lg_ref, bt_ref, w_ref, u_ref, qd_ref, kd_ref, p_ref, g_ref):
    r = lax.broadcasted_iota(jnp.int32, (CHUNK, CHUNK), 0)
    c = lax.broadcasted_iota(jnp.int32, (CHUNK, CHUNK), 1)
    eye = (r == c).astype(F32)
    ones = jnp.ones((CHUNK, CHUNK), F32)
    incl = (r >= c, r <= c)
    strict = (r > c, r < c)
    tri = (incl[0].astype(F32), incl[1].astype(F32))
    tri_t = (tri[1], tri[0])
    last = (CHUNK - 1, 0)
    chains = [(cg, d, h) for cg in range(PREP_CHUNKS) for d in range(2) for h in range(A_HEADS)]
    rows = lambda cg: slice(cg * CHUNK, (cg + 1) * CHUNK)
    cols = lambda h: slice(h * A_DK, (h + 1) * A_DK)
    kk = {}
    qk = {}
    for cg in range(PREP_CHUNKS):
        for h in range(A_HEADS):
            kb = k_ref[0, rows(cg), cols(h)].astype(BF16)
            kk[cg, h] = lax.dot_general(kb, kb, NT_DIMS, preferred_element_type=F32)
            qk[cg, h] = lax.dot_general(q_ref[0, rows(cg), cols(h)].astype(BF16), kb, NT_DIMS,
                                        preferred_element_type=F32)
    lgw = {}
    btw = {}
    for (cg, d, h) in chains:
        col = d * A_HEADS + h
        lgw[cg, d, h] = jnp.broadcast_to(lg_ref[0, rows(cg), col:col + 1], (CHUNK, A_DK))
        btw[cg, d, h] = jnp.broadcast_to(bt_ref[0, rows(cg), col:col + 1], (CHUNK, A_DK))
    gam = {ch: _dot_exact_lhs(tri[ch[1]], lgw[ch]) for ch in chains}
    gam_row = {ch: _dot_exact_lhs(ones, lgw[ch][:, :CHUNK] * tri_t[ch[1]]) for ch in chains}
    decay = {}
    lmat = {}
    for ch in chains:
        cg, d, h = ch
        diff = gam[ch][:, :CHUNK] - gam_row[ch]
        decay[ch] = jnp.where(incl[d], jnp.exp(jnp.where(incl[d], diff, 0.0)), 0.0)
        lmat[ch] = jnp.where(strict[d], btw[ch][:, :CHUNK] * decay[ch] * kk[cg, h], 0.0)
    pinv = {ch: eye - jnp.where((r // 2 == c // 2) & (r != c), lmat[ch], 0.0) for ch in chains}
    s = 2
    while s < CHUNK:
        join = (r // (2 * s) == c // (2 * s)) & (r // s != c // s)
        tc = {ch: _dot3(pinv[ch], jnp.where(join, lmat[ch], 0.0)) for ch in chains}
        pinv = {ch: pinv[ch] - _dot3(tc[ch], pinv[ch]) for ch in chains}
        s *= 2
    for ch in chains:
        cg, d, h = ch
        kh = k_ref[0, rows(cg), cols(h)]
        vh = v_ref[0, rows(cg), cols(h)]
        qh = q_ref[0, rows(cg), cols(h)]
        egam = jnp.exp(gam[ch])
        rhs = jnp.concatenate([kh * (btw[ch] * egam), vh * btw[ch]], axis=1)
        sol = _dot3(pinv[ch], rhs)
        w_ref[0, d, rows(cg), cols(h)] = sol[:, :A_DK].astype(BF16)
        u_ref[0, d, rows(cg), cols(h)] = sol[:, A_DK:]
        qd_ref[0, d, rows(cg), cols(h)] = (qh * egam).astype(BF16)
        glast = jnp.broadcast_to(gam[ch][last[d]:last[d] + 1, :], (CHUNK, A_DK))
        kd_ref[0, d, rows(cg), cols(h)] = (kh * jnp.exp(glast - gam[ch])).astype(BF16)
        g_ref[0, d, rows(cg), cols(h)] = jnp.exp(glast)
        p_ref[0, d, rows(cg), h * CHUNK:(h + 1) * CHUNK] = (decay[ch] * qk[cg, h]).astype(BF16)


def delta_prep(q, k, v, lg, bt):
    b, s, _ = q.shape
    rt = PREP_CHUNKS * CHUNK
    blk = lambda w: pl.BlockSpec((1, rt, w), lambda bi, ni: (bi, ni, 0))
    oblk = lambda w: pl.BlockSpec((1, 2, rt, w), lambda bi, ni: (bi, 0, ni, 0))
    sh = lambda w, dt: jax.ShapeDtypeStruct((b, 2, s, w), dt)
    return pl.pallas_call(
        _delta_prep_kernel, grid=(b, s // rt),
        in_specs=[blk(A_DIM), blk(A_DIM), blk(A_DIM), blk(2 * A_HEADS), blk(2 * A_HEADS)],
        out_specs=[oblk(A_DIM), oblk(A_DIM), oblk(A_DIM), oblk(A_DIM), oblk(A_HEADS * CHUNK), oblk(A_DIM)],
        out_shape=[sh(A_DIM, BF16), sh(A_DIM, F32), sh(A_DIM, BF16), sh(A_DIM, BF16),
                   sh(A_HEADS * CHUNK, BF16), sh(A_DIM, F32)],
        compiler_params=pltpu.CompilerParams(dimension_semantics=("parallel", "parallel")),
        name="delta_prep",
    )(q, k, v, lg, bt)


def _delta_scan_kernel(*refs):
    ins = refs[:12]
    s0_ref = refs[12]
    of_ref, ob_ref, sout_ref, state = refs[13:]
    n = pl.program_id(1)

    @pl.when(n == 0)
    def _():
        state[...] = s0_ref[...]

    outs = (of_ref, ob_ref)
    chains = [(g, d, h) for g in range(SCAN_SEQS) for d in range(2) for h in range(A_HEADS)]
    cs = lambda h: slice(h * A_DK, (h + 1) * A_DK)
    ref = lambda d, i: ins[d * 6 + i]
    s_old = {ch: state[ch] for ch in chains}
    wqs = {}
    for (g, d, h) in chains:
        wq = jnp.concatenate([ref(d, 0)[g, 0, :, cs(h)], ref(d, 2)[g, 0, :, cs(h)]], axis=0)
        wqs[g, d, h] = _bdot(wq, s_old[g, d, h].astype(BF16))
    unb = {}
    for (g, d, h) in chains:
        unb[g, d, h] = (ref(d, 1)[g, 0, :, cs(h)] - wqs[g, d, h][:CHUNK]).astype(BF16)
    for (g, d, h) in chains:
        o = wqs[g, d, h][CHUNK:] + _bdot(ref(d, 4)[g, 0, :, h * CHUNK:(h + 1) * CHUNK], unb[g, d, h])
        outs[d][g, :, cs(h)] = o
    for (g, d, h) in chains:
        upd = lax.dot_general(ref(d, 3)[g, 0, :, cs(h)], unb[g, d, h], TN_DIMS, preferred_element_type=F32)
        gs = jnp.broadcast_to(ref(d, 5)[g, 0, 0:1, cs(h)], (A_DK, A_DV))
        state[g, d, h] = gs * s_old[g, d, h] + upd

    @pl.when(n == pl.num_programs(1) - 1)
    def _():
        sout_ref[...] = state[...]


def delta_scan(w, u, qd, kd, p, gl, s0):
    b, _, s, _ = u.shape
    n = s // CHUNK

    def spec(wd, d):
        if d == 0:
            return pl.BlockSpec((SCAN_SEQS, 1, CHUNK, wd), lambda bi, ni: (bi, 0, ni, 0))
        return pl.BlockSpec((SCAN_SEQS, 1, CHUNK, wd), lambda bi, ni: (bi, 1, n - 1 - ni, 0))

    arrs = (w, u, qd, kd, p, gl)
    in_specs = [spec(a.shape[-1], d) for d in range(2) for a in arrs]
    st = pl.BlockSpec((SCAN_SEQS, 2, A_HEADS, A_DK, A_DV), lambda bi, ni: (bi, 0, 0, 0, 0))
    of = pl.BlockSpec((SCAN_SEQS, CHUNK, A_DIM), lambda bi, ni: (bi, ni, 0))
    ob = pl.BlockSpec((SCAN_SEQS, CHUNK, A_DIM), lambda bi, ni: (bi, n - 1 - ni, 0))
    return pl.pallas_call(
        _delta_scan_kernel, grid=(b // SCAN_SEQS, n),
        in_specs=in_specs + [st],
        out_specs=[of, ob, st],
        out_shape=[jax.ShapeDtypeStruct((b, s, A_DIM), F32), jax.ShapeDtypeStruct((b, s, A_DIM), F32),
                   jax.ShapeDtypeStruct((b, 2, A_HEADS, A_DK, A_DV), F32)],
        scratch_shapes=[pltpu.VMEM((SCAN_SEQS, 2, A_HEADS, A_DK, A_DV), F32)],
        compiler_params=pltpu.CompilerParams(dimension_semantics=("parallel", "arbitrary")),
        name="delta_scan",
    )(*(arrs + arrs), s0)


def _cand_tables():
    pairs = [(r, c) for r in range(PEER_TOPK) for c in range(PEER_TOPK) if (r + 1) * (c + 1) <= PEER_TOPK]
    npad = 64
    e1 = np.zeros((npad, PEER_TOPK), np.float32)
    e2 = np.zeros((npad, PEER_TOPK), np.float32)
    m = np.zeros((PEER_TOPK, npad), np.float32)
    for k, (r, c) in enumerate(pairs):
        e1[k, r] = 1
        e2[k, c] = 1
        m[r, k] = 1
    return len(pairs), e1, e2, m


N_CAND, _CAND_E1, _CAND_E2, _CAND_ROW = _cand_tables()


def _extract_topk(s, n_iter):
    k, t = s.shape
    work = s.reshape(k // 8, 8, t)
    rank = jnp.full(work.shape, float(n_iter), F32)
    vals = []
    for r in range(n_iter):
        m = jnp.max(jnp.max(work, axis=0), axis=0, keepdims=True)
        hit = work == jnp.broadcast_to(m, (8, t))[None]
        rank = jnp.where(hit, float(r), rank)
        work = jnp.where(hit, NEG_INF, work)
        vals.append(m)
    return vals, rank.reshape(k, t)


def _route_kernel(x_ref, g_ref, sc_ref, sh_ref, wqh_ref, wql_ref, k1_ref, k2_ref, e1_ref, e2_ref, mrow_ref,
                  ht_ref, a_ref, n_ref, b_ref, r_ref, q_scr):
    hmod = _norm_modulate(x_ref[...], g_ref[...], sc_ref[0], sh_ref[0])
    ht_ref[...] = hmod.T.astype(BF16)
    hh, hl = _split_bf16(hmod)
    q_scr[...] = _bdot(hh, wqh_ref[...]) + (_bdot(hh, wql_ref[...]) + _bdot(hl, wqh_ref[...]))
    tt = x_ref.shape[0]

    def head(h, carry):
        c1 = pl.ds(pl.multiple_of(h * 2 * PEER_DK, PEER_DK), PEER_DK)
        c2 = pl.ds(pl.multiple_of(h * 2 * PEER_DK + PEER_DK, PEER_DK), PEER_DK)
        hp = lax.Precision.HIGHEST
        s1 = lax.dot_general(k1_ref[h], q_scr[:, c1], NT_DIMS, precision=hp, preferred_element_type=F32)
        s2 = lax.dot_general(k2_ref[h], q_scr[:, c2], NT_DIMS, precision=hp, preferred_element_type=F32)
        v1, rank1 = _extract_topk(s1, PEER_TOPK)
        v2, rank2 = _extract_topk(s2, PEER_TOPK)
        v1m = jnp.concatenate(v1, axis=0)
        v2m = jnp.concatenate(v2, axis=0)
        cand = (jnp.dot(e1_ref[...], v1m, precision=hp, preferred_element_type=F32)
                + jnp.dot(e2_ref[...], v2m, precision=hp, preferred_element_type=F32))
        row = lax.broadcasted_iota(jnp.int32, cand.shape, 0)
        cand = jnp.where(row < N_CAND, cand, NEG_INF)
        _, crank = _extract_topk(cand, PEER_TOPK)
        sel = crank < float(PEER_TOPK)
        cmax = v1[0] + v2[0]
        z = jnp.sum(jnp.where(sel, jnp.exp(cand - cmax), 0.0), axis=0, keepdims=True)
        n_r = _bdot(mrow_ref[...], jnp.where(sel, 1.0, 0.0).astype(BF16))
        rank1_3 = rank1.reshape(PEER_KEYS // 8, 8, tt)
        nn = jnp.zeros_like(rank1_3)
        for r in range(PEER_TOPK):
            nn = jnp.where(rank1_3 == float(r), jnp.broadcast_to(n_r[r:r + 1, :], (8, tt))[None], nn)
        a_ref[h] = jnp.exp(s1 - v1[0]) / z
        n_ref[h] = nn.reshape(PEER_KEYS, tt)
        b_ref[h] = jnp.where(rank2 < float(PEER_TOPK), jnp.exp(s2 - v2[0]), 0.0).astype(BF16)
        r_ref[h] = rank2.astype(BF16)
        return carry

    lax.fori_loop(0, PEER_HEADS, head, 0, unroll=2)


def peer_route(x, norm_g, sc, sh, wq_hi, wq_lo, k1, k2, rows_per_seg):
    t = x.shape[0]
    tt = ROUTE_TOKEN_TILE
    gate = pl.BlockSpec((PEER_HEADS, PEER_KEYS, tt), lambda i: (0, 0, i))
    full = lambda shp: pl.BlockSpec(shp, lambda i: (0,) * len(shp))
    seg = _seg_spec(rows_per_seg, tt)
    return pl.pallas_call(
        _route_kernel, grid=(t // tt,),
        in_specs=[pl.BlockSpec((tt, D_MODEL), lambda i: (i, 0)), full((1, D_MODEL)), seg, seg,
                  full((D_MODEL, 2 * PEER_HEADS * PEER_DK)), full((D_MODEL, 2 * PEER_HEADS * PEER_DK)),
                  full((PEER_HEADS, PEER_KEYS, PEER_DK)),
                  full((PEER_HEADS, PEER_KEYS, PEER_DK)), full((64, PEER_TOPK)), full((64, PEER_TOPK)),
                  full((PEER_TOPK, 64))],
        out_specs=[pl.BlockSpec((D_MODEL, tt), lambda i: (0, i)), gate, gate, gate, gate],
        out_shape=[jax.ShapeDtypeStruct((D_MODEL, t), BF16)]
        + [jax.ShapeDtypeStruct((PEER_HEADS, PEER_KEYS, t), dt) for dt in (F32, F32, BF16, BF16)],
        scratch_shapes=[pltpu.VMEM((tt, 2 * PEER_HEADS * PEER_DK), F32)],
        compiler_params=pltpu.CompilerParams(dimension_semantics=("parallel",),
                                             vmem_limit_bytes=VMEM_LIMIT_BYTES),
        name="peer_route",
    )(x, norm_g.reshape(1, D_MODEL), sc, sh, wq_hi, wq_lo, k1, k2,
      jnp.asarray(_CAND_E1), jnp.asarray(_CAND_E2), jnp.asarray(_CAND_ROW, BF16))


def _peer_dense_kernel(ht_ref, u_ref, vt_ref, a_ref, n_ref, b_ref, r_ref, x_ref, g2_ref, o_ref,
                       acc_ref, hid_ref, w_ref, bp_ref, rp_ref):
    e = pl.program_id(1)

    @pl.when(e == 0)
    def _():
        acc_ref[...] = jnp.zeros_like(acc_ref)
        for c in range(PEER_TOKEN_TILE // LANES):
            ls = slice(c * LANES, (c + 1) * LANES)
            bp_ref[:, c] = b_ref[:, :, ls].astype(BF16)
            rp_ref[:, c] = r_ref[:, :, ls].astype(BF16)

    def hidden(pp):
        rows = slice(pp * PIECE_ROWS, (pp + 1) * PIECE_ROWS)
        hid_ref[pp % 2] = _bdot(u_ref[rows, :], ht_ref[...])

    def gates(pp):
        for c in range(PEER_TOKEN_TILE // LANES):
            ls = slice(c * LANES, (c + 1) * LANES)
            accs = [jnp.zeros((PEER_KEYS, LANES), BF16) for _ in range(PIECE_KEYS)]
            for h in range(PEER_HEADS):
                b = bp_ref[h, c]
                r = rp_ref[h, c]
                for ii in range(PIECE_KEYS):
                    k = pp * PIECE_KEYS + ii
                    a16 = jnp.broadcast_to(a_ref[h, k:k + 1, ls], (BF16_ROWS, LANES)).astype(BF16)
                    n16 = jnp.broadcast_to(n_ref[h, k:k + 1, ls], (BF16_ROWS, LANES)).astype(BF16)
                    a128 = pltpu.repeat(a16, PEER_KEYS // BF16_ROWS, axis=0)
                    n128 = pltpu.repeat(n16, PEER_KEYS // BF16_ROWS, axis=0)
                    accs[ii] = accs[ii] + jnp.where(r < n128, b, jnp.zeros_like(b)) * a128
            for ii in range(PIECE_KEYS):
                rows = slice(ii * PEER_KEYS, (ii + 1) * PEER_KEYS)
                act = _gelu_tanh(hid_ref[pp % 2, rows, ls].astype(BF16))
                w_ref[pp % 2, rows, ls] = act * accs[ii]

    def project(pp):
        rows = slice(pp * PIECE_ROWS, (pp + 1) * PIECE_ROWS)
        acc_ref[...] += _bdot(vt_ref[0, :, rows], w_ref[pp % 2])

    hidden(0)
    for pp in range(PEER_PIECES):
        if pp + 1 < PEER_PIECES:
            hidden(pp + 1)
        gates(pp)
        project(pp)

    @pl.when(e == pl.num_programs(1) - 1)
    def _():
        o_ref[...] = x_ref[...] + g2_ref[0] * acc_ref[...].T


def peer_dense(ht_bf, u_bf, vt_bf, a_t, n_t, b_t, r_t, x, g2, rows_per_seg):
    t = x.shape[0]
    tt = PEER_TOKEN_TILE
    grid = (t // tt, PEER_N // PEER_EXPERT_TILE)
    gate_spec = pl.BlockSpec((PEER_HEADS, PEER_KEYS, tt), lambda ti, ei: (0, 0, ti))
    step_keys = pl.BlockSpec((PEER_HEADS, PEER_KEYS_PER_STEP, tt), lambda ti, ei: (0, ei, ti))
    return pl.pallas_call(
        _peer_dense_kernel, grid=grid,
        in_specs=[pl.BlockSpec((D_MODEL, tt), lambda ti, ei: (0, ti)),
                  pl.BlockSpec((PEER_EXPERT_TILE, D_MODEL), lambda ti, ei: (ei, 0)),
                  pl.BlockSpec((1, D_MODEL, PEER_EXPERT_TILE), lambda ti, ei: (ei, 0, 0)),
                  step_keys, step_keys, gate_spec, gate_spec,
                  pl.BlockSpec((tt, D_MODEL), lambda ti, ei: (ti, 0)),
                  _seg_spec(rows_per_seg, tt)],
        out_specs=pl.BlockSpec((tt, D_MODEL), lambda ti, ei: (ti, 0)),
        out_shape=jax.ShapeDtypeStruct((t, D_MODEL), F32),
        scratch_shapes=[pltpu.VMEM((D_MODEL, tt), F32),
                        pltpu.VMEM((2, PIECE_ROWS, tt), F32),
                        pltpu.VMEM((2, PIECE_ROWS, tt), BF16),
                        pltpu.VMEM((PEER_HEADS, tt // LANES, PEER_KEYS, LANES), BF16),
                        pltpu.VMEM((PEER_HEADS, tt // LANES, PEER_KEYS, LANES), BF16)],
        compiler_params=pltpu.CompilerParams(dimension_semantics=("parallel", "arbitrary"),
                                             vmem_limit_bytes=VMEM_LIMIT_BYTES),
        name="peer_dense",
    )(ht_bf, u_bf, vt_bf, a_t, n_t, b_t, r_t, x, g2)


def _fill_halo_scratch(scr, prev, cur, nxt, halo, ts):
    i = pl.program_id(1)
    scr[0:halo, :] = jnp.where(i > 0, prev, 0.0)
    scr[halo:halo + ts, :] = cur
    scr[halo + ts:halo + ts + halo, :] = jnp.where(i < pl.num_programs(1) - 1, nxt, 0.0)


def _depthwise_taps(scr, w_ref, taps, halo, r0, rows):
    off = halo - taps // 2
    acc = scr[off + r0:off + r0 + rows, :] * w_ref[0:1, :]
    for k in range(1, taps):
        acc = acc + scr[off + r0 + k:off + r0 + k + rows, :] * w_ref[k:k + 1, :]
    return acc


def _halo_specs(width, col_block, halo, ts, seq_len, total_rows):
    tiles = seq_len // ts
    per_tile = ts // halo
    last = total_rows // halo - 1
    prev = pl.BlockSpec((halo, width), lambda s, i: (jnp.maximum((s * tiles + i) * per_tile - 1, 0), col_block))
    cur = pl.BlockSpec((ts, width), lambda s, i: (s * tiles + i, col_block))
    nxt = pl.BlockSpec((halo, width), lambda s, i: (jnp.minimum((s * tiles + i + 1) * per_tile, last), col_block))
    return prev, cur, nxt


CONF_ROW_BLOCK = 32
QKV_ROW_BLOCK = 16


def _conformer_kernel(prev_ref, cur_ref, next_ref, w_ref, b_ref, lg_ref, lb_ref, o_ref, scr):
    ts = cur_ref.shape[0]
    glu = lambda blk: blk[:, :B_CH] * jax.nn.sigmoid(blk[:, B_CH:])
    _fill_halo_scratch(scr, glu(prev_ref[...]), glu(cur_ref[...]), glu(next_ref[...]), CONF_HALO, ts)
    for rb in range(ts // CONF_ROW_BLOCK):
        r0 = rb * CONF_ROW_BLOCK
        hh = _depthwise_taps(scr, w_ref, B_CONV, CONF_HALO, r0, CONF_ROW_BLOCK) + b_ref[...]
        mu = jnp.mean(hh, axis=-1, keepdims=True)
        var = jnp.mean(jnp.square(hh - mu), axis=-1, keepdims=True)
        y = (hh - mu) * lax.rsqrt(var + EPS) * lg_ref[...] + lb_ref[...]
        o_ref[r0:r0 + CONF_ROW_BLOCK, :] = (y * jax.nn.sigmoid(y)).astype(o_ref.dtype)


def conformer_branch(p, dw_w, dw_b, ln_g, ln_b, nseq, seq_len):
    t = p.shape[0]
    ts = CONV_ROW_TILE
    prev, cur, nxt = _halo_specs(2 * B_CH, COL_GLU // (2 * B_CH), CONF_HALO, ts, seq_len, t)
    row = lambda: pl.BlockSpec((1, B_CH), lambda s, i: (0, 0))
    w_pad = jnp.pad(dw_w, ((0, 32 - B_CONV), (0, 0)))
    return pl.pallas_call(
        _conformer_kernel, grid=(nseq, seq_len // ts),
        in_specs=[prev, cur, nxt, pl.BlockSpec((32, B_CH), lambda s, i: (0, 0)), row(), row(), row()],
        out_specs=pl.BlockSpec((ts, B_CH), lambda s, i: (s * (seq_len // ts) + i, 0)),
        out_shape=jax.ShapeDtypeStruct((t, B_CH), BF16),
        scratch_shapes=[pltpu.VMEM((ts + 2 * CONF_HALO, B_CH), F32)],
        compiler_params=pltpu.CompilerParams(dimension_semantics=("parallel", "parallel")),
        name="conformer_branch",
    )(p, p, p, w_pad, dw_b.reshape(1, B_CH), ln_g.reshape(1, B_CH), ln_b.reshape(1, B_CH))


def _qkv_conv_kernel(prev_ref, cur_ref, next_ref, w_ref, q_ref, k_ref, v_ref, scr):
    ts = cur_ref.shape[0]
    _fill_halo_scratch(scr, prev_ref[...], cur_ref[...], next_ref[...], QKV_HALO, ts)
    for rb in range(ts // QKV_ROW_BLOCK):
        r0 = rb * QKV_ROW_BLOCK
        rows = slice(r0, r0 + QKV_ROW_BLOCK)
        y = _depthwise_taps(scr, w_ref, SHORT_CONV, QKV_HALO, r0, QKV_ROW_BLOCK)
        y = y * jax.nn.sigmoid(y)
        for h in range(A_HEADS):
            cs = slice(h * A_DK, (h + 1) * A_DK)
            qh = y[:, h * A_DK:(h + 1) * A_DK]
            kh = y[:, A_DIM + h * A_DK:A_DIM + (h + 1) * A_DK]
            q_ref[rows, cs] = qh * (lax.rsqrt(jnp.sum(qh * qh, axis=-1, keepdims=True) + EPS) * (A_DK ** -0.5))
            k_ref[rows, cs] = kh * lax.rsqrt(jnp.sum(kh * kh, axis=-1, keepdims=True) + EPS)
        v_ref[rows, :] = y[:, 2 * A_DIM:]


def qkv_conv(p, conv_w, nseq, seq_len):
    t = p.shape[0]
    ts = CONV_ROW_TILE
    prev, cur, nxt = _halo_specs(3 * A_DIM, 0, QKV_HALO, ts, seq_len, t)
    out = pl.BlockSpec((ts, A_DIM), lambda s, i: (s * (seq_len // ts) + i, 0))
    w_pad = jnp.pad(conv_w, ((0, 8 - SHORT_CONV), (0, 0)))
    return pl.pallas_call(
        _qkv_conv_kernel, grid=(nseq, seq_len // ts),
        in_specs=[prev, cur, nxt, pl.BlockSpec((8, 3 * A_DIM), lambda s, i: (0, 0))],
        out_specs=[out, out, out],
        out_shape=[jax.ShapeDtypeStruct((t, A_DIM), F32)] * 3,
        scratch_shapes=[pltpu.VMEM((ts + 2 * QKV_HALO, 3 * A_DIM), F32)],
        compiler_params=pltpu.CompilerParams(dimension_semantics=("parallel", "parallel")),
        name="qkv_conv",
    )(p, p, p, w_pad)


def _mix_out_kernel(of_ref, ob_ref, z_ref, conf_ref, ng_ref, w_ref, x_ref, gate_ref, o_ref):
    o = of_ref[...] + ob_ref[...]
    z = z_ref[...]
    parts = []
    for h in range(A_HEADS):
        cs = slice(h * A_DV, (h + 1) * A_DV)
        oh = o[:, cs]
        zh = z[:, cs]
        scale = lax.rsqrt(jnp.mean(oh * oh, axis=-1, keepdims=True) + EPS)
        parts.append((oh * scale * ng_ref[...] * (zh * jax.nn.sigmoid(zh))).astype(BF16))
    oa = jnp.concatenate(parts, axis=1)
    mix = _bdot(oa, w_ref[:A_DIM, :]) + _bdot(conf_ref[...], w_ref[A_DIM:, :])
    o_ref[...] = x_ref[...] + gate_ref[0] * mix


def mix_out(o_f, o_b, p, conf, norm_g, w_bf, x, gate, rows_per_seg):
    t = x.shape[0]
    tm = MM_ROW_TILE
    half = lambda cb: pl.BlockSpec((tm, A_DIM), lambda i: (i, cb))
    return pl.pallas_call(
        _mix_out_kernel, grid=(t // tm,),
        in_specs=[half(0), half(0), half(COL_Z // A_DIM), half(0), pl.BlockSpec((1, A_DV), lambda i: (0, 0)),
                  pl.BlockSpec((D_MODEL, D_MODEL), lambda i: (0, 0)), pl.BlockSpec((tm, D_MODEL), lambda i: (i, 0)),
                  _seg_spec(rows_per_seg, tm)],
        out_specs=pl.BlockSpec((tm, D_MODEL), lambda i: (i, 0)),
        out_shape=jax.ShapeDtypeStruct((t, D_MODEL), F32),
        compiler_params=pltpu.CompilerParams(dimension_semantics=("parallel",),
                                             vmem_limit_bytes=VMEM_LIMIT_BYTES),
        name="mix_out",
    )(o_f, o_b, p, conf, norm_g.reshape(1, A_DV), w_bf, x, gate)


def grid_pos_emb(n_tokens):
    rows = n_tokens // GRID_W
    r = jnp.repeat(jnp.arange(rows, dtype=F32), GRID_W)
    col = jnp.tile(jnp.arange(GRID_W, dtype=F32), rows)
    nf = D_MODEL // 4
    freqs = jnp.exp(-math.log(POS_BASE) * jnp.arange(nf, dtype=F32) / nf)
    ar = r[:, None] * freqs
    ac = col[:, None] * freqs
    return jnp.concatenate([jnp.sin(ar), jnp.cos(ar), jnp.sin(ac), jnp.cos(ac)], axis=-1)


def delta_conformer_layer(x, p, g1, nseq, seq_len, s0, e, prm, rows_per_seg):
    t = p.shape[0]
    seq = lambda m: m.reshape(nseq, seq_len, m.shape[-1])
    q, k, v = qkv_conv(p, prm['conv_qkv_w'][e], nseq, seq_len)
    alpha = p[:, COL_AB:COL_AB + 2 * A_HEADS]
    beta = jax.nn.sigmoid(p[:, COL_AB + 2 * A_HEADS:COL_AB + 4 * A_HEADS])
    log_g = (-jnp.exp(prm['a_log'][e]).reshape(1, 2 * A_HEADS)
             * jax.nn.softplus(alpha + prm['dt_bias'][e].reshape(1, 2 * A_HEADS)))
    o_f, o_b, st = delta_scan(*delta_prep(seq(q), seq(k), seq(v), seq(log_g), seq(beta)), s0)
    conf = conformer_branch(p, prm['conf_dw_w'][e], prm['conf_dw_b'][e], prm['conf_ln_g'][e],
                            prm['conf_ln_b'][e], nseq, seq_len)
    x = mix_out(o_f.reshape(t, A_DIM), o_b.reshape(t, A_DIM), p, conf, prm['delta_norm_g'][e],
                prm['w_out_bf'][e], x, g1, rows_per_seg)
    return x, st


def trunk(x, nseq, seq_len, cond, s0, prm):
    t = x.shape[0]
    rows_per_seg = t // cond.shape[0]
    states = []
    for l in range(DEPTH):
        mod = jax.nn.silu(cond) @ prm['ada_w'][l] + prm['ada_b'][l]
        sh1, sc1, g1, sh2, sc2, g2 = [m[:, None, :] for m in jnp.split(mod, 6, axis=-1)]
        e = l // 2
        if l % 2 == 0:
            p = norm_mm(x, prm['norm1_g'][l], sc1, sh1, prm['w_in_bf'][e], rows_per_seg, F32)
            x, st = delta_conformer_layer(x, p, g1, nseq, seq_len, s0[:, e], e, prm, rows_per_seg)
            states.append(st)
        else:
            z = norm_mm(x, prm['norm1_g'][l], sc1, sh1, prm['w_fnet_bf'][e], rows_per_seg, BF16)
            tm = min(seq_len, 512)
            x = seq_mix_res(prm['dft_seq'][seq_len], z, x, g1, seq_len, tm, min(seq_len, 1024))
        ht, a_t, n_t, b_t, r_t = peer_route(x, prm['norm2_g'][l], sc2, sh2, prm['peer_wq_hi'][l],
                                            prm['peer_wq_lo'][l], prm['peer_k1'][l], prm['peer_k2'][l],
                                            rows_per_seg)
        x = peer_dense(ht, prm['peer_u_bf'][l], prm['peer_vt_bf'][l], a_t, n_t, b_t, r_t, x, g2, rows_per_seg)
    xf = x * lax.rsqrt(jnp.mean(x * x, axis=-1, keepdims=True) + EPS) * prm['final_norm_g']
    return xf, jnp.stack(states, axis=1)


def kernel(x_prompt, x_sample, state_delta, c, c_ctx, ada_w, ada_b, norm1_g, norm2_g, w_in_ab, conv_qkv_w,
           a_log, dt_bias, delta_norm_g, conf_dw_w, conf_dw_b, conf_ln_g, conf_ln_b, w_out_ab, w_out_c,
           peer_wq, peer_k1, peer_k2, peer_u, peer_v, final_norm_g):
    bp, sp, _ = x_prompt.shape
    bs, ss, _ = x_sample.shape
    bdc, bds = dft_group_matrices(D_MODEL // C_GROUPS, C_GROUPS)
    w_fnet = [jnp.concatenate([mm3(bdc, w_out_c[e], BF16), mm3(bds, w_out_c[e], BF16)], axis=1)
              for e in range(DEPTH // 2)]
    o4 = 4 * A_DIM
    w_in = jnp.concatenate([w_in_ab[:, :, :o4], w_in_ab[:, :, o4 + 4 * A_HEADS:], w_in_ab[:, :, o4:o4 + 4 * A_HEADS]],
                           axis=-1).astype(BF16)
    wq_hi, wq_lo = split_bf16(peer_wq)
    prm = {'ada_w': ada_w, 'ada_b': ada_b, 'norm1_g': norm1_g, 'norm2_g': norm2_g,
           'w_in_bf': jnp.pad(w_in, ((0, 0), (0, 0), (0, P_AB_PAD - P_AB))),
           'conv_qkv_w': conv_qkv_w, 'a_log': a_log, 'dt_bias': dt_bias,
           'delta_norm_g': delta_norm_g, 'conf_dw_w': conf_dw_w, 'conf_dw_b': conf_dw_b,
           'conf_ln_g': conf_ln_g, 'conf_ln_b': conf_ln_b, 'w_out_bf': w_out_ab.astype(BF16),
           'w_fnet_bf': w_fnet, 'dft_seq': {s: dft_seq_matrix(s) for s in {sp, ss}},
           'peer_wq_hi': wq_hi, 'peer_wq_lo': wq_lo, 'peer_k1': peer_k1, 'peer_k2': peer_k2,
           'peer_u_bf': peer_u.astype(BF16),
           'peer_vt_bf': jnp.transpose(peer_v.astype(BF16).reshape(DEPTH, PEER_N // PEER_EXPERT_TILE,
                                                                   PEER_EXPERT_TILE, D_MODEL), (0, 1, 3, 2)),
           'final_norm_g': final_norm_g}
    ne = (DEPTH + 1) // 2
    s0_ctx = jnp.zeros((bp, ne, 2, A_HEADS, A_DK, A_DV), F32)
    y_prompt, ctx_states = trunk(x_prompt.reshape(bp * sp, D_MODEL), bp, sp, c_ctx[None, :], s0_ctx, prm)
    xs = (x_sample + grid_pos_emb(ss)[None]).reshape(bs * ss, D_MODEL)
    y_sample, _ = trunk(xs, bs, ss, c, state_delta, prm)
    return (y_prompt.reshape(bp, sp, D_MODEL), y_sample.reshape(bs, ss, D_MODEL), ctx_states)
```

```python
import math

import jax
import jax.numpy as jnp
import numpy as np
from jax import lax
from jax.experimental import pallas as pl
from jax.experimental.pallas import tpu as pltpu

D_MODEL = 1024
DEPTH = 4
GRID_W = 64
POS_BASE = 10000.0
EPS = 1e-6
A_HEADS = 4
A_DK = 128
A_DV = 128
A_DIM = A_HEADS * A_DV
CHUNK = 64
B_CH = D_MODEL // 2
P_AB = 4 * A_DIM + 4 * A_HEADS + 2 * B_CH
C_GROUPS = 8
PEER_HEADS = 8
PEER_KEYS = 128
PEER_N = PEER_KEYS * PEER_KEYS
PEER_DK = 128
PEER_TOPK = 16

F32 = jnp.float32
BF16 = jnp.bfloat16
NEG_INF = float("-inf")

LANES = 128
VMEM_LIMIT_BYTES = 56 * 1024 * 1024
MM_ROW_TILE = 512
ROUTE_TOKEN_TILE = 256
PEER_TOKEN_TILE = 512
PEER_KEYS_PER_STEP = 8
PEER_EXPERT_TILE = PEER_KEYS_PER_STEP * PEER_KEYS
PREP_CHUNKS = 2
SCAN_SEQS = 2
P_AB_PAD = 3200
COL_Z = 3 * A_DIM
COL_GLU = 4 * A_DIM
COL_AB = 4 * A_DIM + 2 * B_CH
CONV_ROW_TILE = 256
SHORT_CONV = 7
B_CONV = 31
CONF_HALO = 16
QKV_HALO = 8
PEER_PIECES = 2
PIECE_KEYS = PEER_KEYS_PER_STEP // PEER_PIECES
PIECE_ROWS = PIECE_KEYS * PEER_KEYS
BF16_ROWS = 16

NT_DIMS = (((1,), (1,)), ((), ()))
TN_DIMS = (((0,), (0,)), ((), ()))


def _bdot(a, b):
    return jnp.dot(a, b, preferred_element_type=F32)


def _split_bf16(a):
    hi = a.astype(BF16)
    lo = (a - hi.astype(F32)).astype(BF16)
    return hi, lo


def _dot3(a, b):
    ah, al = _split_bf16(a)
    bh, bl = _split_bf16(b)
    return _bdot(ah, bh) + (_bdot(ah, bl) + _bdot(al, bh))


def _dot_exact_lhs(a01, b):
    a = a01.astype(BF16)
    bh = b.astype(BF16)
    r1 = b - bh.astype(F32)
    bm = r1.astype(BF16)
    bl = (r1 - bm.astype(F32)).astype(BF16)
    return _bdot(a, bh) + (_bdot(a, bm) + _bdot(a, bl))


def _gelu_tanh(x):
    return 0.5 * x * (1.0 + jnp.tanh(math.sqrt(2.0 / math.pi) * (x + 0.044715 * (x * x * x))))


def _seg_spec(rows_per_seg, tile):
    per = rows_per_seg // tile
    return pl.BlockSpec((1, 1, D_MODEL), lambda i, *_: (i // per, 0, 0))


def _norm_modulate(x, g, sc, sh):
    hn = x * lax.rsqrt(jnp.mean(x * x, axis=-1, keepdims=True) + EPS) * g
    return hn * (1.0 + sc) + sh


def _norm_mm_kernel(x_ref, g_ref, sc_ref, sh_ref, w_ref, o_ref, *tail_ref):
    h = _norm_modulate(x_ref[...], g_ref[...], sc_ref[0], sh_ref[0]).astype(BF16)
    res = _bdot(h, w_ref[...])
    o_ref[...] = res.astype(o_ref.dtype)
    if tail_ref:
        tail_ref[0][...] = res[:, res.shape[1] - LANES:]


def norm_mm(x, norm_g, sc, sh, w_bf, rows_per_seg, out_dtype, tail=False):
    m = x.shape[0]
    n = w_bf.shape[1]
    tm = MM_ROW_TILE
    seg = _seg_spec(rows_per_seg, tm)
    out_specs = [pl.BlockSpec((tm, n), lambda i: (i, 0))]
    out_shape = [jax.ShapeDtypeStruct((m, n), out_dtype)]
    if tail:
        out_specs.append(pl.BlockSpec((tm, LANES), lambda i: (i, 0)))
        out_shape.append(jax.ShapeDtypeStruct((m, LANES), F32))
    outs = pl.pallas_call(
        _norm_mm_kernel, grid=(m // tm,),
        in_specs=[pl.BlockSpec((tm, D_MODEL), lambda i: (i, 0)), pl.BlockSpec((1, D_MODEL), lambda i: (0, 0)),
                  seg, seg, pl.BlockSpec((D_MODEL, n), lambda i: (0, 0))],
        out_specs=out_specs, out_shape=out_shape,
        compiler_params=pltpu.CompilerParams(dimension_semantics=("parallel",),
                                             vmem_limit_bytes=VMEM_LIMIT_BYTES),
        name="norm_mm",
    )(x, norm_g.reshape(1, D_MODEL), sc, sh, w_bf)
    return outs if tail else outs[0]


def _split_kernel(a_ref, hi_ref, lo_ref):
    hi, lo = _split_bf16(a_ref[0])
    hi_ref[0] = hi
    lo_ref[0] = lo


def split_bf16(a):
    nl, nr, nc = a.shape
    blk = pl.BlockSpec((1, MM_ROW_TILE, nc), lambda l, i: (l, i, 0))
    return pl.pallas_call(
        _split_kernel, grid=(nl, nr // MM_ROW_TILE), in_specs=[blk], out_specs=[blk, blk],
        out_shape=[jax.ShapeDtypeStruct(a.shape, BF16)] * 2,
        compiler_params=pltpu.CompilerParams(dimension_semantics=("parallel", "parallel")),
        name="split_bf16",
    )(a)


def _mm3_kernel(a_ref, b_ref, o_ref):
    o_ref[...] = _dot3(a_ref[...], b_ref[...]).astype(o_ref.dtype)


def mm3(a, b, out_dtype):
    return pl.pallas_call(_mm3_kernel, out_shape=jax.ShapeDtypeStruct((a.shape[0], b.shape[1]), out_dtype),
                          compiler_params=pltpu.CompilerParams(vmem_limit_bytes=VMEM_LIMIT_BYTES),
                          name="mm3")(a, b)


def _seqmix_kernel(f_ref, z_ref, x_ref, gate_ref, o_ref, acc_ref):
    k = pl.program_id(2)

    @pl.when(k == 0)
    def _():
        acc_ref[...] = jnp.zeros_like(acc_ref)

    acc_ref[...] += _bdot(f_ref[...], z_ref[...])

    @pl.when(k == pl.num_programs(2) - 1)
    def _():
        o_ref[...] = x_ref[...] + gate_ref[0] * acc_ref[...]


def seq_mix_res(fmat, z, x, gate, seq_len, tm, tk):
    t = x.shape[0]
    nseq = t // seq_len
    seqs_per_seg = nseq // gate.shape[0]
    mt = seq_len // tm
    kt_half = seq_len // tk
    return pl.pallas_call(
        _seqmix_kernel, grid=(nseq, mt, 2 * kt_half),
        in_specs=[pl.BlockSpec((tm, tk), lambda s, i, k: (i, k)),
                  pl.BlockSpec((tk, D_MODEL), lambda s, i, k: (s * kt_half + k % kt_half, k // kt_half)),
                  pl.BlockSpec((tm, D_MODEL), lambda s, i, k: (s * mt + i, 0)),
                  pl.BlockSpec((1, 1, D_MODEL), lambda s, i, k: (s // seqs_per_seg, 0, 0))],
        out_specs=pl.BlockSpec((tm, D_MODEL), lambda s, i, k: (s * mt + i, 0)),
        out_shape=jax.ShapeDtypeStruct((t, D_MODEL), F32),
        scratch_shapes=[pltpu.VMEM((tm, D_MODEL), F32)],
        compiler_params=pltpu.CompilerParams(dimension_semantics=("parallel", "parallel", "arbitrary"),
                                             vmem_limit_bytes=VMEM_LIMIT_BYTES),
        name="seq_mix_res",
    )(fmat, z, x, gate)


def _dft_tables(n, cols):
    r = jnp.arange(n, dtype=jnp.int32)[:, None]
    ang = ((r * cols[None, :]) % n).astype(F32) * (2.0 * math.pi / n)
    return jnp.cos(ang), jnp.sin(ang)


def dft_seq_matrix(s):
    w = 1 << (int(math.log2(s)) // 2)
    ch, sh_ = _dft_tables(s, jnp.arange(s // w, dtype=jnp.int32) * w)
    cl, sl = _dft_tables(s, jnp.arange(w, dtype=jnp.int32))
    sc = 1.0 / math.sqrt(s)
    c = (ch[:, :, None] * cl[:, None, :] - sh_[:, :, None] * sl[:, None, :]).reshape(s, s) * sc
    sn = (sh_[:, :, None] * cl[:, None, :] + ch[:, :, None] * sl[:, None, :]).reshape(s, s) * sc
    return jnp.concatenate([c, -sn], axis=1).astype(BF16)


def dft_group_matrices(n, groups):
    c, s = _dft_tables(n, jnp.arange(n, dtype=jnp.int32))
    sc = 1.0 / math.sqrt(n)
    eye = jnp.eye(groups, dtype=F32)
    return jnp.kron(eye, c * sc), jnp.kron(eye, s * sc)


def _delta_prep_kernel(q_ref, k_ref, v_ref, lg_ref, bt_ref, w_ref, u_ref, qd_ref, kd_ref, p_ref, g_ref):
    r = lax.broadcasted_iota(jnp.int32, (CHUNK, CHUNK), 0)
    c = lax.broadcasted_iota(jnp.int32, (CHUNK, CHUNK), 1)
    eye = (r == c).astype(F32)
    ones = jnp.ones((CHUNK, CHUNK), F32)
    incl = (r >= c, r <= c)
    strict = (r > c, r < c)
    tri = (incl[0].astype(F32), incl[1].astype(F32))
    tri_t = (tri[1], tri[0])
    last = (CHUNK - 1, 0)
    chains = [(cg, d, h) for cg in range(PREP_CHUNKS) for d in range(2) for h in range(A_HEADS)]
    rows = lambda cg: slice(cg * CHUNK, (cg + 1) * CHUNK)
    cols = lambda h: slice(h * A_DK, (h + 1) * A_DK)
    kk = {}
    qk = {}
    for cg in range(PREP_CHUNKS):
        for h in range(A_HEADS):
            kb = k_ref[0, rows(cg), cols(h)].astype(BF16)
            kk[cg, h] = lax.dot_general(kb, kb, NT_DIMS, preferred_element_type=F32)
            qk[cg, h] = lax.dot_general(q_ref[0, rows(cg), cols(h)].astype(BF16), kb, NT_DIMS,
                                        preferred_element_type=F32)
    lgw = {}
    btw = {}
    for (cg, d, h) in chains:
        col = d * A_HEADS + h
        lgw[cg, d, h] = jnp.broadcast_to(lg_ref[0, rows(cg), col:col + 1], (CHUNK, A_DK))
        btw[cg, d, h] = jnp.broadcast_to(bt_ref[0, rows(cg), col:col + 1], (CHUNK, A_DK))
    gam = {ch: _dot_exact_lhs(tri[ch[1]], lgw[ch]) for ch in chains}
    gam_row = {ch: _dot_exact_lhs(ones, lgw[ch][:, :CHUNK] * tri_t[ch[1]]) for ch in chains}
    decay = {}
    lmat = {}
    for ch in chains:
        cg, d, h = ch
        diff = gam[ch][:, :CHUNK] - gam_row[ch]
        decay[ch] = jnp.where(incl[d], jnp.exp(jnp.where(incl[d], diff, 0.0)), 0.0)
        lmat[ch] = jnp.where(strict[d], btw[ch][:, :CHUNK] * decay[ch] * kk[cg, h], 0.0)
    pinv = {ch: eye - jnp.where((r // 2 == c // 2) & (r != c), lmat[ch], 0.0) for ch in chains}
    s = 2
    while s < CHUNK:
        join = (r // (2 * s) == c // (2 * s)) & (r // s != c // s)
        tc = {ch: _dot3(pinv[ch], jnp.where(join, lmat[ch], 0.0)) for ch in chains}
        pinv = {ch: pinv[ch] - _dot3(tc[ch], pinv[ch]) for ch in chains}
        s *= 2
    for ch in chains:
        cg, d, h = ch
        kh = k_ref[0, rows(cg), cols(h)]
        vh = v_ref[0, rows(cg), cols(h)]
        qh = q_ref[0, rows(cg), cols(h)]
        egam = jnp.exp(gam[ch])
        rhs = jnp.concatenate([kh * (btw[ch] * egam), vh * btw[ch]], axis=1)
        sol = _dot3(pinv[ch], rhs)
        w_ref[0, d, rows(cg), cols(h)] = sol[:, :A_DK].astype(BF16)
        u_ref[0, d, rows(cg), cols(h)] = sol[:, A_DK:]
        qd_ref[0, d, rows(cg), cols(h)] = (qh * egam).astype(BF16)
        glast = jnp.broadcast_to(gam[ch][last[d]:last[d] + 1, :], (CHUNK, A_DK))
        kd_ref[0, d, rows(cg), cols(h)] = (kh * jnp.exp(glast - gam[ch])).astype(BF16)
        g_ref[0, d, rows(cg), cols(h)] = jnp.exp(glast)
        p_ref[0, d, rows(cg), h * CHUNK:(h + 1) * CHUNK] = (decay[ch] * qk[cg, h]).astype(BF16)


def delta_prep(q, k, v, lg, bt):
    b, s, _ = q.shape
    rt = PREP_CHUNKS * CHUNK
    blk = lambda w: pl.BlockSpec((1, rt, w), lambda bi, ni: (bi, ni, 0))
    oblk = lambda w: pl.BlockSpec((1, 2, rt, w), lambda bi, ni: (bi, 0, ni, 0))
    sh = lambda w, dt: jax.ShapeDtypeStruct((b, 2, s, w), dt)
    return pl.pallas_call(
        _delta_prep_kernel, grid=(b, s // rt),
        in_specs=[blk(A_DIM), blk(A_DIM), blk(A_DIM), blk(2 * A_HEADS), blk(2 * A_HEADS)],
        out_specs=[oblk(A_DIM), oblk(A_DIM), oblk(A_DIM), oblk(A_DIM), oblk(A_HEADS * CHUNK), oblk(A_DIM)],
        out_shape=[sh(A_DIM, BF16), sh(A_DIM, F32), sh(A_DIM, BF16), sh(A_DIM, BF16),
                   sh(A_HEADS * CHUNK, BF16), sh(A_DIM, F32)],
        compiler_params=pltpu.CompilerParams(dimension_semantics=("parallel", "parallel")),
        name="delta_prep",
    )(q, k, v, lg, bt)


def _delta_scan_kernel(*refs):
    ins = refs[:12]
    s0_ref = refs[12]
    of_ref, ob_ref, sout_ref, state = refs[13:]
    n = pl.program_id(1)

    @pl.when(n == 0)
    def _():
        state[...] = s0_ref[...]

    outs = (of_ref, ob_ref)
    chains = [(g, d, h) for g in range(SCAN_SEQS) for d in range(2) for h in range(A_HEADS)]
    cs = lambda h: slice(h * A_DK, (h + 1) * A_DK)
    ref = lambda d, i: ins[d * 6 + i]
    s_old = {ch: state[ch] for ch in chains}
    wqs = {}
    for (g, d, h) in chains:
        wq = jnp.concatenate([ref(d, 0)[g, 0, :, cs(h)], ref(d, 2)[g, 0, :, cs(h)]], axis=0)
        wqs[g, d, h] = _bdot(wq, s_old[g, d, h].astype(BF16))
    unb = {}
    for (g, d, h) in chains:
        unb[g, d, h] = (ref(d, 1)[g, 0, :, cs(h)] - wqs[g, d, h][:CHUNK]).astype(BF16)
    for (g, d, h) in chains:
        o = wqs[g, d, h][CHUNK:] + _bdot(ref(d, 4)[g, 0, :, h * CHUNK:(h + 1) * CHUNK], unb[g, d, h])
        outs[d][g, :, cs(h)] = o
    for (g, d, h) in chains:
        upd = lax.dot_general(ref(d, 3)[g, 0, :, cs(h)], unb[g, d, h], TN_DIMS, preferred_element_type=F32)
        gs = jnp.broadcast_to(ref(d, 5)[g, 0, 0:1, cs(h)], (A_DK, A_DV))
        state[g, d, h] = gs * s_old[g, d, h] + upd

    @pl.when(n == pl.num_programs(1) - 1)
    def _():
        sout_ref[...] = state[...]


def delta_scan(w, u, qd, kd, p, gl, s0):
    b, _, s, _ = u.shape
    n = s // CHUNK

    def spec(wd, d):
        if d == 0:
            return pl.BlockSpec((SCAN_SEQS, 1, CHUNK, wd), lambda bi, ni: (bi, 0, ni, 0))
        return pl.BlockSpec((SCAN_SEQS, 1, CHUNK, wd), lambda bi, ni: (bi, 1, n - 1 - ni, 0))

    arrs = (w, u, qd, kd, p, gl)
    in_specs = [spec(a.shape[-1], d) for d in range(2) for a in arrs]
    st = pl.BlockSpec((SCAN_SEQS, 2, A_HEADS, A_DK, A_DV), lambda bi, ni: (bi, 0, 0, 0, 0))
    of = pl.BlockSpec((SCAN_SEQS, CHUNK, A_DIM), lambda bi, ni: (bi, ni, 0))
    ob = pl.BlockSpec((SCAN_SEQS, CHUNK, A_DIM), lambda bi, ni: (bi, n - 1 - ni, 0))
    return pl.pallas_call(
        _delta_scan_kernel, grid=(b // SCAN_SEQS, n),
        in_specs=in_specs + [st],
        out_specs=[of, ob, st],
        out_shape=[jax.ShapeDtypeStruct((b, s, A_DIM), F32), jax.ShapeDtypeStruct((b, s, A_DIM), F32),
                   jax.ShapeDtypeStruct((b, 2, A_HEADS, A_DK, A_DV), F32)],
        scratch_shapes=[pltpu.VMEM((SCAN_SEQS, 2, A_HEADS, A_DK, A_DV), F32)],
        compiler_params=pltpu.CompilerParams(dimension_semantics=("parallel", "arbitrary")),
        name="delta_scan",
    )(*(arrs + arrs), s0)


def _cand_tables():
    pairs = [(r, c) for r in range(PEER_TOPK) for c in range(PEER_TOPK) if (r + 1) * (c + 1) <= PEER_TOPK]
    npad = 64
    e1 = np.zeros((npad, PEER_TOPK), np.float32)
    e2 = np.zeros((npad, PEER_TOPK), np.float32)
    m = np.zeros((PEER_TOPK, npad), np.float32)
    for k, (r, c) in enumerate(pairs):
        e1[k, r] = 1
        e2[k, c] = 1
        m[r, k] = 1
    return len(pairs), e1, e2, m


N_CAND, _CAND_E1, _CAND_E2, _CAND_ROW = _cand_tables()


def _extract_topk(s, n_iter, want_rank=True):
    k, t = s.shape
    work = s.reshape(k // 8, 8, t)
    rank = jnp.full(work.shape, float(n_iter), F32) if want_rank else None
    vals = []
    for r in range(n_iter):
        m = jnp.max(jnp.max(work, axis=0), axis=0, keepdims=True)
        hit = work == jnp.broadcast_to(m, (8, t))[None]
        if want_rank:
            rank = jnp.where(hit, float(r), rank)
        work = jnp.where(hit, NEG_INF, work)
        vals.append(m)
    return vals, (rank.reshape(k, t) if want_rank else None)


def _route_kernel(x_ref, g_ref, sc_ref, sh_ref, wqh_ref, wql_ref, k1_ref, k2_ref, e1_ref, e2_ref, mrow_ref,
                  ht_ref, a_ref, n_ref, b_ref, r_ref, q_scr):
    hmod = _norm_modulate(x_ref[...], g_ref[...], sc_ref[0], sh_ref[0])
    ht_ref[...] = hmod.T.astype(BF16)
    hh, hl = _split_bf16(hmod)
    q_scr[...] = _bdot(hh, wqh_ref[...]) + (_bdot(hh, wql_ref[...]) + _bdot(hl, wqh_ref[...]))
    tt = x_ref.shape[0]

    def head(h, carry):
        c1 = pl.ds(pl.multiple_of(h * 2 * PEER_DK, PEER_DK), PEER_DK)
        c2 = pl.ds(pl.multiple_of(h * 2 * PEER_DK + PEER_DK, PEER_DK), PEER_DK)
        hp = lax.Precision.HIGHEST
        s1 = lax.dot_general(k1_ref[h], q_scr[:, c1], NT_DIMS, precision=hp, preferred_element_type=F32)
        s2 = lax.dot_general(k2_ref[h], q_scr[:, c2], NT_DIMS, precision=hp, preferred_element_type=F32)
        v1, _ = _extract_topk(s1, PEER_TOPK, want_rank=False)
        v2, rank2 = _extract_topk(s2, PEER_TOPK)
        v1m = jnp.concatenate(v1, axis=0)
        v2m = jnp.concatenate(v2, axis=0)
        cand = (jnp.dot(e1_ref[...], v1m, precision=hp, preferred_element_type=F32)
                + jnp.dot(e2_ref[...], v2m, precision=hp, preferred_element_type=F32))
        row = lax.broadcasted_iota(jnp.int32, cand.shape, 0)
        cand = jnp.where(row < N_CAND, cand, NEG_INF)
        _, crank = _extract_topk(cand, PEER_TOPK)
        sel = crank < float(PEER_TOPK)
        cmax = v1[0] + v2[0]
        z = jnp.sum(jnp.where(sel, jnp.exp(cand - cmax), 0.0), axis=0, keepdims=True)
        n_r = _bdot(mrow_ref[...], jnp.where(sel, 1.0, 0.0).astype(BF16))
        s1_3 = s1.reshape(PEER_KEYS // 8, 8, tt)
        nn = jnp.zeros_like(s1_3)
        for r in range(PEER_TOPK):
            nn = jnp.where(s1_3 == jnp.broadcast_to(v1[r], (8, tt))[None],
                           jnp.broadcast_to(n_r[r:r + 1, :], (8, tt))[None], nn)
        a_ref[h] = jnp.exp(s1 - v1[0]) / z
        n_ref[h] = nn.reshape(PEER_KEYS, tt)
        b_ref[h] = jnp.where(rank2 < float(PEER_TOPK), jnp.exp(s2 - v2[0]), 0.0).astype(BF16)
        r_ref[h] = rank2.astype(BF16)
        return carry

    lax.fori_loop(0, PEER_HEADS, head, 0, unroll=2)


def peer_route(x, norm_g, sc, sh, wq_hi, wq_lo, k1, k2, rows_per_seg):
    t = x.shape[0]
    tt = ROUTE_TOKEN_TILE
    gate = pl.BlockSpec((PEER_HEADS, PEER_KEYS, tt), lambda i: (0, 0, i))
    full = lambda shp: pl.BlockSpec(shp, lambda i: (0,) * len(shp))
    seg = _seg_spec(rows_per_seg, tt)
    return pl.pallas_call(
        _route_kernel, grid=(t // tt,),
        in_specs=[pl.BlockSpec((tt, D_MODEL), lambda i: (i, 0)), full((1, D_MODEL)), seg, seg,
                  full((D_MODEL, 2 * PEER_HEADS * PEER_DK)), full((D_MODEL, 2 * PEER_HEADS * PEER_DK)),
                  full((PEER_HEADS, PEER_KEYS, PEER_DK)),
                  full((PEER_HEADS, PEER_KEYS, PEER_DK)), full((64, PEER_TOPK)), full((64, PEER_TOPK)),
                  full((PEER_TOPK, 64))],
        out_specs=[pl.BlockSpec((D_MODEL, tt), lambda i: (0, i)), gate, gate, gate, gate],
        out_shape=[jax.ShapeDtypeStruct((D_MODEL, t), BF16)]
        + [jax.ShapeDtypeStruct((PEER_HEADS, PEER_KEYS, t), dt) for dt in (F32, F32, BF16, BF16)],
        scratch_shapes=[pltpu.VMEM((tt, 2 * PEER_HEADS * PEER_DK), F32)],
        compiler_params=pltpu.CompilerParams(dimension_semantics=("parallel",),
                                             vmem_limit_bytes=VMEM_LIMIT_BYTES),
        name="peer_route",
    )(x, norm_g.reshape(1, D_MODEL), sc, sh, wq_hi, wq_lo, k1, k2,
      jnp.asarray(_CAND_E1), jnp.asarray(_CAND_E2), jnp.asarray(_CAND_ROW, BF16))


def _peer_dense_kernel(ht_ref, u_ref, vt_ref, a_ref, n_ref, b_ref, r_ref, x_ref, g2_ref, o_ref,
                       acc_ref, hid_ref, w_ref, bp_ref, rp_ref):
    e = pl.program_id(1)

    @pl.when(e == 0)
    def _():
        acc_ref[...] = jnp.zeros_like(acc_ref)
        for c in range(PEER_TOKEN_TILE // LANES):
            ls = slice(c * LANES, (c + 1) * LANES)
            bp_ref[:, c] = b_ref[:, :, ls].astype(BF16)
            rp_ref[:, c] = r_ref[:, :, ls].astype(BF16)

    def hidden(pp):
        rows = slice(pp * PIECE_ROWS, (pp + 1) * PIECE_ROWS)
        hid_ref[pp % 2] = _bdot(u_ref[rows, :], ht_ref[...])

    def gates(pp):
        for c in range(PEER_TOKEN_TILE // LANES):
            ls = slice(c * LANES, (c + 1) * LANES)
            accs = [jnp.zeros((PEER_KEYS, LANES), BF16) for _ in range(PIECE_KEYS)]
            for h in range(PEER_HEADS):
                b = bp_ref[h, c]
                r = rp_ref[h, c]
                for ii in range(PIECE_KEYS):
                    k = pp * PIECE_KEYS + ii
                    a16 = jnp.broadcast_to(a_ref[h, k:k + 1, ls], (BF16_ROWS, LANES)).astype(BF16)
                    n16 = jnp.broadcast_to(n_ref[h, k:k + 1, ls], (BF16_ROWS, LANES)).astype(BF16)
                    a128 = jnp.concatenate([a16] * (PEER_KEYS // BF16_ROWS), axis=0)
                    n128 = jnp.concatenate([n16] * (PEER_KEYS // BF16_ROWS), axis=0)
                    accs[ii] = accs[ii] + jnp.where(r < n128, b, jnp.zeros_like(b)) * a128
            for ii in range(PIECE_KEYS):
                rows = slice(ii * PEER_KEYS, (ii + 1) * PEER_KEYS)
                act = _gelu_tanh(hid_ref[pp % 2, rows, ls].astype(BF16))
                w_ref[pp % 2, rows, ls] = act * accs[ii]

    def project(pp):
        rows = slice(pp * PIECE_ROWS, (pp + 1) * PIECE_ROWS)
        acc_ref[...] += _bdot(vt_ref[0, :, rows], w_ref[pp % 2])

    hidden(0)
    for pp in range(PEER_PIECES):
        if pp + 1 < PEER_PIECES:
            hidden(pp + 1)
        gates(pp)
        project(pp)

    @pl.when(e == pl.num_programs(1) - 1)
    def _():
        o_ref[...] = x_ref[...] + g2_ref[0] * acc_ref[...].T


def peer_dense(ht_bf, u_bf, vt_bf, a_t, n_t, b_t, r_t, x, g2, rows_per_seg):
    t = x.shape[0]
    tt = PEER_TOKEN_TILE
    grid = (t // tt, PEER_N // PEER_EXPERT_TILE)
    gate_spec = pl.BlockSpec((PEER_HEADS, PEER_KEYS, tt), lambda ti, ei: (0, 0, ti))
    step_keys = pl.BlockSpec((PEER_HEADS, PEER_KEYS_PER_STEP, tt), lambda ti, ei: (0, ei, ti))
    return pl.pallas_call(
        _peer_dense_kernel, grid=grid,
        in_specs=[pl.BlockSpec((D_MODEL, tt), lambda ti, ei: (0, ti)),
                  pl.BlockSpec((PEER_EXPERT_TILE, D_MODEL), lambda ti, ei: (ei, 0)),
                  pl.BlockSpec((1, D_MODEL, PEER_EXPERT_TILE), lambda ti, ei: (ei, 0, 0)),
                  step_keys, step_keys, gate_spec, gate_spec,
                  pl.BlockSpec((tt, D_MODEL), lambda ti, ei: (ti, 0)),
                  _seg_spec(rows_per_seg, tt)],
        out_specs=pl.BlockSpec((tt, D_MODEL), lambda ti, ei: (ti, 0)),
        out_shape=jax.ShapeDtypeStruct((t, D_MODEL), F32),
        scratch_shapes=[pltpu.VMEM((D_MODEL, tt), F32),
                        pltpu.VMEM((2, PIECE_ROWS, tt), F32),
                        pltpu.VMEM((2, PIECE_ROWS, tt), BF16),
                        pltpu.VMEM((PEER_HEADS, tt // LANES, PEER_KEYS, LANES), BF16),
                        pltpu.VMEM((PEER_HEADS, tt // LANES, PEER_KEYS, LANES), BF16)],
        compiler_params=pltpu.CompilerParams(dimension_semantics=("parallel", "arbitrary"),
                                             vmem_limit_bytes=VMEM_LIMIT_BYTES),
        name="peer_dense",
    )(ht_bf, u_bf, vt_bf, a_t, n_t, b_t, r_t, x, g2)


def _fill_halo_scratch(scr, prev, cur, nxt, halo, ts):
    i = pl.program_id(1)
    scr[0:halo, :] = jnp.where(i > 0, prev, 0.0)
    scr[halo:halo + ts, :] = cur
    scr[halo + ts:halo + ts + halo, :] = jnp.where(i < pl.num_programs(1) - 1, nxt, 0.0)


def _depthwise_taps(scr, w_ref, taps, halo, r0, rows):
    off = halo - taps // 2
    acc = scr[off + r0:off + r0 + rows, :] * w_ref[0:1, :]
    for k in range(1, taps):
        acc = acc + scr[off + r0 + k:off + r0 + k + rows, :] * w_ref[k:k + 1, :]
    return acc


def _halo_specs(width, col_block, halo, ts, seq_len, total_rows):
    tiles = seq_len // ts
    per_tile = ts // halo
    last = total_rows // halo - 1
    prev = pl.BlockSpec((halo, width), lambda s, i: (jnp.maximum((s * tiles + i) * per_tile - 1, 0), col_block))
    cur = pl.BlockSpec((ts, width), lambda s, i: (s * tiles + i, col_block))
    nxt = pl.BlockSpec((halo, width), lambda s, i: (jnp.minimum((s * tiles + i + 1) * per_tile, last), col_block))
    return prev, cur, nxt


CONF_ROW_BLOCK = 32
QKV_ROW_BLOCK = 16


def _conformer_kernel(prev_ref, cur_ref, next_ref, w_ref, b_ref, lg_ref, lb_ref, o_ref, scr):
    ts = cur_ref.shape[0]
    glu = lambda blk: blk[:, :B_CH] * jax.nn.sigmoid(blk[:, B_CH:])
    _fill_halo_scratch(scr, glu(prev_ref[...]), glu(cur_ref[...]), glu(next_ref[...]), CONF_HALO, ts)
    for rb in range(ts // CONF_ROW_BLOCK):
        r0 = rb * CONF_ROW_BLOCK
        hh = _depthwise_taps(scr, w_ref, B_CONV, CONF_HALO, r0, CONF_ROW_BLOCK) + b_ref[...]
        mu = jnp.mean(hh, axis=-1, keepdims=True)
        var = jnp.mean(jnp.square(hh - mu), axis=-1, keepdims=True)
        y = (hh - mu) * lax.rsqrt(var + EPS) * lg_ref[...] + lb_ref[...]
        o_ref[r0:r0 + CONF_ROW_BLOCK, :] = (y * jax.nn.sigmoid(y)).astype(o_ref.dtype)


def conformer_branch(p, dw_w, dw_b, ln_g, ln_b, nseq, seq_len):
    t = p.shape[0]
    ts = CONV_ROW_TILE
    prev, cur, nxt = _halo_specs(2 * B_CH, COL_GLU // (2 * B_CH), CONF_HALO, ts, seq_len, t)
    row = lambda: pl.BlockSpec((1, B_CH), lambda s, i: (0, 0))
    w_pad = jnp.pad(dw_w, ((0, 32 - B_CONV), (0, 0)))
    return pl.pallas_call(
        _conformer_kernel, grid=(nseq, seq_len // ts),
        in_specs=[prev, cur, nxt, pl.BlockSpec((32, B_CH), lambda s, i: (0, 0)), row(), row(), row()],
        out_specs=pl.BlockSpec((ts, B_CH), lambda s, i: (s * (seq_len // ts) + i, 0)),
        out_shape=jax.ShapeDtypeStruct((t, B_CH), BF16),
        scratch_shapes=[pltpu.VMEM((ts + 2 * CONF_HALO, B_CH), F32)],
        compiler_params=pltpu.CompilerParams(dimension_semantics=("parallel", "parallel")),
        name="conformer_branch",
    )(p, p, p, w_pad, dw_b.reshape(1, B_CH), ln_g.reshape(1, B_CH), ln_b.reshape(1, B_CH))


def _qkv_conv_kernel(prev_ref, cur_ref, next_ref, w_ref, q_ref, k_ref, v_ref, scr):
    ts = cur_ref.shape[0]
    _fill_halo_scratch(scr, prev_ref[...], cur_ref[...], next_ref[...], QKV_HALO, ts)
    for rb in range(ts // QKV_ROW_BLOCK):
        r0 = rb * QKV_ROW_BLOCK
        rows = slice(r0, r0 + QKV_ROW_BLOCK)
        y = _depthwise_taps(scr, w_ref, SHORT_CONV, QKV_HALO, r0, QKV_ROW_BLOCK)
        y = y * jax.nn.sigmoid(y)
        for h in range(A_HEADS):
            cs = slice(h * A_DK, (h + 1) * A_DK)
            qh = y[:, h * A_DK:(h + 1) * A_DK]
            kh = y[:, A_DIM + h * A_DK:A_DIM + (h + 1) * A_DK]
            q_ref[rows, cs] = qh * (lax.rsqrt(jnp.sum(qh * qh, axis=-1, keepdims=True) + EPS) * (A_DK ** -0.5))
            k_ref[rows, cs] = kh * lax.rsqrt(jnp.sum(kh * kh, axis=-1, keepdims=True) + EPS)
        v_ref[rows, :] = y[:, 2 * A_DIM:]


def qkv_conv(p, conv_w, nseq, seq_len):
    t = p.shape[0]
    ts = CONV_ROW_TILE
    prev, cur, nxt = _halo_specs(3 * A_DIM, 0, QKV_HALO, ts, seq_len, t)
    out = pl.BlockSpec((ts, A_DIM), lambda s, i: (s * (seq_len // ts) + i, 0))
    w_pad = jnp.pad(conv_w, ((0, 8 - SHORT_CONV), (0, 0)))
    return pl.pallas_call(
        _qkv_conv_kernel, grid=(nseq, seq_len // ts),
        in_specs=[prev, cur, nxt, pl.BlockSpec((8, 3 * A_DIM), lambda s, i: (0, 0))],
        out_specs=[out, out, out],
        out_shape=[jax.ShapeDtypeStruct((t, A_DIM), F32)] * 3,
        scratch_shapes=[pltpu.VMEM((ts + 2 * QKV_HALO, 3 * A_DIM), F32)],
        compiler_params=pltpu.CompilerParams(dimension_semantics=("parallel", "parallel")),
        name="qkv_conv",
    )(p, p, p, w_pad)


def _mix_out_kernel(of_ref, ob_ref, z_ref, conf_ref, ng_ref, w_ref, x_ref, gate_ref, o_ref):
    o = of_ref[...] + ob_ref[...]
    z = z_ref[...]
    parts = []
    for h in range(A_HEADS):
        cs = slice(h * A_DV, (h + 1) * A_DV)
        oh = o[:, cs]
        zh = z[:, cs]
        scale = lax.rsqrt(jnp.mean(oh * oh, axis=-1, keepdims=True) + EPS)
        parts.append((oh * scale * ng_ref[...] * (zh * jax.nn.sigmoid(zh))).astype(BF16))
    oa = jnp.concatenate(parts, axis=1)
    mix = _bdot(oa, w_ref[:A_DIM, :]) + _bdot(conf_ref[...], w_ref[A_DIM:, :])
    o_ref[...] = x_ref[...] + gate_ref[0] * mix


def mix_out(o_f, o_b, p, conf, norm_g, w_bf, x, gate, rows_per_seg):
    t = x.shape[0]
    tm = MM_ROW_TILE
    half = lambda cb: pl.BlockSpec((tm, A_DIM), lambda i: (i, cb))
    return pl.pallas_call(
        _mix_out_kernel, grid=(t // tm,),
        in_specs=[half(0), half(0), half(COL_Z // A_DIM), half(0), pl.BlockSpec((1, A_DV), lambda i: (0, 0)),
                  pl.BlockSpec((D_MODEL, D_MODEL), lambda i: (0, 0)), pl.BlockSpec((tm, D_MODEL), lambda i: (i, 0)),
                  _seg_spec(rows_per_seg, tm)],
        out_specs=pl.BlockSpec((tm, D_MODEL), lambda i: (i, 0)),
        out_shape=jax.ShapeDtypeStruct((t, D_MODEL), F32),
        compiler_params=pltpu.CompilerParams(dimension_semantics=("parallel",),
                                             vmem_limit_bytes=VMEM_LIMIT_BYTES),
        name="mix_out",
    )(o_f, o_b, p, conf, norm_g.reshape(1, A_DV), w_bf, x, gate)


def grid_pos_emb(n_tokens):
    rows = n_tokens // GRID_W
    r = jnp.repeat(jnp.arange(rows, dtype=F32), GRID_W)
    col = jnp.tile(jnp.arange(GRID_W, dtype=F32), rows)
    nf = D_MODEL // 4
    freqs = jnp.exp(-math.log(POS_BASE) * jnp.arange(nf, dtype=F32) / nf)
    ar = r[:, None] * freqs
    ac = col[:, None] * freqs
    return jnp.concatenate([jnp.sin(ar), jnp.cos(ar), jnp.sin(ac), jnp.cos(ac)], axis=-1)


def delta_conformer_layer(x, p, ab, g1, nseq, seq_len, s0, e, prm, rows_per_seg):
    t = p.shape[0]
    seq = lambda m: m.reshape(nseq, seq_len, m.shape[-1])
    q, k, v = qkv_conv(p, prm['conv_qkv_w'][e], nseq, seq_len)
    alpha = ab[:, :2 * A_HEADS]
    beta = jax.nn.sigmoid(ab[:, 2 * A_HEADS:4 * A_HEADS])
    log_g = (-jnp.exp(prm['a_log'][e]).reshape(1, 2 * A_HEADS)
             * jax.nn.softplus(alpha + prm['dt_bias'][e].reshape(1, 2 * A_HEADS)))
    o_f, o_b, st = delta_scan(*delta_prep(seq(q), seq(k), seq(v), seq(log_g), seq(beta)), s0)
    conf = conformer_branch(p, prm['conf_dw_w'][e], prm['conf_dw_b'][e], prm['conf_ln_g'][e],
                            prm['conf_ln_b'][e], nseq, seq_len)
    x = mix_out(o_f.reshape(t, A_DIM), o_b.reshape(t, A_DIM), p, conf, prm['delta_norm_g'][e],
                prm['w_out_bf'][e], x, g1, rows_per_seg)
    return x, st


def trunk(x, nseq, seq_len, cond, s0, prm):
    t = x.shape[0]
    rows_per_seg = t // cond.shape[0]
    states = []
    for l in range(DEPTH):
        mod = jax.nn.silu(cond) @ prm['ada_w'][l] + prm['ada_b'][l]
        sh1, sc1, g1, sh2, sc2, g2 = [m[:, None, :] for m in jnp.split(mod, 6, axis=-1)]
        e = l // 2
        if l % 2 == 0:
            p, ab = norm_mm(x, prm['norm1_g'][l], sc1, sh1, prm['w_in_bf'][e], rows_per_seg, F32, tail=True)
            x, st = delta_conformer_layer(x, p, ab, g1, nseq, seq_len, s0[:, e], e, prm, rows_per_seg)
            states.append(st)
        else:
            z = norm_mm(x, prm['norm1_g'][l], sc1, sh1, prm['w_fnet_bf'][e], rows_per_seg, BF16)
            tm = min(seq_len, 512)
            x = seq_mix_res(prm['dft_seq'][seq_len], z, x, g1, seq_len, tm, min(seq_len, 1024))
        ht, a_t, n_t, b_t, r_t = peer_route(x, prm['norm2_g'][l], sc2, sh2, prm['peer_wq_hi'][l],
                                            prm['peer_wq_lo'][l], prm['peer_k1'][l], prm['peer_k2'][l],
                                            rows_per_seg)
        x = peer_dense(ht, prm['peer_u_bf'][l], prm['peer_vt_bf'][l], a_t, n_t, b_t, r_t, x, g2, rows_per_seg)
    xf = x * lax.rsqrt(jnp.mean(x * x, axis=-1, keepdims=True) + EPS) * prm['final_norm_g']
    return xf, jnp.stack(states, axis=1)


def kernel(x_prompt, x_sample, state_delta, c, c_ctx, ada_w, ada_b, norm1_g, norm2_g, w_in_ab, conv_qkv_w,
           a_log, dt_bias, delta_norm_g, conf_dw_w, conf_dw_b, conf_ln_g, conf_ln_b, w_out_ab, w_out_c,
           peer_wq, peer_k1, peer_k2, peer_u, peer_v, final_norm_g):
    bp, sp, _ = x_prompt.shape
    bs, ss, _ = x_sample.shape
    bdc, bds = dft_group_matrices(D_MODEL // C_GROUPS, C_GROUPS)
    w_fnet = [jnp.concatenate([mm3(bdc, w_out_c[e], BF16), mm3(bds, w_out_c[e], BF16)], axis=1)
              for e in range(DEPTH // 2)]
    o4 = 4 * A_DIM
    w_in = jnp.concatenate([w_in_ab[:, :, :o4], w_in_ab[:, :, o4 + 4 * A_HEADS:], w_in_ab[:, :, o4:o4 + 4 * A_HEADS]],
                           axis=-1).astype(BF16)
    wq_hi, wq_lo = split_bf16(peer_wq)
    prm = {'ada_w': ada_w, 'ada_b': ada_b, 'norm1_g': norm1_g, 'norm2_g': norm2_g,
           'w_in_bf': jnp.pad(w_in, ((0, 0), (0, 0), (0, P_AB_PAD - P_AB))),
           'conv_qkv_w': conv_qkv_w, 'a_log': a_log, 'dt_bias': dt_bias,
           'delta_norm_g': delta_norm_g, 'conf_dw_w': conf_dw_w, 'conf_dw_b': conf_dw_b,
           'conf_ln_g': conf_ln_g, 'conf_ln_b': conf_ln_b, 'w_out_bf': w_out_ab.astype(BF16),
           'w_fnet_bf': w_fnet, 'dft_seq': {s: dft_seq_matrix(s) for s in {sp, ss}},
           'peer_wq_hi': wq_hi, 'peer_wq_lo': wq_lo, 'peer_k1': peer_k1, 'peer_k2': peer_k2,
           'peer_u_bf': peer_u.astype(BF16),
           'peer_vt_bf': jnp.transpose(peer_v.astype(BF16).reshape(DEPTH, PEER_N // PEER_EXPERT_TILE,
                                                                   PEER_EXPERT_TILE, D_MODEL), (0, 1, 3, 2)),
           'final_norm_g': final_norm_g}
    ne = (DEPTH + 1) // 2
    s0_ctx = jnp.zeros((bp, ne, 2, A_HEADS, A_DK, A_DV), F32)
    y_prompt, ctx_states = trunk(x_prompt.reshape(bp * sp, D_MODEL), bp, sp, c_ctx[None, :], s0_ctx, prm)
    xs = (x_sample + grid_pos_emb(ss)[None]).reshape(bs * ss, D_MODEL)
    y_sample, _ = trunk(xs, bs, ss, c, state_delta, prm)
    return (y_prompt.reshape(bp, sp, D_MODEL), y_sample.reshape(bs, ss, D_MODEL), ctx_states)
```

```python
import math

import jax
import jax.numpy as jnp
import numpy as np
from jax import lax
from jax.experimental import pallas as pl
from jax.experimental.pallas import tpu as pltpu

D_MODEL = 1024
DEPTH = 4
GRID_W = 64
POS_BASE = 10000.0
EPS = 1e-6
A_HEADS = 4
A_DK = 128
A_DV = 128
A_DIM = A_HEADS * A_DV
CHUNK = 64
B_CH = D_MODEL // 2
P_AB = 4 * A_DIM + 4 * A_HEADS + 2 * B_CH
C_GROUPS = 8
PEER_HEADS = 8
PEER_KEYS = 128
PEER_N = PEER_KEYS * PEER_KEYS
PEER_DK = 128
PEER_TOPK = 16

F32 = jnp.float32
BF16 = jnp.bfloat16
NEG_INF = float("-inf")

LANES = 128
VMEM_LIMIT_BYTES = 56 * 1024 * 1024
MM_ROW_TILE = 512
ROUTE_TOKEN_TILE = 256
PEER_TOKEN_TILE = 512
PEER_KEYS_PER_STEP = 8
PEER_EXPERT_TILE = PEER_KEYS_PER_STEP * PEER_KEYS
PREP_CHUNKS = 2
SCAN_SEQS = 2
P_AB_PAD = 3200
COL_Z = 3 * A_DIM
COL_GLU = 4 * A_DIM
COL_AB = 4 * A_DIM + 2 * B_CH
CONV_ROW_TILE = 256
SHORT_CONV = 7
B_CONV = 31
CONF_HALO = 16
QKV_HALO = 8
PEER_PIECES = 2
PIECE_KEYS = PEER_KEYS_PER_STEP // PEER_PIECES
PIECE_ROWS = PIECE_KEYS * PEER_KEYS
BF16_ROWS = 16

NT_DIMS = (((1,), (1,)), ((), ()))
TN_DIMS = (((0,), (0,)), ((), ()))


def _bdot(a, b):
    return jnp.dot(a, b, preferred_element_type=F32)


def _split_bf16(a):
    hi = a.astype(BF16)
    lo = (a - hi.astype(F32)).astype(BF16)
    return hi, lo


def _dot3(a, b):
    ah, al = _split_bf16(a)
    bh, bl = _split_bf16(b)
    return _bdot(ah, bh) + (_bdot(ah, bl) + _bdot(al, bh))


def _dot_exact_lhs(a01, b):
    a = a01.astype(BF16)
    bh = b.astype(BF16)
    r1 = b - bh.astype(F32)
    bm = r1.astype(BF16)
    bl = (r1 - bm.astype(F32)).astype(BF16)
    return _bdot(a, bh) + (_bdot(a, bm) + _bdot(a, bl))


def _gelu_tanh(x):
    return 0.5 * x * (1.0 + jnp.tanh(math.sqrt(2.0 / math.pi) * (x + 0.044715 * (x * x * x))))


def _seg_spec(rows_per_seg, tile):
    per = rows_per_seg // tile
    return pl.BlockSpec((1, 1, D_MODEL), lambda i, *_: (i // per, 0, 0))


def _norm_modulate(x, g, sc, sh):
    hn = x * lax.rsqrt(jnp.mean(x * x, axis=-1, keepdims=True) + EPS) * g
    return hn * (1.0 + sc) + sh


def _norm_mm_kernel(x_ref, g_ref, sc_ref, sh_ref, w_ref, o_ref, *tail_ref):
    h = _norm_modulate(x_ref[...], g_ref[...], sc_ref[0], sh_ref[0]).astype(BF16)
    res = _bdot(h, w_ref[...])
    o_ref[...] = res.astype(o_ref.dtype)
    if tail_ref:
        tail_ref[0][...] = res[:, res.shape[1] - LANES:]


def norm_mm(x, norm_g, sc, sh, w_bf, rows_per_seg, out_dtype, tail=False):
    m = x.shape[0]
    n = w_bf.shape[1]
    tm = MM_ROW_TILE
    seg = _seg_spec(rows_per_seg, tm)
    out_specs = [pl.BlockSpec((tm, n), lambda i: (i, 0))]
    out_shape = [jax.ShapeDtypeStruct((m, n), out_dtype)]
    if tail:
        out_specs.append(pl.BlockSpec((tm, LANES), lambda i: (i, 0)))
        out_shape.append(jax.ShapeDtypeStruct((m, LANES), F32))
    outs = pl.pallas_call(
        _norm_mm_kernel, grid=(m // tm,),
        in_specs=[pl.BlockSpec((tm, D_MODEL), lambda i: (i, 0)), pl.BlockSpec((1, D_MODEL), lambda i: (0, 0)),
                  seg, seg, pl.BlockSpec((D_MODEL, n), lambda i: (0, 0))],
        out_specs=out_specs, out_shape=out_shape,
        compiler_params=pltpu.CompilerParams(dimension_semantics=("parallel",),
                                             vmem_limit_bytes=VMEM_LIMIT_BYTES),
        name="norm_mm",
    )(x, norm_g.reshape(1, D_MODEL), sc, sh, w_bf)
    return outs if tail else outs[0]


def _split_kernel(a_ref, hi_ref, lo_ref):
    hi, lo = _split_bf16(a_ref[0])
    hi_ref[0] = hi
    lo_ref[0] = lo


def split_bf16(a):
    nl, nr, nc = a.shape
    blk = pl.BlockSpec((1, MM_ROW_TILE, nc), lambda l, i: (l, i, 0))
    return pl.pallas_call(
        _split_kernel, grid=(nl, nr // MM_ROW_TILE), in_specs=[blk], out_specs=[blk, blk],
        out_shape=[jax.ShapeDtypeStruct(a.shape, BF16)] * 2,
        compiler_params=pltpu.CompilerParams(dimension_semantics=("parallel", "parallel")),
        name="split_bf16",
    )(a)


def _mm3_kernel(a_ref, b_ref, o_ref):
    o_ref[...] = _dot3(a_ref[...], b_ref[...]).astype(o_ref.dtype)


def mm3(a, b, out_dtype):
    return pl.pallas_call(_mm3_kernel, out_shape=jax.ShapeDtypeStruct((a.shape[0], b.shape[1]), out_dtype),
                          compiler_params=pltpu.CompilerParams(vmem_limit_bytes=VMEM_LIMIT_BYTES),
                          name="mm3")(a, b)


def _seqmix_kernel(f_ref, z_ref, x_ref, gate_ref, o_ref, acc_ref):
    k = pl.program_id(2)

    @pl.when(k == 0)
    def _():
        acc_ref[...] = jnp.zeros_like(acc_ref)

    acc_ref[...] += _bdot(f_ref[...], z_ref[...])

    @pl.when(k == pl.num_programs(2) - 1)
    def _():
        o_ref[...] = x_ref[...] + gate_ref[0] * acc_ref[...]


def seq_mix_res(fmat, z, x, gate, seq_len, tm, tk):
    t = x.shape[0]
    nseq = t // seq_len
    seqs_per_seg = nseq // gate.shape[0]
    mt = seq_len // tm
    kt_half = seq_len // tk
    return pl.pallas_call(
        _seqmix_kernel, grid=(nseq, mt, 2 * kt_half),
        in_specs=[pl.BlockSpec((tm, tk), lambda s, i, k: (i, k)),
                  pl.BlockSpec((tk, D_MODEL), lambda s, i, k: (s * kt_half + k % kt_half, k // kt_half)),
                  pl.BlockSpec((tm, D_MODEL), lambda s, i, k: (s * mt + i, 0)),
                  pl.BlockSpec((1, 1, D_MODEL), lambda s, i, k: (s // seqs_per_seg, 0, 0))],
        out_specs=pl.BlockSpec((tm, D_MODEL), lambda s, i, k: (s * mt + i, 0)),
        out_shape=jax.ShapeDtypeStruct((t, D_MODEL), F32),
        scratch_shapes=[pltpu.VMEM((tm, D_MODEL), F32)],
        compiler_params=pltpu.CompilerParams(dimension_semantics=("parallel", "parallel", "arbitrary"),
                                             vmem_limit_bytes=VMEM_LIMIT_BYTES),
        name="seq_mix_res",
    )(fmat, z, x, gate)


def _dft_tables(n, cols):
    r = jnp.arange(n, dtype=jnp.int32)[:, None]
    ang = ((r * cols[None, :]) % n).astype(F32) * (2.0 * math.pi / n)
    return jnp.cos(ang), jnp.sin(ang)


def dft_seq_matrix(s):
    w = 1 << (int(math.log2(s)) // 2)
    ch, sh_ = _dft_tables(s, jnp.arange(s // w, dtype=jnp.int32) * w)
    cl, sl = _dft_tables(s, jnp.arange(w, dtype=jnp.int32))
    sc = 1.0 / math.sqrt(s)
    c = (ch[:, :, None] * cl[:, None, :] - sh_[:, :, None] * sl[:, None, :]).reshape(s, s) * sc
    sn = (sh_[:, :, None] * cl[:, None, :] + ch[:, :, None] * sl[:, None, :]).reshape(s, s) * sc
    return jnp.concatenate([c, -sn], axis=1).astype(BF16)


def dft_group_matrices(n, groups):
    c, s = _dft_tables(n, jnp.arange(n, dtype=jnp.int32))
    sc = 1.0 / math.sqrt(n)
    eye = jnp.eye(groups, dtype=F32)
    return jnp.kron(eye, c * sc), jnp.kron(eye, s * sc)


def _delta_prep_kernel(q_ref, k_ref, v_ref, lg_ref, bt_ref, w_ref, u_ref, qd_ref, kd_ref, p_ref, g_ref):
    r = lax.broadcasted_iota(jnp.int32, (CHUNK, CHUNK), 0)
    c = lax.broadcasted_iota(jnp.int32, (CHUNK, CHUNK), 1)
    eye = (r == c).astype(F32)
    ones = jnp.ones((CHUNK, CHUNK), F32)
    incl = (r >= c, r <= c)
    strict = (r > c, r < c)
    tri = (incl[0].astype(F32), incl[1].astype(F32))
    tri_t = (tri[1], tri[0])
    last = (CHUNK - 1, 0)
    chains = [(cg, d, h) for cg in range(PREP_CHUNKS) for d in range(2) for h in range(A_HEADS)]
    rows = lambda cg: slice(cg * CHUNK, (cg + 1) * CHUNK)
    cols = lambda h: slice(h * A_DK, (h + 1) * A_DK)
    kk = {}
    qk = {}
    for cg in range(PREP_CHUNKS):
        for h in range(A_HEADS):
            kb = k_ref[0, rows(cg), cols(h)].astype(BF16)
            kk[cg, h] = lax.dot_general(kb, kb, NT_DIMS, preferred_element_type=F32)
            qk[cg, h] = lax.dot_general(q_ref[0, rows(cg), cols(h)].astype(BF16), kb, NT_DIMS,
                                        preferred_element_type=F32)
    lgw = {}
    btw = {}
    for (cg, d, h) in chains:
        col = d * A_HEADS + h
        lgw[cg, d, h] = jnp.broadcast_to(lg_ref[0, rows(cg), col:col + 1], (CHUNK, A_DK))
        btw[cg, d, h] = jnp.broadcast_to(bt_ref[0, rows(cg), col:col + 1], (CHUNK, A_DK))
    gam = {ch: _dot_exact_lhs(tri[ch[1]], lgw[ch]) for ch in chains}
    gam_row = {ch: _dot_exact_lhs(ones, lgw[ch][:, :CHUNK] * tri_t[ch[1]]) for ch in chains}
    decay = {}
    lmat = {}
    for ch in chains:
        cg, d, h = ch
        diff = gam[ch][:, :CHUNK] - gam_row[ch]
        decay[ch] = jnp.where(incl[d], jnp.exp(jnp.where(incl[d], diff, 0.0)), 0.0)
        lmat[ch] = jnp.where(strict[d], btw[ch][:, :CHUNK] * decay[ch] * kk[cg, h], 0.0)
    pinv = {ch: eye - jnp.where((r // 2 == c // 2) & (r != c), lmat[ch], 0.0) for ch in chains}
    s = 2
    while s < CHUNK:
        join = (r // (2 * s) == c // (2 * s)) & (r // s != c // s)
        tc = {ch: _dot3(pinv[ch], jnp.where(join, lmat[ch], 0.0)) for ch in chains}
        pinv = {ch: pinv[ch] - _dot3(tc[ch], pinv[ch]) for ch in chains}
        s *= 2
    for ch in chains:
        cg, d, h = ch
        kh = k_ref[0, rows(cg), cols(h)]
        vh = v_ref[0, rows(cg), cols(h)]
        qh = q_ref[0, rows(cg), cols(h)]
        egam = jnp.exp(gam[ch])
        rhs = jnp.concatenate([kh * (btw[ch] * egam), vh * btw[ch]], axis=1)
        sol = _dot3(pinv[ch], rhs)
        w_ref[0, d, rows(cg), cols(h)] = sol[:, :A_DK].astype(BF16)
        u_ref[0, d, rows(cg), cols(h)] = sol[:, A_DK:]
        qd_ref[0, d, rows(cg), cols(h)] = (qh * egam).astype(BF16)
        glast = jnp.broadcast_to(gam[ch][last[d]:last[d] + 1, :], (CHUNK, A_DK))
        kd_ref[0, d, rows(cg), cols(h)] = (kh * jnp.exp(glast - gam[ch])).astype(BF16)
        g_ref[0, d, rows(cg), cols(h)] = jnp.exp(glast)
        p_ref[0, d, rows(cg), h * CHUNK:(h + 1) * CHUNK] = (decay[ch] * qk[cg, h]).astype(BF16)


def delta_prep(q, k, v, lg, bt):
    b, s, _ = q.shape
    rt = PREP_CHUNKS * CHUNK
    blk = lambda w: pl.BlockSpec((1, rt, w), lambda bi, ni: (bi, ni, 0))
    oblk = lambda w: pl.BlockSpec((1, 2, rt, w), lambda bi, ni: (bi, 0, ni, 0))
    sh = lambda w, dt: jax.ShapeDtypeStruct((b, 2, s, w), dt)
    return pl.pallas_call(
        _delta_prep_kernel, grid=(b, s // rt),
        in_specs=[blk(A_DIM), blk(A_DIM), blk(A_DIM), blk(2 * A_HEADS), blk(2 * A_HEADS)],
        out_specs=[oblk(A_DIM), oblk(A_DIM), oblk(A_DIM), oblk(A_DIM), oblk(A_HEADS * CHUNK), oblk(A_DIM)],
        out_shape=[sh(A_DIM, BF16), sh(A_DIM, F32), sh(A_DIM, BF16), sh(A_DIM, BF16),
                   sh(A_HEADS * CHUNK, BF16), sh(A_DIM, F32)],
        compiler_params=pltpu.CompilerParams(dimension_semantics=("parallel", "parallel")),
        name="delta_prep",
    )(q, k, v, lg, bt)


def _delta_scan_kernel(*refs):
    ins = refs[:12]
    s0_ref = refs[12]
    of_ref, ob_ref, sout_ref, state = refs[13:]
    n = pl.program_id(1)

    @pl.when(n == 0)
    def _():
        state[...] = s0_ref[...]

    outs = (of_ref, ob_ref)
    chains = [(g, d, h) for g in range(SCAN_SEQS) for d in range(2) for h in range(A_HEADS)]
    cs = lambda h: slice(h * A_DK, (h + 1) * A_DK)
    ref = lambda d, i: ins[d * 6 + i]
    s_old = {ch: state[ch] for ch in chains}
    wqs = {}
    for (g, d, h) in chains:
        wq = jnp.concatenate([ref(d, 0)[g, 0, :, cs(h)], ref(d, 2)[g, 0, :, cs(h)]], axis=0)
        wqs[g, d, h] = _bdot(wq, s_old[g, d, h].astype(BF16))
    unb = {}
    for (g, d, h) in chains:
        unb[g, d, h] = (ref(d, 1)[g, 0, :, cs(h)] - wqs[g, d, h][:CHUNK]).astype(BF16)
    for (g, d, h) in chains:
        o = wqs[g, d, h][CHUNK:] + _bdot(ref(d, 4)[g, 0, :, h * CHUNK:(h + 1) * CHUNK], unb[g, d, h])
        outs[d][g, :, cs(h)] = o
    for (g, d, h) in chains:
        upd = lax.dot_general(ref(d, 3)[g, 0, :, cs(h)], unb[g, d, h], TN_DIMS, preferred_element_type=F32)
        gs = jnp.broadcast_to(ref(d, 5)[g, 0, 0:1, cs(h)], (A_DK, A_DV))
        state[g, d, h] = gs * s_old[g, d, h] + upd

    @pl.when(n == pl.num_programs(1) - 1)
    def _():
        sout_ref[...] = state[...]


def delta_scan(w, u, qd, kd, p, gl, s0):
    b, _, s, _ = u.shape
    n = s // CHUNK

    def spec(wd, d):
        if d == 0:
            return pl.BlockSpec((SCAN_SEQS, 1, CHUNK, wd), lambda bi, ni: (bi, 0, ni, 0))
        return pl.BlockSpec((SCAN_SEQS, 1, CHUNK, wd), lambda bi, ni: (bi, 1, n - 1 - ni, 0))

    arrs = (w, u, qd, kd, p, gl)
    in_specs = [spec(a.shape[-1], d) for d in range(2) for a in arrs]
    st = pl.BlockSpec((SCAN_SEQS, 2, A_HEADS, A_DK, A_DV), lambda bi, ni: (bi, 0, 0, 0, 0))
    of = pl.BlockSpec((SCAN_SEQS, CHUNK, A_DIM), lambda bi, ni: (bi, ni, 0))
    ob = pl.BlockSpec((SCAN_SEQS, CHUNK, A_DIM), lambda bi, ni: (bi, n - 1 - ni, 0))
    return pl.pallas_call(
        _delta_scan_kernel, grid=(b // SCAN_SEQS, n),
        in_specs=in_specs + [st],
        out_specs=[of, ob, st],
        out_shape=[jax.ShapeDtypeStruct((b, s, A_DIM), F32), jax.ShapeDtypeStruct((b, s, A_DIM), F32),
                   jax.ShapeDtypeStruct((b, 2, A_HEADS, A_DK, A_DV), F32)],
        scratch_shapes=[pltpu.VMEM((SCAN_SEQS, 2, A_HEADS, A_DK, A_DV), F32)],
        compiler_params=pltpu.CompilerParams(dimension_semantics=("parallel", "arbitrary")),
        name="delta_scan",
    )(*(arrs + arrs), s0)


def _cand_tables():
    pairs = [(r, c) for r in range(PEER_TOPK) for c in range(PEER_TOPK) if (r + 1) * (c + 1) <= PEER_TOPK]
    npad = 64
    e1 = np.zeros((npad, PEER_TOPK), np.float32)
    e2 = np.zeros((npad, PEER_TOPK), np.float32)
    m = np.zeros((PEER_TOPK, npad), np.float32)
    for k, (r, c) in enumerate(pairs):
        e1[k, r] = 1
        e2[k, c] = 1
        m[r, k] = 1
    return len(pairs), e1, e2, m


N_CAND, _CAND_E1, _CAND_E2, _CAND_ROW = _cand_tables()


def _extract_topk(s, n_iter, want_rank=True):
    k, t = s.shape
    work = s.reshape(k // 8, 8, t)
    rank = jnp.full(work.shape, float(n_iter), F32) if want_rank else None
    vals = []
    for r in range(n_iter):
        m = jnp.max(jnp.max(work, axis=0), axis=0, keepdims=True)
        hit = work == jnp.broadcast_to(m, (8, t))[None]
        if want_rank:
            rank = jnp.where(hit, float(r), rank)
        work = jnp.where(hit, NEG_INF, work)
        vals.append(m)
    return vals, (rank.reshape(k, t) if want_rank else None)


def _route_kernel(x_ref, g_ref, sc_ref, sh_ref, wqh_ref, wql_ref, k1_ref, k2_ref, e1_ref, e2_ref, mrow_ref,
                  ht_ref, a_ref, n_ref, b_ref, r_ref, q_scr):
    hmod = _norm_modulate(x_ref[...], g_ref[...], sc_ref[0], sh_ref[0])
    ht_ref[...] = hmod.T.astype(BF16)
    hh, hl = _split_bf16(hmod)
    q_scr[...] = _bdot(hh, wqh_ref[...]) + (_bdot(hh, wql_ref[...]) + _bdot(hl, wqh_ref[...]))
    tt = x_ref.shape[0]

    def head(h, carry):
        c1 = pl.ds(pl.multiple_of(h * 2 * PEER_DK, PEER_DK), PEER_DK)
        c2 = pl.ds(pl.multiple_of(h * 2 * PEER_DK + PEER_DK, PEER_DK), PEER_DK)
        hp = lax.Precision.HIGHEST
        s1 = lax.dot_general(k1_ref[h], q_scr[:, c1], NT_DIMS, precision=hp, preferred_element_type=F32)
        s2 = lax.dot_general(k2_ref[h], q_scr[:, c2], NT_DIMS, precision=hp, preferred_element_type=F32)
        v1, _ = _extract_topk(s1, PEER_TOPK, want_rank=False)
        v2, rank2 = _extract_topk(s2, PEER_TOPK)
        v1m = jnp.concatenate(v1, axis=0)
        v2m = jnp.concatenate(v2, axis=0)
        cand = (jnp.dot(e1_ref[...], v1m, precision=hp, preferred_element_type=F32)
                + jnp.dot(e2_ref[...], v2m, precision=hp, preferred_element_type=F32))
        row = lax.broadcasted_iota(jnp.int32, cand.shape, 0)
        cand = jnp.where(row < N_CAND, cand, NEG_INF)
        _, crank = _extract_topk(cand, PEER_TOPK)
        sel = crank < float(PEER_TOPK)
        cmax = v1[0] + v2[0]
        z = jnp.sum(jnp.where(sel, jnp.exp(cand - cmax), 0.0), axis=0, keepdims=True)
        n_r = _bdot(mrow_ref[...], jnp.where(sel, 1.0, 0.0).astype(BF16))
        s1_3 = s1.reshape(PEER_KEYS // 8, 8, tt)
        nn = jnp.zeros_like(s1_3)
        for r in range(PEER_TOPK):
            nn = jnp.where(s1_3 == jnp.broadcast_to(v1[r], (8, tt))[None],
                           jnp.broadcast_to(n_r[r:r + 1, :], (8, tt))[None], nn)
        a_ref[h] = jnp.exp(s1 - v1[0]) / z
        n_ref[h] = nn.reshape(PEER_KEYS, tt)
        b_ref[h] = jnp.where(rank2 < float(PEER_TOPK), jnp.exp(s2 - v2[0]), 0.0).astype(BF16)
        r_ref[h] = rank2.astype(BF16)
        return carry

    lax.fori_loop(0, PEER_HEADS, head, 0, unroll=2)


def peer_route(x, norm_g, sc, sh, wq_hi, wq_lo, k1, k2, rows_per_seg, layer):
    wq_spec = pl.BlockSpec((None, D_MODEL, 2 * PEER_HEADS * PEER_DK), lambda i: (layer, 0, 0))
    t = x.shape[0]
    tt = ROUTE_TOKEN_TILE
    gate = pl.BlockSpec((PEER_HEADS, PEER_KEYS, tt), lambda i: (0, 0, i))
    full = lambda shp: pl.BlockSpec(shp, lambda i: (0,) * len(shp))
    seg = _seg_spec(rows_per_seg, tt)
    return pl.pallas_call(
        _route_kernel, grid=(t // tt,),
        in_specs=[pl.BlockSpec((tt, D_MODEL), lambda i: (i, 0)), full((1, D_MODEL)), seg, seg,
                  wq_spec, wq_spec,
                  full((PEER_HEADS, PEER_KEYS, PEER_DK)),
                  full((PEER_HEADS, PEER_KEYS, PEER_DK)), full((64, PEER_TOPK)), full((64, PEER_TOPK)),
                  full((PEER_TOPK, 64))],
        out_specs=[pl.BlockSpec((D_MODEL, tt), lambda i: (0, i)), gate, gate, gate, gate],
        out_shape=[jax.ShapeDtypeStruct((D_MODEL, t), BF16)]
        + [jax.ShapeDtypeStruct((PEER_HEADS, PEER_KEYS, t), dt) for dt in (F32, F32, BF16, BF16)],
        scratch_shapes=[pltpu.VMEM((tt, 2 * PEER_HEADS * PEER_DK), F32)],
        compiler_params=pltpu.CompilerParams(dimension_semantics=("parallel",),
                                             vmem_limit_bytes=VMEM_LIMIT_BYTES),
        name="peer_route",
    )(x, norm_g.reshape(1, D_MODEL), sc, sh, wq_hi, wq_lo, k1, k2,
      jnp.asarray(_CAND_E1), jnp.asarray(_CAND_E2), jnp.asarray(_CAND_ROW, BF16))


def _peer_dense_kernel(ht_ref, u_ref, vt_ref, a_ref, n_ref, b_ref, r_ref, x_ref, g2_ref, o_ref,
                       acc_ref, hid_ref, w_ref, bp_ref, rp_ref):
    e = pl.program_id(1)

    @pl.when(e == 0)
    def _():
        acc_ref[...] = jnp.zeros_like(acc_ref)
        for c in range(PEER_TOKEN_TILE // LANES):
            ls = slice(c * LANES, (c + 1) * LANES)
            bp_ref[:, c] = b_ref[:, :, ls].astype(BF16)
            rp_ref[:, c] = r_ref[:, :, ls].astype(BF16)

    def hidden(pp):
        rows = slice(pp * PIECE_ROWS, (pp + 1) * PIECE_ROWS)
        hid_ref[pp % 2] = _bdot(u_ref[rows, :], ht_ref[...])

    def gates(pp):
        for c in range(PEER_TOKEN_TILE // LANES):
            ls = slice(c * LANES, (c + 1) * LANES)
            accs = [jnp.zeros((PEER_KEYS, LANES), BF16) for _ in range(PIECE_KEYS)]
            for h in range(PEER_HEADS):
                b = bp_ref[h, c]
                r = rp_ref[h, c]
                for ii in range(PIECE_KEYS):
                    k = pp * PIECE_KEYS + ii
                    a16 = jnp.broadcast_to(a_ref[h, k:k + 1, ls], (BF16_ROWS, LANES)).astype(BF16)
                    n16 = jnp.broadcast_to(n_ref[h, k:k + 1, ls], (BF16_ROWS, LANES)).astype(BF16)
                    a128 = jnp.concatenate([a16] * (PEER_KEYS // BF16_ROWS), axis=0)
                    n128 = jnp.concatenate([n16] * (PEER_KEYS // BF16_ROWS), axis=0)
                    accs[ii] = accs[ii] + jnp.where(r < n128, b, jnp.zeros_like(b)) * a128
            for ii in range(PIECE_KEYS):
                rows = slice(ii * PEER_KEYS, (ii + 1) * PEER_KEYS)
                act = _gelu_tanh(hid_ref[pp % 2, rows, ls].astype(BF16))
                w_ref[pp % 2, rows, ls] = act * accs[ii]

    def project(pp):
        rows = slice(pp * PIECE_ROWS, (pp + 1) * PIECE_ROWS)
        acc_ref[...] += _bdot(vt_ref[0, :, rows], w_ref[pp % 2])

    hidden(0)
    for pp in range(PEER_PIECES):
        if pp + 1 < PEER_PIECES:
            hidden(pp + 1)
        gates(pp)
        project(pp)

    @pl.when(e == pl.num_programs(1) - 1)
    def _():
        o_ref[...] = x_ref[...] + g2_ref[0] * acc_ref[...].T


def peer_dense(ht_bf, u_bf, vt_bf, a_t, n_t, b_t, r_t, x, g2, rows_per_seg, layer):
    t = x.shape[0]
    tt = PEER_TOKEN_TILE
    grid = (t // tt, PEER_N // PEER_EXPERT_TILE)
    gate_spec = pl.BlockSpec((PEER_HEADS, PEER_KEYS, tt), lambda ti, ei: (0, 0, ti))
    step_keys = pl.BlockSpec((PEER_HEADS, PEER_KEYS_PER_STEP, tt), lambda ti, ei: (0, ei, ti))
    return pl.pallas_call(
        _peer_dense_kernel, grid=grid,
        in_specs=[pl.BlockSpec((D_MODEL, tt), lambda ti, ei: (0, ti)),
                  pl.BlockSpec((None, PEER_EXPERT_TILE, D_MODEL), lambda ti, ei: (layer, ei, 0)),
                  pl.BlockSpec((None, 1, D_MODEL, PEER_EXPERT_TILE), lambda ti, ei: (layer, ei, 0, 0)),
                  step_keys, step_keys, gate_spec, gate_spec,
                  pl.BlockSpec((tt, D_MODEL), lambda ti, ei: (ti, 0)),
                  _seg_spec(rows_per_seg, tt)],
        out_specs=pl.BlockSpec((tt, D_MODEL), lambda ti, ei: (ti, 0)),
        out_shape=jax.ShapeDtypeStruct((t, D_MODEL), F32),
        scratch_shapes=[pltpu.VMEM((D_MODEL, tt), F32),
                        pltpu.VMEM((2, PIECE_ROWS, tt), F32),
                        pltpu.VMEM((2, PIECE_ROWS, tt), BF16),
                        pltpu.VMEM((PEER_HEADS, tt // LANES, PEER_KEYS, LANES), BF16),
                        pltpu.VMEM((PEER_HEADS, tt // LANES, PEER_KEYS, LANES), BF16)],
        compiler_params=pltpu.CompilerParams(dimension_semantics=("parallel", "arbitrary"),
                                             vmem_limit_bytes=VMEM_LIMIT_BYTES),
        name="peer_dense",
    )(ht_bf, u_bf, vt_bf, a_t, n_t, b_t, r_t, x, g2)


def _fill_halo_scratch(scr, prev, cur, nxt, halo, ts):
    i = pl.program_id(1)
    scr[0:halo, :] = jnp.where(i > 0, prev, 0.0)
    scr[halo:halo + ts, :] = cur
    scr[halo + ts:halo + ts + halo, :] = jnp.where(i < pl.num_programs(1) - 1, nxt, 0.0)


def _depthwise_taps(scr, w_ref, taps, halo, r0, rows):
    off = halo - taps // 2
    acc = scr[off + r0:off + r0 + rows, :] * w_ref[0:1, :]
    for k in range(1, taps):
        acc = acc + scr[off + r0 + k:off + r0 + k + rows, :] * w_ref[k:k + 1, :]
    return acc


def _halo_specs(width, col_block, halo, ts, seq_len, total_rows):
    tiles = seq_len // ts
    per_tile = ts // halo
    last = total_rows // halo - 1
    prev = pl.BlockSpec((halo, width), lambda s, i: (jnp.maximum((s * tiles + i) * per_tile - 1, 0), col_block))
    cur = pl.BlockSpec((ts, width), lambda s, i: (s * tiles + i, col_block))
    nxt = pl.BlockSpec((halo, width), lambda s, i: (jnp.minimum((s * tiles + i + 1) * per_tile, last), col_block))
    return prev, cur, nxt


CONF_ROW_BLOCK = 32
QKV_ROW_BLOCK = 16


def _conformer_kernel(prev_ref, cur_ref, next_ref, w_ref, b_ref, lg_ref, lb_ref, o_ref, scr):
    ts = cur_ref.shape[0]
    glu = lambda blk: blk[:, :B_CH] * jax.nn.sigmoid(blk[:, B_CH:])
    _fill_halo_scratch(scr, glu(prev_ref[...]), glu(cur_ref[...]), glu(next_ref[...]), CONF_HALO, ts)
    for rb in range(ts // CONF_ROW_BLOCK):
        r0 = rb * CONF_ROW_BLOCK
        hh = _depthwise_taps(scr, w_ref, B_CONV, CONF_HALO, r0, CONF_ROW_BLOCK) + b_ref[...]
        mu = jnp.mean(hh, axis=-1, keepdims=True)
        var = jnp.mean(jnp.square(hh - mu), axis=-1, keepdims=True)
        y = (hh - mu) * lax.rsqrt(var + EPS) * lg_ref[...] + lb_ref[...]
        o_ref[r0:r0 + CONF_ROW_BLOCK, :] = (y * jax.nn.sigmoid(y)).astype(o_ref.dtype)


def conformer_branch(p, dw_w, dw_b, ln_g, ln_b, nseq, seq_len):
    t = p.shape[0]
    ts = CONV_ROW_TILE
    prev, cur, nxt = _halo_specs(2 * B_CH, COL_GLU // (2 * B_CH), CONF_HALO, ts, seq_len, t)
    row = lambda: pl.BlockSpec((1, B_CH), lambda s, i: (0, 0))
    w_pad = jnp.pad(dw_w, ((0, 32 - B_CONV), (0, 0)))
    return pl.pallas_call(
        _conformer_kernel, grid=(nseq, seq_len // ts),
        in_specs=[prev, cur, nxt, pl.BlockSpec((32, B_CH), lambda s, i: (0, 0)), row(), row(), row()],
        out_specs=pl.BlockSpec((ts, B_CH), lambda s, i: (s * (seq_len // ts) + i, 0)),
        out_shape=jax.ShapeDtypeStruct((t, B_CH), BF16),
        scratch_shapes=[pltpu.VMEM((ts + 2 * CONF_HALO, B_CH), F32)],
        compiler_params=pltpu.CompilerParams(dimension_semantics=("parallel", "parallel")),
        name="conformer_branch",
    )(p, p, p, w_pad, dw_b.reshape(1, B_CH), ln_g.reshape(1, B_CH), ln_b.reshape(1, B_CH))


def _qkv_conv_kernel(prev_ref, cur_ref, next_ref, w_ref, q_ref, k_ref, v_ref, scr):
    ts = cur_ref.shape[0]
    _fill_halo_scratch(scr, prev_ref[...], cur_ref[...], next_ref[...], QKV_HALO, ts)
    for rb in range(ts // QKV_ROW_BLOCK):
        r0 = rb * QKV_ROW_BLOCK
        rows = slice(r0, r0 + QKV_ROW_BLOCK)
        y = _depthwise_taps(scr, w_ref, SHORT_CONV, QKV_HALO, r0, QKV_ROW_BLOCK)
        y = y * jax.nn.sigmoid(y)
        for h in range(A_HEADS):
            cs = slice(h * A_DK, (h + 1) * A_DK)
            qh = y[:, h * A_DK:(h + 1) * A_DK]
            kh = y[:, A_DIM + h * A_DK:A_DIM + (h + 1) * A_DK]
            q_ref[rows, cs] = qh * (lax.rsqrt(jnp.sum(qh * qh, axis=-1, keepdims=True) + EPS) * (A_DK ** -0.5))
            k_ref[rows, cs] = kh * lax.rsqrt(jnp.sum(kh * kh, axis=-1, keepdims=True) + EPS)
        v_ref[rows, :] = y[:, 2 * A_DIM:]


def qkv_conv(p, conv_w, nseq, seq_len):
    t = p.shape[0]
    ts = CONV_ROW_TILE
    prev, cur, nxt = _halo_specs(3 * A_DIM, 0, QKV_HALO, ts, seq_len, t)
    out = pl.BlockSpec((ts, A_DIM), lambda s, i: (s * (seq_len // ts) + i, 0))
    w_pad = jnp.pad(conv_w, ((0, 8 - SHORT_CONV), (0, 0)))
    return pl.pallas_call(
        _qkv_conv_kernel, grid=(nseq, seq_len // ts),
        in_specs=[prev, cur, nxt, pl.BlockSpec((8, 3 * A_DIM), lambda s, i: (0, 0))],
        out_specs=[out, out, out],
        out_shape=[jax.ShapeDtypeStruct((t, A_DIM), F32)] * 3,
        scratch_shapes=[pltpu.VMEM((ts + 2 * QKV_HALO, 3 * A_DIM), F32)],
        compiler_params=pltpu.CompilerParams(dimension_semantics=("parallel", "parallel")),
        name="qkv_conv",
    )(p, p, p, w_pad)


def _mix_out_kernel(of_ref, ob_ref, z_ref, conf_ref, ng_ref, w_ref, x_ref, gate_ref, o_ref):
    o = of_ref[...] + ob_ref[...]
    z = z_ref[...]
    parts = []
    for h in range(A_HEADS):
        cs = slice(h * A_DV, (h + 1) * A_DV)
        oh = o[:, cs]
        zh = z[:, cs]
        scale = lax.rsqrt(jnp.mean(oh * oh, axis=-1, keepdims=True) + EPS)
        parts.append((oh * scale * ng_ref[...] * (zh * jax.nn.sigmoid(zh))).astype(BF16))
    oa = jnp.concatenate(parts, axis=1)
    mix = _bdot(oa, w_ref[:A_DIM, :]) + _bdot(conf_ref[...], w_ref[A_DIM:, :])
    o_ref[...] = x_ref[...] + gate_ref[0] * mix


def mix_out(o_f, o_b, p, conf, norm_g, w_bf, x, gate, rows_per_seg):
    t = x.shape[0]
    tm = MM_ROW_TILE
    half = lambda cb: pl.BlockSpec((tm, A_DIM), lambda i: (i, cb))
    return pl.pallas_call(
        _mix_out_kernel, grid=(t // tm,),
        in_specs=[half(0), half(0), half(COL_Z // A_DIM), half(0), pl.BlockSpec((1, A_DV), lambda i: (0, 0)),
                  pl.BlockSpec((D_MODEL, D_MODEL), lambda i: (0, 0)), pl.BlockSpec((tm, D_MODEL), lambda i: (i, 0)),
                  _seg_spec(rows_per_seg, tm)],
        out_specs=pl.BlockSpec((tm, D_MODEL), lambda i: (i, 0)),
        out_shape=jax.ShapeDtypeStruct((t, D_MODEL), F32),
        compiler_params=pltpu.CompilerParams(dimension_semantics=("parallel",),
                                             vmem_limit_bytes=VMEM_LIMIT_BYTES),
        name="mix_out",
    )(o_f, o_b, p, conf, norm_g.reshape(1, A_DV), w_bf, x, gate)


def grid_pos_emb(n_tokens):
    rows = n_tokens // GRID_W
    r = jnp.repeat(jnp.arange(rows, dtype=F32), GRID_W)
    col = jnp.tile(jnp.arange(GRID_W, dtype=F32), rows)
    nf = D_MODEL // 4
    freqs = jnp.exp(-math.log(POS_BASE) * jnp.arange(nf, dtype=F32) / nf)
    ar = r[:, None] * freqs
    ac = col[:, None] * freqs
    return jnp.concatenate([jnp.sin(ar), jnp.cos(ar), jnp.sin(ac), jnp.cos(ac)], axis=-1)


def delta_conformer_layer(x, p, ab, g1, nseq, seq_len, s0, e, prm, rows_per_seg):
    t = p.shape[0]
    seq = lambda m: m.reshape(nseq, seq_len, m.shape[-1])
    q, k, v = qkv_conv(p, prm['conv_qkv_w'][e], nseq, seq_len)
    alpha = ab[:, :2 * A_HEADS]
    beta = jax.nn.sigmoid(ab[:, 2 * A_HEADS:4 * A_HEADS])
    log_g = (-jnp.exp(prm['a_log'][e]).reshape(1, 2 * A_HEADS)
             * jax.nn.softplus(alpha + prm['dt_bias'][e].reshape(1, 2 * A_HEADS)))
    o_f, o_b, st = delta_scan(*delta_prep(seq(q), seq(k), seq(v), seq(log_g), seq(beta)), s0)
    conf = conformer_branch(p, prm['conf_dw_w'][e], prm['conf_dw_b'][e], prm['conf_ln_g'][e],
                            prm['conf_ln_b'][e], nseq, seq_len)
    x = mix_out(o_f.reshape(t, A_DIM), o_b.reshape(t, A_DIM), p, conf, prm['delta_norm_g'][e],
                prm['w_out_bf'][e], x, g1, rows_per_seg)
    return x, st


def trunk(x, nseq, seq_len, cond, s0, prm):
    t = x.shape[0]
    rows_per_seg = t // cond.shape[0]
    states = []
    for l in range(DEPTH):
        mod = jax.nn.silu(cond) @ prm['ada_w'][l] + prm['ada_b'][l]
        sh1, sc1, g1, sh2, sc2, g2 = [m[:, None, :] for m in jnp.split(mod, 6, axis=-1)]
        e = l // 2
        if l % 2 == 0:
            p, ab = norm_mm(x, prm['norm1_g'][l], sc1, sh1, prm['w_in_bf'][e], rows_per_seg, F32, tail=True)
            x, st = delta_conformer_layer(x, p, ab, g1, nseq, seq_len, s0[:, e], e, prm, rows_per_seg)
            states.append(st)
        else:
            z = norm_mm(x, prm['norm1_g'][l], sc1, sh1, prm['w_fnet_bf'][e], rows_per_seg, BF16)
            tm = min(seq_len, 512)
            x = seq_mix_res(prm['dft_seq'][seq_len], z, x, g1, seq_len, tm, min(seq_len, 1024))
        ht, a_t, n_t, b_t, r_t = peer_route(x, prm['norm2_g'][l], sc2, sh2, prm['peer_wq_hi'],
                                            prm['peer_wq_lo'], prm['peer_k1'][l], prm['peer_k2'][l],
                                            rows_per_seg, l)
        x = peer_dense(ht, prm['peer_u_bf'], prm['peer_vt_bf'], a_t, n_t, b_t, r_t, x, g2, rows_per_seg, l)
    xf = x * lax.rsqrt(jnp.mean(x * x, axis=-1, keepdims=True) + EPS) * prm['final_norm_g']
    return xf, jnp.stack(states, axis=1)


def kernel(x_prompt, x_sample, state_delta, c, c_ctx, ada_w, ada_b, norm1_g, norm2_g, w_in_ab, conv_qkv_w,
           a_log, dt_bias, delta_norm_g, conf_dw_w, conf_dw_b, conf_ln_g, conf_ln_b, w_out_ab, w_out_c,
           peer_wq, peer_k1, peer_k2, peer_u, peer_v, final_norm_g):
    bp, sp, _ = x_prompt.shape
    bs, ss, _ = x_sample.shape
    bdc, bds = dft_group_matrices(D_MODEL // C_GROUPS, C_GROUPS)
    w_fnet = [jnp.concatenate([mm3(bdc, w_out_c[e], BF16), mm3(bds, w_out_c[e], BF16)], axis=1)
              for e in range(DEPTH // 2)]
    o4 = 4 * A_DIM
    w_in = jnp.concatenate([w_in_ab[:, :, :o4], w_in_ab[:, :, o4 + 4 * A_HEADS:], w_in_ab[:, :, o4:o4 + 4 * A_HEADS]],
                           axis=-1).astype(BF16)
    wq_hi, wq_lo = split_bf16(peer_wq)
    prm = {'ada_w': ada_w, 'ada_b': ada_b, 'norm1_g': norm1_g, 'norm2_g': norm2_g,
           'w_in_bf': jnp.pad(w_in, ((0, 0), (0, 0), (0, P_AB_PAD - P_AB))),
           'conv_qkv_w': conv_qkv_w, 'a_log': a_log, 'dt_bias': dt_bias,
           'delta_norm_g': delta_norm_g, 'conf_dw_w': conf_dw_w, 'conf_dw_b': conf_dw_b,
           'conf_ln_g': conf_ln_g, 'conf_ln_b': conf_ln_b, 'w_out_bf': w_out_ab.astype(BF16),
           'w_fnet_bf': w_fnet, 'dft_seq': {s: dft_seq_matrix(s) for s in {sp, ss}},
           'peer_wq_hi': wq_hi, 'peer_wq_lo': wq_lo, 'peer_k1': peer_k1, 'peer_k2': peer_k2,
           'peer_u_bf': peer_u.astype(BF16),
           'peer_vt_bf': jnp.transpose(peer_v.astype(BF16).reshape(DEPTH, PEER_N // PEER_EXPERT_TILE,
                                                                   PEER_EXPERT_TILE, D_MODEL), (0, 1, 3, 2)),
           'final_norm_g': final_norm_g}
    ne = (DEPTH + 1) // 2
    s0_ctx = jnp.zeros((bp, ne, 2, A_HEADS, A_DK, A_DV), F32)
    y_prompt, ctx_states = trunk(x_prompt.reshape(bp * sp, D_MODEL), bp, sp, c_ctx[None, :], s0_ctx, prm)
    xs = (x_sample + grid_pos_emb(ss)[None]).reshape(bs * ss, D_MODEL)
    y_sample, _ = trunk(xs, bs, ss, c, state_delta, prm)
    return (y_prompt.reshape(bp, sp, D_MODEL), y_sample.reshape(bs, ss, D_MODEL), ctx_states)
```

```python
import math

import jax
import jax.numpy as jnp
import numpy as np
from jax import lax
from jax.experimental import pallas as pl
from jax.experimental.pallas import tpu as pltpu

D_MODEL = 1024
DEPTH = 4
GRID_W = 64
POS_BASE = 10000.0
EPS = 1e-6
A_HEADS = 4
A_DK = 128
A_DV = 128
A_DIM = A_HEADS * A_DV
CHUNK = 64
B_CH = D_MODEL // 2
P_AB = 4 * A_DIM + 4 * A_HEADS + 2 * B_CH
C_GROUPS = 8
PEER_HEADS = 8
PEER_KEYS = 128
PEER_N = PEER_KEYS * PEER_KEYS
PEER_DK = 128
PEER_TOPK = 16

F32 = jnp.float32
BF16 = jnp.bfloat16
NEG_INF = float("-inf")

LANES = 128
VMEM_LIMIT_BYTES = 56 * 1024 * 1024
MM_ROW_TILE = 512
ROUTE_TOKEN_TILE = 512
PEER_TOKEN_TILE = 512
PEER_KEYS_PER_STEP = 8
PEER_EXPERT_TILE = PEER_KEYS_PER_STEP * PEER_KEYS
PREP_CHUNKS = 2
SCAN_SEQS = 2
P_AB_PAD = 3200
COL_Z = 3 * A_DIM
COL_GLU = 4 * A_DIM
COL_AB = 4 * A_DIM + 2 * B_CH
CONV_ROW_TILE = 256
SHORT_CONV = 7
B_CONV = 31
CONF_HALO = 16
QKV_HALO = 8
PEER_PIECES = 2
PIECE_KEYS = PEER_KEYS_PER_STEP // PEER_PIECES
PIECE_ROWS = PIECE_KEYS * PEER_KEYS
BF16_ROWS = 16

NT_DIMS = (((1,), (1,)), ((), ()))
TN_DIMS = (((0,), (0,)), ((), ()))


def _bdot(a, b):
    return jnp.dot(a, b, preferred_element_type=F32)


def _split_bf16(a):
    hi = a.astype(BF16)
    lo = (a - hi.astype(F32)).astype(BF16)
    return hi, lo


def _dot3(a, b):
    ah, al = _split_bf16(a)
    bh, bl = _split_bf16(b)
    return _bdot(ah, bh) + (_bdot(ah, bl) + _bdot(al, bh))


def _dot_exact_lhs(a01, b):
    a = a01.astype(BF16)
    bh = b.astype(BF16)
    r1 = b - bh.astype(F32)
    bm = r1.astype(BF16)
    bl = (r1 - bm.astype(F32)).astype(BF16)
    return _bdot(a, bh) + (_bdot(a, bm) + _bdot(a, bl))


def _gelu_tanh(x):
    return 0.5 * x * (1.0 + jnp.tanh(math.sqrt(2.0 / math.pi) * (x + 0.044715 * (x * x * x))))


def _seg_spec(rows_per_seg, tile):
    per = rows_per_seg // tile
    return pl.BlockSpec((1, 1, D_MODEL), lambda i, *_: (i // per, 0, 0))


def _norm_modulate(x, g, sc, sh):
    hn = x * lax.rsqrt(jnp.mean(x * x, axis=-1, keepdims=True) + EPS) * g
    return hn * (1.0 + sc) + sh


def _norm_mm_kernel(x_ref, g_ref, sc_ref, sh_ref, w_ref, o_ref, *tail_ref):
    h = _norm_modulate(x_ref[...], g_ref[...], sc_ref[0], sh_ref[0]).astype(BF16)
    res = _bdot(h, w_ref[...])
    o_ref[...] = res.astype(o_ref.dtype)
    if tail_ref:
        tail_ref[0][...] = res[:, res.shape[1] - LANES:]


def norm_mm(x, norm_g, sc, sh, w_bf, rows_per_seg, out_dtype, tail=False):
    m = x.shape[0]
    n = w_bf.shape[1]
    tm = MM_ROW_TILE
    seg = _seg_spec(rows_per_seg, tm)
    out_specs = [pl.BlockSpec((tm, n), lambda i: (i, 0))]
    out_shape = [jax.ShapeDtypeStruct((m, n), out_dtype)]
    if tail:
        out_specs.append(pl.BlockSpec((tm, LANES), lambda i: (i, 0)))
        out_shape.append(jax.ShapeDtypeStruct((m, LANES), F32))
    outs = pl.pallas_call(
        _norm_mm_kernel, grid=(m // tm,),
        in_specs=[pl.BlockSpec((tm, D_MODEL), lambda i: (i, 0)), pl.BlockSpec((1, D_MODEL), lambda i: (0, 0)),
                  seg, seg, pl.BlockSpec((D_MODEL, n), lambda i: (0, 0))],
        out_specs=out_specs, out_shape=out_shape,
        compiler_params=pltpu.CompilerParams(dimension_semantics=("parallel",),
                                             vmem_limit_bytes=VMEM_LIMIT_BYTES),
        name="norm_mm",
    )(x, norm_g.reshape(1, D_MODEL), sc, sh, w_bf)
    return outs if tail else outs[0]


def _split_kernel(a_ref, hi_ref, lo_ref):
    hi, lo = _split_bf16(a_ref[0])
    hi_ref[0] = hi
    lo_ref[0] = lo


def split_bf16(a):
    nl, nr, nc = a.shape
    blk = pl.BlockSpec((1, MM_ROW_TILE, nc), lambda l, i: (l, i, 0))
    return pl.pallas_call(
        _split_kernel, grid=(nl, nr // MM_ROW_TILE), in_specs=[blk], out_specs=[blk, blk],
        out_shape=[jax.ShapeDtypeStruct(a.shape, BF16)] * 2,
        compiler_params=pltpu.CompilerParams(dimension_semantics=("parallel", "parallel")),
        name="split_bf16",
    )(a)


def _mm3_kernel(a_ref, b_ref, o_ref):
    o_ref[...] = _dot3(a_ref[...], b_ref[...]).astype(o_ref.dtype)


def mm3(a, b, out_dtype):
    return pl.pallas_call(_mm3_kernel, out_shape=jax.ShapeDtypeStruct((a.shape[0], b.shape[1]), out_dtype),
                          compiler_params=pltpu.CompilerParams(vmem_limit_bytes=VMEM_LIMIT_BYTES),
                          name="mm3")(a, b)


def _seqmix_kernel(f_ref, z_ref, x_ref, gate_ref, o_ref, acc_ref):
    k = pl.program_id(2)

    @pl.when(k == 0)
    def _():
        acc_ref[...] = jnp.zeros_like(acc_ref)

    acc_ref[...] += _bdot(f_ref[...], z_ref[...])

    @pl.when(k == pl.num_programs(2) - 1)
    def _():
        o_ref[...] = x_ref[...] + gate_ref[0] * acc_ref[...]


def seq_mix_res(fmat, z, x, gate, seq_len, tm, tk):
    t = x.shape[0]
    nseq = t // seq_len
    seqs_per_seg = nseq // gate.shape[0]
    mt = seq_len // tm
    kt_half = seq_len // tk
    return pl.pallas_call(
        _seqmix_kernel, grid=(nseq, mt, 2 * kt_half),
        in_specs=[pl.BlockSpec((tm, tk), lambda s, i, k: (i, k)),
                  pl.BlockSpec((tk, D_MODEL), lambda s, i, k: (s * kt_half + k % kt_half, k // kt_half)),
                  pl.BlockSpec((tm, D_MODEL), lambda s, i, k: (s * mt + i, 0)),
                  pl.BlockSpec((1, 1, D_MODEL), lambda s, i, k: (s // seqs_per_seg, 0, 0))],
        out_specs=pl.BlockSpec((tm, D_MODEL), lambda s, i, k: (s * mt + i, 0)),
        out_shape=jax.ShapeDtypeStruct((t, D_MODEL), F32),
        scratch_shapes=[pltpu.VMEM((tm, D_MODEL), F32)],
        compiler_params=pltpu.CompilerParams(dimension_semantics=("parallel", "parallel", "arbitrary"),
                                             vmem_limit_bytes=VMEM_LIMIT_BYTES),
        name="seq_mix_res",
    )(fmat, z, x, gate)


def _dft_tables(n, cols):
    r = jnp.arange(n, dtype=jnp.int32)[:, None]
    ang = ((r * cols[None, :]) % n).astype(F32) * (2.0 * math.pi / n)
    return jnp.cos(ang), jnp.sin(ang)


def dft_seq_matrix(s):
    w = 1 << (int(math.log2(s)) // 2)
    ch, sh_ = _dft_tables(s, jnp.arange(s // w, dtype=jnp.int32) * w)
    cl, sl = _dft_tables(s, jnp.arange(w, dtype=jnp.int32))
    sc = 1.0 / math.sqrt(s)
    c = (ch[:, :, None] * cl[:, None, :] - sh_[:, :, None] * sl[:, None, :]).reshape(s, s) * sc
    sn = (sh_[:, :, None] * cl[:, None, :] + ch[:, :, None] * sl[:, None, :]).reshape(s, s) * sc
    return jnp.concatenate([c, -sn], axis=1).astype(BF16)


def dft_group_matrices(n, groups):
    c, s = _dft_tables(n, jnp.arange(n, dtype=jnp.int32))
    sc = 1.0 / math.sqrt(n)
    eye = jnp.eye(groups, dtype=F32)
    return jnp.kron(eye, c * sc), jnp.kron(eye, s * sc)


def _delta_prep_kernel(q_ref, k_ref, v_ref, lg_ref, bt_ref, w_ref, u_ref, qd_ref, kd_ref, p_ref, g_ref):
    r = lax.broadcasted_iota(jnp.int32, (CHUNK, CHUNK), 0)
    c = lax.broadcasted_iota(jnp.int32, (CHUNK, CHUNK), 1)
    eye = (r == c).astype(F32)
    ones = jnp.ones((CHUNK, CHUNK), F32)
    incl = (r >= c, r <= c)
    strict = (r > c, r < c)
    tri = (incl[0].astype(F32), incl[1].astype(F32))
    tri_t = (tri[1], tri[0])
    last = (CHUNK - 1, 0)
    chains = [(cg, d, h) for cg in range(PREP_CHUNKS) for d in range(2) for h in range(A_HEADS)]
    rows = lambda cg: slice(cg * CHUNK, (cg + 1) * CHUNK)
    cols = lambda h: slice(h * A_DK, (h + 1) * A_DK)
    kk = {}
    qk = {}
    for cg in range(PREP_CHUNKS):
        for h in range(A_HEADS):
            kb = k_ref[0, rows(cg), cols(h)].astype(BF16)
            kk[cg, h] = lax.dot_general(kb, kb, NT_DIMS, preferred_element_type=F32)
            qk[cg, h] = lax.dot_general(q_ref[0, rows(cg), cols(h)].astype(BF16), kb, NT_DIMS,
                                        preferred_element_type=F32)
    lgw = {}
    btw = {}
    for (cg, d, h) in chains:
        col = d * A_HEADS + h
        lgw[cg, d, h] = jnp.broadcast_to(lg_ref[0, rows(cg), col:col + 1], (CHUNK, A_DK))
        btw[cg, d, h] = jnp.broadcast_to(bt_ref[0, rows(cg), col:col + 1], (CHUNK, A_DK))
    gam = {ch: _dot_exact_lhs(tri[ch[1]], lgw[ch]) for ch in chains}
    gam_row = {ch: _dot_exact_lhs(ones, lgw[ch][:, :CHUNK] * tri_t[ch[1]]) for ch in chains}
    decay = {}
    lmat = {}
    for ch in chains:
        cg, d, h = ch
        diff = gam[ch][:, :CHUNK] - gam_row[ch]
        decay[ch] = jnp.where(incl[d], jnp.exp(jnp.where(incl[d], diff, 0.0)), 0.0)
        lmat[ch] = jnp.where(strict[d], btw[ch][:, :CHUNK] * decay[ch] * kk[cg, h], 0.0)
    pinv = {ch: eye - jnp.where((r // 2 == c // 2) & (r != c), lmat[ch], 0.0) for ch in chains}
    s = 2
    while s < CHUNK:
        join = (r // (2 * s) == c // (2 * s)) & (r // s != c // s)
        tc = {ch: _dot3(pinv[ch], jnp.where(join, lmat[ch], 0.0)) for ch in chains}
        pinv = {ch: pinv[ch] - _dot3(tc[ch], pinv[ch]) for ch in chains}
        s *= 2
    for ch in chains:
        cg, d, h = ch
        kh = k_ref[0, rows(cg), cols(h)]
        vh = v_ref[0, rows(cg), cols(h)]
        qh = q_ref[0, rows(cg), cols(h)]
        egam = jnp.exp(gam[ch])
        rhs = jnp.concatenate([kh * (btw[ch] * egam), vh * btw[ch]], axis=1)
        sol = _dot3(pinv[ch], rhs)
        w_ref[0, d, rows(cg), cols(h)] = sol[:, :A_DK].astype(BF16)
        u_ref[0, d, rows(cg), cols(h)] = sol[:, A_DK:]
        qd_ref[0, d, rows(cg), cols(h)] = (qh * egam).astype(BF16)
        glast = jnp.broadcast_to(gam[ch][last[d]:last[d] + 1, :], (CHUNK, A_DK))
        kd_ref[0, d, rows(cg), cols(h)] = (kh * jnp.exp(glast - gam[ch])).astype(BF16)
        g_ref[0, d, rows(cg), cols(h)] = jnp.exp(glast)
        p_ref[0, d, rows(cg), h * CHUNK:(h + 1) * CHUNK] = (decay[ch] * qk[cg, h]).astype(BF16)


def delta_prep(q, k, v, lg, bt):
    b, s, _ = q.shape
    rt = PREP_CHUNKS * CHUNK
    blk = lambda w: pl.BlockSpec((1, rt, w), lambda bi, ni: (bi, ni, 0))
    oblk = lambda w: pl.BlockSpec((1, 2, rt, w), lambda bi, ni: (bi, 0, ni, 0))
    sh = lambda w, dt: jax.ShapeDtypeStruct((b, 2, s, w), dt)
    return pl.pallas_call(
        _delta_prep_kernel, grid=(b, s // rt),
        in_specs=[blk(A_DIM), blk(A_DIM), blk(A_DIM), blk(2 * A_HEADS), blk(2 * A_HEADS)],
        out_specs=[oblk(A_DIM), oblk(A_DIM), oblk(A_DIM), oblk(A_DIM), oblk(A_HEADS * CHUNK), oblk(A_DIM)],
        out_shape=[sh(A_DIM, BF16), sh(A_DIM, F32), sh(A_DIM, BF16), sh(A_DIM, BF16),
                   sh(A_HEADS * CHUNK, BF16), sh(A_DIM, F32)],
        compiler_params=pltpu.CompilerParams(dimension_semantics=("parallel", "parallel")),
        name="delta_prep",
    )(q, k, v, lg, bt)


def _delta_scan_kernel(*refs):
    ins = refs[:12]
    s0_ref = refs[12]
    of_ref, ob_ref, sout_ref, state = refs[13:]
    n = pl.program_id(1)

    @pl.when(n == 0)
    def _():
        state[...] = s0_ref[...]

    outs = (of_ref, ob_ref)
    chains = [(g, d, h) for g in range(SCAN_SEQS) for d in range(2) for h in range(A_HEADS)]
    cs = lambda h: slice(h * A_DK, (h + 1) * A_DK)
    ref = lambda d, i: ins[d * 6 + i]
    s_old = {ch: state[ch] for ch in chains}
    wqs = {}
    for (g, d, h) in chains:
        wq = jnp.concatenate([ref(d, 0)[g, 0, :, cs(h)], ref(d, 2)[g, 0, :, cs(h)]], axis=0)
        wqs[g, d, h] = _bdot(wq, s_old[g, d, h].astype(BF16))
    unb = {}
    for (g, d, h) in chains:
        unb[g, d, h] = (ref(d, 1)[g, 0, :, cs(h)] - wqs[g, d, h][:CHUNK]).astype(BF16)
    for (g, d, h) in chains:
        o = wqs[g, d, h][CHUNK:] + _bdot(ref(d, 4)[g, 0, :, h * CHUNK:(h + 1) * CHUNK], unb[g, d, h])
        outs[d][g, :, cs(h)] = o
    for (g, d, h) in chains:
        upd = lax.dot_general(ref(d, 3)[g, 0, :, cs(h)], unb[g, d, h], TN_DIMS, preferred_element_type=F32)
        gs = jnp.broadcast_to(ref(d, 5)[g, 0, 0:1, cs(h)], (A_DK, A_DV))
        state[g, d, h] = gs * s_old[g, d, h] + upd

    @pl.when(n == pl.num_programs(1) - 1)
    def _():
        sout_ref[...] = state[...]


def delta_scan(w, u, qd, kd, p, gl, s0):
    b, _, s, _ = u.shape
    n = s // CHUNK

    def spec(wd, d):
        if d == 0:
            return pl.BlockSpec((SCAN_SEQS, 1, CHUNK, wd), lambda bi, ni: (bi, 0, ni, 0))
        return pl.BlockSpec((SCAN_SEQS, 1, CHUNK, wd), lambda bi, ni: (bi, 1, n - 1 - ni, 0))

    arrs = (w, u, qd, kd, p, gl)
    in_specs = [spec(a.shape[-1], d) for d in range(2) for a in arrs]
    st = pl.BlockSpec((SCAN_SEQS, 2, A_HEADS, A_DK, A_DV), lambda bi, ni: (bi, 0, 0, 0, 0))
    of = pl.BlockSpec((SCAN_SEQS, CHUNK, A_DIM), lambda bi, ni: (bi, ni, 0))
    ob = pl.BlockSpec((SCAN_SEQS, CHUNK, A_DIM), lambda bi, ni: (bi, n - 1 - ni, 0))
    return pl.pallas_call(
        _delta_scan_kernel, grid=(b // SCAN_SEQS, n),
        in_specs=in_specs + [st],
        out_specs=[of, ob, st],
        out_shape=[jax.ShapeDtypeStruct((b, s, A_DIM), F32), jax.ShapeDtypeStruct((b, s, A_DIM), F32),
                   jax.ShapeDtypeStruct((b, 2, A_HEADS, A_DK, A_DV), F32)],
        scratch_shapes=[pltpu.VMEM((SCAN_SEQS, 2, A_HEADS, A_DK, A_DV), F32)],
        compiler_params=pltpu.CompilerParams(dimension_semantics=("parallel", "arbitrary")),
        name="delta_scan",
    )(*(arrs + arrs), s0)


def _cand_tables():
    pairs = [(r, c) for r in range(PEER_TOPK) for c in range(PEER_TOPK) if (r + 1) * (c + 1) <= PEER_TOPK]
    npad = 64
    e1 = np.zeros((npad, PEER_TOPK), np.float32)
    e2 = np.zeros((npad, PEER_TOPK), np.float32)
    m = np.zeros((PEER_TOPK, npad), np.float32)
    for k, (r, c) in enumerate(pairs):
        e1[k, r] = 1
        e2[k, c] = 1
        m[r, k] = 1
    return len(pairs), e1, e2, m


N_CAND, _CAND_E1, _CAND_E2, _CAND_ROW = _cand_tables()


def _extract_topk(s, n_iter, want_rank=True):
    k, t = s.shape
    work = s.reshape(k // 8, 8, t)
    rank = jnp.full(work.shape, float(n_iter), F32) if want_rank else None
    vals = []
    for r in range(n_iter):
        m = jnp.max(jnp.max(work, axis=0), axis=0, keepdims=True)
        hit = work == jnp.broadcast_to(m, (8, t))[None]
        if want_rank:
            rank = jnp.where(hit, float(r), rank)
        work = jnp.where(hit, NEG_INF, work)
        vals.append(m)
    return vals, (rank.reshape(k, t) if want_rank else None)


def _route_kernel(x_ref, g_ref, sc_ref, sh_ref, wqh_ref, wql_ref, k1_ref, k2_ref, e1_ref, e2_ref, mrow_ref,
                  ht_ref, a_ref, n_ref, b_ref, r_ref, q_scr):
    hmod = _norm_modulate(x_ref[...], g_ref[...], sc_ref[0], sh_ref[0])
    ht_ref[...] = hmod.T.astype(BF16)
    hh, hl = _split_bf16(hmod)
    q_scr[...] = _bdot(hh, wqh_ref[...]) + (_bdot(hh, wql_ref[...]) + _bdot(hl, wqh_ref[...]))
    tt = x_ref.shape[0]

    def head(h, carry):
        c1 = pl.ds(pl.multiple_of(h * 2 * PEER_DK, PEER_DK), PEER_DK)
        c2 = pl.ds(pl.multiple_of(h * 2 * PEER_DK + PEER_DK, PEER_DK), PEER_DK)
        hp = lax.Precision.HIGHEST
        s1 = lax.dot_general(k1_ref[h], q_scr[:, c1], NT_DIMS, precision=hp, preferred_element_type=F32)
        s2 = lax.dot_general(k2_ref[h], q_scr[:, c2], NT_DIMS, precision=hp, preferred_element_type=F32)
        v1, _ = _extract_topk(s1, PEER_TOPK, want_rank=False)
        v2, rank2 = _extract_topk(s2, PEER_TOPK)
        v1m = jnp.concatenate(v1, axis=0)
        v2m = jnp.concatenate(v2, axis=0)
        cand = (jnp.dot(e1_ref[...], v1m, precision=hp, preferred_element_type=F32)
                + jnp.dot(e2_ref[...], v2m, precision=hp, preferred_element_type=F32))
        row = lax.broadcasted_iota(jnp.int32, cand.shape, 0)
        cand = jnp.where(row < N_CAND, cand, NEG_INF)
        _, crank = _extract_topk(cand, PEER_TOPK)
        sel = crank < float(PEER_TOPK)
        cmax = v1[0] + v2[0]
        z = jnp.sum(jnp.where(sel, jnp.exp(cand - cmax), 0.0), axis=0, keepdims=True)
        n_r = _bdot(mrow_ref[...], jnp.where(sel, 1.0, 0.0).astype(BF16))
        s1_3 = s1.reshape(PEER_KEYS // 8, 8, tt)
        nn = jnp.zeros_like(s1_3)
        for r in range(PEER_TOPK):
            nn = jnp.where(s1_3 == jnp.broadcast_to(v1[r], (8, tt))[None],
                           jnp.broadcast_to(n_r[r:r + 1, :], (8, tt))[None], nn)
        a_ref[h] = jnp.exp(s1 - v1[0]) / z
        n_ref[h] = nn.reshape(PEER_KEYS, tt)
        b_ref[h] = jnp.where(rank2 < float(PEER_TOPK), jnp.exp(s2 - v2[0]), 0.0).astype(BF16)
        r_ref[h] = rank2.astype(BF16)
        return carry

    lax.fori_loop(0, PEER_HEADS, head, 0, unroll=2)


def peer_route(x, norm_g, sc, sh, wq_hi, wq_lo, k1, k2, rows_per_seg, layer):
    wq_spec = pl.BlockSpec((None, D_MODEL, 2 * PEER_HEADS * PEER_DK), lambda i: (layer, 0, 0))
    t = x.shape[0]
    tt = ROUTE_TOKEN_TILE
    gate = pl.BlockSpec((PEER_HEADS, PEER_KEYS, tt), lambda i: (0, 0, i))
    full = lambda shp: pl.BlockSpec(shp, lambda i: (0,) * len(shp))
    seg = _seg_spec(rows_per_seg, tt)
    return pl.pallas_call(
        _route_kernel, grid=(t // tt,),
        in_specs=[pl.BlockSpec((tt, D_MODEL), lambda i: (i, 0)), full((1, D_MODEL)), seg, seg,
                  wq_spec, wq_spec,
                  full((PEER_HEADS, PEER_KEYS, PEER_DK)),
                  full((PEER_HEADS, PEER_KEYS, PEER_DK)), full((64, PEER_TOPK)), full((64, PEER_TOPK)),
                  full((PEER_TOPK, 64))],
        out_specs=[pl.BlockSpec((D_MODEL, tt), lambda i: (0, i)), gate, gate, gate, gate],
        out_shape=[jax.ShapeDtypeStruct((D_MODEL, t), BF16)]
        + [jax.ShapeDtypeStruct((PEER_HEADS, PEER_KEYS, t), dt) for dt in (F32, F32, BF16, BF16)],
        scratch_shapes=[pltpu.VMEM((tt, 2 * PEER_HEADS * PEER_DK), F32)],
        compiler_params=pltpu.CompilerParams(dimension_semantics=("parallel",),
                                             vmem_limit_bytes=VMEM_LIMIT_BYTES),
        name="peer_route",
    )(x, norm_g.reshape(1, D_MODEL), sc, sh, wq_hi, wq_lo, k1, k2,
      jnp.asarray(_CAND_E1), jnp.asarray(_CAND_E2), jnp.asarray(_CAND_ROW, BF16))


def _peer_dense_kernel(ht_ref, u_ref, vt_ref, a_ref, n_ref, b_ref, r_ref, x_ref, g2_ref, o_ref,
                       acc_ref, hid_ref, w_ref, bp_ref, rp_ref):
    e = pl.program_id(1)

    @pl.when(e == 0)
    def _():
        acc_ref[...] = jnp.zeros_like(acc_ref)
        for c in range(PEER_TOKEN_TILE // LANES):
            ls = slice(c * LANES, (c + 1) * LANES)
            bp_ref[:, c] = b_ref[:, :, ls].astype(BF16)
            rp_ref[:, c] = r_ref[:, :, ls].astype(BF16)

    def hidden(pp):
        rows = slice(pp * PIECE_ROWS, (pp + 1) * PIECE_ROWS)
        hid_ref[pp % 2] = _bdot(u_ref[rows, :], ht_ref[...])

    def gates(pp):
        for c in range(PEER_TOKEN_TILE // LANES):
            ls = slice(c * LANES, (c + 1) * LANES)
            accs = [jnp.zeros((PEER_KEYS, LANES), BF16) for _ in range(PIECE_KEYS)]
            for h in range(PEER_HEADS):
                b = bp_ref[h, c]
                r = rp_ref[h, c]
                for ii in range(PIECE_KEYS):
                    k = pp * PIECE_KEYS + ii
                    a16 = jnp.broadcast_to(a_ref[h, k:k + 1, ls], (BF16_ROWS, LANES)).astype(BF16)
                    n16 = jnp.broadcast_to(n_ref[h, k:k + 1, ls], (BF16_ROWS, LANES)).astype(BF16)
                    a128 = jnp.concatenate([a16] * (PEER_KEYS // BF16_ROWS), axis=0)
                    n128 = jnp.concatenate([n16] * (PEER_KEYS // BF16_ROWS), axis=0)
                    accs[ii] = accs[ii] + jnp.where(r < n128, b, jnp.zeros_like(b)) * a128
            for ii in range(PIECE_KEYS):
                rows = slice(ii * PEER_KEYS, (ii + 1) * PEER_KEYS)
                act = _gelu_tanh(hid_ref[pp % 2, rows, ls].astype(BF16))
                w_ref[pp % 2, rows, ls] = act * accs[ii]

    def project(pp):
        rows = slice(pp * PIECE_ROWS, (pp + 1) * PIECE_ROWS)
        acc_ref[...] += _bdot(vt_ref[0, :, rows], w_ref[pp % 2])

    hidden(0)
    for pp in range(PEER_PIECES):
        if pp + 1 < PEER_PIECES:
            hidden(pp + 1)
        gates(pp)
        project(pp)

    @pl.when(e == pl.num_programs(1) - 1)
    def _():
        o_ref[...] = x_ref[...] + g2_ref[0] * acc_ref[...].T


def peer_dense(ht_bf, u_bf, vt_bf, a_t, n_t, b_t, r_t, x, g2, rows_per_seg, layer):
    t = x.shape[0]
    tt = PEER_TOKEN_TILE
    grid = (t // tt, PEER_N // PEER_EXPERT_TILE)
    gate_spec = pl.BlockSpec((PEER_HEADS, PEER_KEYS, tt), lambda ti, ei: (0, 0, ti))
    step_keys = pl.BlockSpec((PEER_HEADS, PEER_KEYS_PER_STEP, tt), lambda ti, ei: (0, ei, ti))
    return pl.pallas_call(
        _peer_dense_kernel, grid=grid,
        in_specs=[pl.BlockSpec((D_MODEL, tt), lambda ti, ei: (0, ti)),
                  pl.BlockSpec((None, PEER_EXPERT_TILE, D_MODEL), lambda ti, ei: (layer, ei, 0)),
                  pl.BlockSpec((None, 1, D_MODEL, PEER_EXPERT_TILE), lambda ti, ei: (layer, ei, 0, 0)),
                  step_keys, step_keys, gate_spec, gate_spec,
                  pl.BlockSpec((tt, D_MODEL), lambda ti, ei: (ti, 0)),
                  _seg_spec(rows_per_seg, tt)],
        out_specs=pl.BlockSpec((tt, D_MODEL), lambda ti, ei: (ti, 0)),
        out_shape=jax.ShapeDtypeStruct((t, D_MODEL), F32),
        scratch_shapes=[pltpu.VMEM((D_MODEL, tt), F32),
                        pltpu.VMEM((2, PIECE_ROWS, tt), F32),
                        pltpu.VMEM((2, PIECE_ROWS, tt), BF16),
                        pltpu.VMEM((PEER_HEADS, tt // LANES, PEER_KEYS, LANES), BF16),
                        pltpu.VMEM((PEER_HEADS, tt // LANES, PEER_KEYS, LANES), BF16)],
        compiler_params=pltpu.CompilerParams(dimension_semantics=("parallel", "arbitrary"),
                                             vmem_limit_bytes=VMEM_LIMIT_BYTES),
        name="peer_dense",
    )(ht_bf, u_bf, vt_bf, a_t, n_t, b_t, r_t, x, g2)


def _fill_halo_scratch(scr, prev, cur, nxt, halo, ts):
    i = pl.program_id(1)
    scr[0:halo, :] = jnp.where(i > 0, prev, 0.0)
    scr[halo:halo + ts, :] = cur
    scr[halo + ts:halo + ts + halo, :] = jnp.where(i < pl.num_programs(1) - 1, nxt, 0.0)


def _depthwise_taps(scr, w_ref, taps, halo, r0, rows):
    off = halo - taps // 2
    acc = scr[off + r0:off + r0 + rows, :] * w_ref[0:1, :]
    for k in range(1, taps):
        acc = acc + scr[off + r0 + k:off + r0 + k + rows, :] * w_ref[k:k + 1, :]
    return acc


def _halo_specs(width, col_block, halo, ts, seq_len, total_rows):
    tiles = seq_len // ts
    per_tile = ts // halo
    last = total_rows // halo - 1
    prev = pl.BlockSpec((halo, width), lambda s, i: (jnp.maximum((s * tiles + i) * per_tile - 1, 0), col_block))
    cur = pl.BlockSpec((ts, width), lambda s, i: (s * tiles + i, col_block))
    nxt = pl.BlockSpec((halo, width), lambda s, i: (jnp.minimum((s * tiles + i + 1) * per_tile, last), col_block))
    return prev, cur, nxt


CONF_ROW_BLOCK = 32
QKV_ROW_BLOCK = 16


def _conformer_kernel(prev_ref, cur_ref, next_ref, w_ref, b_ref, lg_ref, lb_ref, o_ref, scr):
    ts = cur_ref.shape[0]
    glu = lambda blk: blk[:, :B_CH] * jax.nn.sigmoid(blk[:, B_CH:])
    _fill_halo_scratch(scr, glu(prev_ref[...]), glu(cur_ref[...]), glu(next_ref[...]), CONF_HALO, ts)
    for rb in range(ts // CONF_ROW_BLOCK):
        r0 = rb * CONF_ROW_BLOCK
        hh = _depthwise_taps(scr, w_ref, B_CONV, CONF_HALO, r0, CONF_ROW_BLOCK) + b_ref[...]
        mu = jnp.mean(hh, axis=-1, keepdims=True)
        var = jnp.mean(jnp.square(hh - mu), axis=-1, keepdims=True)
        y = (hh - mu) * lax.rsqrt(var + EPS) * lg_ref[...] + lb_ref[...]
        o_ref[r0:r0 + CONF_ROW_BLOCK, :] = (y * jax.nn.sigmoid(y)).astype(o_ref.dtype)


def conformer_branch(p, dw_w, dw_b, ln_g, ln_b, nseq, seq_len):
    t = p.shape[0]
    ts = CONV_ROW_TILE
    prev, cur, nxt = _halo_specs(2 * B_CH, COL_GLU // (2 * B_CH), CONF_HALO, ts, seq_len, t)
    row = lambda: pl.BlockSpec((1, B_CH), lambda s, i: (0, 0))
    w_pad = jnp.pad(dw_w, ((0, 32 - B_CONV), (0, 0)))
    return pl.pallas_call(
        _conformer_kernel, grid=(nseq, seq_len // ts),
        in_specs=[prev, cur, nxt, pl.BlockSpec((32, B_CH), lambda s, i: (0, 0)), row(), row(), row()],
        out_specs=pl.BlockSpec((ts, B_CH), lambda s, i: (s * (seq_len // ts) + i, 0)),
        out_shape=jax.ShapeDtypeStruct((t, B_CH), BF16),
        scratch_shapes=[pltpu.VMEM((ts + 2 * CONF_HALO, B_CH), F32)],
        compiler_params=pltpu.CompilerParams(dimension_semantics=("parallel", "parallel")),
        name="conformer_branch",
    )(p, p, p, w_pad, dw_b.reshape(1, B_CH), ln_g.reshape(1, B_CH), ln_b.reshape(1, B_CH))


def _qkv_conv_kernel(prev_ref, cur_ref, next_ref, w_ref, q_ref, k_ref, v_ref, scr):
    ts = cur_ref.shape[0]
    _fill_halo_scratch(scr, prev_ref[...], cur_ref[...], next_ref[...], QKV_HALO, ts)
    for rb in range(ts // QKV_ROW_BLOCK):
        r0 = rb * QKV_ROW_BLOCK
        rows = slice(r0, r0 + QKV_ROW_BLOCK)
        y = _depthwise_taps(scr, w_ref, SHORT_CONV, QKV_HALO, r0, QKV_ROW_BLOCK)
        y = y * jax.nn.sigmoid(y)
        for h in range(A_HEADS):
            cs = slice(h * A_DK, (h + 1) * A_DK)
            qh = y[:, h * A_DK:(h + 1) * A_DK]
            kh = y[:, A_DIM + h * A_DK:A_DIM + (h + 1) * A_DK]
            q_ref[rows, cs] = qh * (lax.rsqrt(jnp.sum(qh * qh, axis=-1, keepdims=True) + EPS) * (A_DK ** -0.5))
            k_ref[rows, cs] = kh * lax.rsqrt(jnp.sum(kh * kh, axis=-1, keepdims=True) + EPS)
        v_ref[rows, :] = y[:, 2 * A_DIM:]


def qkv_conv(p, conv_w, nseq, seq_len):
    t = p.shape[0]
    ts = CONV_ROW_TILE
    prev, cur, nxt = _halo_specs(3 * A_DIM, 0, QKV_HALO, ts, seq_len, t)
    out = pl.BlockSpec((ts, A_DIM), lambda s, i: (s * (seq_len // ts) + i, 0))
    w_pad = jnp.pad(conv_w, ((0, 8 - SHORT_CONV), (0, 0)))
    return pl.pallas_call(
        _qkv_conv_kernel, grid=(nseq, seq_len // ts),
        in_specs=[prev, cur, nxt, pl.BlockSpec((8, 3 * A_DIM), lambda s, i: (0, 0))],
        out_specs=[out, out, out],
        out_shape=[jax.ShapeDtypeStruct((t, A_DIM), F32)] * 3,
        scratch_shapes=[pltpu.VMEM((ts + 2 * QKV_HALO, 3 * A_DIM), F32)],
        compiler_params=pltpu.CompilerParams(dimension_semantics=("parallel", "parallel")),
        name="qkv_conv",
    )(p, p, p, w_pad)


def _mix_out_kernel(of_ref, ob_ref, z_ref, conf_ref, ng_ref, w_ref, x_ref, gate_ref, o_ref):
    o = of_ref[...] + ob_ref[...]
    z = z_ref[...]
    parts = []
    for h in range(A_HEADS):
        cs = slice(h * A_DV, (h + 1) * A_DV)
        oh = o[:, cs]
        zh = z[:, cs]
        scale = lax.rsqrt(jnp.mean(oh * oh, axis=-1, keepdims=True) + EPS)
        parts.append((oh * scale * ng_ref[...] * (zh * jax.nn.sigmoid(zh))).astype(BF16))
    oa = jnp.concatenate(parts, axis=1)
    mix = _bdot(oa, w_ref[:A_DIM, :]) + _bdot(conf_ref[...], w_ref[A_DIM:, :])
    o_ref[...] = x_ref[...] + gate_ref[0] * mix


def mix_out(o_f, o_b, p, conf, norm_g, w_bf, x, gate, rows_per_seg):
    t = x.shape[0]
    tm = MM_ROW_TILE
    half = lambda cb: pl.BlockSpec((tm, A_DIM), lambda i: (i, cb))
    return pl.pallas_call(
        _mix_out_kernel, grid=(t // tm,),
        in_specs=[half(0), half(0), half(COL_Z // A_DIM), half(0), pl.BlockSpec((1, A_DV), lambda i: (0, 0)),
                  pl.BlockSpec((D_MODEL, D_MODEL), lambda i: (0, 0)), pl.BlockSpec((tm, D_MODEL), lambda i: (i, 0)),
                  _seg_spec(rows_per_seg, tm)],
        out_specs=pl.BlockSpec((tm, D_MODEL), lambda i: (i, 0)),
        out_shape=jax.ShapeDtypeStruct((t, D_MODEL), F32),
        compiler_params=pltpu.CompilerParams(dimension_semantics=("parallel",),
                                             vmem_limit_bytes=VMEM_LIMIT_BYTES),
        name="mix_out",
    )(o_f, o_b, p, conf, norm_g.reshape(1, A_DV), w_bf, x, gate)


def grid_pos_emb(n_tokens):
    rows = n_tokens // GRID_W
    r = jnp.repeat(jnp.arange(rows, dtype=F32), GRID_W)
    col = jnp.tile(jnp.arange(GRID_W, dtype=F32), rows)
    nf = D_MODEL // 4
    freqs = jnp.exp(-math.log(POS_BASE) * jnp.arange(nf, dtype=F32) / nf)
    ar = r[:, None] * freqs
    ac = col[:, None] * freqs
    return jnp.concatenate([jnp.sin(ar), jnp.cos(ar), jnp.sin(ac), jnp.cos(ac)], axis=-1)


def delta_conformer_layer(x, p, ab, g1, nseq, seq_len, s0, e, prm, rows_per_seg):
    t = p.shape[0]
    seq = lambda m: m.reshape(nseq, seq_len, m.shape[-1])
    q, k, v = qkv_conv(p, prm['conv_qkv_w'][e], nseq, seq_len)
    alpha = ab[:, :2 * A_HEADS]
    beta = jax.nn.sigmoid(ab[:, 2 * A_HEADS:4 * A_HEADS])
    log_g = (-jnp.exp(prm['a_log'][e]).reshape(1, 2 * A_HEADS)
             * jax.nn.softplus(alpha + prm['dt_bias'][e].reshape(1, 2 * A_HEADS)))
    o_f, o_b, st = delta_scan(*delta_prep(seq(q), seq(k), seq(v), seq(log_g), seq(beta)), s0)
    conf = conformer_branch(p, prm['conf_dw_w'][e], prm['conf_dw_b'][e], prm['conf_ln_g'][e],
                            prm['conf_ln_b'][e], nseq, seq_len)
    x = mix_out(o_f.reshape(t, A_DIM), o_b.reshape(t, A_DIM), p, conf, prm['delta_norm_g'][e],
                prm['w_out_bf'][e], x, g1, rows_per_seg)
    return x, st


def trunk(x, nseq, seq_len, cond, s0, prm):
    t = x.shape[0]
    rows_per_seg = t // cond.shape[0]
    states = []
    for l in range(DEPTH):
        mod = jax.nn.silu(cond) @ prm['ada_w'][l] + prm['ada_b'][l]
        sh1, sc1, g1, sh2, sc2, g2 = [m[:, None, :] for m in jnp.split(mod, 6, axis=-1)]
        e = l // 2
        if l % 2 == 0:
            p, ab = norm_mm(x, prm['norm1_g'][l], sc1, sh1, prm['w_in_bf'][e], rows_per_seg, F32, tail=True)
            x, st = delta_conformer_layer(x, p, ab, g1, nseq, seq_len, s0[:, e], e, prm, rows_per_seg)
            states.append(st)
        else:
            z = norm_mm(x, prm['norm1_g'][l], sc1, sh1, prm['w_fnet_bf'][e], rows_per_seg, BF16)
            tm = min(seq_len, 512)
            x = seq_mix_res(prm['dft_seq'][seq_len], z, x, g1, seq_len, tm, min(seq_len, 1024))
        ht, a_t, n_t, b_t, r_t = peer_route(x, prm['norm2_g'][l], sc2, sh2, prm['peer_wq_hi'],
                                            prm['peer_wq_lo'], prm['peer_k1'][l], prm['peer_k2'][l],
                                            rows_per_seg, l)
        x = peer_dense(ht, prm['peer_u_bf'], prm['peer_vt_bf'], a_t, n_t, b_t, r_t, x, g2, rows_per_seg, l)
    xf = x * lax.rsqrt(jnp.mean(x * x, axis=-1, keepdims=True) + EPS) * prm['final_norm_g']
    return xf, jnp.stack(states, axis=1)


def kernel(x_prompt, x_sample, state_delta, c, c_ctx, ada_w, ada_b, norm1_g, norm2_g, w_in_ab, conv_qkv_w,
           a_log, dt_bias, delta_norm_g, conf_dw_w, conf_dw_b, conf_ln_g, conf_ln_b, w_out_ab, w_out_c,
           peer_wq, peer_k1, peer_k2, peer_u, peer_v, final_norm_g):
    bp, sp, _ = x_prompt.shape
    bs, ss, _ = x_sample.shape
    bdc, bds = dft_group_matrices(D_MODEL // C_GROUPS, C_GROUPS)
    w_fnet = [jnp.concatenate([mm3(bdc, w_out_c[e], BF16), mm3(bds, w_out_c[e], BF16)], axis=1)
              for e in range(DEPTH // 2)]
    o4 = 4 * A_DIM
    w_in = jnp.concatenate([w_in_ab[:, :, :o4], w_in_ab[:, :, o4 + 4 * A_HEADS:], w_in_ab[:, :, o4:o4 + 4 * A_HEADS]],
                           axis=-1).astype(BF16)
    wq_hi, wq_lo = split_bf16(peer_wq)
    prm = {'ada_w': ada_w, 'ada_b': ada_b, 'norm1_g': norm1_g, 'norm2_g': norm2_g,
           'w_in_bf': jnp.pad(w_in, ((0, 0), (0, 0), (0, P_AB_PAD - P_AB))),
           'conv_qkv_w': conv_qkv_w, 'a_log': a_log, 'dt_bias': dt_bias,
           'delta_norm_g': delta_norm_g, 'conf_dw_w': conf_dw_w, 'conf_dw_b': conf_dw_b,
           'conf_ln_g': conf_ln_g, 'conf_ln_b': conf_ln_b, 'w_out_bf': w_out_ab.astype(BF16),
           'w_fnet_bf': w_fnet, 'dft_seq': {s: dft_seq_matrix(s) for s in {sp, ss}},
           'peer_wq_hi': wq_hi, 'peer_wq_lo': wq_lo, 'peer_k1': peer_k1, 'peer_k2': peer_k2,
           'peer_u_bf': peer_u.astype(BF16),
           'peer_vt_bf': jnp.transpose(peer_v.astype(BF16).reshape(DEPTH, PEER_N // PEER_EXPERT_TILE,
                                                                   PEER_EXPERT_TILE, D_MODEL), (0, 1, 3, 2)),
           'final_norm_g': final_norm_g}
    ne = (DEPTH + 1) // 2
    s0_ctx = jnp.zeros((bp, ne, 2, A_HEADS, A_DK, A_DV), F32)
    y_prompt, ctx_states = trunk(x_prompt.reshape(bp * sp, D_MODEL), bp, sp, c_ctx[None, :], s0_ctx, prm)
    xs = (x_sample + grid_pos_emb(ss)[None]).reshape(bs * ss, D_MODEL)
    y_sample, _ = trunk(xs, bs, ss, c, state_delta, prm)
    return (y_prompt.reshape(bp, sp, D_MODEL), y_sample.reshape(bs, ss, D_MODEL), ctx_states)
```

```python
import math

import jax
import jax.numpy as jnp
import numpy as np
from jax import lax
from jax.experimental import pallas as pl
from jax.experimental.pallas import tpu as pltpu

D_MODEL = 1024
DEPTH = 4
GRID_W = 64
POS_BASE = 10000.0
EPS = 1e-6
A_HEADS = 4
A_DK = 128
A_DV = 128
A_DIM = A_HEADS * A_DV
CHUNK = 64
B_CH = D_MODEL // 2
P_AB = 4 * A_DIM + 4 * A_HEADS + 2 * B_CH
C_GROUPS = 8
PEER_HEADS = 8
PEER_KEYS = 128
PEER_N = PEER_KEYS * PEER_KEYS
PEER_DK = 128
PEER_TOPK = 16

F32 = jnp.float32
BF16 = jnp.bfloat16
NEG_INF = float("-inf")

LANES = 128
VMEM_LIMIT_BYTES = 56 * 1024 * 1024
MM_ROW_TILE = 512
ROUTE_TOKEN_TILE = 512
ROUTE_HEAD_UNROLL = 2
PEER_TOKEN_TILE = 512
PEER_KEYS_PER_STEP = 8
PEER_EXPERT_TILE = PEER_KEYS_PER_STEP * PEER_KEYS
PREP_CHUNKS = 2
SCAN_SEQS = 2
P_AB_PAD = 3200
COL_Z = 3 * A_DIM
COL_GLU = 4 * A_DIM
COL_AB = 4 * A_DIM + 2 * B_CH
CONV_ROW_TILE = 256
SHORT_CONV = 7
B_CONV = 31
CONF_HALO = 16
QKV_HALO = 8
PEER_PIECES = 1
PIECE_KEYS = PEER_KEYS_PER_STEP // PEER_PIECES
PIECE_ROWS = PIECE_KEYS * PEER_KEYS
BF16_ROWS = 16

NT_DIMS = (((1,), (1,)), ((), ()))
TN_DIMS = (((0,), (0,)), ((), ()))


def _bdot(a, b):
    return jnp.dot(a, b, preferred_element_type=F32)


def _split_bf16(a):
    hi = a.astype(BF16)
    lo = (a - hi.astype(F32)).astype(BF16)
    return hi, lo


def _dot3(a, b):
    ah, al = _split_bf16(a)
    bh, bl = _split_bf16(b)
    return _bdot(ah, bh) + (_bdot(ah, bl) + _bdot(al, bh))


def _dot_exact_lhs(a01, b):
    a = a01.astype(BF16)
    bh = b.astype(BF16)
    r1 = b - bh.astype(F32)
    bm = r1.astype(BF16)
    bl = (r1 - bm.astype(F32)).astype(BF16)
    return _bdot(a, bh) + (_bdot(a, bm) + _bdot(a, bl))


def _gelu_tanh(x):
    return 0.5 * x * (1.0 + jnp.tanh(math.sqrt(2.0 / math.pi) * (x + 0.044715 * (x * x * x))))


def _seg_spec(rows_per_seg, tile):
    per = rows_per_seg // tile
    return pl.BlockSpec((1, 1, D_MODEL), lambda i, *_: (i // per, 0, 0))


def _norm_modulate(x, g, sc, sh):
    hn = x * lax.rsqrt(jnp.mean(x * x, axis=-1, keepdims=True) + EPS) * g
    return hn * (1.0 + sc) + sh


def _norm_mm_kernel(x_ref, g_ref, sc_ref, sh_ref, w_ref, o_ref, *tail_ref):
    h = _norm_modulate(x_ref[...], g_ref[...], sc_ref[0], sh_ref[0]).astype(BF16)
    res = _bdot(h, w_ref[...])
    o_ref[...] = res.astype(o_ref.dtype)
    if tail_ref:
        tail_ref[0][...] = res[:, res.shape[1] - LANES:]


def norm_mm(x, norm_g, sc, sh, w_bf, rows_per_seg, out_dtype, tail=False):
    m = x.shape[0]
    n = w_bf.shape[1]
    tm = MM_ROW_TILE
    seg = _seg_spec(rows_per_seg, tm)
    out_specs = [pl.BlockSpec((tm, n), lambda i: (i, 0))]
    out_shape = [jax.ShapeDtypeStruct((m, n), out_dtype)]
    if tail:
        out_specs.append(pl.BlockSpec((tm, LANES), lambda i: (i, 0)))
        out_shape.append(jax.ShapeDtypeStruct((m, LANES), F32))
    outs = pl.pallas_call(
        _norm_mm_kernel, grid=(m // tm,),
        in_specs=[pl.BlockSpec((tm, D_MODEL), lambda i: (i, 0)), pl.BlockSpec((1, D_MODEL), lambda i: (0, 0)),
                  seg, seg, pl.BlockSpec((D_MODEL, n), lambda i: (0, 0))],
        out_specs=out_specs, out_shape=out_shape,
        compiler_params=pltpu.CompilerParams(dimension_semantics=("parallel",),
                                             vmem_limit_bytes=VMEM_LIMIT_BYTES),
        name="norm_mm",
    )(x, norm_g.reshape(1, D_MODEL), sc, sh, w_bf)
    return outs if tail else outs[0]


def _split_kernel(a_ref, hi_ref, lo_ref):
    hi, lo = _split_bf16(a_ref[0])
    hi_ref[0] = hi
    lo_ref[0] = lo


def split_bf16(a):
    nl, nr, nc = a.shape
    blk = pl.BlockSpec((1, MM_ROW_TILE, nc), lambda l, i: (l, i, 0))
    return pl.pallas_call(
        _split_kernel, grid=(nl, nr // MM_ROW_TILE), in_specs=[blk], out_specs=[blk, blk],
        out_shape=[jax.ShapeDtypeStruct(a.shape, BF16)] * 2,
        compiler_params=pltpu.CompilerParams(dimension_semantics=("parallel", "parallel")),
        name="split_bf16",
    )(a)


def _mm3_kernel(a_ref, b_ref, o_ref):
    o_ref[...] = _dot3(a_ref[...], b_ref[...]).astype(o_ref.dtype)


def mm3(a, b, out_dtype):
    return pl.pallas_call(_mm3_kernel, out_shape=jax.ShapeDtypeStruct((a.shape[0], b.shape[1]), out_dtype),
                          compiler_params=pltpu.CompilerParams(vmem_limit_bytes=VMEM_LIMIT_BYTES),
                          name="mm3")(a, b)


def _seqmix_kernel(f_ref, z_ref, x_ref, gate_ref, o_ref, acc_ref):
    k = pl.program_id(2)

    @pl.when(k == 0)
    def _():
        acc_ref[...] = jnp.zeros_like(acc_ref)

    acc_ref[...] += _bdot(f_ref[...], z_ref[...])

    @pl.when(k == pl.num_programs(2) - 1)
    def _():
        o_ref[...] = x_ref[...] + gate_ref[0] * acc_ref[...]


def seq_mix_res(fmat, z, x, gate, seq_len, tm, tk):
    t = x.shape[0]
    nseq = t // seq_len
    seqs_per_seg = nseq // gate.shape[0]
    mt = seq_len // tm
    kt_half = seq_len // tk
    return pl.pallas_call(
        _seqmix_kernel, grid=(nseq, mt, 2 * kt_half),
        in_specs=[pl.BlockSpec((tm, tk), lambda s, i, k: (i, k)),
                  pl.BlockSpec((tk, D_MODEL), lambda s, i, k: (s * kt_half + k % kt_half, k // kt_half)),
                  pl.BlockSpec((tm, D_MODEL), lambda s, i, k: (s * mt + i, 0)),
                  pl.BlockSpec((1, 1, D_MODEL), lambda s, i, k: (s // seqs_per_seg, 0, 0))],
        out_specs=pl.BlockSpec((tm, D_MODEL), lambda s, i, k: (s * mt + i, 0)),
        out_shape=jax.ShapeDtypeStruct((t, D_MODEL), F32),
        scratch_shapes=[pltpu.VMEM((tm, D_MODEL), F32)],
        compiler_params=pltpu.CompilerParams(dimension_semantics=("parallel", "parallel", "arbitrary"),
                                             vmem_limit_bytes=VMEM_LIMIT_BYTES),
        name="seq_mix_res",
    )(fmat, z, x, gate)


def _dft_tables(n, cols):
    r = jnp.arange(n, dtype=jnp.int32)[:, None]
    ang = ((r * cols[None, :]) % n).astype(F32) * (2.0 * math.pi / n)
    return jnp.cos(ang), jnp.sin(ang)


def dft_seq_matrix(s):
    w = 1 << (int(math.log2(s)) // 2)
    ch, sh_ = _dft_tables(s, jnp.arange(s // w, dtype=jnp.int32) * w)
    cl, sl = _dft_tables(s, jnp.arange(w, dtype=jnp.int32))
    sc = 1.0 / math.sqrt(s)
    c = (ch[:, :, None] * cl[:, None, :] - sh_[:, :, None] * sl[:, None, :]).reshape(s, s) * sc
    sn = (sh_[:, :, None] * cl[:, None, :] + ch[:, :, None] * sl[:, None, :]).reshape(s, s) * sc
    return jnp.concatenate([c, -sn], axis=1).astype(BF16)


def dft_group_matrices(n, groups):
    c, s = _dft_tables(n, jnp.arange(n, dtype=jnp.int32))
    sc = 1.0 / math.sqrt(n)
    eye = jnp.eye(groups, dtype=F32)
    return jnp.kron(eye, c * sc), jnp.kron(eye, s * sc)


def _delta_prep_kernel(q_ref, k_ref, v_ref, lg_ref, bt_ref, w_ref, u_ref, qd_ref, kd_ref, p_ref, g_ref):
    r = lax.broadcasted_iota(jnp.int32, (CHUNK, CHUNK), 0)
    c = lax.broadcasted_iota(jnp.int32, (CHUNK, CHUNK), 1)
    eye = (r == c).astype(F32)
    ones = jnp.ones((CHUNK, CHUNK), F32)
    incl = (r >= c, r <= c)
    strict = (r > c, r < c)
    tri = (incl[0].astype(F32), incl[1].astype(F32))
    tri_t = (tri[1], tri[0])
    last = (CHUNK - 1, 0)
    chains = [(cg, d, h) for cg in range(PREP_CHUNKS) for d in range(2) for h in range(A_HEADS)]
    rows = lambda cg: slice(cg * CHUNK, (cg + 1) * CHUNK)
    cols = lambda h: slice(h * A_DK, (h + 1) * A_DK)
    kk = {}
    qk = {}
    for cg in range(PREP_CHUNKS):
        for h in range(A_HEADS):
            kb = k_ref[0, rows(cg), cols(h)].astype(BF16)
            kk[cg, h] = lax.dot_general(kb, kb, NT_DIMS, preferred_element_type=F32)
            qk[cg, h] = lax.dot_general(q_ref[0, rows(cg), cols(h)].astype(BF16), kb, NT_DIMS,
                                        preferred_element_type=F32)
    lgw = {}
    btw = {}
    for (cg, d, h) in chains:
        col = d * A_HEADS + h
        lgw[cg, d, h] = jnp.broadcast_to(lg_ref[0, rows(cg), col:col + 1], (CHUNK, A_DK))
        btw[cg, d, h] = jnp.broadcast_to(bt_ref[0, rows(cg), col:col + 1], (CHUNK, A_DK))
    gam = {ch: _dot_exact_lhs(tri[ch[1]], lgw[ch]) for ch in chains}
    gam_row = {ch: _dot_exact_lhs(ones, lgw[ch][:, :CHUNK] * tri_t[ch[1]]) for ch in chains}
    decay = {}
    lmat = {}
    for ch in chains:
        cg, d, h = ch
        diff = gam[ch][:, :CHUNK] - gam_row[ch]
        decay[ch] = jnp.where(incl[d], jnp.exp(jnp.where(incl[d], diff, 0.0)), 0.0)
        lmat[ch] = jnp.where(strict[d], btw[ch][:, :CHUNK] * decay[ch] * kk[cg, h], 0.0)
    pinv = {ch: eye - jnp.where((r // 2 == c // 2) & (r != c), lmat[ch], 0.0) for ch in chains}
    s = 2
    while s < CHUNK:
        join = (r // (2 * s) == c // (2 * s)) & (r // s != c // s)
        tc = {ch: _dot3(pinv[ch], jnp.where(join, lmat[ch], 0.0)) for ch in chains}
        pinv = {ch: pinv[ch] - _dot3(tc[ch], pinv[ch]) for ch in chains}
        s *= 2
    for ch in chains:
        cg, d, h = ch
        kh = k_ref[0, rows(cg), cols(h)]
        vh = v_ref[0, rows(cg), cols(h)]
        qh = q_ref[0, rows(cg), cols(h)]
        egam = jnp.exp(gam[ch])
        rhs = jnp.concatenate([kh * (btw[ch] * egam), vh * btw[ch]], axis=1)
        sol = _dot3(pinv[ch], rhs)
        w_ref[0, d, rows(cg), cols(h)] = sol[:, :A_DK].astype(BF16)
        u_ref[0, d, rows(cg), cols(h)] = sol[:, A_DK:]
        qd_ref[0, d, rows(cg), cols(h)] = (qh * egam).astype(BF16)
        glast = jnp.broadcast_to(gam[ch][last[d]:last[d] + 1, :], (CHUNK, A_DK))
        kd_ref[0, d, rows(cg), cols(h)] = (kh * jnp.exp(glast - gam[ch])).astype(BF16)
        g_ref[0, d, rows(cg), cols(h)] = jnp.exp(glast)
        p_ref[0, d, rows(cg), h * CHUNK:(h + 1) * CHUNK] = (decay[ch] * qk[cg, h]).astype(BF16)


def delta_prep(q, k, v, lg, bt):
    b, s, _ = q.shape
    rt = PREP_CHUNKS * CHUNK
    blk = lambda w: pl.BlockSpec((1, rt, w), lambda bi, ni: (bi, ni, 0))
    oblk = lambda w: pl.BlockSpec((1, 2, rt, w), lambda bi, ni: (bi, 0, ni, 0))
    sh = lambda w, dt: jax.ShapeDtypeStruct((b, 2, s, w), dt)
    return pl.pallas_call(
        _delta_prep_kernel, grid=(b, s // rt),
        in_specs=[blk(A_DIM), blk(A_DIM), blk(A_DIM), blk(2 * A_HEADS), blk(2 * A_HEADS)],
        out_specs=[oblk(A_DIM), oblk(A_DIM), oblk(A_DIM), oblk(A_DIM), oblk(A_HEADS * CHUNK), oblk(A_DIM)],
        out_shape=[sh(A_DIM, BF16), sh(A_DIM, F32), sh(A_DIM, BF16), sh(A_DIM, BF16),
                   sh(A_HEADS * CHUNK, BF16), sh(A_DIM, F32)],
        compiler_params=pltpu.CompilerParams(dimension_semantics=("parallel", "parallel")),
        name="delta_prep",
    )(q, k, v, lg, bt)


def _delta_scan_kernel(*refs):
    ins = refs[:12]
    s0_ref = refs[12]
    of_ref, ob_ref, sout_ref, state = refs[13:]
    n = pl.program_id(1)

    @pl.when(n == 0)
    def _():
        state[...] = s0_ref[...]

    outs = (of_ref, ob_ref)
    chains = [(g, d, h) for g in range(SCAN_SEQS) for d in range(2) for h in range(A_HEADS)]
    cs = lambda h: slice(h * A_DK, (h + 1) * A_DK)
    ref = lambda d, i: ins[d * 6 + i]
    s_old = {ch: state[ch] for ch in chains}
    wqs = {}
    for (g, d, h) in chains:
        wq = jnp.concatenate([ref(d, 0)[g, 0, :, cs(h)], ref(d, 2)[g, 0, :, cs(h)]], axis=0)
        wqs[g, d, h] = _bdot(wq, s_old[g, d, h].astype(BF16))
    unb = {}
    for (g, d, h) in chains:
        unb[g, d, h] = (ref(d, 1)[g, 0, :, cs(h)] - wqs[g, d, h][:CHUNK]).astype(BF16)
    for (g, d, h) in chains:
        o = wqs[g, d, h][CHUNK:] + _bdot(ref(d, 4)[g, 0, :, h * CHUNK:(h + 1) * CHUNK], unb[g, d, h])
        outs[d][g, :, cs(h)] = o
    for (g, d, h) in chains:
        upd = lax.dot_general(ref(d, 3)[g, 0, :, cs(h)], unb[g, d, h], TN_DIMS, preferred_element_type=F32)
        gs = jnp.broadcast_to(ref(d, 5)[g, 0, 0:1, cs(h)], (A_DK, A_DV))
        state[g, d, h] = gs * s_old[g, d, h] + upd

    @pl.when(n == pl.num_programs(1) - 1)
    def _():
        sout_ref[...] = state[...]


def delta_scan(w, u, qd, kd, p, gl, s0):
    b, _, s, _ = u.shape
    n = s // CHUNK

    def spec(wd, d):
        if d == 0:
            return pl.BlockSpec((SCAN_SEQS, 1, CHUNK, wd), lambda bi, ni: (bi, 0, ni, 0))
        return pl.BlockSpec((SCAN_SEQS, 1, CHUNK, wd), lambda bi, ni: (bi, 1, n - 1 - ni, 0))

    arrs = (w, u, qd, kd, p, gl)
    in_specs = [spec(a.shape[-1], d) for d in range(2) for a in arrs]
    st = pl.BlockSpec((SCAN_SEQS, 2, A_HEADS, A_DK, A_DV), lambda bi, ni: (bi, 0, 0, 0, 0))
    of = pl.BlockSpec((SCAN_SEQS, CHUNK, A_DIM), lambda bi, ni: (bi, ni, 0))
    ob = pl.BlockSpec((SCAN_SEQS, CHUNK, A_DIM), lambda bi, ni: (bi, n - 1 - ni, 0))
    return pl.pallas_call(
        _delta_scan_kernel, grid=(b // SCAN_SEQS, n),
        in_specs=in_specs + [st],
        out_specs=[of, ob, st],
        out_shape=[jax.ShapeDtypeStruct((b, s, A_DIM), F32), jax.ShapeDtypeStruct((b, s, A_DIM), F32),
                   jax.ShapeDtypeStruct((b, 2, A_HEADS, A_DK, A_DV), F32)],
        scratch_shapes=[pltpu.VMEM((SCAN_SEQS, 2, A_HEADS, A_DK, A_DV), F32)],
        compiler_params=pltpu.CompilerParams(dimension_semantics=("parallel", "arbitrary")),
        name="delta_scan",
    )(*(arrs + arrs), s0)


def _cand_tables():
    pairs = [(r, c) for r in range(PEER_TOPK) for c in range(PEER_TOPK) if (r + 1) * (c + 1) <= PEER_TOPK]
    npad = 64
    e1 = np.zeros((npad, PEER_TOPK), np.float32)
    e2 = np.zeros((npad, PEER_TOPK), np.float32)
    m = np.zeros((PEER_TOPK, npad), np.float32)
    for k, (r, c) in enumerate(pairs):
        e1[k, r] = 1
        e2[k, c] = 1
        m[r, k] = 1
    return len(pairs), e1, e2, m


N_CAND, _CAND_E1, _CAND_E2, _CAND_ROW = _cand_tables()


def _extract_topk(s, n_iter, want_rank=True):
    k, t = s.shape
    work = s.reshape(k // 8, 8, t)
    rank = jnp.full(work.shape, float(n_iter), F32) if want_rank else None
    vals = []
    for r in range(n_iter):
        m = jnp.max(jnp.max(work, axis=0), axis=0, keepdims=True)
        hit = work == jnp.broadcast_to(m, (8, t))[None]
        if want_rank:
            rank = jnp.where(hit, float(r), rank)
        work = jnp.where(hit, NEG_INF, work)
        vals.append(m)
    return vals, (rank.reshape(k, t) if want_rank else None)


def _route_kernel(x_ref, g_ref, sc_ref, sh_ref, wqh_ref, wql_ref, k1_ref, k2_ref, e1_ref, e2_ref, mrow_ref,
                  ht_ref, a_ref, n_ref, b_ref, r_ref, q_scr):
    hmod = _norm_modulate(x_ref[...], g_ref[...], sc_ref[0], sh_ref[0])
    ht_ref[...] = hmod.T.astype(BF16)
    hh, hl = _split_bf16(hmod)
    q_scr[...] = _bdot(hh, wqh_ref[...]) + (_bdot(hh, wql_ref[...]) + _bdot(hl, wqh_ref[...]))
    tt = x_ref.shape[0]

    def head(h, carry):
        c1 = pl.ds(pl.multiple_of(h * 2 * PEER_DK, PEER_DK), PEER_DK)
        c2 = pl.ds(pl.multiple_of(h * 2 * PEER_DK + PEER_DK, PEER_DK), PEER_DK)
        hp = lax.Precision.HIGHEST
        s1 = lax.dot_general(k1_ref[h], q_scr[:, c1], NT_DIMS, precision=hp, preferred_element_type=F32)
        s2 = lax.dot_general(k2_ref[h], q_scr[:, c2], NT_DIMS, precision=hp, preferred_element_type=F32)
        v1, _ = _extract_topk(s1, PEER_TOPK, want_rank=False)
        v2, rank2 = _extract_topk(s2, PEER_TOPK)
        v1m = jnp.concatenate(v1, axis=0)
        v2m = jnp.concatenate(v2, axis=0)
        cand = (jnp.dot(e1_ref[...], v1m, precision=hp, preferred_element_type=F32)
                + jnp.dot(e2_ref[...], v2m, precision=hp, preferred_element_type=F32))
        row = lax.broadcasted_iota(jnp.int32, cand.shape, 0)
        cand = jnp.where(row < N_CAND, cand, NEG_INF)
        _, crank = _extract_topk(cand, PEER_TOPK)
        sel = crank < float(PEER_TOPK)
        cmax = v1[0] + v2[0]
        z = jnp.sum(jnp.where(sel, jnp.exp(cand - cmax), 0.0), axis=0, keepdims=True)
        n_r = _bdot(mrow_ref[...], jnp.where(sel, 1.0, 0.0).astype(BF16))
        s1_3 = s1.reshape(PEER_KEYS // 8, 8, tt)
        nn = jnp.zeros_like(s1_3)
        for r in range(PEER_TOPK):
            nn = jnp.where(s1_3 == jnp.broadcast_to(v1[r], (8, tt))[None],
                           jnp.broadcast_to(n_r[r:r + 1, :], (8, tt))[None], nn)
        a_ref[h] = jnp.exp(s1 - v1[0]) / z
        n_ref[h] = nn.reshape(PEER_KEYS, tt)
        b_ref[h] = jnp.where(rank2 < float(PEER_TOPK), jnp.exp(s2 - v2[0]), 0.0).astype(BF16)
        r_ref[h] = rank2.astype(BF16)
        return carry

    lax.fori_loop(0, PEER_HEADS, head, 0, unroll=ROUTE_HEAD_UNROLL)


def peer_route(x, norm_g, sc, sh, wq_hi, wq_lo, k1, k2, rows_per_seg, layer):
    wq_spec = pl.BlockSpec((None, D_MODEL, 2 * PEER_HEADS * PEER_DK), lambda i: (layer, 0, 0))
    t = x.shape[0]
    tt = ROUTE_TOKEN_TILE
    gate = pl.BlockSpec((PEER_HEADS, PEER_KEYS, tt), lambda i: (0, 0, i))
    full = lambda shp: pl.BlockSpec(shp, lambda i: (0,) * len(shp))
    seg = _seg_spec(rows_per_seg, tt)
    return pl.pallas_call(
        _route_kernel, grid=(t // tt,),
        in_specs=[pl.BlockSpec((tt, D_MODEL), lambda i: (i, 0)), full((1, D_MODEL)), seg, seg,
                  wq_spec, wq_spec,
                  full((PEER_HEADS, PEER_KEYS, PEER_DK)),
                  full((PEER_HEADS, PEER_KEYS, PEER_DK)), full((64, PEER_TOPK)), full((64, PEER_TOPK)),
                  full((PEER_TOPK, 64))],
        out_specs=[pl.BlockSpec((D_MODEL, tt), lambda i: (0, i)), gate, gate, gate, gate],
        out_shape=[jax.ShapeDtypeStruct((D_MODEL, t), BF16)]
        + [jax.ShapeDtypeStruct((PEER_HEADS, PEER_KEYS, t), dt) for dt in (F32, F32, BF16, BF16)],
        scratch_shapes=[pltpu.VMEM((tt, 2 * PEER_HEADS * PEER_DK), F32)],
        compiler_params=pltpu.CompilerParams(dimension_semantics=("parallel",),
                                             vmem_limit_bytes=VMEM_LIMIT_BYTES),
        name="peer_route",
    )(x, norm_g.reshape(1, D_MODEL), sc, sh, wq_hi, wq_lo, k1, k2,
      jnp.asarray(_CAND_E1), jnp.asarray(_CAND_E2), jnp.asarray(_CAND_ROW, BF16))


def _peer_dense_kernel(ht_ref, u_ref, vt_ref, a_ref, n_ref, b_ref, r_ref, x_ref, g2_ref, o_ref,
                       acc_ref, hid_ref, w_ref, bp_ref, rp_ref):
    e = pl.program_id(1)

    @pl.when(e == 0)
    def _():
        acc_ref[...] = jnp.zeros_like(acc_ref)
        for c in range(PEER_TOKEN_TILE // LANES):
            ls = slice(c * LANES, (c + 1) * LANES)
            bp_ref[:, c] = b_ref[:, :, ls].astype(BF16)
            rp_ref[:, c] = r_ref[:, :, ls].astype(BF16)

    def hidden(pp):
        rows = slice(pp * PIECE_ROWS, (pp + 1) * PIECE_ROWS)
        hid_ref[pp % 2] = _bdot(u_ref[rows, :], ht_ref[...])

    def gates(pp):
        for c in range(PEER_TOKEN_TILE // LANES):
            ls = slice(c * LANES, (c + 1) * LANES)
            accs = [jnp.zeros((PEER_KEYS, LANES), BF16) for _ in range(PIECE_KEYS)]
            for h in range(PEER_HEADS):
                b = bp_ref[h, c]
                r = rp_ref[h, c]
                for ii in range(PIECE_KEYS):
                    k = pp * PIECE_KEYS + ii
                    a16 = jnp.broadcast_to(a_ref[h, k:k + 1, ls], (BF16_ROWS, LANES)).astype(BF16)
                    n16 = jnp.broadcast_to(n_ref[h, k:k + 1, ls], (BF16_ROWS, LANES)).astype(BF16)
                    a128 = jnp.concatenate([a16] * (PEER_KEYS // BF16_ROWS), axis=0)
                    n128 = jnp.concatenate([n16] * (PEER_KEYS // BF16_ROWS), axis=0)
                    accs[ii] = accs[ii] + jnp.where(r < n128, b, jnp.zeros_like(b)) * a128
            for ii in range(PIECE_KEYS):
                rows = slice(ii * PEER_KEYS, (ii + 1) * PEER_KEYS)
                act = _gelu_tanh(hid_ref[pp % 2, rows, ls].astype(BF16))
                w_ref[pp % 2, rows, ls] = act * accs[ii]

    def project(pp):
        rows = slice(pp * PIECE_ROWS, (pp + 1) * PIECE_ROWS)
        acc_ref[...] += _bdot(vt_ref[0, :, rows], w_ref[pp % 2])

    hidden(0)
    for pp in range(PEER_PIECES):
        if pp + 1 < PEER_PIECES:
            hidden(pp + 1)
        gates(pp)
        project(pp)

    @pl.when(e == pl.num_programs(1) - 1)
    def _():
        o_ref[...] = x_ref[...] + g2_ref[0] * acc_ref[...].T


def peer_dense(ht_bf, u_bf, vt_bf, a_t, n_t, b_t, r_t, x, g2, rows_per_seg, layer):
    t = x.shape[0]
    tt = PEER_TOKEN_TILE
    grid = (t // tt, PEER_N // PEER_EXPERT_TILE)
    gate_spec = pl.BlockSpec((PEER_HEADS, PEER_KEYS, tt), lambda ti, ei: (0, 0, ti))
    step_keys = pl.BlockSpec((PEER_HEADS, PEER_KEYS_PER_STEP, tt), lambda ti, ei: (0, ei, ti))
    return pl.pallas_call(
        _peer_dense_kernel, grid=grid,
        in_specs=[pl.BlockSpec((D_MODEL, tt), lambda ti, ei: (0, ti)),
                  pl.BlockSpec((None, PEER_EXPERT_TILE, D_MODEL), lambda ti, ei: (layer, ei, 0)),
                  pl.BlockSpec((None, 1, D_MODEL, PEER_EXPERT_TILE), lambda ti, ei: (layer, ei, 0, 0)),
                  step_keys, step_keys, gate_spec, gate_spec,
                  pl.BlockSpec((tt, D_MODEL), lambda ti, ei: (ti, 0)),
                  _seg_spec(rows_per_seg, tt)],
        out_specs=pl.BlockSpec((tt, D_MODEL), lambda ti, ei: (ti, 0)),
        out_shape=jax.ShapeDtypeStruct((t, D_MODEL), F32),
        scratch_shapes=[pltpu.VMEM((D_MODEL, tt), F32),
                        pltpu.VMEM((2, PIECE_ROWS, tt), F32),
                        pltpu.VMEM((2, PIECE_ROWS, tt), BF16),
                        pltpu.VMEM((PEER_HEADS, tt // LANES, PEER_KEYS, LANES), BF16),
                        pltpu.VMEM((PEER_HEADS, tt // LANES, PEER_KEYS, LANES), BF16)],
        compiler_params=pltpu.CompilerParams(dimension_semantics=("parallel", "arbitrary"),
                                             vmem_limit_bytes=VMEM_LIMIT_BYTES),
        name="peer_dense",
    )(ht_bf, u_bf, vt_bf, a_t, n_t, b_t, r_t, x, g2)


def _fill_halo_scratch(scr, prev, cur, nxt, halo, ts):
    i = pl.program_id(1)
    scr[0:halo, :] = jnp.where(i > 0, prev, 0.0)
    scr[halo:halo + ts, :] = cur
    scr[halo + ts:halo + ts + halo, :] = jnp.where(i < pl.num_programs(1) - 1, nxt, 0.0)


def _depthwise_taps(scr, w_ref, taps, halo, r0, rows):
    off = halo - taps // 2
    acc = scr[off + r0:off + r0 + rows, :] * w_ref[0:1, :]
    for k in range(1, taps):
        acc = acc + scr[off + r0 + k:off + r0 + k + rows, :] * w_ref[k:k + 1, :]
    return acc


def _halo_specs(width, col_block, halo, ts, seq_len, total_rows):
    tiles = seq_len // ts
    per_tile = ts // halo
    last = total_rows // halo - 1
    prev = pl.BlockSpec((halo, width), lambda s, i: (jnp.maximum((s * tiles + i) * per_tile - 1, 0), col_block))
    cur = pl.BlockSpec((ts, width), lambda s, i: (s * tiles + i, col_block))
    nxt = pl.BlockSpec((halo, width), lambda s, i: (jnp.minimum((s * tiles + i + 1) * per_tile, last), col_block))
    return prev, cur, nxt


CONF_ROW_BLOCK = 32
QKV_ROW_BLOCK = 16


def _conformer_kernel(prev_ref, cur_ref, next_ref, w_ref, b_ref, lg_ref, lb_ref, o_ref, scr):
    ts = cur_ref.shape[0]
    glu = lambda blk: blk[:, :B_CH] * jax.nn.sigmoid(blk[:, B_CH:])
    _fill_halo_scratch(scr, glu(prev_ref[...]), glu(cur_ref[...]), glu(next_ref[...]), CONF_HALO, ts)
    for rb in range(ts // CONF_ROW_BLOCK):
        r0 = rb * CONF_ROW_BLOCK
        hh = _depthwise_taps(scr, w_ref, B_CONV, CONF_HALO, r0, CONF_ROW_BLOCK) + b_ref[...]
        mu = jnp.mean(hh, axis=-1, keepdims=True)
        var = jnp.mean(jnp.square(hh - mu), axis=-1, keepdims=True)
        y = (hh - mu) * lax.rsqrt(var + EPS) * lg_ref[...] + lb_ref[...]
        o_ref[r0:r0 + CONF_ROW_BLOCK, :] = (y * jax.nn.sigmoid(y)).astype(o_ref.dtype)


def conformer_branch(p, dw_w, dw_b, ln_g, ln_b, nseq, seq_len):
    t = p.shape[0]
    ts = CONV_ROW_TILE
    prev, cur, nxt = _halo_specs(2 * B_CH, COL_GLU // (2 * B_CH), CONF_HALO, ts, seq_len, t)
    row = lambda: pl.BlockSpec((1, B_CH), lambda s, i: (0, 0))
    w_pad = jnp.pad(dw_w, ((0, 32 - B_CONV), (0, 0)))
    return pl.pallas_call(
        _conformer_kernel, grid=(nseq, seq_len // ts),
        in_specs=[prev, cur, nxt, pl.BlockSpec((32, B_CH), lambda s, i: (0, 0)), row(), row(), row()],
        out_specs=pl.BlockSpec((ts, B_CH), lambda s, i: (s * (seq_len // ts) + i, 0)),
        out_shape=jax.ShapeDtypeStruct((t, B_CH), BF16),
        scratch_shapes=[pltpu.VMEM((ts + 2 * CONF_HALO, B_CH), F32)],
        compiler_params=pltpu.CompilerParams(dimension_semantics=("parallel", "parallel")),
        name="conformer_branch",
    )(p, p, p, w_pad, dw_b.reshape(1, B_CH), ln_g.reshape(1, B_CH), ln_b.reshape(1, B_CH))


def _qkv_conv_kernel(prev_ref, cur_ref, next_ref, w_ref, q_ref, k_ref, v_ref, scr):
    ts = cur_ref.shape[0]
    _fill_halo_scratch(scr, prev_ref[...], cur_ref[...], next_ref[...], QKV_HALO, ts)
    for rb in range(ts // QKV_ROW_BLOCK):
        r0 = rb * QKV_ROW_BLOCK
        rows = slice(r0, r0 + QKV_ROW_BLOCK)
        y = _depthwise_taps(scr, w_ref, SHORT_CONV, QKV_HALO, r0, QKV_ROW_BLOCK)
        y = y * jax.nn.sigmoid(y)
        for h in range(A_HEADS):
            cs = slice(h * A_DK, (h + 1) * A_DK)
            qh = y[:, h * A_DK:(h + 1) * A_DK]
            kh = y[:, A_DIM + h * A_DK:A_DIM + (h + 1) * A_DK]
            q_ref[rows, cs] = qh * (lax.rsqrt(jnp.sum(qh * qh, axis=-1, keepdims=True) + EPS) * (A_DK ** -0.5))
            k_ref[rows, cs] = kh * lax.rsqrt(jnp.sum(kh * kh, axis=-1, keepdims=True) + EPS)
        v_ref[rows, :] = y[:, 2 * A_DIM:]


def qkv_conv(p, conv_w, nseq, seq_len):
    t = p.shape[0]
    ts = CONV_ROW_TILE
    prev, cur, nxt = _halo_specs(3 * A_DIM, 0, QKV_HALO, ts, seq_len, t)
    out = pl.BlockSpec((ts, A_DIM), lambda s, i: (s * (seq_len // ts) + i, 0))
    w_pad = jnp.pad(conv_w, ((0, 8 - SHORT_CONV), (0, 0)))
    return pl.pallas_call(
        _qkv_conv_kernel, grid=(nseq, seq_len // ts),
        in_specs=[prev, cur, nxt, pl.BlockSpec((8, 3 * A_DIM), lambda s, i: (0, 0))],
        out_specs=[out, out, out],
        out_shape=[jax.ShapeDtypeStruct((t, A_DIM), F32)] * 3,
        scratch_shapes=[pltpu.VMEM((ts + 2 * QKV_HALO, 3 * A_DIM), F32)],
        compiler_params=pltpu.CompilerParams(dimension_semantics=("parallel", "parallel")),
        name="qkv_conv",
    )(p, p, p, w_pad)


def _mix_out_kernel(of_ref, ob_ref, z_ref, conf_ref, ng_ref, w_ref, x_ref, gate_ref, o_ref):
    o = of_ref[...] + ob_ref[...]
    z = z_ref[...]
    parts = []
    for h in range(A_HEADS):
        cs = slice(h * A_DV, (h + 1) * A_DV)
        oh = o[:, cs]
        zh = z[:, cs]
        scale = lax.rsqrt(jnp.mean(oh * oh, axis=-1, keepdims=True) + EPS)
        parts.append((oh * scale * ng_ref[...] * (zh * jax.nn.sigmoid(zh))).astype(BF16))
    oa = jnp.concatenate(parts, axis=1)
    mix = _bdot(oa, w_ref[:A_DIM, :]) + _bdot(conf_ref[...], w_ref[A_DIM:, :])
    o_ref[...] = x_ref[...] + gate_ref[0] * mix


def mix_out(o_f, o_b, p, conf, norm_g, w_bf, x, gate, rows_per_seg):
    t = x.shape[0]
    tm = MM_ROW_TILE
    half = lambda cb: pl.BlockSpec((tm, A_DIM), lambda i: (i, cb))
    return pl.pallas_call(
        _mix_out_kernel, grid=(t // tm,),
        in_specs=[half(0), half(0), half(COL_Z // A_DIM), half(0), pl.BlockSpec((1, A_DV), lambda i: (0, 0)),
                  pl.BlockSpec((D_MODEL, D_MODEL), lambda i: (0, 0)), pl.BlockSpec((tm, D_MODEL), lambda i: (i, 0)),
                  _seg_spec(rows_per_seg, tm)],
        out_specs=pl.BlockSpec((tm, D_MODEL), lambda i: (i, 0)),
        out_shape=jax.ShapeDtypeStruct((t, D_MODEL), F32),
        compiler_params=pltpu.CompilerParams(dimension_semantics=("parallel",),
                                             vmem_limit_bytes=VMEM_LIMIT_BYTES),
        name="mix_out",
    )(o_f, o_b, p, conf, norm_g.reshape(1, A_DV), w_bf, x, gate)


def grid_pos_emb(n_tokens):
    rows = n_tokens // GRID_W
    r = jnp.repeat(jnp.arange(rows, dtype=F32), GRID_W)
    col = jnp.tile(jnp.arange(GRID_W, dtype=F32), rows)
    nf = D_MODEL // 4
    freqs = jnp.exp(-math.log(POS_BASE) * jnp.arange(nf, dtype=F32) / nf)
    ar = r[:, None] * freqs
    ac = col[:, None] * freqs
    return jnp.concatenate([jnp.sin(ar), jnp.cos(ar), jnp.sin(ac), jnp.cos(ac)], axis=-1)


def delta_conformer_layer(x, p, ab, g1, nseq, seq_len, s0, e, prm, rows_per_seg):
    t = p.shape[0]
    seq = lambda m: m.reshape(nseq, seq_len, m.shape[-1])
    q, k, v = qkv_conv(p, prm['conv_qkv_w'][e], nseq, seq_len)
    alpha = ab[:, :2 * A_HEADS]
    beta = jax.nn.sigmoid(ab[:, 2 * A_HEADS:4 * A_HEADS])
    log_g = (-jnp.exp(prm['a_log'][e]).reshape(1, 2 * A_HEADS)
             * jax.nn.softplus(alpha + prm['dt_bias'][e].reshape(1, 2 * A_HEADS)))
    o_f, o_b, st = delta_scan(*delta_prep(seq(q), seq(k), seq(v), seq(log_g), seq(beta)), s0)
    conf = conformer_branch(p, prm['conf_dw_w'][e], prm['conf_dw_b'][e], prm['conf_ln_g'][e],
                            prm['conf_ln_b'][e], nseq, seq_len)
    x = mix_out(o_f.reshape(t, A_DIM), o_b.reshape(t, A_DIM), p, conf, prm['delta_norm_g'][e],
                prm['w_out_bf'][e], x, g1, rows_per_seg)
    return x, st


def trunk(x, nseq, seq_len, cond, s0, prm):
    t = x.shape[0]
    rows_per_seg = t // cond.shape[0]
    states = []
    for l in range(DEPTH):
        mod = jax.nn.silu(cond) @ prm['ada_w'][l] + prm['ada_b'][l]
        sh1, sc1, g1, sh2, sc2, g2 = [m[:, None, :] for m in jnp.split(mod, 6, axis=-1)]
        e = l // 2
        if l % 2 == 0:
            p, ab = norm_mm(x, prm['norm1_g'][l], sc1, sh1, prm['w_in_bf'][e], rows_per_seg, F32, tail=True)
            x, st = delta_conformer_layer(x, p, ab, g1, nseq, seq_len, s0[:, e], e, prm, rows_per_seg)
            states.append(st)
        else:
            z = norm_mm(x, prm['norm1_g'][l], sc1, sh1, prm['w_fnet_bf'][e], rows_per_seg, BF16)
            tm = min(seq_len, 512)
            x = seq_mix_res(prm['dft_seq'][seq_len], z, x, g1, seq_len, tm, min(seq_len, 1024))
        ht, a_t, n_t, b_t, r_t = peer_route(x, prm['norm2_g'][l], sc2, sh2, prm['peer_wq_hi'],
                                            prm['peer_wq_lo'], prm['peer_k1'][l], prm['peer_k2'][l],
                                            rows_per_seg, l)
        x = peer_dense(ht, prm['peer_u_bf'], prm['peer_vt_bf'], a_t, n_t, b_t, r_t, x, g2, rows_per_seg, l)
    xf = x * lax.rsqrt(jnp.mean(x * x, axis=-1, keepdims=True) + EPS) * prm['final_norm_g']
    return xf, jnp.stack(states, axis=1)


def kernel(x_prompt, x_sample, state_delta, c, c_ctx, ada_w, ada_b, norm1_g, norm2_g, w_in_ab, conv_qkv_w,
           a_log, dt_bias, delta_norm_g, conf_dw_w, conf_dw_b, conf_ln_g, conf_ln_b, w_out_ab, w_out_c,
           peer_wq, peer_k1, peer_k2, peer_u, peer_v, final_norm_g):
    bp, sp, _ = x_prompt.shape
    bs, ss, _ = x_sample.shape
    bdc, bds = dft_group_matrices(D_MODEL // C_GROUPS, C_GROUPS)
    w_fnet = [jnp.concatenate([mm3(bdc, w_out_c[e], BF16), mm3(bds, w_out_c[e], BF16)], axis=1)
              for e in range(DEPTH // 2)]
    o4 = 4 * A_DIM
    w_in = jnp.concatenate([w_in_ab[:, :, :o4], w_in_ab[:, :, o4 + 4 * A_HEADS:], w_in_ab[:, :, o4:o4 + 4 * A_HEADS]],
                           axis=-1).astype(BF16)
    wq_hi, wq_lo = split_bf16(peer_wq)
    prm = {'ada_w': ada_w, 'ada_b': ada_b, 'norm1_g': norm1_g, 'norm2_g': norm2_g,
           'w_in_bf': jnp.pad(w_in, ((0, 0), (0, 0), (0, P_AB_PAD - P_AB))),
           'conv_qkv_w': conv_qkv_w, 'a_log': a_log, 'dt_bias': dt_bias,
           'delta_norm_g': delta_norm_g, 'conf_dw_w': conf_dw_w, 'conf_dw_b': conf_dw_b,
           'conf_ln_g': conf_ln_g, 'conf_ln_b': conf_ln_b, 'w_out_bf': w_out_ab.astype(BF16),
           'w_fnet_bf': w_fnet, 'dft_seq': {s: dft_seq_matrix(s) for s in {sp, ss}},
           'peer_wq_hi': wq_hi, 'peer_wq_lo': wq_lo, 'peer_k1': peer_k1, 'peer_k2': peer_k2,
           'peer_u_bf': peer_u.astype(BF16),
           'peer_vt_bf': jnp.transpose(peer_v.astype(BF16).reshape(DEPTH, PEER_N // PEER_EXPERT_TILE,
                                                                   PEER_EXPERT_TILE, D_MODEL), (0, 1, 3, 2)),
           'final_norm_g': final_norm_g}
    ne = (DEPTH + 1) // 2
    s0_ctx = jnp.zeros((bp, ne, 2, A_HEADS, A_DK, A_DV), F32)
    y_prompt, ctx_states = trunk(x_prompt.reshape(bp * sp, D_MODEL), bp, sp, c_ctx[None, :], s0_ctx, prm)
    xs = (x_sample + grid_pos_emb(ss)[None]).reshape(bs * ss, D_MODEL)
    y_sample, _ = trunk(xs, bs, ss, c, state_delta, prm)
    return (y_prompt.reshape(bp, sp, D_MODEL), y_sample.reshape(bs, ss, D_MODEL), ctx_states)
```

```python
import math

import jax
import jax.numpy as jnp
import numpy as np
from jax import lax
from jax.experimental import pallas as pl
from jax.experimental.pallas import tpu as pltpu

D_MODEL = 1024
DEPTH = 4
GRID_W = 64
POS_BASE = 10000.0
EPS = 1e-6
A_HEADS = 4
A_DK = 128
A_DV = 128
A_DIM = A_HEADS * A_DV
CHUNK = 64
B_CH = D_MODEL // 2
P_AB = 4 * A_DIM + 4 * A_HEADS + 2 * B_CH
C_GROUPS = 8
PEER_HEADS = 8
PEER_KEYS = 128
PEER_N = PEER_KEYS * PEER_KEYS
PEER_DK = 128
PEER_TOPK = 16

F32 = jnp.float32
BF16 = jnp.bfloat16
NEG_INF = float("-inf")

LANES = 128
VMEM_LIMIT_BYTES = 56 * 1024 * 1024
MM_ROW_TILE = 512
SEQ_MIX_TILE = 1024
ROUTE_TOKEN_TILE = 512
ROUTE_HEAD_UNROLL = 2
PEER_TOKEN_TILE = 512
PEER_KEYS_PER_STEP = 8
PEER_EXPERT_TILE = PEER_KEYS_PER_STEP * PEER_KEYS
PREP_CHUNKS = 2
SCAN_SEQS = 2
P_AB_PAD = 3200
COL_Z = 3 * A_DIM
COL_GLU = 4 * A_DIM
COL_AB = 4 * A_DIM + 2 * B_CH
CONV_ROW_TILE = 256
SHORT_CONV = 7
B_CONV = 31
CONF_HALO = 16
QKV_HALO = 8
PEER_PIECES = 2
PIECE_KEYS = PEER_KEYS_PER_STEP // PEER_PIECES
PIECE_ROWS = PIECE_KEYS * PEER_KEYS
BF16_ROWS = 16

NT_DIMS = (((1,), (1,)), ((), ()))
TN_DIMS = (((0,), (0,)), ((), ()))


def _bdot(a, b):
    return jnp.dot(a, b, preferred_element_type=F32)


def _split_bf16(a):
    hi = a.astype(BF16)
    lo = (a - hi.astype(F32)).astype(BF16)
    return hi, lo


def _dot3(a, b):
    ah, al = _split_bf16(a)
    bh, bl = _split_bf16(b)
    return _bdot(ah, bh) + (_bdot(ah, bl) + _bdot(al, bh))


def _dot_exact_lhs(a01, b):
    a = a01.astype(BF16)
    bh = b.astype(BF16)
    r1 = b - bh.astype(F32)
    bm = r1.astype(BF16)
    bl = (r1 - bm.astype(F32)).astype(BF16)
    return _bdot(a, bh) + (_bdot(a, bm) + _bdot(a, bl))


def _gelu_tanh(x):
    return 0.5 * x * (1.0 + jnp.tanh(math.sqrt(2.0 / math.pi) * (x + 0.044715 * (x * x * x))))


def _seg_spec(rows_per_seg, tile):
    per = rows_per_seg // tile
    return pl.BlockSpec((1, 1, D_MODEL), lambda i, *_: (i // per, 0, 0))


def _norm_modulate(x, g, sc, sh):
    hn = x * lax.rsqrt(jnp.mean(x * x, axis=-1, keepdims=True) + EPS) * g
    return hn * (1.0 + sc) + sh


def _norm_mm_kernel(x_ref, g_ref, sc_ref, sh_ref, w_ref, o_ref, *tail_ref):
    h = _norm_modulate(x_ref[...], g_ref[...], sc_ref[0], sh_ref[0]).astype(BF16)
    res = _bdot(h, w_ref[...])
    o_ref[...] = res.astype(o_ref.dtype)
    if tail_ref:
        tail_ref[0][...] = res[:, res.shape[1] - LANES:]


def norm_mm(x, norm_g, sc, sh, w_bf, rows_per_seg, out_dtype, tail=False):
    m = x.shape[0]
    n = w_bf.shape[1]
    tm = MM_ROW_TILE
    seg = _seg_spec(rows_per_seg, tm)
    out_specs = [pl.BlockSpec((tm, n), lambda i: (i, 0))]
    out_shape = [jax.ShapeDtypeStruct((m, n), out_dtype)]
    if tail:
        out_specs.append(pl.BlockSpec((tm, LANES), lambda i: (i, 0)))
        out_shape.append(jax.ShapeDtypeStruct((m, LANES), F32))
    outs = pl.pallas_call(
        _norm_mm_kernel, grid=(m // tm,),
        in_specs=[pl.BlockSpec((tm, D_MODEL), lambda i: (i, 0)), pl.BlockSpec((1, D_MODEL), lambda i: (0, 0)),
                  seg, seg, pl.BlockSpec((D_MODEL, n), lambda i: (0, 0))],
        out_specs=out_specs, out_shape=out_shape,
        compiler_params=pltpu.CompilerParams(dimension_semantics=("parallel",),
                                             vmem_limit_bytes=VMEM_LIMIT_BYTES),
        name="norm_mm",
    )(x, norm_g.reshape(1, D_MODEL), sc, sh, w_bf)
    return outs if tail else outs[0]


def _split_kernel(a_ref, hi_ref, lo_ref):
    hi, lo = _split_bf16(a_ref[0])
    hi_ref[0] = hi
    lo_ref[0] = lo


def split_bf16(a):
    nl, nr, nc = a.shape
    blk = pl.BlockSpec((1, MM_ROW_TILE, nc), lambda l, i: (l, i, 0))
    return pl.pallas_call(
        _split_kernel, grid=(nl, nr // MM_ROW_TILE), in_specs=[blk], out_specs=[blk, blk],
        out_shape=[jax.ShapeDtypeStruct(a.shape, BF16)] * 2,
        compiler_params=pltpu.CompilerParams(dimension_semantics=("parallel", "parallel")),
        name="split_bf16",
    )(a)


def _mm3_kernel(a_ref, b_ref, o_ref):
    o_ref[...] = _dot3(a_ref[...], b_ref[...]).astype(o_ref.dtype)


def mm3(a, b, out_dtype):
    return pl.pallas_call(_mm3_kernel, out_shape=jax.ShapeDtypeStruct((a.shape[0], b.shape[1]), out_dtype),
                          compiler_params=pltpu.CompilerParams(vmem_limit_bytes=VMEM_LIMIT_BYTES),
                          name="mm3")(a, b)


def _seqmix_kernel(f_ref, z_ref, x_ref, gate_ref, o_ref, acc_ref):
    k = pl.program_id(2)

    @pl.when(k == 0)
    def _():
        acc_ref[...] = jnp.zeros_like(acc_ref)

    acc_ref[...] += _bdot(f_ref[...], z_ref[...])

    @pl.when(k == pl.num_programs(2) - 1)
    def _():
        o_ref[...] = x_ref[...] + gate_ref[0] * acc_ref[...]


def seq_mix_res(fmat, z, x, gate, seq_len, tm, tk):
    t = x.shape[0]
    nseq = t // seq_len
    seqs_per_seg = nseq // gate.shape[0]
    mt = seq_len // tm
    kt_half = seq_len // tk
    return pl.pallas_call(
        _seqmix_kernel, grid=(nseq, mt, 2 * kt_half),
        in_specs=[pl.BlockSpec((tm, tk), lambda s, i, k: (i, k)),
                  pl.BlockSpec((tk, D_MODEL), lambda s, i, k: (s * kt_half + k % kt_half, k // kt_half)),
                  pl.BlockSpec((tm, D_MODEL), lambda s, i, k: (s * mt + i, 0)),
                  pl.BlockSpec((1, 1, D_MODEL), lambda s, i, k: (s // seqs_per_seg, 0, 0))],
        out_specs=pl.BlockSpec((tm, D_MODEL), lambda s, i, k: (s * mt + i, 0)),
        out_shape=jax.ShapeDtypeStruct((t, D_MODEL), F32),
        scratch_shapes=[pltpu.VMEM((tm, D_MODEL), F32)],
        compiler_params=pltpu.CompilerParams(dimension_semantics=("parallel", "parallel", "arbitrary"),
                                             vmem_limit_bytes=VMEM_LIMIT_BYTES),
        name="seq_mix_res",
    )(fmat, z, x, gate)


def _dft_tables(n, cols):
    r = jnp.arange(n, dtype=jnp.int32)[:, None]
    ang = ((r * cols[None, :]) % n).astype(F32) * (2.0 * math.pi / n)
    return jnp.cos(ang), jnp.sin(ang)


def dft_seq_matrix(s):
    w = 1 << (int(math.log2(s)) // 2)
    ch, sh_ = _dft_tables(s, jnp.arange(s // w, dtype=jnp.int32) * w)
    cl, sl = _dft_tables(s, jnp.arange(w, dtype=jnp.int32))
    sc = 1.0 / math.sqrt(s)
    c = (ch[:, :, None] * cl[:, None, :] - sh_[:, :, None] * sl[:, None, :]).reshape(s, s) * sc
    sn = (sh_[:, :, None] * cl[:, None, :] + ch[:, :, None] * sl[:, None, :]).reshape(s, s) * sc
    return jnp.concatenate([c, -sn], axis=1).astype(BF16)


def dft_group_matrices(n, groups):
    c, s = _dft_tables(n, jnp.arange(n, dtype=jnp.int32))
    sc = 1.0 / math.sqrt(n)
    eye = jnp.eye(groups, dtype=F32)
    return jnp.kron(eye, c * sc), jnp.kron(eye, s * sc)


def _delta_prep_kernel(q_ref, k_ref, v_ref, lg_ref, bt_ref, w_ref, u_ref, qd_ref, kd_ref, p_ref, g_ref):
    r = lax.broadcasted_iota(jnp.int32, (CHUNK, CHUNK), 0)
    c = lax.broadcasted_iota(jnp.int32, (CHUNK, CHUNK), 1)
    eye = (r == c).astype(F32)
    ones = jnp.ones((CHUNK, CHUNK), F32)
    incl = (r >= c, r <= c)
    strict = (r > c, r < c)
    tri = (incl[0].astype(F32), incl[1].astype(F32))
    tri_t = (tri[1], tri[0])
    last = (CHUNK - 1, 0)
    chains = [(cg, d, h) for cg in range(PREP_CHUNKS) for d in range(2) for h in range(A_HEADS)]
    rows = lambda cg: slice(cg * CHUNK, (cg + 1) * CHUNK)
    cols = lambda h: slice(h * A_DK, (h + 1) * A_DK)
    kk = {}
    qk = {}
    for cg in range(PREP_CHUNKS):
        for h in range(A_HEADS):
            kb = k_ref[0, rows(cg), cols(h)].astype(BF16)
            kk[cg, h] = lax.dot_general(kb, kb, NT_DIMS, preferred_element_type=F32)
            qk[cg, h] = lax.dot_general(q_ref[0, rows(cg), cols(h)].astype(BF16), kb, NT_DIMS,
                                        preferred_element_type=F32)
    lgw = {}
    btw = {}
    for (cg, d, h) in chains:
        col = d * A_HEADS + h
        lgw[cg, d, h] = jnp.broadcast_to(lg_ref[0, rows(cg), col:col + 1], (CHUNK, A_DK))
        btw[cg, d, h] = jnp.broadcast_to(bt_ref[0, rows(cg), col:col + 1], (CHUNK, A_DK))
    gam = {ch: _dot_exact_lhs(tri[ch[1]], lgw[ch]) for ch in chains}
    gam_row = {ch: _dot_exact_lhs(ones, lgw[ch][:, :CHUNK] * tri_t[ch[1]]) for ch in chains}
    decay = {}
    lmat = {}
    for ch in chains:
        cg, d, h = ch
        diff = gam[ch][:, :CHUNK] - gam_row[ch]
        decay[ch] = jnp.where(incl[d], jnp.exp(jnp.where(incl[d], diff, 0.0)), 0.0)
        lmat[ch] = jnp.where(strict[d], btw[ch][:, :CHUNK] * decay[ch] * kk[cg, h], 0.0)
    pinv = {ch: eye - jnp.where((r // 2 == c // 2) & (r != c), lmat[ch], 0.0) for ch in chains}
    s = 2
    while s < CHUNK:
        join = (r // (2 * s) == c // (2 * s)) & (r // s != c // s)
        tc = {ch: _dot3(pinv[ch], jnp.where(join, lmat[ch], 0.0)) for ch in chains}
        pinv = {ch: pinv[ch] - _dot3(tc[ch], pinv[ch]) for ch in chains}
        s *= 2
    for ch in chains:
        cg, d, h = ch
        kh = k_ref[0, rows(cg), cols(h)]
        vh = v_ref[0, rows(cg), cols(h)]
        qh = q_ref[0, rows(cg), cols(h)]
        egam = jnp.exp(gam[ch])
        rhs = jnp.concatenate([kh * (btw[ch] * egam), vh * btw[ch]], axis=1)
        sol = _dot3(pinv[ch], rhs)
        w_ref[0, d, rows(cg), cols(h)] = sol[:, :A_DK].astype(BF16)
        u_ref[0, d, rows(cg), cols(h)] = sol[:, A_DK:]
        qd_ref[0, d, rows(cg), cols(h)] = (qh * egam).astype(BF16)
        glast = jnp.broadcast_to(gam[ch][last[d]:last[d] + 1, :], (CHUNK, A_DK))
        kd_ref[0, d, rows(cg), cols(h)] = (kh * jnp.exp(glast - gam[ch])).astype(BF16)
        g_ref[0, d, rows(cg), cols(h)] = jnp.exp(glast)
        p_ref[0, d, rows(cg), h * CHUNK:(h + 1) * CHUNK] = (decay[ch] * qk[cg, h]).astype(BF16)


def delta_prep(q, k, v, lg, bt):
    b, s, _ = q.shape
    rt = PREP_CHUNKS * CHUNK
    blk = lambda w: pl.BlockSpec((1, rt, w), lambda bi, ni: (bi, ni, 0))
    oblk = lambda w: pl.BlockSpec((1, 2, rt, w), lambda bi, ni: (bi, 0, ni, 0))
    sh = lambda w, dt: jax.ShapeDtypeStruct((b, 2, s, w), dt)
    return pl.pallas_call(
        _delta_prep_kernel, grid=(b, s // rt),
        in_specs=[blk(A_DIM), blk(A_DIM), blk(A_DIM), blk(2 * A_HEADS), blk(2 * A_HEADS)],
        out_specs=[oblk(A_DIM), oblk(A_DIM), oblk(A_DIM), oblk(A_DIM), oblk(A_HEADS * CHUNK), oblk(A_DIM)],
        out_shape=[sh(A_DIM, BF16), sh(A_DIM, F32), sh(A_DIM, BF16), sh(A_DIM, BF16),
                   sh(A_HEADS * CHUNK, BF16), sh(A_DIM, F32)],
        compiler_params=pltpu.CompilerParams(dimension_semantics=("parallel", "parallel")),
        name="delta_prep",
    )(q, k, v, lg, bt)


def _delta_scan_kernel(*refs):
    ins = refs[:12]
    s0_ref = refs[12]
    of_ref, ob_ref, sout_ref, state = refs[13:]
    n = pl.program_id(1)

    @pl.when(n == 0)
    def _():
        state[...] = s0_ref[...]

    outs = (of_ref, ob_ref)
    chains = [(g, d, h) for g in range(SCAN_SEQS) for d in range(2) for h in range(A_HEADS)]
    cs = lambda h: slice(h * A_DK, (h + 1) * A_DK)
    ref = lambda d, i: ins[d * 6 + i]
    s_old = {ch: state[ch] for ch in chains}
    wqs = {}
    for (g, d, h) in chains:
        wq = jnp.concatenate([ref(d, 0)[g, 0, :, cs(h)], ref(d, 2)[g, 0, :, cs(h)]], axis=0)
        wqs[g, d, h] = _bdot(wq, s_old[g, d, h].astype(BF16))
    unb = {}
    for (g, d, h) in chains:
        unb[g, d, h] = (ref(d, 1)[g, 0, :, cs(h)] - wqs[g, d, h][:CHUNK]).astype(BF16)
    for (g, d, h) in chains:
        o = wqs[g, d, h][CHUNK:] + _bdot(ref(d, 4)[g, 0, :, h * CHUNK:(h + 1) * CHUNK], unb[g, d, h])
        outs[d][g, :, cs(h)] = o
    for (g, d, h) in chains:
        upd = lax.dot_general(ref(d, 3)[g, 0, :, cs(h)], unb[g, d, h], TN_DIMS, preferred_element_type=F32)
        gs = jnp.broadcast_to(ref(d, 5)[g, 0, 0:1, cs(h)], (A_DK, A_DV))
        state[g, d, h] = gs * s_old[g, d, h] + upd

    @pl.when(n == pl.num_programs(1) - 1)
    def _():
        sout_ref[...] = state[...]


def delta_scan(w, u, qd, kd, p, gl, s0):
    b, _, s, _ = u.shape
    n = s // CHUNK

    def spec(wd, d):
        if d == 0:
            return pl.BlockSpec((SCAN_SEQS, 1, CHUNK, wd), lambda bi, ni: (bi, 0, ni, 0))
        return pl.BlockSpec((SCAN_SEQS, 1, CHUNK, wd), lambda bi, ni: (bi, 1, n - 1 - ni, 0))

    arrs = (w, u, qd, kd, p, gl)
    in_specs = [spec(a.shape[-1], d) for d in range(2) for a in arrs]
    st = pl.BlockSpec((SCAN_SEQS, 2, A_HEADS, A_DK, A_DV), lambda bi, ni: (bi, 0, 0, 0, 0))
    of = pl.BlockSpec((SCAN_SEQS, CHUNK, A_DIM), lambda bi, ni: (bi, ni, 0))
    ob = pl.BlockSpec((SCAN_SEQS, CHUNK, A_DIM), lambda bi, ni: (bi, n - 1 - ni, 0))
    return pl.pallas_call(
        _delta_scan_kernel, grid=(b // SCAN_SEQS, n),
        in_specs=in_specs + [st],
        out_specs=[of, ob, st],
        out_shape=[jax.ShapeDtypeStruct((b, s, A_DIM), F32), jax.ShapeDtypeStruct((b, s, A_DIM), F32),
                   jax.ShapeDtypeStruct((b, 2, A_HEADS, A_DK, A_DV), F32)],
        scratch_shapes=[pltpu.VMEM((SCAN_SEQS, 2, A_HEADS, A_DK, A_DV), F32)],
        compiler_params=pltpu.CompilerParams(dimension_semantics=("parallel", "arbitrary")),
        name="delta_scan",
    )(*(arrs + arrs), s0)


def _cand_tables():
    pairs = [(r, c) for r in range(PEER_TOPK) for c in range(PEER_TOPK) if (r + 1) * (c + 1) <= PEER_TOPK]
    npad = 64
    e1 = np.zeros((npad, PEER_TOPK), np.float32)
    e2 = np.zeros((npad, PEER_TOPK), np.float32)
    m = np.zeros((PEER_TOPK, npad), np.float32)
    for k, (r, c) in enumerate(pairs):
        e1[k, r] = 1
        e2[k, c] = 1
        m[r, k] = 1
    return len(pairs), e1, e2, m


N_CAND, _CAND_E1, _CAND_E2, _CAND_ROW = _cand_tables()


def _extract_topk(s, n_iter, want_rank=True):
    k, t = s.shape
    work = s.reshape(k // 8, 8, t)
    rank = jnp.full(work.shape, float(n_iter), F32) if want_rank else None
    vals = []
    for r in range(n_iter):
        m = jnp.max(jnp.max(work, axis=0), axis=0, keepdims=True)
        hit = work == jnp.broadcast_to(m, (8, t))[None]
        if want_rank:
            rank = jnp.where(hit, float(r), rank)
        work = jnp.where(hit, NEG_INF, work)
        vals.append(m)
    return vals, (rank.reshape(k, t) if want_rank else None)


def _route_kernel(x_ref, g_ref, sc_ref, sh_ref, wqh_ref, wql_ref, k1_ref, k2_ref, e1_ref, e2_ref, mrow_ref,
                  ht_ref, a_ref, n_ref, b_ref, r_ref, q_scr):
    hmod = _norm_modulate(x_ref[...], g_ref[...], sc_ref[0], sh_ref[0])
    ht_ref[...] = hmod.T.astype(BF16)
    hh, hl = _split_bf16(hmod)
    q_scr[...] = _bdot(hh, wqh_ref[...]) + (_bdot(hh, wql_ref[...]) + _bdot(hl, wqh_ref[...]))
    tt = x_ref.shape[0]

    def head(h, carry):
        c1 = pl.ds(pl.multiple_of(h * 2 * PEER_DK, PEER_DK), PEER_DK)
        c2 = pl.ds(pl.multiple_of(h * 2 * PEER_DK + PEER_DK, PEER_DK), PEER_DK)
        hp = lax.Precision.HIGHEST
        s1 = lax.dot_general(k1_ref[h], q_scr[:, c1], NT_DIMS, precision=hp, preferred_element_type=F32)
        s2 = lax.dot_general(k2_ref[h], q_scr[:, c2], NT_DIMS, precision=hp, preferred_element_type=F32)
        v1, _ = _extract_topk(s1, PEER_TOPK, want_rank=False)
        v2, rank2 = _extract_topk(s2, PEER_TOPK)
        v1m = jnp.concatenate(v1, axis=0)
        v2m = jnp.concatenate(v2, axis=0)
        cand = (jnp.dot(e1_ref[...], v1m, precision=hp, preferred_element_type=F32)
                + jnp.dot(e2_ref[...], v2m, precision=hp, preferred_element_type=F32))
        row = lax.broadcasted_iota(jnp.int32, cand.shape, 0)
        cand = jnp.where(row < N_CAND, cand, NEG_INF)
        _, crank = _extract_topk(cand, PEER_TOPK)
        sel = crank < float(PEER_TOPK)
        cmax = v1[0] + v2[0]
        z = jnp.sum(jnp.where(sel, jnp.exp(cand - cmax), 0.0), axis=0, keepdims=True)
        n_r = _bdot(mrow_ref[...], jnp.where(sel, 1.0, 0.0).astype(BF16))
        s1_3 = s1.reshape(PEER_KEYS // 8, 8, tt)
        nn = jnp.zeros_like(s1_3)
        for r in range(PEER_TOPK):
            nn = jnp.where(s1_3 == jnp.broadcast_to(v1[r], (8, tt))[None],
                           jnp.broadcast_to(n_r[r:r + 1, :], (8, tt))[None], nn)
        a_ref[h] = jnp.exp(s1 - v1[0]) / z
        n_ref[h] = nn.reshape(PEER_KEYS, tt)
        b_ref[h] = jnp.where(rank2 < float(PEER_TOPK), jnp.exp(s2 - v2[0]), 0.0).astype(BF16)
        r_ref[h] = rank2.astype(BF16)
        return carry

    lax.fori_loop(0, PEER_HEADS, head, 0, unroll=ROUTE_HEAD_UNROLL)


def peer_route(x, norm_g, sc, sh, wq_hi, wq_lo, k1, k2, rows_per_seg, layer):
    wq_spec = pl.BlockSpec((None, D_MODEL, 2 * PEER_HEADS * PEER_DK), lambda i: (layer, 0, 0))
    t = x.shape[0]
    tt = ROUTE_TOKEN_TILE
    gate = pl.BlockSpec((PEER_HEADS, PEER_KEYS, tt), lambda i: (0, 0, i))
    full = lambda shp: pl.BlockSpec(shp, lambda i: (0,) * len(shp))
    seg = _seg_spec(rows_per_seg, tt)
    return pl.pallas_call(
        _route_kernel, grid=(t // tt,),
        in_specs=[pl.BlockSpec((tt, D_MODEL), lambda i: (i, 0)), full((1, D_MODEL)), seg, seg,
                  wq_spec, wq_spec,
                  full((PEER_HEADS, PEER_KEYS, PEER_DK)),
                  full((PEER_HEADS, PEER_KEYS, PEER_DK)), full((64, PEER_TOPK)), full((64, PEER_TOPK)),
                  full((PEER_TOPK, 64))],
        out_specs=[pl.BlockSpec((D_MODEL, tt), lambda i: (0, i)), gate, gate, gate, gate],
        out_shape=[jax.ShapeDtypeStruct((D_MODEL, t), BF16)]
        + [jax.ShapeDtypeStruct((PEER_HEADS, PEER_KEYS, t), dt) for dt in (F32, F32, BF16, BF16)],
        scratch_shapes=[pltpu.VMEM((tt, 2 * PEER_HEADS * PEER_DK), F32)],
        compiler_params=pltpu.CompilerParams(dimension_semantics=("parallel",),
                                             vmem_limit_bytes=VMEM_LIMIT_BYTES),
        name="peer_route",
    )(x, norm_g.reshape(1, D_MODEL), sc, sh, wq_hi, wq_lo, k1, k2,
      jnp.asarray(_CAND_E1), jnp.asarray(_CAND_E2), jnp.asarray(_CAND_ROW, BF16))


def _peer_dense_kernel(ht_ref, u_ref, vt_ref, a_ref, n_ref, b_ref, r_ref, x_ref, g2_ref, o_ref,
                       acc_ref, hid_ref, w_ref, bp_ref, rp_ref):
    e = pl.program_id(1)

    @pl.when(e == 0)
    def _():
        acc_ref[...] = jnp.zeros_like(acc_ref)
        for c in range(PEER_TOKEN_TILE // LANES):
            ls = slice(c * LANES, (c + 1) * LANES)
            bp_ref[:, c] = b_ref[:, :, ls].astype(BF16)
            rp_ref[:, c] = r_ref[:, :, ls].astype(BF16)

    def hidden(pp):
        rows = slice(pp * PIECE_ROWS, (pp + 1) * PIECE_ROWS)
        hid_ref[pp % 2] = _bdot(u_ref[rows, :], ht_ref[...])

    def gates(pp):
        for c in range(PEER_TOKEN_TILE // LANES):
            ls = slice(c * LANES, (c + 1) * LANES)
            accs = [jnp.zeros((PEER_KEYS, LANES), BF16) for _ in range(PIECE_KEYS)]
            for h in range(PEER_HEADS):
                b = bp_ref[h, c]
                r = rp_ref[h, c]
                for ii in range(PIECE_KEYS):
                    k = pp * PIECE_KEYS + ii
                    a16 = jnp.broadcast_to(a_ref[h, k:k + 1, ls], (BF16_ROWS, LANES)).astype(BF16)
                    n16 = jnp.broadcast_to(n_ref[h, k:k + 1, ls], (BF16_ROWS, LANES)).astype(BF16)
                    a128 = jnp.concatenate([a16] * (PEER_KEYS // BF16_ROWS), axis=0)
                    n128 = jnp.concatenate([n16] * (PEER_KEYS // BF16_ROWS), axis=0)
                    accs[ii] = accs[ii] + jnp.where(r < n128, b, jnp.zeros_like(b)) * a128
            for ii in range(PIECE_KEYS):
                rows = slice(ii * PEER_KEYS, (ii + 1) * PEER_KEYS)
                act = _gelu_tanh(hid_ref[pp % 2, rows, ls].astype(BF16))
                w_ref[pp % 2, rows, ls] = act * accs[ii]

    def project(pp):
        rows = slice(pp * PIECE_ROWS, (pp + 1) * PIECE_ROWS)
        acc_ref[...] += _bdot(vt_ref[0, :, rows], w_ref[pp % 2])

    hidden(0)
    for pp in range(PEER_PIECES):
        if pp + 1 < PEER_PIECES:
            hidden(pp + 1)
        gates(pp)
        project(pp)

    @pl.when(e == pl.num_programs(1) - 1)
    def _():
        o_ref[...] = x_ref[...] + g2_ref[0] * acc_ref[...].T


def peer_dense(ht_bf, u_bf, vt_bf, a_t, n_t, b_t, r_t, x, g2, rows_per_seg, layer):
    t = x.shape[0]
    tt = PEER_TOKEN_TILE
    grid = (t // tt, PEER_N // PEER_EXPERT_TILE)
    gate_spec = pl.BlockSpec((PEER_HEADS, PEER_KEYS, tt), lambda ti, ei: (0, 0, ti))
    step_keys = pl.BlockSpec((PEER_HEADS, PEER_KEYS_PER_STEP, tt), lambda ti, ei: (0, ei, ti))
    return pl.pallas_call(
        _peer_dense_kernel, grid=grid,
        in_specs=[pl.BlockSpec((D_MODEL, tt), lambda ti, ei: (0, ti)),
                  pl.BlockSpec((None, PEER_EXPERT_TILE, D_MODEL), lambda ti, ei: (layer, ei, 0)),
                  pl.BlockSpec((None, 1, D_MODEL, PEER_EXPERT_TILE), lambda ti, ei: (layer, ei, 0, 0)),
                  step_keys, step_keys, gate_spec, gate_spec,
                  pl.BlockSpec((tt, D_MODEL), lambda ti, ei: (ti, 0)),
                  _seg_spec(rows_per_seg, tt)],
        out_specs=pl.BlockSpec((tt, D_MODEL), lambda ti, ei: (ti, 0)),
        out_shape=jax.ShapeDtypeStruct((t, D_MODEL), F32),
        scratch_shapes=[pltpu.VMEM((D_MODEL, tt), F32),
                        pltpu.VMEM((2, PIECE_ROWS, tt), F32),
                        pltpu.VMEM((2, PIECE_ROWS, tt), BF16),
                        pltpu.VMEM((PEER_HEADS, tt // LANES, PEER_KEYS, LANES), BF16),
                        pltpu.VMEM((PEER_HEADS, tt // LANES, PEER_KEYS, LANES), BF16)],
        compiler_params=pltpu.CompilerParams(dimension_semantics=("parallel", "arbitrary"),
                                             vmem_limit_bytes=VMEM_LIMIT_BYTES),
        name="peer_dense",
    )(ht_bf, u_bf, vt_bf, a_t, n_t, b_t, r_t, x, g2)


def _fill_halo_scratch(scr, prev, cur, nxt, halo, ts):
    i = pl.program_id(1)
    scr[0:halo, :] = jnp.where(i > 0, prev, 0.0)
    scr[halo:halo + ts, :] = cur
    scr[halo + ts:halo + ts + halo, :] = jnp.where(i < pl.num_programs(1) - 1, nxt, 0.0)


def _depthwise_taps(scr, w_ref, taps, halo, r0, rows):
    off = halo - taps // 2
    acc = scr[off + r0:off + r0 + rows, :] * w_ref[0:1, :]
    for k in range(1, taps):
        acc = acc + scr[off + r0 + k:off + r0 + k + rows, :] * w_ref[k:k + 1, :]
    return acc


def _halo_specs(width, col_block, halo, ts, seq_len, total_rows):
    tiles = seq_len // ts
    per_tile = ts // halo
    last = total_rows // halo - 1
    prev = pl.BlockSpec((halo, width), lambda s, i: (jnp.maximum((s * tiles + i) * per_tile - 1, 0), col_block))
    cur = pl.BlockSpec((ts, width), lambda s, i: (s * tiles + i, col_block))
    nxt = pl.BlockSpec((halo, width), lambda s, i: (jnp.minimum((s * tiles + i + 1) * per_tile, last), col_block))
    return prev, cur, nxt


CONF_ROW_BLOCK = 32
QKV_ROW_BLOCK = 16


def _conformer_kernel(prev_ref, cur_ref, next_ref, w_ref, b_ref, lg_ref, lb_ref, o_ref, scr):
    ts = cur_ref.shape[0]
    glu = lambda blk: blk[:, :B_CH] * jax.nn.sigmoid(blk[:, B_CH:])
    _fill_halo_scratch(scr, glu(prev_ref[...]), glu(cur_ref[...]), glu(next_ref[...]), CONF_HALO, ts)
    for rb in range(ts // CONF_ROW_BLOCK):
        r0 = rb * CONF_ROW_BLOCK
        hh = _depthwise_taps(scr, w_ref, B_CONV, CONF_HALO, r0, CONF_ROW_BLOCK) + b_ref[...]
        mu = jnp.mean(hh, axis=-1, keepdims=True)
        var = jnp.mean(jnp.square(hh - mu), axis=-1, keepdims=True)
        y = (hh - mu) * lax.rsqrt(var + EPS) * lg_ref[...] + lb_ref[...]
        o_ref[r0:r0 + CONF_ROW_BLOCK, :] = (y * jax.nn.sigmoid(y)).astype(o_ref.dtype)


def conformer_branch(p, dw_w, dw_b, ln_g, ln_b, nseq, seq_len):
    t = p.shape[0]
    ts = CONV_ROW_TILE
    prev, cur, nxt = _halo_specs(2 * B_CH, COL_GLU // (2 * B_CH), CONF_HALO, ts, seq_len, t)
    row = lambda: pl.BlockSpec((1, B_CH), lambda s, i: (0, 0))
    w_pad = jnp.pad(dw_w, ((0, 32 - B_CONV), (0, 0)))
    return pl.pallas_call(
        _conformer_kernel, grid=(nseq, seq_len // ts),
        in_specs=[prev, cur, nxt, pl.BlockSpec((32, B_CH), lambda s, i: (0, 0)), row(), row(), row()],
        out_specs=pl.BlockSpec((ts, B_CH), lambda s, i: (s * (seq_len // ts) + i, 0)),
        out_shape=jax.ShapeDtypeStruct((t, B_CH), BF16),
        scratch_shapes=[pltpu.VMEM((ts + 2 * CONF_HALO, B_CH), F32)],
        compiler_params=pltpu.CompilerParams(dimension_semantics=("parallel", "parallel")),
        name="conformer_branch",
    )(p, p, p, w_pad, dw_b.reshape(1, B_CH), ln_g.reshape(1, B_CH), ln_b.reshape(1, B_CH))


def _qkv_conv_kernel(prev_ref, cur_ref, next_ref, w_ref, q_ref, k_ref, v_ref, scr):
    ts = cur_ref.shape[0]
    _fill_halo_scratch(scr, prev_ref[...], cur_ref[...], next_ref[...], QKV_HALO, ts)
    for rb in range(ts // QKV_ROW_BLOCK):
        r0 = rb * QKV_ROW_BLOCK
        rows = slice(r0, r0 + QKV_ROW_BLOCK)
        y = _depthwise_taps(scr, w_ref, SHORT_CONV, QKV_HALO, r0, QKV_ROW_BLOCK)
        y = y * jax.nn.sigmoid(y)
        for h in range(A_HEADS):
            cs = slice(h * A_DK, (h + 1) * A_DK)
            qh = y[:, h * A_DK:(h + 1) * A_DK]
            kh = y[:, A_DIM + h * A_DK:A_DIM + (h + 1) * A_DK]
            q_ref[rows, cs] = qh * (lax.rsqrt(jnp.sum(qh * qh, axis=-1, keepdims=True) + EPS) * (A_DK ** -0.5))
            k_ref[rows, cs] = kh * lax.rsqrt(jnp.sum(kh * kh, axis=-1, keepdims=True) + EPS)
        v_ref[rows, :] = y[:, 2 * A_DIM:]


def qkv_conv(p, conv_w, nseq, seq_len):
    t = p.shape[0]
    ts = CONV_ROW_TILE
    prev, cur, nxt = _halo_specs(3 * A_DIM, 0, QKV_HALO, ts, seq_len, t)
    out = pl.BlockSpec((ts, A_DIM), lambda s, i: (s * (seq_len // ts) + i, 0))
    w_pad = jnp.pad(conv_w, ((0, 8 - SHORT_CONV), (0, 0)))
    return pl.pallas_call(
        _qkv_conv_kernel, grid=(nseq, seq_len // ts),
        in_specs=[prev, cur, nxt, pl.BlockSpec((8, 3 * A_DIM), lambda s, i: (0, 0))],
        out_specs=[out, out, out],
        out_shape=[jax.ShapeDtypeStruct((t, A_DIM), F32)] * 3,
        scratch_shapes=[pltpu.VMEM((ts + 2 * QKV_HALO, 3 * A_DIM), F32)],
        compiler_params=pltpu.CompilerParams(dimension_semantics=("parallel", "parallel")),
        name="qkv_conv",
    )(p, p, p, w_pad)


def _mix_out_kernel(of_ref, ob_ref, z_ref, conf_ref, ng_ref, w_ref, x_ref, gate_ref, o_ref):
    o = of_ref[...] + ob_ref[...]
    z = z_ref[...]
    parts = []
    for h in range(A_HEADS):
        cs = slice(h * A_DV, (h + 1) * A_DV)
        oh = o[:, cs]
        zh = z[:, cs]
        scale = lax.rsqrt(jnp.mean(oh * oh, axis=-1, keepdims=True) + EPS)
        parts.append((oh * scale * ng_ref[...] * (zh * jax.nn.sigmoid(zh))).astype(BF16))
    oa = jnp.concatenate(parts, axis=1)
    mix = _bdot(oa, w_ref[:A_DIM, :]) + _bdot(conf_ref[...], w_ref[A_DIM:, :])
    o_ref[...] = x_ref[...] + gate_ref[0] * mix


def mix_out(o_f, o_b, p, conf, norm_g, w_bf, x, gate, rows_per_seg):
    t = x.shape[0]
    tm = MM_ROW_TILE
    half = lambda cb: pl.BlockSpec((tm, A_DIM), lambda i: (i, cb))
    return pl.pallas_call(
        _mix_out_kernel, grid=(t // tm,),
        in_specs=[half(0), half(0), half(COL_Z // A_DIM), half(0), pl.BlockSpec((1, A_DV), lambda i: (0, 0)),
                  pl.BlockSpec((D_MODEL, D_MODEL), lambda i: (0, 0)), pl.BlockSpec((tm, D_MODEL), lambda i: (i, 0)),
                  _seg_spec(rows_per_seg, tm)],
        out_specs=pl.BlockSpec((tm, D_MODEL), lambda i: (i, 0)),
        out_shape=jax.ShapeDtypeStruct((t, D_MODEL), F32),
        compiler_params=pltpu.CompilerParams(dimension_semantics=("parallel",),
                                             vmem_limit_bytes=VMEM_LIMIT_BYTES),
        name="mix_out",
    )(o_f, o_b, p, conf, norm_g.reshape(1, A_DV), w_bf, x, gate)


def grid_pos_emb(n_tokens):
    rows = n_tokens // GRID_W
    r = jnp.repeat(jnp.arange(rows, dtype=F32), GRID_W)
    col = jnp.tile(jnp.arange(GRID_W, dtype=F32), rows)
    nf = D_MODEL // 4
    freqs = jnp.exp(-math.log(POS_BASE) * jnp.arange(nf, dtype=F32) / nf)
    ar = r[:, None] * freqs
    ac = col[:, None] * freqs
    return jnp.concatenate([jnp.sin(ar), jnp.cos(ar), jnp.sin(ac), jnp.cos(ac)], axis=-1)


def delta_conformer_layer(x, p, ab, g1, nseq, seq_len, s0, e, prm, rows_per_seg):
    t = p.shape[0]
    seq = lambda m: m.reshape(nseq, seq_len, m.shape[-1])
    q, k, v = qkv_conv(p, prm['conv_qkv_w'][e], nseq, seq_len)
    alpha = ab[:, :2 * A_HEADS]
    beta = jax.nn.sigmoid(ab[:, 2 * A_HEADS:4 * A_HEADS])
    log_g = (-jnp.exp(prm['a_log'][e]).reshape(1, 2 * A_HEADS)
             * jax.nn.softplus(alpha + prm['dt_bias'][e].reshape(1, 2 * A_HEADS)))
    o_f, o_b, st = delta_scan(*delta_prep(seq(q), seq(k), seq(v), seq(log_g), seq(beta)), s0)
    conf = conformer_branch(p, prm['conf_dw_w'][e], prm['conf_dw_b'][e], prm['conf_ln_g'][e],
                            prm['conf_ln_b'][e], nseq, seq_len)
    x = mix_out(o_f.reshape(t, A_DIM), o_b.reshape(t, A_DIM), p, conf, prm['delta_norm_g'][e],
                prm['w_out_bf'][e], x, g1, rows_per_seg)
    return x, st


def trunk(x, nseq, seq_len, cond, s0, prm):
    t = x.shape[0]
    rows_per_seg = t // cond.shape[0]
    states = []
    for l in range(DEPTH):
        mod = jax.nn.silu(cond) @ prm['ada_w'][l] + prm['ada_b'][l]
        sh1, sc1, g1, sh2, sc2, g2 = [m[:, None, :] for m in jnp.split(mod, 6, axis=-1)]
        e = l // 2
        if l % 2 == 0:
            p, ab = norm_mm(x, prm['norm1_g'][l], sc1, sh1, prm['w_in_bf'][e], rows_per_seg, F32, tail=True)
            x, st = delta_conformer_layer(x, p, ab, g1, nseq, seq_len, s0[:, e], e, prm, rows_per_seg)
            states.append(st)
        else:
            z = norm_mm(x, prm['norm1_g'][l], sc1, sh1, prm['w_fnet_bf'][e], rows_per_seg, BF16)
            tile = min(seq_len, SEQ_MIX_TILE)
            x = seq_mix_res(prm['dft_seq'][seq_len], z, x, g1, seq_len, tile, tile)
        ht, a_t, n_t, b_t, r_t = peer_route(x, prm['norm2_g'][l], sc2, sh2, prm['peer_wq_hi'],
                                            prm['peer_wq_lo'], prm['peer_k1'][l], prm['peer_k2'][l],
                                            rows_per_seg, l)
        x = peer_dense(ht, prm['peer_u_bf'], prm['peer_vt_bf'], a_t, n_t, b_t, r_t, x, g2, rows_per_seg, l)
    xf = x * lax.rsqrt(jnp.mean(x * x, axis=-1, keepdims=True) + EPS) * prm['final_norm_g']
    return xf, jnp.stack(states, axis=1)


def kernel(x_prompt, x_sample, state_delta, c, c_ctx, ada_w, ada_b, norm1_g, norm2_g, w_in_ab, conv_qkv_w,
           a_log, dt_bias, delta_norm_g, conf_dw_w, conf_dw_b, conf_ln_g, conf_ln_b, w_out_ab, w_out_c,
           peer_wq, peer_k1, peer_k2, peer_u, peer_v, final_norm_g):
    bp, sp, _ = x_prompt.shape
    bs, ss, _ = x_sample.shape
    bdc, bds = dft_group_matrices(D_MODEL // C_GROUPS, C_GROUPS)
    w_fnet = [jnp.concatenate([mm3(bdc, w_out_c[e], BF16), mm3(bds, w_out_c[e], BF16)], axis=1)
              for e in range(DEPTH // 2)]
    o4 = 4 * A_DIM
    w_in = jnp.concatenate([w_in_ab[:, :, :o4], w_in_ab[:, :, o4 + 4 * A_HEADS:], w_in_ab[:, :, o4:o4 + 4 * A_HEADS]],
                           axis=-1).astype(BF16)
    wq_hi, wq_lo = split_bf16(peer_wq)
    prm = {'ada_w': ada_w, 'ada_b': ada_b, 'norm1_g': norm1_g, 'norm2_g': norm2_g,
           'w_in_bf': jnp.pad(w_in, ((0, 0), (0, 0), (0, P_AB_PAD - P_AB))),
           'conv_qkv_w': conv_qkv_w, 'a_log': a_log, 'dt_bias': dt_bias,
           'delta_norm_g': delta_norm_g, 'conf_dw_w': conf_dw_w, 'conf_dw_b': conf_dw_b,
           'conf_ln_g': conf_ln_g, 'conf_ln_b': conf_ln_b, 'w_out_bf': w_out_ab.astype(BF16),
           'w_fnet_bf': w_fnet, 'dft_seq': {s: dft_seq_matrix(s) for s in {sp, ss}},
           'peer_wq_hi': wq_hi, 'peer_wq_lo': wq_lo, 'peer_k1': peer_k1, 'peer_k2': peer_k2,
           'peer_u_bf': peer_u.astype(BF16),
           'peer_vt_bf': jnp.transpose(peer_v.astype(BF16).reshape(DEPTH, PEER_N // PEER_EXPERT_TILE,
                                                                   PEER_EXPERT_TILE, D_MODEL), (0, 1, 3, 2)),
           'final_norm_g': final_norm_g}
    ne = (DEPTH + 1) // 2
    s0_ctx = jnp.zeros((bp, ne, 2, A_HEADS, A_DK, A_DV), F32)
    y_prompt, ctx_states = trunk(x_prompt.reshape(bp * sp, D_MODEL), bp, sp, c_ctx[None, :], s0_ctx, prm)
    xs = (x_sample + grid_pos_emb(ss)[None]).reshape(bs * ss, D_MODEL)
    y_sample, _ = trunk(xs, bs, ss, c, state_delta, prm)
    return (y_prompt.reshape(bp, sp, D_MODEL), y_sample.reshape(bs, ss, D_MODEL), ctx_states)
```

```python
import math

import jax
import jax.numpy as jnp
import numpy as np
from jax import lax
from jax.experimental import pallas as pl
from jax.experimental.pallas import tpu as pltpu

D_MODEL = 1024
DEPTH = 4
GRID_W = 64
POS_BASE = 10000.0
EPS = 1e-6
A_HEADS = 4
A_DK = 128
A_DV = 128
A_DIM = A_HEADS * A_DV
CHUNK = 64
B_CH = D_MODEL // 2
P_AB = 4 * A_DIM + 4 * A_HEADS + 2 * B_CH
C_GROUPS = 8
PEER_HEADS = 8
PEER_KEYS = 128
PEER_N = PEER_KEYS * PEER_KEYS
PEER_DK = 128
PEER_TOPK = 16

F32 = jnp.float32
BF16 = jnp.bfloat16
NEG_INF = float("-inf")

LANES = 128
VMEM_LIMIT_BYTES = 56 * 1024 * 1024
MM_ROW_TILE = 512
SEQ_MIX_TILE = 1024
ROUTE_TOKEN_TILE = 512
ROUTE_HEAD_UNROLL = 2
PEER_TOKEN_TILE = 512
PEER_KEYS_PER_STEP = 16
PEER_EXPERT_TILE = PEER_KEYS_PER_STEP * PEER_KEYS
PREP_CHUNKS = 2
SCAN_SEQS = 2
P_AB_PAD = 3200
COL_Z = 3 * A_DIM
COL_GLU = 4 * A_DIM
COL_AB = 4 * A_DIM + 2 * B_CH
CONV_ROW_TILE = 256
SHORT_CONV = 7
B_CONV = 31
CONF_HALO = 16
QKV_HALO = 8
PEER_PIECES = 4
PIECE_KEYS = PEER_KEYS_PER_STEP // PEER_PIECES
PIECE_ROWS = PIECE_KEYS * PEER_KEYS
BF16_ROWS = 16

NT_DIMS = (((1,), (1,)), ((), ()))
TN_DIMS = (((0,), (0,)), ((), ()))


def _bdot(a, b):
    return jnp.dot(a, b, preferred_element_type=F32)


def _split_bf16(a):
    hi = a.astype(BF16)
    lo = (a - hi.astype(F32)).astype(BF16)
    return hi, lo


def _dot3(a, b):
    ah, al = _split_bf16(a)
    bh, bl = _split_bf16(b)
    return _bdot(ah, bh) + (_bdot(ah, bl) + _bdot(al, bh))


def _dot_exact_lhs(a01, b):
    a = a01.astype(BF16)
    bh = b.astype(BF16)
    r1 = b - bh.astype(F32)
    bm = r1.astype(BF16)
    bl = (r1 - bm.astype(F32)).astype(BF16)
    return _bdot(a, bh) + (_bdot(a, bm) + _bdot(a, bl))


def _gelu_tanh(x):
    return 0.5 * x * (1.0 + jnp.tanh(math.sqrt(2.0 / math.pi) * (x + 0.044715 * (x * x * x))))


def _seg_spec(rows_per_seg, tile):
    per = rows_per_seg // tile
    return pl.BlockSpec((1, 1, D_MODEL), lambda i, *_: (i // per, 0, 0))


def _norm_modulate(x, g, sc, sh):
    hn = x * lax.rsqrt(jnp.mean(x * x, axis=-1, keepdims=True) + EPS) * g
    return hn * (1.0 + sc) + sh


def _norm_mm_kernel(x_ref, g_ref, sc_ref, sh_ref, w_ref, o_ref, *tail_ref):
    h = _norm_modulate(x_ref[...], g_ref[...], sc_ref[0], sh_ref[0]).astype(BF16)
    res = _bdot(h, w_ref[...])
    o_ref[...] = res.astype(o_ref.dtype)
    if tail_ref:
        tail_ref[0][...] = res[:, res.shape[1] - LANES:]


def norm_mm(x, norm_g, sc, sh, w_bf, rows_per_seg, out_dtype, tail=False):
    m = x.shape[0]
    n = w_bf.shape[1]
    tm = MM_ROW_TILE
    seg = _seg_spec(rows_per_seg, tm)
    out_specs = [pl.BlockSpec((tm, n), lambda i: (i, 0))]
    out_shape = [jax.ShapeDtypeStruct((m, n), out_dtype)]
    if tail:
        out_specs.append(pl.BlockSpec((tm, LANES), lambda i: (i, 0)))
        out_shape.append(jax.ShapeDtypeStruct((m, LANES), F32))
    outs = pl.pallas_call(
        _norm_mm_kernel, grid=(m // tm,),
        in_specs=[pl.BlockSpec((tm, D_MODEL), lambda i: (i, 0)), pl.BlockSpec((1, D_MODEL), lambda i: (0, 0)),
                  seg, seg, pl.BlockSpec((D_MODEL, n), lambda i: (0, 0))],
        out_specs=out_specs, out_shape=out_shape,
        compiler_params=pltpu.CompilerParams(dimension_semantics=("parallel",),
                                             vmem_limit_bytes=VMEM_LIMIT_BYTES),
        name="norm_mm",
    )(x, norm_g.reshape(1, D_MODEL), sc, sh, w_bf)
    return outs if tail else outs[0]


def _split_kernel(a_ref, hi_ref, lo_ref):
    hi, lo = _split_bf16(a_ref[0])
    hi_ref[0] = hi
    lo_ref[0] = lo


def split_bf16(a):
    nl, nr, nc = a.shape
    blk = pl.BlockSpec((1, MM_ROW_TILE, nc), lambda l, i: (l, i, 0))
    return pl.pallas_call(
        _split_kernel, grid=(nl, nr // MM_ROW_TILE), in_specs=[blk], out_specs=[blk, blk],
        out_shape=[jax.ShapeDtypeStruct(a.shape, BF16)] * 2,
        compiler_params=pltpu.CompilerParams(dimension_semantics=("parallel", "parallel")),
        name="split_bf16",
    )(a)


def _mm3_kernel(a_ref, b_ref, o_ref):
    o_ref[...] = _dot3(a_ref[...], b_ref[...]).astype(o_ref.dtype)


def mm3(a, b, out_dtype):
    return pl.pallas_call(_mm3_kernel, out_shape=jax.ShapeDtypeStruct((a.shape[0], b.shape[1]), out_dtype),
                          compiler_params=pltpu.CompilerParams(vmem_limit_bytes=VMEM_LIMIT_BYTES),
                          name="mm3")(a, b)


def _seqmix_kernel(f_ref, z_ref, x_ref, gate_ref, o_ref, acc_ref):
    k = pl.program_id(2)

    @pl.when(k == 0)
    def _():
        acc_ref[...] = jnp.zeros_like(acc_ref)

    acc_ref[...] += _bdot(f_ref[...], z_ref[...])

    @pl.when(k == pl.num_programs(2) - 1)
    def _():
        o_ref[...] = x_ref[...] + gate_ref[0] * acc_ref[...]


def seq_mix_res(fmat, z, x, gate, seq_len, tm, tk):
    t = x.shape[0]
    nseq = t // seq_len
    seqs_per_seg = nseq // gate.shape[0]
    mt = seq_len // tm
    kt_half = seq_len // tk
    return pl.pallas_call(
        _seqmix_kernel, grid=(nseq, mt, 2 * kt_half),
        in_specs=[pl.BlockSpec((tm, tk), lambda s, i, k: (i, k)),
                  pl.BlockSpec((tk, D_MODEL), lambda s, i, k: (s * kt_half + k % kt_half, k // kt_half)),
                  pl.BlockSpec((tm, D_MODEL), lambda s, i, k: (s * mt + i, 0)),
                  pl.BlockSpec((1, 1, D_MODEL), lambda s, i, k: (s // seqs_per_seg, 0, 0))],
        out_specs=pl.BlockSpec((tm, D_MODEL), lambda s, i, k: (s * mt + i, 0)),
        out_shape=jax.ShapeDtypeStruct((t, D_MODEL), F32),
        scratch_shapes=[pltpu.VMEM((tm, D_MODEL), F32)],
        compiler_params=pltpu.CompilerParams(dimension_semantics=("parallel", "parallel", "arbitrary"),
                                             vmem_limit_bytes=VMEM_LIMIT_BYTES),
        name="seq_mix_res",
    )(fmat, z, x, gate)


def _dft_tables(n, cols):
    r = jnp.arange(n, dtype=jnp.int32)[:, None]
    ang = ((r * cols[None, :]) % n).astype(F32) * (2.0 * math.pi / n)
    return jnp.cos(ang), jnp.sin(ang)


def dft_seq_matrix(s):
    w = 1 << (int(math.log2(s)) // 2)
    ch, sh_ = _dft_tables(s, jnp.arange(s // w, dtype=jnp.int32) * w)
    cl, sl = _dft_tables(s, jnp.arange(w, dtype=jnp.int32))
    sc = 1.0 / math.sqrt(s)
    c = (ch[:, :, None] * cl[:, None, :] - sh_[:, :, None] * sl[:, None, :]).reshape(s, s) * sc
    sn = (sh_[:, :, None] * cl[:, None, :] + ch[:, :, None] * sl[:, None, :]).reshape(s, s) * sc
    return jnp.concatenate([c, -sn], axis=1).astype(BF16)


def dft_group_matrices(n, groups):
    c, s = _dft_tables(n, jnp.arange(n, dtype=jnp.int32))
    sc = 1.0 / math.sqrt(n)
    eye = jnp.eye(groups, dtype=F32)
    return jnp.kron(eye, c * sc), jnp.kron(eye, s * sc)


def _delta_prep_kernel(q_ref, k_ref, v_ref, lg_ref, bt_ref, w_ref, u_ref, qd_ref, kd_ref, p_ref, g_ref):
    r = lax.broadcasted_iota(jnp.int32, (CHUNK, CHUNK), 0)
    c = lax.broadcasted_iota(jnp.int32, (CHUNK, CHUNK), 1)
    eye = (r == c).astype(F32)
    ones = jnp.ones((CHUNK, CHUNK), F32)
    incl = (r >= c, r <= c)
    strict = (r > c, r < c)
    tri = (incl[0].astype(F32), incl[1].astype(F32))
    tri_t = (tri[1], tri[0])
    last = (CHUNK - 1, 0)
    chains = [(cg, d, h) for cg in range(PREP_CHUNKS) for d in range(2) for h in range(A_HEADS)]
    rows = lambda cg: slice(cg * CHUNK, (cg + 1) * CHUNK)
    cols = lambda h: slice(h * A_DK, (h + 1) * A_DK)
    kk = {}
    qk = {}
    for cg in range(PREP_CHUNKS):
        for h in range(A_HEADS):
            kb = k_ref[0, rows(cg), cols(h)].astype(BF16)
            kk[cg, h] = lax.dot_general(kb, kb, NT_DIMS, preferred_element_type=F32)
            qk[cg, h] = lax.dot_general(q_ref[0, rows(cg), cols(h)].astype(BF16), kb, NT_DIMS,
                                        preferred_element_type=F32)
    lgw = {}
    btw = {}
    for (cg, d, h) in chains:
        col = d * A_HEADS + h
        lgw[cg, d, h] = jnp.broadcast_to(lg_ref[0, rows(cg), col:col + 1], (CHUNK, A_DK))
        btw[cg, d, h] = jnp.broadcast_to(bt_ref[0, rows(cg), col:col + 1], (CHUNK, A_DK))
    gam = {ch: _dot_exact_lhs(tri[ch[1]], lgw[ch]) for ch in chains}
    gam_row = {ch: _dot_exact_lhs(ones, lgw[ch][:, :CHUNK] * tri_t[ch[1]]) for ch in chains}
    decay = {}
    lmat = {}
    for ch in chains:
        cg, d, h = ch
        diff = gam[ch][:, :CHUNK] - gam_row[ch]
        decay[ch] = jnp.where(incl[d], jnp.exp(jnp.where(incl[d], diff, 0.0)), 0.0)
        lmat[ch] = jnp.where(strict[d], btw[ch][:, :CHUNK] * decay[ch] * kk[cg, h], 0.0)
    pinv = {ch: eye - jnp.where((r // 2 == c // 2) & (r != c), lmat[ch], 0.0) for ch in chains}
    s = 2
    while s < CHUNK:
        join = (r // (2 * s) == c // (2 * s)) & (r // s != c // s)
        tc = {ch: _dot3(pinv[ch], jnp.where(join, lmat[ch], 0.0)) for ch in chains}
        pinv = {ch: pinv[ch] - _dot3(tc[ch], pinv[ch]) for ch in chains}
        s *= 2
    for ch in chains:
        cg, d, h = ch
        kh = k_ref[0, rows(cg), cols(h)]
        vh = v_ref[0, rows(cg), cols(h)]
        qh = q_ref[0, rows(cg), cols(h)]
        egam = jnp.exp(gam[ch])
        rhs = jnp.concatenate([kh * (btw[ch] * egam), vh * btw[ch]], axis=1)
        sol = _dot3(pinv[ch], rhs)
        w_ref[0, d, rows(cg), cols(h)] = sol[:, :A_DK].astype(BF16)
        u_ref[0, d, rows(cg), cols(h)] = sol[:, A_DK:]
        qd_ref[0, d, rows(cg), cols(h)] = (qh * egam).astype(BF16)
        glast = jnp.broadcast_to(gam[ch][last[d]:last[d] + 1, :], (CHUNK, A_DK))
        kd_ref[0, d, rows(cg), cols(h)] = (kh * jnp.exp(glast - gam[ch])).astype(BF16)
        g_ref[0, d, rows(cg), cols(h)] = jnp.exp(glast)
        p_ref[0, d, rows(cg), h * CHUNK:(h + 1) * CHUNK] = (decay[ch] * qk[cg, h]).astype(BF16)


def delta_prep(q, k, v, lg, bt):
    b, s, _ = q.shape
    rt = PREP_CHUNKS * CHUNK
    blk = lambda w: pl.BlockSpec((1, rt, w), lambda bi, ni: (bi, ni, 0))
    oblk = lambda w: pl.BlockSpec((1, 2, rt, w), lambda bi, ni: (bi, 0, ni, 0))
    sh = lambda w, dt: jax.ShapeDtypeStruct((b, 2, s, w), dt)
    return pl.pallas_call(
        _delta_prep_kernel, grid=(b, s // rt),
        in_specs=[blk(A_DIM), blk(A_DIM), blk(A_DIM), blk(2 * A_HEADS), blk(2 * A_HEADS)],
        out_specs=[oblk(A_DIM), oblk(A_DIM), oblk(A_DIM), oblk(A_DIM), oblk(A_HEADS * CHUNK), oblk(A_DIM)],
        out_shape=[sh(A_DIM, BF16), sh(A_DIM, F32), sh(A_DIM, BF16), sh(A_DIM, BF16),
                   sh(A_HEADS * CHUNK, BF16), sh(A_DIM, F32)],
        compiler_params=pltpu.CompilerParams(dimension_semantics=("parallel", "parallel")),
        name="delta_prep",
    )(q, k, v, lg, bt)


def _delta_scan_kernel(*refs):
    ins = refs[:12]
    s0_ref = refs[12]
    of_ref, ob_ref, sout_ref, state = refs[13:]
    n = pl.program_id(1)

    @pl.when(n == 0)
    def _():
        state[...] = s0_ref[...]

    outs = (of_ref, ob_ref)
    chains = [(g, d, h) for g in range(SCAN_SEQS) for d in range(2) for h in range(A_HEADS)]
    cs = lambda h: slice(h * A_DK, (h + 1) * A_DK)
    ref = lambda d, i: ins[d * 6 + i]
    s_old = {ch: state[ch] for ch in chains}
    wqs = {}
    for (g, d, h) in chains:
        wq = jnp.concatenate([ref(d, 0)[g, 0, :, cs(h)], ref(d, 2)[g, 0, :, cs(h)]], axis=0)
        wqs[g, d, h] = _bdot(wq, s_old[g, d, h].astype(BF16))
    unb = {}
    for (g, d, h) in chains:
        unb[g, d, h] = (ref(d, 1)[g, 0, :, cs(h)] - wqs[g, d, h][:CHUNK]).astype(BF16)
    for (g, d, h) in chains:
        o = wqs[g, d, h][CHUNK:] + _bdot(ref(d, 4)[g, 0, :, h * CHUNK:(h + 1) * CHUNK], unb[g, d, h])
        outs[d][g, :, cs(h)] = o
    for (g, d, h) in chains:
        upd = lax.dot_general(ref(d, 3)[g, 0, :, cs(h)], unb[g, d, h], TN_DIMS, preferred_element_type=F32)
        gs = jnp.broadcast_to(ref(d, 5)[g, 0, 0:1, cs(h)], (A_DK, A_DV))
        state[g, d, h] = gs * s_old[g, d, h] + upd

    @pl.when(n == pl.num_programs(1) - 1)
    def _():
        sout_ref[...] = state[...]


def delta_scan(w, u, qd, kd, p, gl, s0):
    b, _, s, _ = u.shape
    n = s // CHUNK

    def spec(wd, d):
        if d == 0:
            return pl.BlockSpec((SCAN_SEQS, 1, CHUNK, wd), lambda bi, ni: (bi, 0, ni, 0))
        return pl.BlockSpec((SCAN_SEQS, 1, CHUNK, wd), lambda bi, ni: (bi, 1, n - 1 - ni, 0))

    arrs = (w, u, qd, kd, p, gl)
    in_specs = [spec(a.shape[-1], d) for d in range(2) for a in arrs]
    st = pl.BlockSpec((SCAN_SEQS, 2, A_HEADS, A_DK, A_DV), lambda bi, ni: (bi, 0, 0, 0, 0))
    of = pl.BlockSpec((SCAN_SEQS, CHUNK, A_DIM), lambda bi, ni: (bi, ni, 0))
    ob = pl.BlockSpec((SCAN_SEQS, CHUNK, A_DIM), lambda bi, ni: (bi, n - 1 - ni, 0))
    return pl.pallas_call(
        _delta_scan_kernel, grid=(b // SCAN_SEQS, n),
        in_specs=in_specs + [st],
        out_specs=[of, ob, st],
        out_shape=[jax.ShapeDtypeStruct((b, s, A_DIM), F32), jax.ShapeDtypeStruct((b, s, A_DIM), F32),
                   jax.ShapeDtypeStruct((b, 2, A_HEADS, A_DK, A_DV), F32)],
        scratch_shapes=[pltpu.VMEM((SCAN_SEQS, 2, A_HEADS, A_DK, A_DV), F32)],
        compiler_params=pltpu.CompilerParams(dimension_semantics=("parallel", "arbitrary")),
        name="delta_scan",
    )(*(arrs + arrs), s0)


def _cand_tables():
    pairs = [(r, c) for r in range(PEER_TOPK) for c in range(PEER_TOPK) if (r + 1) * (c + 1) <= PEER_TOPK]
    npad = 64
    e1 = np.zeros((npad, PEER_TOPK), np.float32)
    e2 = np.zeros((npad, PEER_TOPK), np.float32)
    m = np.zeros((PEER_TOPK, npad), np.float32)
    for k, (r, c) in enumerate(pairs):
        e1[k, r] = 1
        e2[k, c] = 1
        m[r, k] = 1
    return len(pairs), e1, e2, m


N_CAND, _CAND_E1, _CAND_E2, _CAND_ROW = _cand_tables()


def _extract_topk(s, n_iter, want_rank=True):
    k, t = s.shape
    work = s.reshape(k // 8, 8, t)
    rank = jnp.full(work.shape, float(n_iter), F32) if want_rank else None
    vals = []
    for r in range(n_iter):
        m = jnp.max(jnp.max(work, axis=0), axis=0, keepdims=True)
        hit = work == jnp.broadcast_to(m, (8, t))[None]
        if want_rank:
            rank = jnp.where(hit, float(r), rank)
        work = jnp.where(hit, NEG_INF, work)
        vals.append(m)
    return vals, (rank.reshape(k, t) if want_rank else None)


def _route_kernel(x_ref, g_ref, sc_ref, sh_ref, wqh_ref, wql_ref, k1_ref, k2_ref, e1_ref, e2_ref, mrow_ref,
                  ht_ref, a_ref, n_ref, b_ref, r_ref, q_scr):
    hmod = _norm_modulate(x_ref[...], g_ref[...], sc_ref[0], sh_ref[0])
    ht_ref[...] = hmod.T.astype(BF16)
    hh, hl = _split_bf16(hmod)
    q_scr[...] = _bdot(hh, wqh_ref[...]) + (_bdot(hh, wql_ref[...]) + _bdot(hl, wqh_ref[...]))
    tt = x_ref.shape[0]

    def head(h, carry):
        c1 = pl.ds(pl.multiple_of(h * 2 * PEER_DK, PEER_DK), PEER_DK)
        c2 = pl.ds(pl.multiple_of(h * 2 * PEER_DK + PEER_DK, PEER_DK), PEER_DK)
        hp = lax.Precision.HIGHEST
        s1 = lax.dot_general(k1_ref[h], q_scr[:, c1], NT_DIMS, precision=hp, preferred_element_type=F32)
        s2 = lax.dot_general(k2_ref[h], q_scr[:, c2], NT_DIMS, precision=hp, preferred_element_type=F32)
        v1, _ = _extract_topk(s1, PEER_TOPK, want_rank=False)
        v2, rank2 = _extract_topk(s2, PEER_TOPK)
        v1m = jnp.concatenate(v1, axis=0)
        v2m = jnp.concatenate(v2, axis=0)
        cand = (jnp.dot(e1_ref[...], v1m, precision=hp, preferred_element_type=F32)
                + jnp.dot(e2_ref[...], v2m, precision=hp, preferred_element_type=F32))
        row = lax.broadcasted_iota(jnp.int32, cand.shape, 0)
        cand = jnp.where(row < N_CAND, cand, NEG_INF)
        _, crank = _extract_topk(cand, PEER_TOPK)
        sel = crank < float(PEER_TOPK)
        cmax = v1[0] + v2[0]
        z = jnp.sum(jnp.where(sel, jnp.exp(cand - cmax), 0.0), axis=0, keepdims=True)
        n_r = _bdot(mrow_ref[...], jnp.where(sel, 1.0, 0.0).astype(BF16))
        s1_3 = s1.reshape(PEER_KEYS // 8, 8, tt)
        nn = jnp.zeros_like(s1_3)
        for r in range(PEER_TOPK):
            nn = jnp.where(s1_3 == jnp.broadcast_to(v1[r], (8, tt))[None],
                           jnp.broadcast_to(n_r[r:r + 1, :], (8, tt))[None], nn)
        a_ref[h] = jnp.exp(s1 - v1[0]) / z
        n_ref[h] = nn.reshape(PEER_KEYS, tt)
        b_ref[h] = jnp.where(rank2 < float(PEER_TOPK), jnp.exp(s2 - v2[0]), 0.0).astype(BF16)
        r_ref[h] = rank2.astype(BF16)
        return carry

    lax.fori_loop(0, PEER_HEADS, head, 0, unroll=ROUTE_HEAD_UNROLL)


def peer_route(x, norm_g, sc, sh, wq_hi, wq_lo, k1, k2, rows_per_seg, layer):
    wq_spec = pl.BlockSpec((None, D_MODEL, 2 * PEER_HEADS * PEER_DK), lambda i: (layer, 0, 0))
    t = x.shape[0]
    tt = ROUTE_TOKEN_TILE
    gate = pl.BlockSpec((PEER_HEADS, PEER_KEYS, tt), lambda i: (0, 0, i))
    full = lambda shp: pl.BlockSpec(shp, lambda i: (0,) * len(shp))
    seg = _seg_spec(rows_per_seg, tt)
    return pl.pallas_call(
        _route_kernel, grid=(t // tt,),
        in_specs=[pl.BlockSpec((tt, D_MODEL), lambda i: (i, 0)), full((1, D_MODEL)), seg, seg,
                  wq_spec, wq_spec,
                  full((PEER_HEADS, PEER_KEYS, PEER_DK)),
                  full((PEER_HEADS, PEER_KEYS, PEER_DK)), full((64, PEER_TOPK)), full((64, PEER_TOPK)),
                  full((PEER_TOPK, 64))],
        out_specs=[pl.BlockSpec((D_MODEL, tt), lambda i: (0, i)), gate, gate, gate, gate],
        out_shape=[jax.ShapeDtypeStruct((D_MODEL, t), BF16)]
        + [jax.ShapeDtypeStruct((PEER_HEADS, PEER_KEYS, t), dt) for dt in (F32, F32, BF16, BF16)],
        scratch_shapes=[pltpu.VMEM((tt, 2 * PEER_HEADS * PEER_DK), F32)],
        compiler_params=pltpu.CompilerParams(dimension_semantics=("parallel",),
                                             vmem_limit_bytes=VMEM_LIMIT_BYTES),
        name="peer_route",
    )(x, norm_g.reshape(1, D_MODEL), sc, sh, wq_hi, wq_lo, k1, k2,
      jnp.asarray(_CAND_E1), jnp.asarray(_CAND_E2), jnp.asarray(_CAND_ROW, BF16))


def _peer_dense_kernel(ht_ref, u_ref, vt_ref, a_ref, n_ref, b_ref, r_ref, x_ref, g2_ref, o_ref,
                       acc_ref, hid_ref, w_ref, bp_ref, rp_ref):
    e = pl.program_id(1)

    @pl.when(e == 0)
    def _():
        acc_ref[...] = jnp.zeros_like(acc_ref)
        for c in range(PEER_TOKEN_TILE // LANES):
            ls = slice(c * LANES, (c + 1) * LANES)
            bp_ref[:, c] = b_ref[:, :, ls].astype(BF16)
            rp_ref[:, c] = r_ref[:, :, ls].astype(BF16)

    def hidden(pp):
        rows = slice(pp * PIECE_ROWS, (pp + 1) * PIECE_ROWS)
        hid_ref[pp % 2] = _bdot(u_ref[rows, :], ht_ref[...])

    def gates(pp):
        for c in range(PEER_TOKEN_TILE // LANES):
            ls = slice(c * LANES, (c + 1) * LANES)
            accs = [jnp.zeros((PEER_KEYS, LANES), BF16) for _ in range(PIECE_KEYS)]
            for h in range(PEER_HEADS):
                b = bp_ref[h, c]
                r = rp_ref[h, c]
                for ii in range(PIECE_KEYS):
                    k = pp * PIECE_KEYS + ii
                    a16 = jnp.broadcast_to(a_ref[h, k:k + 1, ls], (BF16_ROWS, LANES)).astype(BF16)
                    n16 = jnp.broadcast_to(n_ref[h, k:k + 1, ls], (BF16_ROWS, LANES)).astype(BF16)
                    a128 = jnp.concatenate([a16] * (PEER_KEYS // BF16_ROWS), axis=0)
                    n128 = jnp.concatenate([n16] * (PEER_KEYS // BF16_ROWS), axis=0)
                    accs[ii] = accs[ii] + jnp.where(r < n128, b, jnp.zeros_like(b)) * a128
            for ii in range(PIECE_KEYS):
                rows = slice(ii * PEER_KEYS, (ii + 1) * PEER_KEYS)
                act = _gelu_tanh(hid_ref[pp % 2, rows, ls].astype(BF16))
                w_ref[pp % 2, rows, ls] = act * accs[ii]

    def project(pp):
        rows = slice(pp * PIECE_ROWS, (pp + 1) * PIECE_ROWS)
        acc_ref[...] += _bdot(vt_ref[0, :, rows], w_ref[pp % 2])

    hidden(0)
    for pp in range(PEER_PIECES):
        if pp + 1 < PEER_PIECES:
            hidden(pp + 1)
        gates(pp)
        project(pp)

    @pl.when(e == pl.num_programs(1) - 1)
    def _():
        o_ref[...] = x_ref[...] + g2_ref[0] * acc_ref[...].T


def peer_dense(ht_bf, u_bf, vt_bf, a_t, n_t, b_t, r_t, x, g2, rows_per_seg, layer):
    t = x.shape[0]
    tt = PEER_TOKEN_TILE
    grid = (t // tt, PEER_N // PEER_EXPERT_TILE)
    gate_spec = pl.BlockSpec((PEER_HEADS, PEER_KEYS, tt), lambda ti, ei: (0, 0, ti))
    step_keys = pl.BlockSpec((PEER_HEADS, PEER_KEYS_PER_STEP, tt), lambda ti, ei: (0, ei, ti))
    return pl.pallas_call(
        _peer_dense_kernel, grid=grid,
        in_specs=[pl.BlockSpec((D_MODEL, tt), lambda ti, ei: (0, ti)),
                  pl.BlockSpec((None, PEER_EXPERT_TILE, D_MODEL), lambda ti, ei: (layer, ei, 0)),
                  pl.BlockSpec((None, 1, D_MODEL, PEER_EXPERT_TILE), lambda ti, ei: (layer, ei, 0, 0)),
                  step_keys, step_keys, gate_spec, gate_spec,
                  pl.BlockSpec((tt, D_MODEL), lambda ti, ei: (ti, 0)),
                  _seg_spec(rows_per_seg, tt)],
        out_specs=pl.BlockSpec((tt, D_MODEL), lambda ti, ei: (ti, 0)),
        out_shape=jax.ShapeDtypeStruct((t, D_MODEL), F32),
        scratch_shapes=[pltpu.VMEM((D_MODEL, tt), F32),
                        pltpu.VMEM((2, PIECE_ROWS, tt), F32),
                        pltpu.VMEM((2, PIECE_ROWS, tt), BF16),
                        pltpu.VMEM((PEER_HEADS, tt // LANES, PEER_KEYS, LANES), BF16),
                        pltpu.VMEM((PEER_HEADS, tt // LANES, PEER_KEYS, LANES), BF16)],
        compiler_params=pltpu.CompilerParams(dimension_semantics=("parallel", "arbitrary"),
                                             vmem_limit_bytes=VMEM_LIMIT_BYTES),
        name="peer_dense",
    )(ht_bf, u_bf, vt_bf, a_t, n_t, b_t, r_t, x, g2)


def _fill_halo_scratch(scr, prev, cur, nxt, halo, ts):
    i = pl.program_id(1)
    scr[0:halo, :] = jnp.where(i > 0, prev, 0.0)
    scr[halo:halo + ts, :] = cur
    scr[halo + ts:halo + ts + halo, :] = jnp.where(i < pl.num_programs(1) - 1, nxt, 0.0)


def _depthwise_taps(scr, w_ref, taps, halo, r0, rows):
    off = halo - taps // 2
    acc = scr[off + r0:off + r0 + rows, :] * w_ref[0:1, :]
    for k in range(1, taps):
        acc = acc + scr[off + r0 + k:off + r0 + k + rows, :] * w_ref[k:k + 1, :]
    return acc


def _halo_specs(width, col_block, halo, ts, seq_len, total_rows):
    tiles = seq_len // ts
    per_tile = ts // halo
    last = total_rows // halo - 1
    prev = pl.BlockSpec((halo, width), lambda s, i: (jnp.maximum((s * tiles + i) * per_tile - 1, 0), col_block))
    cur = pl.BlockSpec((ts, width), lambda s, i: (s * tiles + i, col_block))
    nxt = pl.BlockSpec((halo, width), lambda s, i: (jnp.minimum((s * tiles + i + 1) * per_tile, last), col_block))
    return prev, cur, nxt


CONF_ROW_BLOCK = 32
QKV_ROW_BLOCK = 16


def _conformer_kernel(prev_ref, cur_ref, next_ref, w_ref, b_ref, lg_ref, lb_ref, o_ref, scr):
    ts = cur_ref.shape[0]
    glu = lambda blk: blk[:, :B_CH] * jax.nn.sigmoid(blk[:, B_CH:])
    _fill_halo_scratch(scr, glu(prev_ref[...]), glu(cur_ref[...]), glu(next_ref[...]), CONF_HALO, ts)
    for rb in range(ts // CONF_ROW_BLOCK):
        r0 = rb * CONF_ROW_BLOCK
        hh = _depthwise_taps(scr, w_ref, B_CONV, CONF_HALO, r0, CONF_ROW_BLOCK) + b_ref[...]
        mu = jnp.mean(hh, axis=-1, keepdims=True)
        var = jnp.mean(jnp.square(hh - mu), axis=-1, keepdims=True)
        y = (hh - mu) * lax.rsqrt(var + EPS) * lg_ref[...] + lb_ref[...]
        o_ref[r0:r0 + CONF_ROW_BLOCK, :] = (y * jax.nn.sigmoid(y)).astype(o_ref.dtype)


def conformer_branch(p, dw_w, dw_b, ln_g, ln_b, nseq, seq_len):
    t = p.shape[0]
    ts = CONV_ROW_TILE
    prev, cur, nxt = _halo_specs(2 * B_CH, COL_GLU // (2 * B_CH), CONF_HALO, ts, seq_len, t)
    row = lambda: pl.BlockSpec((1, B_CH), lambda s, i: (0, 0))
    w_pad = jnp.pad(dw_w, ((0, 32 - B_CONV), (0, 0)))
    return pl.pallas_call(
        _conformer_kernel, grid=(nseq, seq_len // ts),
        in_specs=[prev, cur, nxt, pl.BlockSpec((32, B_CH), lambda s, i: (0, 0)), row(), row(), row()],
        out_specs=pl.BlockSpec((ts, B_CH), lambda s, i: (s * (seq_len // ts) + i, 0)),
        out_shape=jax.ShapeDtypeStruct((t, B_CH), BF16),
        scratch_shapes=[pltpu.VMEM((ts + 2 * CONF_HALO, B_CH), F32)],
        compiler_params=pltpu.CompilerParams(dimension_semantics=("parallel", "parallel")),
        name="conformer_branch",
    )(p, p, p, w_pad, dw_b.reshape(1, B_CH), ln_g.reshape(1, B_CH), ln_b.reshape(1, B_CH))


def _qkv_conv_kernel(prev_ref, cur_ref, next_ref, w_ref, q_ref, k_ref, v_ref, scr):
    ts = cur_ref.shape[0]
    _fill_halo_scratch(scr, prev_ref[...], cur_ref[...], next_ref[...], QKV_HALO, ts)
    for rb in range(ts // QKV_ROW_BLOCK):
        r0 = rb * QKV_ROW_BLOCK
        rows = slice(r0, r0 + QKV_ROW_BLOCK)
        y = _depthwise_taps(scr, w_ref, SHORT_CONV, QKV_HALO, r0, QKV_ROW_BLOCK)
        y = y * jax.nn.sigmoid(y)
        for h in range(A_HEADS):
            cs = slice(h * A_DK, (h + 1) * A_DK)
            qh = y[:, h * A_DK:(h + 1) * A_DK]
            kh = y[:, A_DIM + h * A_DK:A_DIM + (h + 1) * A_DK]
            q_ref[rows, cs] = qh * (lax.rsqrt(jnp.sum(qh * qh, axis=-1, keepdims=True) + EPS) * (A_DK ** -0.5))
            k_ref[rows, cs] = kh * lax.rsqrt(jnp.sum(kh * kh, axis=-1, keepdims=True) + EPS)
        v_ref[rows, :] = y[:, 2 * A_DIM:]


def qkv_conv(p, conv_w, nseq, seq_len):
    t = p.shape[0]
    ts = CONV_ROW_TILE
    prev, cur, nxt = _halo_specs(3 * A_DIM, 0, QKV_HALO, ts, seq_len, t)
    out = pl.BlockSpec((ts, A_DIM), lambda s, i: (s * (seq_len // ts) + i, 0))
    w_pad = jnp.pad(conv_w, ((0, 8 - SHORT_CONV), (0, 0)))
    return pl.pallas_call(
        _qkv_conv_kernel, grid=(nseq, seq_len // ts),
        in_specs=[prev, cur, nxt, pl.BlockSpec((8, 3 * A_DIM), lambda s, i: (0, 0))],
        out_specs=[out, out, out],
        out_shape=[jax.ShapeDtypeStruct((t, A_DIM), F32)] * 3,
        scratch_shapes=[pltpu.VMEM((ts + 2 * QKV_HALO, 3 * A_DIM), F32)],
        compiler_params=pltpu.CompilerParams(dimension_semantics=("parallel", "parallel")),
        name="qkv_conv",
    )(p, p, p, w_pad)


def _mix_out_kernel(of_ref, ob_ref, z_ref, conf_ref, ng_ref, w_ref, x_ref, gate_ref, o_ref):
    o = of_ref[...] + ob_ref[...]
    z = z_ref[...]
    parts = []
    for h in range(A_HEADS):
        cs = slice(h * A_DV, (h + 1) * A_DV)
        oh = o[:, cs]
        zh = z[:, cs]
        scale = lax.rsqrt(jnp.mean(oh * oh, axis=-1, keepdims=True) + EPS)
        parts.append((oh * scale * ng_ref[...] * (zh * jax.nn.sigmoid(zh))).astype(BF16))
    oa = jnp.concatenate(parts, axis=1)
    mix = _bdot(oa, w_ref[:A_DIM, :]) + _bdot(conf_ref[...], w_ref[A_DIM:, :])
    o_ref[...] = x_ref[...] + gate_ref[0] * mix


def mix_out(o_f, o_b, p, conf, norm_g, w_bf, x, gate, rows_per_seg):
    t = x.shape[0]
    tm = MM_ROW_TILE
    half = lambda cb: pl.BlockSpec((tm, A_DIM), lambda i: (i, cb))
    return pl.pallas_call(
        _mix_out_kernel, grid=(t // tm,),
        in_specs=[half(0), half(0), half(COL_Z // A_DIM), half(0), pl.BlockSpec((1, A_DV), lambda i: (0, 0)),
                  pl.BlockSpec((D_MODEL, D_MODEL), lambda i: (0, 0)), pl.BlockSpec((tm, D_MODEL), lambda i: (i, 0)),
                  _seg_spec(rows_per_seg, tm)],
        out_specs=pl.BlockSpec((tm, D_MODEL), lambda i: (i, 0)),
        out_shape=jax.ShapeDtypeStruct((t, D_MODEL), F32),
        compiler_params=pltpu.CompilerParams(dimension_semantics=("parallel",),
                                             vmem_limit_bytes=VMEM_LIMIT_BYTES),
        name="mix_out",
    )(o_f, o_b, p, conf, norm_g.reshape(1, A_DV), w_bf, x, gate)


def grid_pos_emb(n_tokens):
    rows = n_tokens // GRID_W
    r = jnp.repeat(jnp.arange(rows, dtype=F32), GRID_W)
    col = jnp.tile(jnp.arange(GRID_W, dtype=F32), rows)
    nf = D_MODEL // 4
    freqs = jnp.exp(-math.log(POS_BASE) * jnp.arange(nf, dtype=F32) / nf)
    ar = r[:, None] * freqs
    ac = col[:, None] * freqs
    return jnp.concatenate([jnp.sin(ar), jnp.cos(ar), jnp.sin(ac), jnp.cos(ac)], axis=-1)


def delta_conformer_layer(x, p, ab, g1, nseq, seq_len, s0, e, prm, rows_per_seg):
    t = p.shape[0]
    seq = lambda m: m.reshape(nseq, seq_len, m.shape[-1])
    q, k, v = qkv_conv(p, prm['conv_qkv_w'][e], nseq, seq_len)
    alpha = ab[:, :2 * A_HEADS]
    beta = jax.nn.sigmoid(ab[:, 2 * A_HEADS:4 * A_HEADS])
    log_g = (-jnp.exp(prm['a_log'][e]).reshape(1, 2 * A_HEADS)
             * jax.nn.softplus(alpha + prm['dt_bias'][e].reshape(1, 2 * A_HEADS)))
    o_f, o_b, st = delta_scan(*delta_prep(seq(q), seq(k), seq(v), seq(log_g), seq(beta)), s0)
    conf = conformer_branch(p, prm['conf_dw_w'][e], prm['conf_dw_b'][e], prm['conf_ln_g'][e],
                            prm['conf_ln_b'][e], nseq, seq_len)
    x = mix_out(o_f.reshape(t, A_DIM), o_b.reshape(t, A_DIM), p, conf, prm['delta_norm_g'][e],
                prm['w_out_bf'][e], x, g1, rows_per_seg)
    return x, st


def trunk(x, nseq, seq_len, cond, s0, prm):
    t = x.shape[0]
    rows_per_seg = t // cond.shape[0]
    states = []
    for l in range(DEPTH):
        mod = jax.nn.silu(cond) @ prm['ada_w'][l] + prm['ada_b'][l]
        sh1, sc1, g1, sh2, sc2, g2 = [m[:, None, :] for m in jnp.split(mod, 6, axis=-1)]
        e = l // 2
        if l % 2 == 0:
            p, ab = norm_mm(x, prm['norm1_g'][l], sc1, sh1, prm['w_in_bf'][e], rows_per_seg, F32, tail=True)
            x, st = delta_conformer_layer(x, p, ab, g1, nseq, seq_len, s0[:, e], e, prm, rows_per_seg)
            states.append(st)
        else:
            z = norm_mm(x, prm['norm1_g'][l], sc1, sh1, prm['w_fnet_bf'][e], rows_per_seg, BF16)
            tile = min(seq_len, SEQ_MIX_TILE)
            x = seq_mix_res(prm['dft_seq'][seq_len], z, x, g1, seq_len, tile, tile)
        ht, a_t, n_t, b_t, r_t = peer_route(x, prm['norm2_g'][l], sc2, sh2, prm['peer_wq_hi'],
                                            prm['peer_wq_lo'], prm['peer_k1'][l], prm['peer_k2'][l],
                                            rows_per_seg, l)
        x = peer_dense(ht, prm['peer_u_bf'], prm['peer_vt_bf'], a_t, n_t, b_t, r_t, x, g2, rows_per_seg, l)
    xf = x * lax.rsqrt(jnp.mean(x * x, axis=-1, keepdims=True) + EPS) * prm['final_norm_g']
    return xf, jnp.stack(states, axis=1)


def kernel(x_prompt, x_sample, state_delta, c, c_ctx, ada_w, ada_b, norm1_g, norm2_g, w_in_ab, conv_qkv_w,
           a_log, dt_bias, delta_norm_g, conf_dw_w, conf_dw_b, conf_ln_g, conf_ln_b, w_out_ab, w_out_c,
           peer_wq, peer_k1, peer_k2, peer_u, peer_v, final_norm_g):
    bp, sp, _ = x_prompt.shape
    bs, ss, _ = x_sample.shape
    bdc, bds = dft_group_matrices(D_MODEL // C_GROUPS, C_GROUPS)
    w_fnet = [jnp.concatenate([mm3(bdc, w_out_c[e], BF16), mm3(bds, w_out_c[e], BF16)], axis=1)
              for e in range(DEPTH // 2)]
    o4 = 4 * A_DIM
    w_in = jnp.concatenate([w_in_ab[:, :, :o4], w_in_ab[:, :, o4 + 4 * A_HEADS:], w_in_ab[:, :, o4:o4 + 4 * A_HEADS]],
                           axis=-1).astype(BF16)
    wq_hi, wq_lo = split_bf16(peer_wq)
    prm = {'ada_w': ada_w, 'ada_b': ada_b, 'norm1_g': norm1_g, 'norm2_g': norm2_g,
           'w_in_bf': jnp.pad(w_in, ((0, 0), (0, 0), (0, P_AB_PAD - P_AB))),
           'conv_qkv_w': conv_qkv_w, 'a_log': a_log, 'dt_bias': dt_bias,
           'delta_norm_g': delta_norm_g, 'conf_dw_w': conf_dw_w, 'conf_dw_b': conf_dw_b,
           'conf_ln_g': conf_ln_g, 'conf_ln_b': conf_ln_b, 'w_out_bf': w_out_ab.astype(BF16),
           'w_fnet_bf': w_fnet, 'dft_seq': {s: dft_seq_matrix(s) for s in {sp, ss}},
           'peer_wq_hi': wq_hi, 'peer_wq_lo': wq_lo, 'peer_k1': peer_k1, 'peer_k2': peer_k2,
           'peer_u_bf': peer_u.astype(BF16),
           'peer_vt_bf': jnp.transpose(peer_v.astype(BF16).reshape(DEPTH, PEER_N // PEER_EXPERT_TILE,
                                                                   PEER_EXPERT_TILE, D_MODEL), (0, 1, 3, 2)),
           'final_norm_g': final_norm_g}
    ne = (DEPTH + 1) // 2
    s0_ctx = jnp.zeros((bp, ne, 2, A_HEADS, A_DK, A_DV), F32)
    y_prompt, ctx_states = trunk(x_prompt.reshape(bp * sp, D_MODEL), bp, sp, c_ctx[None, :], s0_ctx, prm)
    xs = (x_sample + grid_pos_emb(ss)[None]).reshape(bs * ss, D_MODEL)
    y_sample, _ = trunk(xs, bs, ss, c, state_delta, prm)
    return (y_prompt.reshape(bp, sp, D_MODEL), y_sample.reshape(bs, ss, D_MODEL), ctx_states)
```
